```python
import jax, jax.numpy as jnp
from jax import lax
import numpy as np

D_MODEL = 1024
BATCH = 8
SEQ = 2048
DEPTH = 2

HEAD_DIM = 64
NSA_HEADS = 8
NSA_KV_GROUPS = 2
NSA_HPG = NSA_HEADS // NSA_KV_GROUPS
NSA_WIDTH = NSA_HEADS * HEAD_DIM
KV_WIDTH = NSA_KV_GROUPS * HEAD_DIM
CMP_BLOCK = 32
CMP_STRIDE = 16
CMP_HIDDEN = 128
SEL_BLOCK = 64
SEL_TOPK = 16
WINDOW = 512
Q_BLOCK = 128
GM_GROUPS = 8
GM_CHUNK = 128
GM_WIDTH = GM_GROUPS * HEAD_DIM
N_BRANCH = 2
D_FF = 2816
ROPE_THETA = 10000.0
EPS = 1e-6
NEG = -1e30
FORCE = 1e4
IN_SIZES = [NSA_WIDTH, 6 * KV_WIDTH, 3 * NSA_HEADS, 2 * GM_WIDTH, D_MODEL, D_MODEL]
IN_WIDTH = sum(IN_SIZES)
IN_SPLITS = np.cumsum(IN_SIZES)[:-1].tolist()

kernel_name = "hybrid_nsa_gmlp_macaron_adaln"


def _rmsnorm(x, g):
    x32 = x.astype(jnp.float32)
    y = x32 * lax.rsqrt(jnp.mean(x32 * x32, axis=-1, keepdims=True) + EPS)
    return y.astype(x.dtype) * g


def _layernorm(x, g, b):
    x32 = x.astype(jnp.float32)
    mu = jnp.mean(x32, axis=-1, keepdims=True)
    var = jnp.mean(jnp.square(x32 - mu), axis=-1, keepdims=True)
    return ((x32 - mu) * lax.rsqrt(var + EPS)).astype(x.dtype) * g + b


def _rope_tables(pos):
    inv = 1.0 / (ROPE_THETA ** (jnp.arange(0, HEAD_DIM, 2, dtype=jnp.float32) / HEAD_DIM))
    ang = pos.astype(jnp.float32)[:, None] * inv[None, :]
    return jnp.cos(ang), jnp.sin(ang)


def _rope(x, cos, sin):
    x32 = x.astype(jnp.float32)
    x1, x2 = jnp.split(x32, 2, axis=-1)
    c = cos[None, :, None, :]
    s = sin[None, :, None, :]
    return jnp.concatenate([x1 * c - x2 * s, x2 * c + x1 * s], axis=-1).astype(x.dtype)


def _modulate(xn, shift, scale):
    return xn * (1.0 + scale) + shift


def _swiglu(x, w_in, w_out):
    a, b = jnp.split(x @ w_in, 2, axis=-1)
    return (jax.nn.silu(a) * b) @ w_out


def _nsa(q, k_cmp, v_cmp, k_sel, v_sel, k_win, v_win, gates, pe, w1, w2):
    B, S = q.shape[0], q.shape[1]
    G, HG = NSA_KV_GROUPS, NSA_HPG
    scale = HEAD_DIM ** -0.5
    pos = jnp.arange(S, dtype=jnp.int32)
    cos, sin = _rope_tables(pos)
    qr = _rope(q, cos, sin).reshape(B, S, G, HG, HEAD_DIM)

    n_cmp = (S - CMP_BLOCK) // CMP_STRIDE + 1
    starts = jnp.arange(n_cmp, dtype=jnp.int32) * CMP_STRIDE
    win_idx = starts[:, None] + jnp.arange(CMP_BLOCK, dtype=jnp.int32)[None, :]

    def compress(t, j):
        blk = t[:, win_idx] + pe[j][None, None, :, None, :]
        h = jax.nn.silu(jnp.einsum('bnlgd,ldf->bngf', blk, w1[j]))
        return h @ w2[j]

    kc = compress(k_cmp, 0)
    vc = compress(v_cmp, 1)
    cend = starts + CMP_BLOCK - 1
    ccos, csin = _rope_tables(cend)
    kc = _rope(kc, ccos, csin)
    s_c = jnp.einsum('bsghd,bngd->bsghn', qr, kc).astype(jnp.float32) * scale
    m_c = (cend[None, :] <= pos[:, None])[None, :, None, None, :]
    p_c = jax.nn.softmax(jnp.where(m_c, s_c, NEG), axis=-1) * m_c
    o_cmp = jnp.einsum('bsghn,bngd->bsghd', p_c.astype(vc.dtype), vc)

    n_sel = S // SEL_BLOCK
    top = min(SEL_TOPK, n_sel)
    sel_start = jnp.arange(n_sel, dtype=jnp.int32) * SEL_BLOCK
    overlap = jnp.clip(
        jnp.minimum(starts[:, None] + CMP_BLOCK, sel_start[None, :] + SEL_BLOCK)
        - jnp.maximum(starts[:, None], sel_start[None, :]), 0, None
    ).astype(jnp.float32) / CMP_BLOCK
    imp = jnp.einsum('bsgn,nj->bsgj', jnp.sum(p_c, axis=3), overlap)
    cur = pos // SEL_BLOCK
    blk = jnp.arange(n_sel, dtype=jnp.int32)
    forced = (blk[None, :] == 0) | (blk[None, :] == cur[:, None]) | (blk[None, :] == cur[:, None] - 1)
    valid = blk[None, :] <= cur[:, None]
    imp = jnp.where(forced[None, :, None, :], FORCE,
                    jnp.where(valid[None, :, None, :], imp, -FORCE))
    _, sel_idx = lax.top_k(imp, top)
    sel_idx = sel_idx.transpose(0, 2, 1, 3)

    ks = _rope(k_sel, cos, sin).reshape(B, n_sel, SEL_BLOCK, G, HEAD_DIM).transpose(0, 3, 1, 2, 4)
    vs = v_sel.reshape(B, n_sel, SEL_BLOCK, G, HEAD_DIM).transpose(0, 3, 1, 2, 4)
    pad = ((0, 0), (WINDOW, 0), (0, 0), (0, 0))
    kw = jnp.pad(_rope(k_win, cos, sin), pad)
    vw = jnp.pad(v_win, pad)
    bi = jnp.arange(B)[:, None, None, None]
    gi = jnp.arange(G)[None, :, None, None]
    jj = jnp.arange(SEL_BLOCK, dtype=jnp.int32)
    wj = jnp.arange(Q_BLOCK + WINDOW, dtype=jnp.int32)

    def block(qb):
        q0 = qb * Q_BLOCK
        qq = lax.dynamic_slice_in_dim(qr, q0, Q_BLOCK, axis=1)
        tq = q0 + jnp.arange(Q_BLOCK, dtype=jnp.int32)
        idx = lax.dynamic_slice_in_dim(sel_idx, q0, Q_BLOCK, axis=2)
        kg = ks[bi, gi, idx]
        vg = vs[bi, gi, idx]
        s = jnp.einsum('bqghd,bgqnjd->bgqhnj', qq, kg).astype(jnp.float32) * scale
        kpos = idx[..., None] * SEL_BLOCK + jj
        m = (kpos <= tq[None, None, :, None, None])[:, :, :, None]
        s = jnp.where(m, s, NEG).reshape(B, G, Q_BLOCK, HG, top * SEL_BLOCK)
        p = jax.nn.softmax(s, axis=-1).reshape(B, G, Q_BLOCK, HG, top, SEL_BLOCK)
        o_s = jnp.einsum('bgqhnj,bgqnjd->bqghd', p.astype(vg.dtype), vg)
        kk = lax.dynamic_slice_in_dim(kw, q0, Q_BLOCK + WINDOW, axis=1)
        vv = lax.dynamic_slice_in_dim(vw, q0, Q_BLOCK + WINDOW, axis=1)
        kp = q0 - WINDOW + wj
        d = tq[:, None] - kp[None, :]
        mw = ((d >= 0) & (d < WINDOW) & (kp[None, :] >= 0))[None, :, None, None, :]
        sw = jnp.einsum('bqghd,bkgd->bqghk', qq, kk).astype(jnp.float32) * scale
        pw = jax.nn.softmax(jnp.where(mw, sw, NEG), axis=-1)
        o_w = jnp.einsum('bqghk,bkgd->bqghd', pw.astype(vv.dtype), vv)
        return o_s, o_w

    o_sel, o_win = lax.map(block, jnp.arange(S // Q_BLOCK, dtype=jnp.int32))
    o_sel = o_sel.transpose(1, 0, 2, 3, 4, 5).reshape(B, S, G, HG, HEAD_DIM)
    o_win = o_win.transpose(1, 0, 2, 3, 4, 5).reshape(B, S, G, HG, HEAD_DIM)
    g = gates.reshape(B, S, G, HG, 3)
    o = g[..., 0:1] * o_cmp + g[..., 1:2] * o_sel + g[..., 2:3] * o_win
    return o.reshape(B, S, NSA_WIDTH)


def _gmlp(uv, ln_g, ln_b, ws, bs):
    B, S = uv.shape[0], uv.shape[1]
    u, v = jnp.split(jax.nn.gelu(uv), 2, axis=-1)
    v = _layernorm(v, ln_g, ln_b)
    v = v.reshape(B, S // GM_CHUNK, GM_CHUNK, GM_GROUPS, HEAD_DIM)
    w = ws * jnp.tril(jnp.ones((GM_CHUNK, GM_CHUNK), ws.dtype))
    sv = jnp.einsum('gts,bnsgc->bntgc', w, v) + bs.T[None, None, :, :, None]
    return u * sv.reshape(B, S, GM_WIDTH)


def setup_inputs(seed: int = 0) -> dict:
    key = jax.random.key(seed)
    k = jax.random.split(key, 24)
    L, D = DEPTH, D_MODEL

    def nrm(kk, shape, scale):
        return jax.random.normal(kk, shape, jnp.float32) * scale

    return {
        "x": nrm(k[0], (BATCH, SEQ, D), 1.0),
        "c": nrm(k[1], (BATCH, D), 1.0),
        "ada_w": nrm(k[2], (L, D, 9 * D), 0.5 * D ** -0.5),
        "ada_b": nrm(k[3], (L, 9 * D), 0.1),
        "norm_g": 1.0 + nrm(k[4], (L, 3, D), 0.1),
        "ffn_w_in": nrm(k[5], (L, 2, D, 2 * D_FF), D ** -0.5),
        "ffn_w_out": nrm(k[6], (L, 2, D_FF, D), D_FF ** -0.5),
        "mix_w_in": nrm(k[7], (L, D, IN_WIDTH), D ** -0.5),
        "cmp_pe": nrm(k[8], (L, 2, CMP_BLOCK, HEAD_DIM), 0.1),
        "cmp_w1": nrm(k[9], (L, 2, CMP_BLOCK, HEAD_DIM, CMP_HIDDEN), (CMP_BLOCK * HEAD_DIM) ** -0.5),
        "cmp_w2": nrm(k[10], (L, 2, CMP_HIDDEN, HEAD_DIM), CMP_HIDDEN ** -0.5),
        "gm_ln_g": 1.0 + nrm(k[11], (L, GM_WIDTH), 0.1),
        "gm_ln_b": nrm(k[12], (L, GM_WIDTH), 0.1),
        "gm_ws": nrm(k[13], (L, GM_GROUPS, GM_CHUNK, GM_CHUNK), GM_CHUNK ** -0.5),
        "gm_bs": 1.0 + nrm(k[14], (L, GM_GROUPS, GM_CHUNK), 0.1),
        "proj_a": nrm(k[15], (L, NSA_WIDTH, D), NSA_WIDTH ** -0.5),
        "proj_b": nrm(k[16], (L, GM_WIDTH, D), GM_WIDTH ** -0.5),
        "w_out": nrm(k[17], (L, D, D), D ** -0.5),
        "final_g": 1.0 + nrm(k[18], (D,), 0.1),
    }


def reference(x, c, ada_w, ada_b, norm_g, ffn_w_in, ffn_w_out, mix_w_in, cmp_pe, cmp_w1,
              cmp_w2, gm_ln_g, gm_ln_b, gm_ws, gm_bs, proj_a, proj_b, w_out, final_g):
    B, S, D = x.shape
    h = x
    for l in range(DEPTH):
        mod = (jax.nn.silu(c) @ ada_w[l] + ada_b[l]).reshape(B, 3, 3, 1, D)

        n = _modulate(_rmsnorm(h, norm_g[l, 0]), mod[:, 0, 0], mod[:, 0, 1])
        h = h + 0.5 * mod[:, 0, 2] * _swiglu(n, ffn_w_in[l, 0], ffn_w_out[l, 0])

        n = _modulate(_rmsnorm(h, norm_g[l, 1]), mod[:, 1, 0], mod[:, 1, 1])
        z = n @ mix_w_in[l]
        zq, zkv, zg, zuv, zga, zgb = jnp.split(z, IN_SPLITS, axis=-1)
        q = zq.reshape(B, S, NSA_HEADS, HEAD_DIM)
        kv = zkv.reshape(B, S, 6, NSA_KV_GROUPS, HEAD_DIM)
        nsa_gates = jax.nn.sigmoid(zg).reshape(B, S, NSA_HEADS, 3)
        y_a = _nsa(q, kv[:, :, 0], kv[:, :, 1], kv[:, :, 2], kv[:, :, 3], kv[:, :, 4], kv[:, :, 5],
                   nsa_gates, cmp_pe[l], cmp_w1[l], cmp_w2[l])
        y_b = _gmlp(zuv, gm_ln_g[l], gm_ln_b[l], gm_ws[l], gm_bs[l])
        merged = jax.nn.sigmoid(zga) * (y_a @ proj_a[l]) + jax.nn.sigmoid(zgb) * (y_b @ proj_b[l])
        h = h + mod[:, 1, 2] * (merged @ w_out[l])

        n = _modulate(_rmsnorm(h, norm_g[l, 2]), mod[:, 2, 0], mod[:, 2, 1])
        h = h + 0.5 * mod[:, 2, 2] * _swiglu(n, ffn_w_in[l, 1], ffn_w_out[l, 1])
    return _rmsnorm(h, final_g)
```

```python
import functools

import numpy as np
import jax
import jax.numpy as jnp
from jax import lax
from jax.experimental import pallas as pl
from jax.experimental.pallas import tpu as pltpu

F32 = jnp.float32
BF16 = jnp.bfloat16

HEAD_DIM = 64
NSA_HEADS = 8
KV_GROUPS = 2
HPG = NSA_HEADS // KV_GROUPS
NSA_WIDTH = NSA_HEADS * HEAD_DIM
CMP_BLOCK = 32
CMP_STRIDE = 16
CMP_HIDDEN = 128
SEL_BLOCK = 64
SEL_TOPK = 16
WINDOW = 512
GM_GROUPS = 8
GM_CHUNK = 128
GM_WIDTH = GM_GROUPS * HEAD_DIM
ROPE_THETA = 10000.0
EPS = 1e-6
NEG = -1e30
FORCE = 1e4
LANE = 128
GELU_C = float(np.sqrt(2.0 / np.pi))

FFN_ROWS = 512
MIX_ROWS = 512
ATT_ROWS = 256
ATT_CHUNK = 256
VMEM_LIMIT = 56 * 1024 * 1024

_NT = (((1,), (1,)), ((), ()))


def _sigmoid(x):
    return 1.0 / (1.0 + jnp.exp(-x))


def _silu(x):
    return x * _sigmoid(x)


def _rms_mod(x, g, shift, scale):
    y = x * lax.rsqrt(jnp.mean(x * x, axis=-1, keepdims=True) + EPS) * g
    return y * (1.0 + scale) + shift


def _ada_kernel(c_ref, w_ref, b_ref, o_ref):
    s = _silu(c_ref[...])
    o_ref[0] = jnp.dot(s, w_ref[0], preferred_element_type=F32,
                       precision=lax.Precision.HIGHEST) + b_ref[0]


def _ada(c, ada_w, ada_b):
    L, D, N = ada_w.shape
    B = c.shape[0]
    tn = 1024
    return pl.pallas_call(
        _ada_kernel,
        grid=(L, N // tn),
        in_specs=[
            pl.BlockSpec((B, D), lambda l, j: (0, 0)),
            pl.BlockSpec((1, D, tn), lambda l, j: (l, 0, j)),
            pl.BlockSpec((1, 1, tn), lambda l, j: (l, 0, j)),
        ],
        out_specs=pl.BlockSpec((1, B, tn), lambda l, j: (l, 0, j)),
        out_shape=jax.ShapeDtypeStruct((L, B, N), F32),
        compiler_params=pltpu.CompilerParams(
            dimension_semantics=("parallel", "parallel"), vmem_limit_bytes=VMEM_LIMIT),
        name="ada",
    )(c, ada_w, ada_b.reshape(L, 1, N))


def _ffn_kernel(h_ref, shift_ref, scale_ref, gate_ref, g_ref, wa_ref, wb_ref, wo_ref,
                o_ref, n_scr, acc_scr):
    j = pl.program_id(2)

    @pl.when(j == 0)
    def _():
        n = _rms_mod(h_ref[0], g_ref[...], shift_ref[0], scale_ref[0])
        n_scr[...] = n.astype(BF16)
        acc_scr[...] = jnp.zeros_like(acc_scr)

    n = n_scr[...]
    a = jnp.dot(n, wa_ref[...], preferred_element_type=F32)
    b = jnp.dot(n, wb_ref[...], preferred_element_type=F32)
    hm = (_silu(a) * b).astype(BF16)
    acc_scr[...] += jnp.dot(hm, wo_ref[...], preferred_element_type=F32)

    @pl.when(j == pl.num_programs(2) - 1)
    def _():
        o_ref[0] = h_ref[0] + (0.5 * gate_ref[0]) * acc_scr[...]


def _ffn(h, shift, scale, gate, g, w_in, w_out):
    B, S, D = h.shape
    F = w_out.shape[0]
    tm = FFN_ROWS
    nf = 2
    tf = F // nf
    row = pl.BlockSpec((1, tm, D), lambda b, i, j: (b, i, 0))
    vec = pl.BlockSpec((1, 1, D), lambda b, i, j: (b, 0, 0))
    return pl.pallas_call(
        _ffn_kernel,
        grid=(B, S // tm, nf),
        in_specs=[
            row, vec, vec, vec,
            pl.BlockSpec((1, D), lambda b, i, j: (0, 0)),
            pl.BlockSpec((D, tf), lambda b, i, j: (0, j)),
            pl.BlockSpec((D, tf), lambda b, i, j: (0, j + nf)),
            pl.BlockSpec((tf, D), lambda b, i, j: (j, 0)),
        ],
        out_specs=row,
        out_shape=jax.ShapeDtypeStruct((B, S, D), F32),
        scratch_shapes=[pltpu.VMEM((tm, D), BF16), pltpu.VMEM((tm, D), F32)],
        compiler_params=pltpu.CompilerParams(
            dimension_semantics=("parallel", "parallel", "arbitrary"),
            vmem_limit_bytes=VMEM_LIMIT),
        name="ffn",
    )(h, shift, scale, gate, g, w_in, w_in, w_out)


def _rope_pair(x, c, s):
    lane = lax.broadcasted_iota(jnp.int32, x.shape, 1)
    first_half = (lane & (HEAD_DIM - 1)) < HEAD_DIM // 2
    swapped = jnp.where(first_half, pltpu.roll(x, LANE - HEAD_DIM // 2, 1),
                        pltpu.roll(x, HEAD_DIM // 2, 1))
    return x * c + swapped * s


def _mix_in_kernel(h_ref, shift_ref, scale_ref, g_ref, wq_ref, wkv_ref, wg_ref, wuv_ref,
                   wga_ref, wgb_ref, cos_ref, sin_ref, lng_ref, lnb_ref, ws_ref, bsx_ref,
                   pb_ref,
                   q_ref, kcmp_ref, vcmp_ref, ksel_ref, vsel_ref, kwin_ref, vwin_ref,
                   gates_ref, ga_ref, mb_ref):
    tm = h_ref.shape[1]
    n = _rms_mod(h_ref[0], g_ref[...], shift_ref[0], scale_ref[0]).astype(BF16)
    cos = cos_ref[...]
    sin = sin_ref[...]

    q = jnp.dot(n, wq_ref[...], preferred_element_type=F32)
    for p in range(NSA_HEADS // 2):
        qp = _rope_pair(q[:, p * LANE:(p + 1) * LANE], cos, sin).astype(BF16)
        q_ref[0, 2 * p] = qp[:, :HEAD_DIM]
        q_ref[0, 2 * p + 1] = qp[:, HEAD_DIM:]

    kv = jnp.dot(n, wkv_ref[...], preferred_element_type=F32)
    kcmp_ref[0] = kv[:, 0 * LANE:1 * LANE]
    vcmp_ref[0] = kv[:, 1 * LANE:2 * LANE]
    for idx, ref, roped in ((2, ksel_ref, True), (3, vsel_ref, False),
                            (4, kwin_ref, True), (5, vwin_ref, False)):
        t = kv[:, idx * LANE:(idx + 1) * LANE]
        if roped:
            t = _rope_pair(t, cos, sin)
        t = t.astype(BF16)
        ref[0, 0] = t[:, :HEAD_DIM]
        ref[0, 1] = t[:, HEAD_DIM:]

    gates_ref[0] = _sigmoid(jnp.dot(n, wg_ref[...], preferred_element_type=F32))
    ga_ref[0] = _sigmoid(jnp.dot(n, wga_ref[...], preferred_element_type=F32))

    uv = jnp.dot(n, wuv_ref[...], preferred_element_type=F32)
    ge = uv * (0.5 * (1.0 + jnp.tanh(GELU_C * (uv + 0.044715 * (uv * uv * uv)))))
    u = ge[:, :GM_WIDTH]
    v = ge[:, GM_WIDTH:]
    mu = jnp.mean(v, axis=-1, keepdims=True)
    var = jnp.mean(jnp.square(v - mu), axis=-1, keepdims=True)
    vln = ((v - mu) * lax.rsqrt(var + EPS) * lng_ref[...] + lnb_ref[...]).astype(BF16)

    ti = lax.broadcasted_iota(jnp.int32, (GM_CHUNK, GM_CHUNK), 0)
    si = lax.broadcasted_iota(jnp.int32, (GM_CHUNK, GM_CHUNK), 1)
    tril = si <= ti
    wm = [jnp.where(tril, ws_ref[gg], 0.0).astype(BF16) for gg in range(GM_GROUPS)]
    low_lanes = lax.broadcasted_iota(jnp.int32, (GM_CHUNK, LANE), 1) < HEAD_DIM
    bsx = bsx_ref[...]
    yb_rows = []
    for r in range(tm // GM_CHUNK):
        vch = vln[r * GM_CHUNK:(r + 1) * GM_CHUNK]
        pieces = []
        for p in range(GM_GROUPS // 2):
            vp = vch[:, p * LANE:(p + 1) * LANE]
            a0 = jnp.dot(wm[2 * p], vp, preferred_element_type=F32)
            a1 = jnp.dot(wm[2 * p + 1], vp, preferred_element_type=F32)
            pieces.append(jnp.where(low_lanes, a0, a1))
        sv = jnp.concatenate(pieces, axis=1) + bsx
        yb_rows.append(u[r * GM_CHUNK:(r + 1) * GM_CHUNK] * sv)
    yb = jnp.concatenate(yb_rows, axis=0).astype(BF16)
    gb = _sigmoid(jnp.dot(n, wgb_ref[...], preferred_element_type=F32))
    mb_ref[0] = gb * jnp.dot(yb, pb_ref[...], preferred_element_type=F32)


def _mix_in(h, shift, scale, g, wq, wkv, wg, wuv, wga, wgb, cos_t, sin_t, ln_g, ln_b, ws, bsx,
            proj_b):
    B, S, D = h.shape
    tm = MIX_ROWS
    G = KV_GROUPS

    def const(a):
        nd = a.ndim
        return pl.BlockSpec(a.shape, lambda b, i: (0,) * nd)

    row = lambda w: pl.BlockSpec((1, tm, w), lambda b, i: (b, i, 0))
    vec = pl.BlockSpec((1, 1, D), lambda b, i: (b, 0, 0))
    tab = pl.BlockSpec((tm, LANE), lambda b, i: (i, 0))
    kvspec = pl.BlockSpec((1, G, tm, HEAD_DIM), lambda b, i: (b, 0, i, 0))
    kvshape = jax.ShapeDtypeStruct((B, G, S, HEAD_DIM), BF16)
    return pl.pallas_call(
        _mix_in_kernel,
        grid=(B, S // tm),
        in_specs=[row(D), vec, vec, const(g), const(wq), const(wkv), const(wg), const(wuv),
                  const(wga), const(wgb), tab, tab, const(ln_g), const(ln_b), const(ws),
                  const(bsx), const(proj_b)],
        out_specs=[
            pl.BlockSpec((1, NSA_HEADS, tm, HEAD_DIM), lambda b, i: (b, 0, i, 0)),
            row(LANE), row(LANE), kvspec, kvspec, kvspec, kvspec, row(LANE), row(D), row(D)],
        out_shape=[
            jax.ShapeDtypeStruct((B, NSA_HEADS, S, HEAD_DIM), BF16),
            jax.ShapeDtypeStruct((B, S, LANE), F32), jax.ShapeDtypeStruct((B, S, LANE), F32),
            kvshape, kvshape, kvshape, kvshape,
            jax.ShapeDtypeStruct((B, S, LANE), F32),
            jax.ShapeDtypeStruct((B, S, D), F32), jax.ShapeDtypeStruct((B, S, D), F32)],
        compiler_params=pltpu.CompilerParams(
            dimension_semantics=("parallel", "parallel"), vmem_limit_bytes=VMEM_LIMIT),
        name="mix_in",
    )(h, shift, scale, g, wq, wkv, wg, wuv, wga, wgb, cos_t, sin_t, ln_g, ln_b, ws, bsx, proj_b)


def _compress_kernel(k16_ref, v16_ref, pe_ref, wk_ref, wv_ref, w2k_ref, w2v_ref, cos_ref,
                     sin_ref, kc_ref, vc_ref):
    def hidden(x16, pe_a, pe_b, w_ref):
        ha = jnp.dot((x16 + pe_a).astype(BF16), w_ref[0], preferred_element_type=F32)
        hb = jnp.dot((x16 + pe_b).astype(BF16), w_ref[1], preferred_element_type=F32)
        return _silu(ha + pltpu.roll(hb, hb.shape[0] - 1, 0))

    hk = hidden(k16_ref[0], pe_ref[0], pe_ref[1], wk_ref)
    hv = hidden(v16_ref[0], pe_ref[2], pe_ref[3], wv_ref)
    for g in range(KV_GROUPS):
        k2 = jnp.dot(hk[:, g * CMP_HIDDEN:(g + 1) * CMP_HIDDEN].astype(BF16), w2k_ref[...],
                     preferred_element_type=F32)
        kc = k2[:, :HEAD_DIM] * cos_ref[...] + k2[:, HEAD_DIM:] * sin_ref[...]
        kc_ref[0, g] = kc.astype(BF16)
        vc = jnp.dot(hv[:, g * CMP_HIDDEN:(g + 1) * CMP_HIDDEN].astype(BF16), w2v_ref[...],
                     preferred_element_type=F32)
        vc_ref[0, g] = vc.astype(BF16)


def _compress(k16, v16, pe, wk, wv, w2k, w2v, cos_c, sin_c):
    B, NC, W = k16.shape

    def const(a):
        nd = a.ndim
        return pl.BlockSpec(a.shape, lambda b: (0,) * nd)

    blk = pl.BlockSpec((1, NC, W), lambda b: (b, 0, 0))
    out = pl.BlockSpec((1, KV_GROUPS, NC, HEAD_DIM), lambda b: (b, 0, 0, 0))
    shape = jax.ShapeDtypeStruct((B, KV_GROUPS, NC, HEAD_DIM), BF16)
    return pl.pallas_call(
        _compress_kernel,
        grid=(B,),
        in_specs=[blk, blk, const(pe), const(wk), const(wv), const(w2k), const(w2v),
                  const(cos_c), const(sin_c)],
        out_specs=[out, out],
        out_shape=[shape, shape],
        compiler_params=pltpu.CompilerParams(
            dimension_semantics=("parallel",), vmem_limit_bytes=VMEM_LIMIT),
        name="compress",
    )(k16, v16, pe, wk, wv, w2k, w2v, cos_c, sin_c)


def _attn_kernel(q_ref, ksel_ref, vsel_ref, kwin_ref, vwin_ref, kc_ref, vc_ref, gates_ref,
                 ga_ref, mb_ref, h_ref, gate_ref, ov_ref, e_ref, wb_ref, pa_ref, wo_ref,
                 o_ref, bias_scr, m_scr, l_scr, acc_scr, y_scr):
    TQ = h_ref.shape[1]
    CK = ATT_CHUNK
    R = HPG * TQ
    NC = kc_ref.shape[2]
    NSEL = ksel_ref.shape[2] // SEL_BLOCK
    qi = pl.program_id(1)
    t_col = qi * TQ + lax.broadcasted_iota(jnp.int32, (TQ, 1), 0)

    def flash(qg, k_ref, v_ref, g, lo, hi, bias_fn):
        m_scr[...] = jnp.full_like(m_scr, NEG)
        l_scr[...] = jnp.zeros_like(l_scr)
        acc_scr[...] = jnp.zeros_like(acc_scr)

        def body(c, carry):
            start = pl.multiple_of(c * CK, CK)
            k = k_ref[0, g, pl.ds(start, CK), :]
            v = v_ref[0, g, pl.ds(start, CK), :]
            s = lax.dot_general(qg, k, _NT, preferred_element_type=F32)
            s = (s.reshape(HPG, TQ, CK) + bias_fn(c)[None]).reshape(R, CK)
            m_prev = m_scr[...]
            m_new = jnp.maximum(m_prev, jnp.max(s, axis=-1, keepdims=True))
            alpha = jnp.exp(m_prev - m_new)
            p = jnp.exp(s - m_new)
            l_scr[...] = alpha * l_scr[...] + jnp.sum(p, axis=-1, keepdims=True)
            acc_scr[...] = alpha * acc_scr[...] + jnp.dot(p.astype(BF16), v,
                                                          preferred_element_type=F32)
            m_scr[...] = m_new
            return carry

        lax.fori_loop(lo, hi, body, 0)
        return acc_scr[...] / l_scr[...]

    gts = gates_ref[0]
    for g in range(KV_GROUPS):
        qg = q_ref[0, g * HPG:(g + 1) * HPG].reshape(R, HEAD_DIM)

        sc = lax.dot_general(qg, kc_ref[0, g], _NT, preferred_element_type=F32)
        n_idx = lax.broadcasted_iota(jnp.int32, (TQ, NC), 1)
        vis = (n_idx * CMP_STRIDE + (CMP_BLOCK - 1)) <= t_col
        visf = jnp.where(vis, 1.0, 0.0)
        sc = jnp.where(vis[None], sc.reshape(HPG, TQ, NC), NEG)
        ec = jnp.exp(sc - jnp.max(sc, axis=-1, keepdims=True)) * visf[None]
        den = jnp.sum(ec, axis=-1, keepdims=True)
        pc = ec / jnp.where(den > 0.0, den, 1.0)
        o_cmp = jnp.dot(pc.reshape(R, NC).astype(BF16), vc_ref[0, g],
                        preferred_element_type=F32)

        psum = jnp.sum(pc, axis=0)
        imp = jnp.dot(psum, ov_ref[...], preferred_element_type=F32,
                      precision=lax.Precision.HIGHEST)
        imp_t = jnp.transpose(imp)[:NSEL]
        j_idx = lax.broadcasted_iota(jnp.int32, (NSEL, TQ), 0)
        t_row = qi * TQ + lax.broadcasted_iota(jnp.int32, (NSEL, TQ), 1)
        cur = jnp.right_shift(t_row, SEL_BLOCK.bit_length() - 1)
        forced = (j_idx == 0) | (j_idx == cur) | (j_idx == cur - 1)
        imp_t = jnp.where(forced, FORCE, jnp.where(j_idx <= cur, imp_t, -FORCE))
        rank = jnp.zeros((NSEL, TQ), jnp.int32)
        for kk in range(NSEL):
            rk = imp_t[kk:kk + 1, :]
            ahead = (rk > imp_t) | ((rk == imp_t) & (j_idx > kk))
            rank = rank + jnp.where(ahead, 1, 0)
        sel_t = jnp.where(rank < SEL_TOPK, 1.0, 0.0)
        sel_t = jnp.concatenate([sel_t, jnp.zeros((LANE - NSEL, TQ), F32)], axis=0)
        sel = jnp.transpose(sel_t).astype(BF16)

        def build_bias(c, carry):
            hit = jnp.dot(sel, e_ref[c], preferred_element_type=F32)
            kpos = c * CK + lax.broadcasted_iota(jnp.int32, (TQ, CK), 1)
            ok = (hit > 0.5) & (kpos <= t_col)
            bias_scr[c] = jnp.where(ok, 0.0, NEG)
            return carry

        lax.fori_loop(0, qi + 1, build_bias, 0)

        o_sel = flash(qg, ksel_ref, vsel_ref, g, 0, qi + 1, lambda c: bias_scr[c])
        nwin = WINDOW // CK
        o_win = flash(qg, kwin_ref, vwin_ref, g, jnp.maximum(qi - nwin, 0), qi + 1,
                      lambda c: wb_ref[c - qi + nwin])

        for hh in range(HPG):
            hd = g * HPG + hh
            rows = slice(hh * TQ, (hh + 1) * TQ)
            y = (gts[:, 3 * hd:3 * hd + 1] * o_cmp[rows]
                 + gts[:, 3 * hd + 1:3 * hd + 2] * o_sel[rows]
                 + gts[:, 3 * hd + 2:3 * hd + 3] * o_win[rows])
            y_scr[:, hd * HEAD_DIM:(hd + 1) * HEAD_DIM] = y

    ya = jnp.dot(y_scr[...].astype(BF16), pa_ref[...], preferred_element_type=F32)
    merged = (ga_ref[0] * ya + mb_ref[0]).astype(BF16)
    o_ref[0] = h_ref[0] + gate_ref[0] * jnp.dot(merged, wo_ref[...],
                                                preferred_element_type=F32)


def _attn(q8, ksel, vsel, kwin, vwin, kc, vc, gates, ga, mb, h, gate, ov, e_tab, wbias,
          proj_a, w_out):
    B, S, D = h.shape
    TQ = ATT_ROWS
    CK = ATT_CHUNK
    assert TQ == CK and WINDOW % CK == 0 and S % TQ == 0
    G = KV_GROUPS
    R = HPG * TQ

    def const(a):
        nd = a.ndim
        return pl.BlockSpec(a.shape, lambda b, i: (0,) * nd)

    def per_batch(a):
        nd = a.ndim
        return pl.BlockSpec((1,) + a.shape[1:], lambda b, i: (b,) + (0,) * (nd - 1))

    row = lambda w: pl.BlockSpec((1, TQ, w), lambda b, i: (b, i, 0))
    return pl.pallas_call(
        _attn_kernel,
        grid=(B, S // TQ),
        in_specs=[
            pl.BlockSpec((1, NSA_HEADS, TQ, HEAD_DIM), lambda b, i: (b, 0, i, 0)),
            per_batch(ksel), per_batch(vsel), per_batch(kwin), per_batch(vwin),
            per_batch(kc), per_batch(vc),
            row(LANE), row(D), row(D), row(D),
            pl.BlockSpec((1, 1, D), lambda b, i: (b, 0, 0)),
            const(ov), const(e_tab), const(wbias), const(proj_a), const(w_out)],
        out_specs=row(D),
        out_shape=jax.ShapeDtypeStruct((B, S, D), F32),
        scratch_shapes=[
            pltpu.VMEM((S // CK, TQ, CK), F32),
            pltpu.VMEM((R, 1), F32), pltpu.VMEM((R, 1), F32),
            pltpu.VMEM((R, HEAD_DIM), F32),
            pltpu.VMEM((TQ, NSA_WIDTH), F32)],
        compiler_params=pltpu.CompilerParams(
            dimension_semantics=("parallel", "parallel"), vmem_limit_bytes=VMEM_LIMIT),
        name="attn",
    )(q8, ksel, vsel, kwin, vwin, kc, vc, gates, ga, mb, h, gate, ov, e_tab, wbias,
      proj_a, w_out)


def _final_kernel(h_ref, g_ref, o_ref):
    x = h_ref[0]
    o_ref[0] = x * lax.rsqrt(jnp.mean(x * x, axis=-1, keepdims=True) + EPS) * g_ref[...]


def _final(h, g):
    B, S, D = h.shape
    tm = FFN_ROWS
    row = pl.BlockSpec((1, tm, D), lambda b, i: (b, i, 0))
    return pl.pallas_call(
        _final_kernel,
        grid=(B, S // tm),
        in_specs=[row, pl.BlockSpec((1, D), lambda b, i: (0, 0))],
        out_specs=row,
        out_shape=jax.ShapeDtypeStruct((B, S, D), F32),
        compiler_params=pltpu.CompilerParams(
            dimension_semantics=("parallel", "parallel"), vmem_limit_bytes=VMEM_LIMIT),
        name="final_norm",
    )(h, g)


def _rope_tables(pos):
    inv = 1.0 / (ROPE_THETA ** (np.arange(0, HEAD_DIM, 2, dtype=np.float64) / HEAD_DIM))
    ang = np.asarray(pos, np.float64)[:, None] * inv[None, :]
    cos = np.concatenate([np.cos(ang), np.cos(ang)], axis=1)
    sin = np.concatenate([-np.sin(ang), np.sin(ang)], axis=1)
    return cos.astype(np.float32), sin.astype(np.float32)


def _tables(S):
    cos, sin = _rope_tables(np.arange(S))
    cos_t = np.concatenate([cos, cos], axis=1)
    sin_t = np.concatenate([sin, sin], axis=1)
    n_cmp_pad = S // CMP_STRIDE
    starts = np.arange(n_cmp_pad) * CMP_STRIDE
    cos_c, sin_c = _rope_tables(starts + CMP_BLOCK - 1)
    n_sel = S // SEL_BLOCK
    sel_start = np.arange(n_sel) * SEL_BLOCK
    overlap = np.clip(np.minimum(starts[:, None] + CMP_BLOCK, sel_start[None, :] + SEL_BLOCK)
                      - np.maximum(starts[:, None], sel_start[None, :]), 0, None) / CMP_BLOCK
    ov = np.zeros((n_cmp_pad, LANE), np.float32)
    ov[:, :n_sel] = overlap
    CK = ATT_CHUNK
    kpos = np.arange(S)
    e_full = (kpos[None, :] // SEL_BLOCK == np.arange(LANE)[:, None]).astype(np.float32)
    e_tab = e_full.reshape(LANE, S // CK, CK).transpose(1, 0, 2)
    nwin = WINDOW // CK
    i = np.arange(ATT_ROWS)[:, None]
    j = np.arange(CK)[None, :]
    wbias = np.stack([np.where((0 <= (nwin - r) * CK + i - j) & ((nwin - r) * CK + i - j < WINDOW),
                               0.0, NEG) for r in range(nwin + 1)]).astype(np.float32)
    return (jnp.asarray(cos_t), jnp.asarray(sin_t), jnp.asarray(cos_c), jnp.asarray(sin_c),
            jnp.asarray(ov), jnp.asarray(e_tab, dtype=BF16), jnp.asarray(wbias))


def _compress_weights(w1):
    eye = jnp.eye(KV_GROUPS, dtype=w1.dtype)
    half = CMP_BLOCK // 2

    def one(w):
        return jnp.einsum('ldf,gh->lgdhf', w, eye).reshape(half * KV_GROUPS * HEAD_DIM,
                                                           KV_GROUPS * CMP_HIDDEN)

    return jnp.stack([one(w1[:half]), one(w1[half:])]).astype(BF16)


def _compress_pe(pe):
    half = CMP_BLOCK // 2

    def one(p):
        return jnp.broadcast_to(p[:, None, :], (half, KV_GROUPS, HEAD_DIM)).reshape(1, -1)

    return one(pe[:half]), one(pe[half:])


def kernel(x, c, ada_w, ada_b, norm_g, ffn_w_in, ffn_w_out, mix_w_in, cmp_pe, cmp_w1, cmp_w2,
           gm_ln_g, gm_ln_b, gm_ws, gm_bs, proj_a, proj_b, w_out, final_g):
    B, S, D = x.shape
    L = ada_w.shape[0]
    cos_t, sin_t, cos_c, sin_c, ov, e_tab, wbias = _tables(S)
    swap = np.concatenate([np.arange(HEAD_DIM // 2, HEAD_DIM), np.arange(HEAD_DIM // 2)])

    mods = _ada(c, ada_w, ada_b).reshape(L, B, 3, 3, 1, D)
    h = x
    for l in range(L):
        mod = mods[l]
        h = _ffn(h, mod[:, 0, 0], mod[:, 0, 1], mod[:, 0, 2], norm_g[l, 0][None],
                 ffn_w_in[l, 0].astype(BF16), ffn_w_out[l, 0].astype(BF16))

        w = mix_w_in[l]
        o_kv = NSA_WIDTH
        o_g = o_kv + 6 * KV_GROUPS * HEAD_DIM
        o_uv = o_g + 3 * NSA_HEADS
        o_ga = o_uv + 2 * GM_WIDTH
        o_gb = o_ga + D
        wq = (w[:, :o_kv] * HEAD_DIM ** -0.5).astype(BF16)
        wkv = w[:, o_kv:o_g].astype(BF16)
        wg = jnp.pad(w[:, o_g:o_uv], ((0, 0), (0, LANE - 3 * NSA_HEADS))).astype(BF16)
        wuv = w[:, o_uv:o_ga].astype(BF16)
        wga = w[:, o_ga:o_gb].astype(BF16)
        wgb = w[:, o_gb:].astype(BF16)
        bsx = jnp.repeat(gm_bs[l].T, HEAD_DIM, axis=1)
        q8, kcmp, vcmp, ksel, vsel, kwin, vwin, gates, ga, mb = _mix_in(
            h, mod[:, 1, 0], mod[:, 1, 1], norm_g[l, 1][None], wq, wkv, wg, wuv, wga, wgb,
            cos_t, sin_t, gm_ln_g[l][None], gm_ln_b[l][None], gm_ws[l], bsx,
            proj_b[l].astype(BF16))

        pe_ka, pe_kb = _compress_pe(cmp_pe[l, 0])
        pe_va, pe_vb = _compress_pe(cmp_pe[l, 1])
        pe = jnp.stack([pe_ka, pe_kb, pe_va, pe_vb])
        w2k = jnp.concatenate([cmp_w2[l, 0], cmp_w2[l, 0][:, swap]], axis=1).astype(BF16)
        nc = S // CMP_STRIDE
        kc, vc = _compress(kcmp.reshape(B, nc, CMP_STRIDE * LANE),
                           vcmp.reshape(B, nc, CMP_STRIDE * LANE), pe,
                           _compress_weights(cmp_w1[l, 0]), _compress_weights(cmp_w1[l, 1]),
                           w2k, cmp_w2[l, 1].astype(BF16), cos_c, sin_c)

        h = _attn(q8, ksel, vsel, kwin, vwin, kc, vc, gates, ga, mb, h, mod[:, 1, 2], ov,
                  e_tab, wbias, proj_a[l].astype(BF16), w_out[l].astype(BF16))

        h = _ffn(h, mod[:, 2, 0], mod[:, 2, 1], mod[:, 2, 2], norm_g[l, 2][None],
                 ffn_w_in[l, 1].astype(BF16), ffn_w_out[l, 1].astype(BF16))
    return _final(h, final_g[None])
```

```python
import numpy as np
import jax
import jax.numpy as jnp
from jax import lax
from jax.experimental import pallas as pl
from jax.experimental.pallas import tpu as pltpu

F32 = jnp.float32
BF16 = jnp.bfloat16

HEAD_DIM = 64
NSA_HEADS = 8
KV_GROUPS = 2
HPG = NSA_HEADS // KV_GROUPS
NSA_WIDTH = NSA_HEADS * HEAD_DIM
CMP_BLOCK = 32
CMP_STRIDE = 16
CMP_HIDDEN = 128
SEL_BLOCK = 64
SEL_TOPK = 16
WINDOW = 512
GM_GROUPS = 8
GM_CHUNK = 128
GM_WIDTH = GM_GROUPS * HEAD_DIM
ROPE_THETA = 10000.0
EPS = 1e-6
NEG = -1e30
FORCE = 1e4
LANE = 128
GELU_C = float(np.sqrt(2.0 / np.pi))
LOG2E = float(np.log2(np.e))
GATE_ROWS = 32

FFN_ROWS = 512
MIX_ROWS = 512
ATT_ROWS = 256
ATT_CHUNK = 256
VMEM_LIMIT = 56 * 1024 * 1024


def _sigmoid(x):
    return 1.0 / (1.0 + jnp.exp(-x))


def _silu(x):
    return x * _sigmoid(x)


def _rms_mod(x, g, shift, scale):
    y = x * lax.rsqrt(jnp.mean(x * x, axis=-1, keepdims=True) + EPS) * g
    return y * (1.0 + scale) + shift


def _ada_kernel(c_ref, w_ref, b_ref, o_ref):
    s = _silu(c_ref[...])
    o_ref[0] = jnp.dot(s, w_ref[0], preferred_element_type=F32,
                       precision=lax.Precision.HIGHEST) + b_ref[0]


def _ada(c, ada_w, ada_b):
    L, D, N = ada_w.shape
    B = c.shape[0]
    tn = 1024
    return pl.pallas_call(
        _ada_kernel,
        grid=(L, N // tn),
        in_specs=[
            pl.BlockSpec((B, D), lambda l, j: (0, 0)),
            pl.BlockSpec((1, D, tn), lambda l, j: (l, 0, j)),
            pl.BlockSpec((1, 1, tn), lambda l, j: (l, 0, j)),
        ],
        out_specs=pl.BlockSpec((1, B, tn), lambda l, j: (l, 0, j)),
        out_shape=jax.ShapeDtypeStruct((L, B, N), F32),
        compiler_params=pltpu.CompilerParams(
            dimension_semantics=("parallel", "parallel"), vmem_limit_bytes=VMEM_LIMIT),
        name="ada",
    )(c, ada_w, ada_b.reshape(L, 1, N))


def _ffn_kernel(h_ref, shift_ref, scale_ref, gate_ref, g_ref, wa_ref, wb_ref, wo_ref,
                o_ref, n_scr, acc_scr):
    j = pl.program_id(2)

    @pl.when(j == 0)
    def _():
        n = _rms_mod(h_ref[0], g_ref[...], shift_ref[0], scale_ref[0])
        n_scr[...] = n.astype(BF16)
        acc_scr[...] = jnp.zeros_like(acc_scr)

    n = n_scr[...]
    a = jnp.dot(n, wa_ref[...], preferred_element_type=F32)
    b = jnp.dot(n, wb_ref[...], preferred_element_type=F32)
    hm = (_silu(a) * b).astype(BF16)
    acc_scr[...] += jnp.dot(hm, wo_ref[...], preferred_element_type=F32)

    @pl.when(j == pl.num_programs(2) - 1)
    def _():
        o_ref[0] = h_ref[0] + (0.5 * gate_ref[0]) * acc_scr[...]


def _ffn(h, shift, scale, gate, g, w_in, w_out):
    B, S, D = h.shape
    F = w_out.shape[0]
    tm = FFN_ROWS
    nf = 2
    tf = F // nf
    row = pl.BlockSpec((1, tm, D), lambda b, i, j: (b, i, 0))
    vec = pl.BlockSpec((1, 1, D), lambda b, i, j: (b, 0, 0))
    return pl.pallas_call(
        _ffn_kernel,
        grid=(B, S // tm, nf),
        in_specs=[
            row, vec, vec, vec,
            pl.BlockSpec((1, D), lambda b, i, j: (0, 0)),
            pl.BlockSpec((D, tf), lambda b, i, j: (0, j)),
            pl.BlockSpec((D, tf), lambda b, i, j: (0, j + nf)),
            pl.BlockSpec((tf, D), lambda b, i, j: (j, 0)),
        ],
        out_specs=row,
        out_shape=jax.ShapeDtypeStruct((B, S, D), F32),
        scratch_shapes=[pltpu.VMEM((tm, D), BF16), pltpu.VMEM((tm, D), F32)],
        compiler_params=pltpu.CompilerParams(
            dimension_semantics=("parallel", "parallel", "arbitrary"),
            vmem_limit_bytes=VMEM_LIMIT),
        name="ffn",
    )(h, shift, scale, gate, g, w_in, w_in, w_out)


def _rope_pair(x, c, s):
    lane = lax.broadcasted_iota(jnp.int32, x.shape, 1)
    first_half = (lane & (HEAD_DIM - 1)) < HEAD_DIM // 2
    swapped = jnp.where(first_half, pltpu.roll(x, LANE - HEAD_DIM // 2, 1),
                        pltpu.roll(x, HEAD_DIM // 2, 1))
    return x * c + swapped * s


def _mix_in_kernel(h_ref, shift_ref, scale_ref, g_ref, wq_ref, wkv_ref, wg_ref, wuv_ref,
                   wga_ref, wgb_ref, cos_ref, sin_ref, oh_ref, lng_ref, lnb_ref, ws_ref,
                   bsx_ref, pb_ref,
                   qt_ref, kcmp_ref, vcmp_ref, kselx_ref, vselt_ref, kwinx_ref, vwint_ref,
                   gt_ref, ga_ref, mb_ref):
    tm = h_ref.shape[1]
    CK = vselt_ref.shape[4]
    n = _rms_mod(h_ref[0], g_ref[...], shift_ref[0], scale_ref[0]).astype(BF16)
    cos = cos_ref[...]
    sin = sin_ref[...]
    low_lanes = lax.broadcasted_iota(jnp.int32, (tm, LANE), 1) < HEAD_DIM

    q = jnp.dot(n, wq_ref[...], preferred_element_type=F32)
    for p in range(NSA_HEADS // 2):
        qp = jnp.transpose(_rope_pair(q[:, p * LANE:(p + 1) * LANE], cos, sin)).astype(BF16)
        qt_ref[0, 2 * p] = qp[:HEAD_DIM]
        qt_ref[0, 2 * p + 1] = qp[HEAD_DIM:]

    kv = jnp.dot(n, wkv_ref[...], preferred_element_type=F32)
    kcmp_ref[0] = kv[:, 0 * LANE:1 * LANE]
    vcmp_ref[0] = kv[:, 1 * LANE:2 * LANE]
    for idx, ref, ext in ((2, kselx_ref, oh_ref[...]), (4, kwinx_ref, 0.0)):
        k2 = _rope_pair(kv[:, idx * LANE:(idx + 1) * LANE], cos, sin)
        ref[0, 0] = jnp.where(low_lanes, k2, ext).astype(BF16)
        ref[0, 1] = jnp.where(low_lanes, pltpu.roll(k2, HEAD_DIM, 1), ext).astype(BF16)
    for idx, ref in ((3, vselt_ref), (5, vwint_ref)):
        vt = jnp.transpose(kv[:, idx * LANE:(idx + 1) * LANE]).astype(BF16)
        for g in range(KV_GROUPS):
            for r in range(tm // CK):
                ref[0, g, r] = vt[g * HEAD_DIM:(g + 1) * HEAD_DIM, r * CK:(r + 1) * CK]

    gates = _sigmoid(jnp.dot(n, wg_ref[...], preferred_element_type=F32))
    gt_ref[0] = jnp.transpose(gates)[:GATE_ROWS]
    ga_ref[0] = _sigmoid(jnp.dot(n, wga_ref[...], preferred_element_type=F32))

    uv = jnp.dot(n, wuv_ref[...], preferred_element_type=F32)
    ge = uv * (0.5 * (1.0 + jnp.tanh(GELU_C * (uv + 0.044715 * (uv * uv * uv)))))
    u = ge[:, :GM_WIDTH]
    v = ge[:, GM_WIDTH:]
    mu = jnp.mean(v, axis=-1, keepdims=True)
    var = jnp.mean(jnp.square(v - mu), axis=-1, keepdims=True)
    vln = ((v - mu) * lax.rsqrt(var + EPS) * lng_ref[...] + lnb_ref[...]).astype(BF16)

    ti = lax.broadcasted_iota(jnp.int32, (GM_CHUNK, GM_CHUNK), 0)
    si = lax.broadcasted_iota(jnp.int32, (GM_CHUNK, GM_CHUNK), 1)
    tril = si <= ti
    wm = [jnp.where(tril, ws_ref[gg], 0.0).astype(BF16) for gg in range(GM_GROUPS)]
    low = lax.broadcasted_iota(jnp.int32, (GM_CHUNK, LANE), 1) < HEAD_DIM
    bsx = bsx_ref[...]
    yb_rows = []
    for r in range(tm // GM_CHUNK):
        vch = vln[r * GM_CHUNK:(r + 1) * GM_CHUNK]
        pieces = []
        for p in range(GM_GROUPS // 2):
            vp = vch[:, p * LANE:(p + 1) * LANE]
            a0 = jnp.dot(wm[2 * p], vp, preferred_element_type=F32)
            a1 = jnp.dot(wm[2 * p + 1], vp, preferred_element_type=F32)
            pieces.append(jnp.where(low, a0, a1))
        sv = jnp.concatenate(pieces, axis=1) + bsx
        yb_rows.append(u[r * GM_CHUNK:(r + 1) * GM_CHUNK] * sv)
    yb = jnp.concatenate(yb_rows, axis=0).astype(BF16)
    gb = _sigmoid(jnp.dot(n, wgb_ref[...], preferred_element_type=F32))
    mb_ref[0] = gb * jnp.dot(yb, pb_ref[...], preferred_element_type=F32)


def _mix_in(h, shift, scale, g, wq, wkv, wg, wuv, wga, wgb, cos_t, sin_t, oh_t, ln_g, ln_b, ws,
            bsx, proj_b):
    B, S, D = h.shape
    tm = MIX_ROWS
    G = KV_GROUPS
    CK = ATT_CHUNK

    def const(a):
        nd = a.ndim
        return pl.BlockSpec(a.shape, lambda b, i: (0,) * nd)

    row = lambda w: pl.BlockSpec((1, tm, w), lambda b, i: (b, i, 0))
    vec = pl.BlockSpec((1, 1, D), lambda b, i: (b, 0, 0))
    tab = pl.BlockSpec((tm, LANE), lambda b, i: (i, 0))
    kspec = pl.BlockSpec((1, G, tm, LANE), lambda b, i: (b, 0, i, 0))
    kshape = jax.ShapeDtypeStruct((B, G, S, LANE), BF16)
    vspec = pl.BlockSpec((1, G, tm // CK, HEAD_DIM, CK), lambda b, i: (b, 0, i, 0, 0))
    vshape = jax.ShapeDtypeStruct((B, G, S // CK, HEAD_DIM, CK), BF16)
    return pl.pallas_call(
        _mix_in_kernel,
        grid=(B, S // tm),
        in_specs=[row(D), vec, vec, const(g), const(wq), const(wkv), const(wg), const(wuv),
                  const(wga), const(wgb), tab, tab, tab, const(ln_g), const(ln_b), const(ws),
                  const(bsx), const(proj_b)],
        out_specs=[
            pl.BlockSpec((1, NSA_HEADS, HEAD_DIM, tm), lambda b, i: (b, 0, 0, i)),
            row(LANE), row(LANE), kspec, vspec, kspec, vspec,
            pl.BlockSpec((1, GATE_ROWS, tm), lambda b, i: (b, 0, i)),
            row(D), row(D)],
        out_shape=[
            jax.ShapeDtypeStruct((B, NSA_HEADS, HEAD_DIM, S), BF16),
            jax.ShapeDtypeStruct((B, S, LANE), F32), jax.ShapeDtypeStruct((B, S, LANE), F32),
            kshape, vshape, kshape, vshape,
            jax.ShapeDtypeStruct((B, GATE_ROWS, S), F32),
            jax.ShapeDtypeStruct((B, S, D), F32), jax.ShapeDtypeStruct((B, S, D), F32)],
        compiler_params=pltpu.CompilerParams(
            dimension_semantics=("parallel", "parallel"), vmem_limit_bytes=VMEM_LIMIT),
        name="mix_in",
    )(h, shift, scale, g, wq, wkv, wg, wuv, wga, wgb, cos_t, sin_t, oh_t, ln_g, ln_b, ws, bsx,
      proj_b)


def _compress_kernel(k16_ref, v16_ref, pe_ref, wk_ref, wv_ref, w2k_ref, w2v_ref, cos_ref,
                     sin_ref, kc_ref, vct_ref):
    def hidden(x16, pe_a, pe_b, w_ref):
        ha = jnp.dot((x16 + pe_a).astype(BF16), w_ref[0], preferred_element_type=F32)
        hb = jnp.dot((x16 + pe_b).astype(BF16), w_ref[1], preferred_element_type=F32)
        return _silu(ha + pltpu.roll(hb, hb.shape[0] - 1, 0))

    hk = hidden(k16_ref[0], pe_ref[0], pe_ref[1], wk_ref)
    hv = hidden(v16_ref[0], pe_ref[2], pe_ref[3], wv_ref)
    for g in range(KV_GROUPS):
        k2 = jnp.dot(hk[:, g * CMP_HIDDEN:(g + 1) * CMP_HIDDEN].astype(BF16), w2k_ref[...],
                     preferred_element_type=F32)
        kc = k2[:, :HEAD_DIM] * cos_ref[...] + k2[:, HEAD_DIM:] * sin_ref[...]
        kc_ref[0, g] = kc.astype(BF16)
        vc = jnp.dot(hv[:, g * CMP_HIDDEN:(g + 1) * CMP_HIDDEN].astype(BF16), w2v_ref[...],
                     preferred_element_type=F32)
        vct_ref[0, g] = jnp.transpose(vc)[:HEAD_DIM].astype(BF16)


def _compress(k16, v16, pe, wk, wv, w2k, w2v, cos_c, sin_c):
    B, NC, W = k16.shape

    def const(a):
        nd = a.ndim
        return pl.BlockSpec(a.shape, lambda b: (0,) * nd)

    blk = pl.BlockSpec((1, NC, W), lambda b: (b, 0, 0))
    return pl.pallas_call(
        _compress_kernel,
        grid=(B,),
        in_specs=[blk, blk, const(pe), const(wk), const(wv), const(w2k), const(w2v),
                  const(cos_c), const(sin_c)],
        out_specs=[pl.BlockSpec((1, KV_GROUPS, NC, HEAD_DIM), lambda b: (b, 0, 0, 0)),
                   pl.BlockSpec((1, KV_GROUPS, HEAD_DIM, NC), lambda b: (b, 0, 0, 0))],
        out_shape=[jax.ShapeDtypeStruct((B, KV_GROUPS, NC, HEAD_DIM), BF16),
                   jax.ShapeDtypeStruct((B, KV_GROUPS, HEAD_DIM, NC), BF16)],
        compiler_params=pltpu.CompilerParams(
            dimension_semantics=("parallel",), vmem_limit_bytes=VMEM_LIMIT),
        name="compress",
    )(k16, v16, pe, wk, wv, w2k, w2v, cos_c, sin_c)


def _attn_kernel(qt_ref, kselx_ref, vselt_ref, kwinx_ref, vwint_ref, kc_ref, vct_ref, gt_ref,
                 ga_ref, mb_ref, h_ref, gate_ref, ovt_ref, bdiag_ref, bfar_ref, pa_ref, wo_ref,
                 o_ref, qx_scr, m_scr, l_scr, acc_scr, yt_scr):
    TQ = h_ref.shape[1]
    CK = ATT_CHUNK
    NC = kc_ref.shape[2]
    NSEL = kselx_ref.shape[2] // SEL_BLOCK
    nwin = WINDOW // CK
    qi = pl.program_id(1)

    def reset():
        m_scr[...] = jnp.full_like(m_scr, NEG)
        l_scr[...] = jnp.zeros_like(l_scr)
        acc_scr[...] = jnp.zeros_like(acc_scr)

    def chunk(kx_ref, vt_ref, g, c, bias):
        kx = kx_ref[0, g, pl.ds(pl.multiple_of(c * CK, CK), CK), :]
        vt = vt_ref[0, g, c]
        for hh in range(HPG):
            s = jnp.dot(kx, qx_scr[hh], preferred_element_type=F32)
            if bias is not None:
                s = s + bias
            m_prev = m_scr[hh]
            m_new = jnp.maximum(m_prev, jnp.max(s, axis=0, keepdims=True))
            alpha = jnp.exp2(m_prev - m_new)
            p = jnp.exp2(s - m_new)
            l_scr[hh] = alpha * l_scr[hh] + jnp.sum(p, axis=0, keepdims=True)
            acc_scr[hh] = alpha * acc_scr[hh] + jnp.dot(vt, p.astype(BF16),
                                                        preferred_element_type=F32)
            m_scr[hh] = m_new

    def add_branch(g, gate_row):
        for hh in range(HPG):
            hd = g * HPG + hh
            w = gt_ref[0, 3 * hd + gate_row:3 * hd + gate_row + 1, :] / l_scr[hh]
            rows = slice(hd * HEAD_DIM, (hd + 1) * HEAD_DIM)
            yt_scr[rows, :] = yt_scr[rows, :] + w * acc_scr[hh]

    n_idx = lax.broadcasted_iota(jnp.int32, (NC, TQ), 0)
    t_cmp = qi * TQ + lax.broadcasted_iota(jnp.int32, (NC, TQ), 1)
    vis = (n_idx * CMP_STRIDE + (CMP_BLOCK - 1)) <= t_cmp
    visf = jnp.where(vis, 1.0, 0.0)
    j_idx = lax.broadcasted_iota(jnp.int32, (NSEL, TQ), 0)
    t_sel = qi * TQ + lax.broadcasted_iota(jnp.int32, (NSEL, TQ), 1)
    cur = jnp.right_shift(t_sel, SEL_BLOCK.bit_length() - 1)
    forced = (j_idx == 0) | (j_idx == cur) | (j_idx == cur - 1)
    valid = j_idx <= cur

    for g in range(KV_GROUPS):
        psum = jnp.zeros((NC, TQ), F32)
        for hh in range(HPG):
            hd = g * HPG + hh
            qt = qt_ref[0, hd]
            sc = jnp.where(vis, jnp.dot(kc_ref[0, g], qt, preferred_element_type=F32), NEG)
            ec = jnp.exp2(sc - jnp.max(sc, axis=0, keepdims=True)) * visf
            den = jnp.sum(ec, axis=0, keepdims=True)
            pc = ec / jnp.where(den > 0.0, den, 1.0)
            psum = psum + pc
            oc = jnp.dot(vct_ref[0, g], pc.astype(BF16), preferred_element_type=F32)
            yt_scr[hd * HEAD_DIM:(hd + 1) * HEAD_DIM, :] = gt_ref[0, 3 * hd:3 * hd + 1, :] * oc

        imp = jnp.dot(ovt_ref[...], psum, preferred_element_type=F32,
                      precision=lax.Precision.HIGHEST)
        imp = jnp.where(forced, FORCE, jnp.where(valid, imp, -FORCE))
        rank = jnp.zeros((NSEL, TQ), jnp.int32)
        for kk in range(NSEL):
            rk = imp[kk:kk + 1, :]
            ahead = (rk > imp) | ((rk == imp) & (j_idx > kk))
            rank = rank + jnp.where(ahead, 1, 0)
        selneg = jnp.where(rank < SEL_TOPK, 0.0, NEG).astype(BF16)
        pad = jnp.zeros((LANE - HEAD_DIM - NSEL, TQ), BF16)
        for hh in range(HPG):
            qx_scr[hh] = jnp.concatenate([qt_ref[0, g * HPG + hh], selneg, pad], axis=0)

        reset()

        def sel_body(c, carry):
            chunk(kselx_ref, vselt_ref, g, c, None)
            return carry

        def sel_diag(c, carry):
            chunk(kselx_ref, vselt_ref, g, c, bdiag_ref[...])
            return carry

        lax.fori_loop(0, qi, sel_body, 0)
        lax.fori_loop(qi, qi + 1, sel_diag, 0)
        add_branch(g, 1)

        reset()

        def far_body(c, carry):
            chunk(kwinx_ref, vwint_ref, g, c, bfar_ref[...])
            return carry

        far = qi - nwin
        lax.fori_loop(far, jnp.where(far >= 0, far + 1, far), far_body, 0)

        def win_body(c, carry):
            chunk(kwinx_ref, vwint_ref, g, c, None)
            return carry

        def win_diag(c, carry):
            chunk(kwinx_ref, vwint_ref, g, c, bdiag_ref[...])
            return carry

        lax.fori_loop(jnp.maximum(qi - nwin + 1, 0), qi, win_body, 0)
        lax.fori_loop(qi, qi + 1, win_diag, 0)
        add_branch(g, 2)

    y = jnp.transpose(yt_scr[...]).astype(BF16)
    ya = jnp.dot(y, pa_ref[...], preferred_element_type=F32)
    merged = (ga_ref[0] * ya + mb_ref[0]).astype(BF16)
    o_ref[0] = h_ref[0] + gate_ref[0] * jnp.dot(merged, wo_ref[...],
                                                preferred_element_type=F32)


def _attn(qt, kselx, vselt, kwinx, vwint, kc, vct, gt, ga, mb, h, gate, ovt, bdiag, bfar,
          proj_a, w_out):
    B, S, D = h.shape
    TQ = ATT_ROWS
    CK = ATT_CHUNK
    assert TQ == CK and WINDOW % CK == 0 and S % TQ == 0

    def const(a):
        nd = a.ndim
        return pl.BlockSpec(a.shape, lambda b, i: (0,) * nd)

    def per_batch(a):
        nd = a.ndim
        return pl.BlockSpec((1,) + a.shape[1:], lambda b, i: (b,) + (0,) * (nd - 1))

    row = lambda w: pl.BlockSpec((1, TQ, w), lambda b, i: (b, i, 0))
    return pl.pallas_call(
        _attn_kernel,
        grid=(B, S // TQ),
        in_specs=[
            pl.BlockSpec((1, NSA_HEADS, HEAD_DIM, TQ), lambda b, i: (b, 0, 0, i)),
            per_batch(kselx), per_batch(vselt), per_batch(kwinx), per_batch(vwint),
            per_batch(kc), per_batch(vct),
            pl.BlockSpec((1, GATE_ROWS, TQ), lambda b, i: (b, 0, i)),
            row(D), row(D), row(D),
            pl.BlockSpec((1, 1, D), lambda b, i: (b, 0, 0)),
            const(ovt), const(bdiag), const(bfar), const(proj_a), const(w_out)],
        out_specs=row(D),
        out_shape=jax.ShapeDtypeStruct((B, S, D), F32),
        scratch_shapes=[
            pltpu.VMEM((HPG, LANE, TQ), BF16),
            pltpu.VMEM((HPG, 1, TQ), F32), pltpu.VMEM((HPG, 1, TQ), F32),
            pltpu.VMEM((HPG, HEAD_DIM, TQ), F32),
            pltpu.VMEM((NSA_WIDTH, TQ), F32)],
        compiler_params=pltpu.CompilerParams(
            dimension_semantics=("parallel", "parallel"), vmem_limit_bytes=VMEM_LIMIT),
        name="attn",
    )(qt, kselx, vselt, kwinx, vwint, kc, vct, gt, ga, mb, h, gate, ovt, bdiag, bfar,
      proj_a, w_out)


def _final_kernel(h_ref, g_ref, o_ref):
    x = h_ref[0]
    o_ref[0] = x * lax.rsqrt(jnp.mean(x * x, axis=-1, keepdims=True) + EPS) * g_ref[...]


def _final(h, g):
    B, S, D = h.shape
    tm = FFN_ROWS
    row = pl.BlockSpec((1, tm, D), lambda b, i: (b, i, 0))
    return pl.pallas_call(
        _final_kernel,
        grid=(B, S // tm),
        in_specs=[row, pl.BlockSpec((1, D), lambda b, i: (0, 0))],
        out_specs=row,
        out_shape=jax.ShapeDtypeStruct((B, S, D), F32),
        compiler_params=pltpu.CompilerParams(
            dimension_semantics=("parallel", "parallel"), vmem_limit_bytes=VMEM_LIMIT),
        name="final_norm",
    )(h, g)


def _rope_tables(pos):
    inv = 1.0 / (ROPE_THETA ** (np.arange(0, HEAD_DIM, 2, dtype=np.float64) / HEAD_DIM))
    ang = np.asarray(pos, np.float64)[:, None] * inv[None, :]
    cos = np.concatenate([np.cos(ang), np.cos(ang)], axis=1)
    sin = np.concatenate([-np.sin(ang), np.sin(ang)], axis=1)
    return cos.astype(np.float32), sin.astype(np.float32)


def _tables(S):
    cos, sin = _rope_tables(np.arange(S))
    cos_t = np.concatenate([cos, cos], axis=1)
    sin_t = np.concatenate([sin, sin], axis=1)
    n_cmp_pad = S // CMP_STRIDE
    starts = np.arange(n_cmp_pad) * CMP_STRIDE
    cos_c, sin_c = _rope_tables(starts + CMP_BLOCK - 1)
    n_sel = S // SEL_BLOCK
    sel_start = np.arange(n_sel) * SEL_BLOCK
    overlap = np.clip(np.minimum(starts[:, None] + CMP_BLOCK, sel_start[None, :] + SEL_BLOCK)
                      - np.maximum(starts[:, None], sel_start[None, :]), 0, None) / CMP_BLOCK
    ovt = np.ascontiguousarray(overlap.T).astype(np.float32)
    oh = np.zeros((S, LANE), np.float32)
    oh[np.arange(S), HEAD_DIM + np.arange(S) // SEL_BLOCK] = 1.0
    j = np.arange(ATT_CHUNK)[:, None]
    i = np.arange(ATT_ROWS)[None, :]
    bdiag = np.where(j <= i, 0.0, NEG).astype(np.float32)
    bfar = np.where(j > i, 0.0, NEG).astype(np.float32)
    return tuple(jnp.asarray(a) for a in (cos_t, sin_t, cos_c, sin_c, ovt, oh, bdiag, bfar))


def _compress_weights(w1):
    eye = jnp.eye(KV_GROUPS, dtype=w1.dtype)
    half = CMP_BLOCK // 2

    def one(w):
        return jnp.einsum('ldf,gh->lgdhf', w, eye).reshape(half * KV_GROUPS * HEAD_DIM,
                                                           KV_GROUPS * CMP_HIDDEN)

    return jnp.stack([one(w1[:half]), one(w1[half:])]).astype(BF16)


def _compress_pe(pe):
    half = CMP_BLOCK // 2

    def one(p):
        return jnp.broadcast_to(p[:, None, :], (half, KV_GROUPS, HEAD_DIM)).reshape(1, -1)

    return one(pe[:half]), one(pe[half:])


def kernel(x, c, ada_w, ada_b, norm_g, ffn_w_in, ffn_w_out, mix_w_in, cmp_pe, cmp_w1, cmp_w2,
           gm_ln_g, gm_ln_b, gm_ws, gm_bs, proj_a, proj_b, w_out, final_g):
    B, S, D = x.shape
    L = ada_w.shape[0]
    assert S // SEL_BLOCK + HEAD_DIM <= LANE and 3 * NSA_HEADS <= GATE_ROWS
    cos_t, sin_t, cos_c, sin_c, ovt, oh_t, bdiag, bfar = _tables(S)
    swap = np.concatenate([np.arange(HEAD_DIM // 2, HEAD_DIM), np.arange(HEAD_DIM // 2)])

    mods = _ada(c, ada_w, ada_b).reshape(L, B, 3, 3, 1, D)
    h = x
    for l in range(L):
        mod = mods[l]
        h = _ffn(h, mod[:, 0, 0], mod[:, 0, 1], mod[:, 0, 2], norm_g[l, 0][None],
                 ffn_w_in[l, 0].astype(BF16), ffn_w_out[l, 0].astype(BF16))

        w = mix_w_in[l]
        o_kv = NSA_WIDTH
        o_g = o_kv + 6 * KV_GROUPS * HEAD_DIM
        o_uv = o_g + 3 * NSA_HEADS
        o_ga = o_uv + 2 * GM_WIDTH
        o_gb = o_ga + D
        wq = (w[:, :o_kv] * (HEAD_DIM ** -0.5 * LOG2E)).astype(BF16)
        wkv = w[:, o_kv:o_g].astype(BF16)
        wg = jnp.pad(w[:, o_g:o_uv], ((0, 0), (0, LANE - 3 * NSA_HEADS))).astype(BF16)
        wuv = w[:, o_uv:o_ga].astype(BF16)
        wga = w[:, o_ga:o_gb].astype(BF16)
        wgb = w[:, o_gb:].astype(BF16)
        bsx = jnp.repeat(gm_bs[l].T, HEAD_DIM, axis=1)
        qt, kcmp, vcmp, kselx, vselt, kwinx, vwint, gt, ga, mb = _mix_in(
            h, mod[:, 1, 0], mod[:, 1, 1], norm_g[l, 1][None], wq, wkv, wg, wuv, wga, wgb,
            cos_t, sin_t, oh_t, gm_ln_g[l][None], gm_ln_b[l][None], gm_ws[l], bsx,
            proj_b[l].astype(BF16))

        pe_ka, pe_kb = _compress_pe(cmp_pe[l, 0])
        pe_va, pe_vb = _compress_pe(cmp_pe[l, 1])
        pe = jnp.stack([pe_ka, pe_kb, pe_va, pe_vb])
        w2k = jnp.concatenate([cmp_w2[l, 0], cmp_w2[l, 0][:, swap]], axis=1).astype(BF16)
        w2v = jnp.pad(cmp_w2[l, 1], ((0, 0), (0, LANE - HEAD_DIM))).astype(BF16)
        nc = S // CMP_STRIDE
        kc, vct = _compress(kcmp.reshape(B, nc, CMP_STRIDE * LANE),
                            vcmp.reshape(B, nc, CMP_STRIDE * LANE), pe,
                            _compress_weights(cmp_w1[l, 0]), _compress_weights(cmp_w1[l, 1]),
                            w2k, w2v, cos_c, sin_c)

        h = _attn(qt, kselx, vselt, kwinx, vwint, kc, vct, gt, ga, mb, h, mod[:, 1, 2], ovt,
                  bdiag, bfar, proj_a[l].astype(BF16), w_out[l].astype(BF16))

        h = _ffn(h, mod[:, 2, 0], mod[:, 2, 1], mod[:, 2, 2], norm_g[l, 2][None],
                 ffn_w_in[l, 1].astype(BF16), ffn_w_out[l, 1].astype(BF16))
    return _final(h, final_g[None])
```

```python
import numpy as np
import jax
import jax.numpy as jnp
from jax import lax
from jax.experimental import pallas as pl
from jax.experimental.pallas import tpu as pltpu

F32 = jnp.float32
BF16 = jnp.bfloat16

HEAD_DIM = 64
NSA_HEADS = 8
KV_GROUPS = 2
HPG = NSA_HEADS // KV_GROUPS
NSA_WIDTH = NSA_HEADS * HEAD_DIM
CMP_BLOCK = 32
CMP_STRIDE = 16
CMP_HIDDEN = 128
SEL_BLOCK = 64
SEL_TOPK = 16
WINDOW = 512
GM_GROUPS = 8
GM_CHUNK = 128
GM_WIDTH = GM_GROUPS * HEAD_DIM
ROPE_THETA = 10000.0
EPS = 1e-6
NEG = -1e30
FORCE = 1e4
LANE = 128
GELU_C = float(np.sqrt(2.0 / np.pi))
LOG2E = float(np.log2(np.e))
GATE_ROWS = 32

FFN_ROWS = 512
MIX_ROWS = 512
ATT_ROWS = 256
ATT_CHUNK = 256
VMEM_LIMIT = 56 * 1024 * 1024


def _sigmoid(x):
    return 1.0 / (1.0 + jnp.exp(-x))


def _silu(x):
    return x * _sigmoid(x)


def _rms_mod(x, g, shift, scale):
    y = x * lax.rsqrt(jnp.mean(x * x, axis=-1, keepdims=True) + EPS) * g
    return y * (1.0 + scale) + shift


def _ada_kernel(c_ref, w_ref, b_ref, o_ref):
    s = _silu(c_ref[...])
    o_ref[0] = jnp.dot(s, w_ref[0], preferred_element_type=F32,
                       precision=lax.Precision.HIGHEST) + b_ref[0]


def _ada(c, ada_w, ada_b):
    L, D, N = ada_w.shape
    B = c.shape[0]
    tn = 1024
    return pl.pallas_call(
        _ada_kernel,
        grid=(L, N // tn),
        in_specs=[
            pl.BlockSpec((B, D), lambda l, j: (0, 0)),
            pl.BlockSpec((1, D, tn), lambda l, j: (l, 0, j)),
            pl.BlockSpec((1, 1, tn), lambda l, j: (l, 0, j)),
        ],
        out_specs=pl.BlockSpec((1, B, tn), lambda l, j: (l, 0, j)),
        out_shape=jax.ShapeDtypeStruct((L, B, N), F32),
        compiler_params=pltpu.CompilerParams(
            dimension_semantics=("parallel", "parallel"), vmem_limit_bytes=VMEM_LIMIT),
        name="ada",
    )(c, ada_w, ada_b.reshape(L, 1, N))


def _ffn_kernel(h_ref, shift_ref, scale_ref, gate_ref, g_ref, wa_ref, wb_ref, wo_ref,
                o_ref, n_scr, acc_scr):
    j = pl.program_id(2)

    @pl.when(j == 0)
    def _():
        n = _rms_mod(h_ref[0], g_ref[...], shift_ref[0], scale_ref[0])
        n_scr[...] = n.astype(BF16)
        acc_scr[...] = jnp.zeros_like(acc_scr)

    n = n_scr[...]
    a = jnp.dot(n, wa_ref[...], preferred_element_type=F32)
    b = jnp.dot(n, wb_ref[...], preferred_element_type=F32)
    hm = (_silu(a) * b).astype(BF16)
    acc_scr[...] += jnp.dot(hm, wo_ref[...], preferred_element_type=F32)

    @pl.when(j == pl.num_programs(2) - 1)
    def _():
        o_ref[0] = h_ref[0] + (0.5 * gate_ref[0]) * acc_scr[...]


def _ffn(h, shift, scale, gate, g, w_in, w_out):
    B, S, D = h.shape
    F = w_out.shape[0]
    tm = FFN_ROWS
    nf = 2
    tf = F // nf
    row = pl.BlockSpec((1, tm, D), lambda b, i, j: (b, i, 0))
    vec = pl.BlockSpec((1, 1, D), lambda b, i, j: (b, 0, 0))
    return pl.pallas_call(
        _ffn_kernel,
        grid=(B, S // tm, nf),
        in_specs=[
            row, vec, vec, vec,
            pl.BlockSpec((1, D), lambda b, i, j: (0, 0)),
            pl.BlockSpec((D, tf), lambda b, i, j: (0, j)),
            pl.BlockSpec((D, tf), lambda b, i, j: (0, j + nf)),
            pl.BlockSpec((tf, D), lambda b, i, j: (j, 0)),
        ],
        out_specs=row,
        out_shape=jax.ShapeDtypeStruct((B, S, D), F32),
        scratch_shapes=[pltpu.VMEM((tm, D), BF16), pltpu.VMEM((tm, D), F32)],
        compiler_params=pltpu.CompilerParams(
            dimension_semantics=("parallel", "parallel", "arbitrary"),
            vmem_limit_bytes=VMEM_LIMIT),
        name="ffn",
    )(h, shift, scale, gate, g, w_in, w_in, w_out)


def _rope_pair(x, c, s):
    lane = lax.broadcasted_iota(jnp.int32, x.shape, 1)
    first_half = (lane & (HEAD_DIM - 1)) < HEAD_DIM // 2
    swapped = jnp.where(first_half, pltpu.roll(x, LANE - HEAD_DIM // 2, 1),
                        pltpu.roll(x, HEAD_DIM // 2, 1))
    return x * c + swapped * s


def _mix_in_kernel(h_ref, shift_ref, scale_ref, g_ref, wq_ref, wkv_ref, wg_ref, wuv_ref,
                   wga_ref, wgb_ref, cos_ref, sin_ref, oh_ref, lng_ref, lnb_ref, ws_ref,
                   bsx_ref, pb_ref,
                   qt_ref, kcmp_ref, vcmp_ref, kselx_ref, vselt_ref, kwinx_ref, vwint_ref,
                   gt_ref, ga_ref, mb_ref):
    tm = h_ref.shape[1]
    CK = vselt_ref.shape[4]
    n = _rms_mod(h_ref[0], g_ref[...], shift_ref[0], scale_ref[0]).astype(BF16)
    cos = cos_ref[...]
    sin = sin_ref[...]
    low_lanes = lax.broadcasted_iota(jnp.int32, (tm, LANE), 1) < HEAD_DIM

    q = jnp.dot(n, wq_ref[...], preferred_element_type=F32)
    for p in range(NSA_HEADS // 2):
        qp = jnp.transpose(_rope_pair(q[:, p * LANE:(p + 1) * LANE], cos, sin)).astype(BF16)
        qt_ref[0, 2 * p] = qp[:HEAD_DIM]
        qt_ref[0, 2 * p + 1] = qp[HEAD_DIM:]

    kv = jnp.dot(n, wkv_ref[...], preferred_element_type=F32)
    kcmp_ref[0] = kv[:, 0 * LANE:1 * LANE]
    vcmp_ref[0] = kv[:, 1 * LANE:2 * LANE]
    for idx, ref, ext in ((2, kselx_ref, oh_ref[...]), (4, kwinx_ref, 0.0)):
        k2 = _rope_pair(kv[:, idx * LANE:(idx + 1) * LANE], cos, sin)
        ref[0, 0] = jnp.where(low_lanes, k2, ext).astype(BF16)
        ref[0, 1] = jnp.where(low_lanes, pltpu.roll(k2, HEAD_DIM, 1), ext).astype(BF16)
    for idx, ref in ((3, vselt_ref), (5, vwint_ref)):
        vt = jnp.transpose(kv[:, idx * LANE:(idx + 1) * LANE]).astype(BF16)
        for g in range(KV_GROUPS):
            for r in range(tm // CK):
                ref[0, g, r] = vt[g * HEAD_DIM:(g + 1) * HEAD_DIM, r * CK:(r + 1) * CK]

    gates = _sigmoid(jnp.dot(n, wg_ref[...], preferred_element_type=F32))
    gt_ref[0] = jnp.transpose(gates)[:GATE_ROWS]
    ga_ref[0] = _sigmoid(jnp.dot(n, wga_ref[...], preferred_element_type=F32))

    uv = jnp.dot(n, wuv_ref[...], preferred_element_type=F32)
    ge = uv * (0.5 * (1.0 + jnp.tanh(GELU_C * (uv + 0.044715 * (uv * uv * uv)))))
    u = ge[:, :GM_WIDTH]
    v = ge[:, GM_WIDTH:]
    mu = jnp.mean(v, axis=-1, keepdims=True)
    var = jnp.mean(jnp.square(v - mu), axis=-1, keepdims=True)
    vln = ((v - mu) * lax.rsqrt(var + EPS) * lng_ref[...] + lnb_ref[...]).astype(BF16)

    ti = lax.broadcasted_iota(jnp.int32, (GM_CHUNK, GM_CHUNK), 0)
    si = lax.broadcasted_iota(jnp.int32, (GM_CHUNK, GM_CHUNK), 1)
    tril = si <= ti
    wm = [jnp.where(tril, ws_ref[gg], 0.0).astype(BF16) for gg in range(GM_GROUPS)]
    low = lax.broadcasted_iota(jnp.int32, (GM_CHUNK, LANE), 1) < HEAD_DIM
    bsx = bsx_ref[...]
    yb_rows = []
    for r in range(tm // GM_CHUNK):
        vch = vln[r * GM_CHUNK:(r + 1) * GM_CHUNK]
        pieces = []
        for p in range(GM_GROUPS // 2):
            vp = vch[:, p * LANE:(p + 1) * LANE]
            a0 = jnp.dot(wm[2 * p], vp, preferred_element_type=F32)
            a1 = jnp.dot(wm[2 * p + 1], vp, preferred_element_type=F32)
            pieces.append(jnp.where(low, a0, a1))
        sv = jnp.concatenate(pieces, axis=1) + bsx
        yb_rows.append(u[r * GM_CHUNK:(r + 1) * GM_CHUNK] * sv)
    yb = jnp.concatenate(yb_rows, axis=0).astype(BF16)
    gb = _sigmoid(jnp.dot(n, wgb_ref[...], preferred_element_type=F32))
    mb_ref[0] = gb * jnp.dot(yb, pb_ref[...], preferred_element_type=F32)


def _mix_in(h, shift, scale, g, wq, wkv, wg, wuv, wga, wgb, cos_t, sin_t, oh_t, ln_g, ln_b, ws,
            bsx, proj_b):
    B, S, D = h.shape
    tm = MIX_ROWS
    G = KV_GROUPS
    CK = ATT_CHUNK

    def const(a):
        nd = a.ndim
        return pl.BlockSpec(a.shape, lambda b, i: (0,) * nd)

    row = lambda w: pl.BlockSpec((1, tm, w), lambda b, i: (b, i, 0))
    vec = pl.BlockSpec((1, 1, D), lambda b, i: (b, 0, 0))
    tab = pl.BlockSpec((tm, LANE), lambda b, i: (i, 0))
    kspec = pl.BlockSpec((1, G, tm, LANE), lambda b, i: (b, 0, i, 0))
    kshape = jax.ShapeDtypeStruct((B, G, S, LANE), BF16)
    vspec = pl.BlockSpec((1, G, tm // CK, HEAD_DIM, CK), lambda b, i: (b, 0, i, 0, 0))
    vshape = jax.ShapeDtypeStruct((B, G, S // CK, HEAD_DIM, CK), BF16)
    return pl.pallas_call(
        _mix_in_kernel,
        grid=(B, S // tm),
        in_specs=[row(D), vec, vec, const(g), const(wq), const(wkv), const(wg), const(wuv),
                  const(wga), const(wgb), tab, tab, tab, const(ln_g), const(ln_b), const(ws),
                  const(bsx), const(proj_b)],
        out_specs=[
            pl.BlockSpec((1, NSA_HEADS, HEAD_DIM, tm), lambda b, i: (b, 0, 0, i)),
            row(LANE), row(LANE), kspec, vspec, kspec, vspec,
            pl.BlockSpec((1, GATE_ROWS, tm), lambda b, i: (b, 0, i)),
            row(D), row(D)],
        out_shape=[
            jax.ShapeDtypeStruct((B, NSA_HEADS, HEAD_DIM, S), BF16),
            jax.ShapeDtypeStruct((B, S, LANE), F32), jax.ShapeDtypeStruct((B, S, LANE), F32),
            kshape, vshape, kshape, vshape,
            jax.ShapeDtypeStruct((B, GATE_ROWS, S), F32),
            jax.ShapeDtypeStruct((B, S, D), F32), jax.ShapeDtypeStruct((B, S, D), F32)],
        compiler_params=pltpu.CompilerParams(
            dimension_semantics=("parallel", "parallel"), vmem_limit_bytes=VMEM_LIMIT),
        name="mix_in",
    )(h, shift, scale, g, wq, wkv, wg, wuv, wga, wgb, cos_t, sin_t, oh_t, ln_g, ln_b, ws, bsx,
      proj_b)


def _compress_kernel(k16_ref, v16_ref, pe_ref, wk_ref, wv_ref, w2k_ref, w2v_ref, cos_ref,
                     sin_ref, kc_ref, vct_ref):
    def hidden(x16, pe_a, pe_b, w_ref):
        ha = jnp.dot((x16 + pe_a).astype(BF16), w_ref[0], preferred_element_type=F32)
        hb = jnp.dot((x16 + pe_b).astype(BF16), w_ref[1], preferred_element_type=F32)
        return _silu(ha + pltpu.roll(hb, hb.shape[0] - 1, 0))

    hk = hidden(k16_ref[0], pe_ref[0], pe_ref[1], wk_ref)
    hv = hidden(v16_ref[0], pe_ref[2], pe_ref[3], wv_ref)
    for g in range(KV_GROUPS):
        k2 = jnp.dot(hk[:, g * CMP_HIDDEN:(g + 1) * CMP_HIDDEN].astype(BF16), w2k_ref[...],
                     preferred_element_type=F32)
        kc = k2[:, :HEAD_DIM] * cos_ref[...] + k2[:, HEAD_DIM:] * sin_ref[...]
        kc_ref[0, g] = kc.astype(BF16)
        vc = jnp.dot(hv[:, g * CMP_HIDDEN:(g + 1) * CMP_HIDDEN].astype(BF16), w2v_ref[...],
                     preferred_element_type=F32)
        vct_ref[0, g] = jnp.transpose(vc)[:HEAD_DIM].astype(BF16)


def _compress(k16, v16, pe, wk, wv, w2k, w2v, cos_c, sin_c):
    B, NC, W = k16.shape

    def const(a):
        nd = a.ndim
        return pl.BlockSpec(a.shape, lambda b: (0,) * nd)

    blk = pl.BlockSpec((1, NC, W), lambda b: (b, 0, 0))
    return pl.pallas_call(
        _compress_kernel,
        grid=(B,),
        in_specs=[blk, blk, const(pe), const(wk), const(wv), const(w2k), const(w2v),
                  const(cos_c), const(sin_c)],
        out_specs=[pl.BlockSpec((1, KV_GROUPS, NC, HEAD_DIM), lambda b: (b, 0, 0, 0)),
                   pl.BlockSpec((1, KV_GROUPS, HEAD_DIM, NC), lambda b: (b, 0, 0, 0))],
        out_shape=[jax.ShapeDtypeStruct((B, KV_GROUPS, NC, HEAD_DIM), BF16),
                   jax.ShapeDtypeStruct((B, KV_GROUPS, HEAD_DIM, NC), BF16)],
        compiler_params=pltpu.CompilerParams(
            dimension_semantics=("parallel",), vmem_limit_bytes=VMEM_LIMIT),
        name="compress",
    )(k16, v16, pe, wk, wv, w2k, w2v, cos_c, sin_c)


def _fold_rows(x, op):
    parts = [x[8 * i:8 * (i + 1)] for i in range(x.shape[0] // 8)]
    while len(parts) > 1:
        parts = [op(parts[i], parts[i + 1]) for i in range(0, len(parts), 2)]
    return parts[0]


def _attn_kernel(qt_ref, kselx_ref, vselt_ref, kwinx_ref, vwint_ref, kc_ref, vct_ref, gt_ref,
                 ga_ref, mb_ref, h_ref, gate_ref, ovt_ref, bdiag_ref, bfar_ref, pa_ref, wo_ref,
                 o_ref, qx_scr, s_scr, acc_scr, yt_scr):
    TQ = h_ref.shape[1]
    CK = ATT_CHUNK
    NC = kc_ref.shape[2]
    NSEL = kselx_ref.shape[2] // SEL_BLOCK
    nwin = WINDOW // CK
    qi = pl.program_id(1)

    def branch(kx_ref, vt_ref, g, lo, far, gate_row):
        def scores(bias_ref):
            def body(c, mparts):
                kx = kx_ref[0, g, pl.ds(pl.multiple_of(c * CK, CK), CK), :]
                out = []
                for hh in range(HPG):
                    s = jnp.dot(kx, qx_scr[hh], preferred_element_type=F32)
                    if bias_ref is not None:
                        s = s + bias_ref[...]
                    s_scr[c, hh] = s
                    out.append(jnp.maximum(mparts[hh], _fold_rows(s, jnp.maximum)))
                return tuple(out)
            return body

        mparts = tuple(jnp.full((8, TQ), NEG, F32) for _ in range(HPG))
        if far is not None:
            mparts = lax.fori_loop(far, jnp.where(far >= 0, far + 1, far), scores(bfar_ref),
                                   mparts)
        mparts = lax.fori_loop(lo, qi, scores(None), mparts)
        mparts = lax.fori_loop(qi, qi + 1, scores(bdiag_ref), mparts)
        m = [jnp.max(mp, axis=0, keepdims=True) for mp in mparts]

        acc_scr[...] = jnp.zeros_like(acc_scr)

        def weigh(c, lparts):
            vt = vt_ref[0, g, c]
            out = []
            for hh in range(HPG):
                p = jnp.exp2(s_scr[c, hh] - m[hh])
                out.append(lparts[hh] + _fold_rows(p, jnp.add))
                acc_scr[hh] = acc_scr[hh] + jnp.dot(vt, p.astype(BF16),
                                                    preferred_element_type=F32)
            return tuple(out)

        first = lo if far is None else jnp.maximum(far, 0)
        lparts = lax.fori_loop(first, qi + 1, weigh,
                               tuple(jnp.zeros((8, TQ), F32) for _ in range(HPG)))
        for hh in range(HPG):
            hd = g * HPG + hh
            w = gt_ref[0, 3 * hd + gate_row:3 * hd + gate_row + 1, :] / jnp.sum(
                lparts[hh], axis=0, keepdims=True)
            rows = slice(hd * HEAD_DIM, (hd + 1) * HEAD_DIM)
            yt_scr[rows, :] = yt_scr[rows, :] + w * acc_scr[hh]

    n_idx = lax.broadcasted_iota(jnp.int32, (NC, TQ), 0)
    t_cmp = qi * TQ + lax.broadcasted_iota(jnp.int32, (NC, TQ), 1)
    vis = (n_idx * CMP_STRIDE + (CMP_BLOCK - 1)) <= t_cmp
    visf = jnp.where(vis, 1.0, 0.0)
    j_idx = lax.broadcasted_iota(jnp.int32, (NSEL, TQ), 0)
    t_sel = qi * TQ + lax.broadcasted_iota(jnp.int32, (NSEL, TQ), 1)
    cur = jnp.right_shift(t_sel, SEL_BLOCK.bit_length() - 1)
    forced = (j_idx == 0) | (j_idx == cur) | (j_idx == cur - 1)
    valid = j_idx <= cur

    for g in range(KV_GROUPS):
        psum = jnp.zeros((NC, TQ), F32)
        for hh in range(HPG):
            hd = g * HPG + hh
            qt = qt_ref[0, hd]
            sc = jnp.where(vis, jnp.dot(kc_ref[0, g], qt, preferred_element_type=F32), NEG)
            ec = jnp.exp2(sc - jnp.max(sc, axis=0, keepdims=True)) * visf
            den = jnp.sum(ec, axis=0, keepdims=True)
            pc = ec / jnp.where(den > 0.0, den, 1.0)
            psum = psum + pc
            oc = jnp.dot(vct_ref[0, g], pc.astype(BF16), preferred_element_type=F32)
            yt_scr[hd * HEAD_DIM:(hd + 1) * HEAD_DIM, :] = gt_ref[0, 3 * hd:3 * hd + 1, :] * oc

        imp = jnp.dot(ovt_ref[...], psum, preferred_element_type=F32,
                      precision=lax.Precision.HIGHEST)
        imp = jnp.where(forced, FORCE, jnp.where(valid, imp, -FORCE))
        rank = jnp.zeros((NSEL, TQ), jnp.int32)
        for kk in range(NSEL):
            rk = imp[kk:kk + 1, :]
            ahead = (rk > imp) | ((rk == imp) & (j_idx > kk))
            rank = rank + jnp.where(ahead, 1, 0)
        selneg = jnp.where(rank < SEL_TOPK, 0.0, NEG).astype(BF16)
        pad = jnp.zeros((LANE - HEAD_DIM - NSEL, TQ), BF16)
        for hh in range(HPG):
            qx_scr[hh] = jnp.concatenate([qt_ref[0, g * HPG + hh], selneg, pad], axis=0)

        branch(kselx_ref, vselt_ref, g, 0, None, 1)
        branch(kwinx_ref, vwint_ref, g, jnp.maximum(qi - nwin + 1, 0), qi - nwin, 2)

    y = jnp.transpose(yt_scr[...]).astype(BF16)
    ya = jnp.dot(y, pa_ref[...], preferred_element_type=F32)
    merged = (ga_ref[0] * ya + mb_ref[0]).astype(BF16)
    o_ref[0] = h_ref[0] + gate_ref[0] * jnp.dot(merged, wo_ref[...],
                                                preferred_element_type=F32)


def _attn(qt, kselx, vselt, kwinx, vwint, kc, vct, gt, ga, mb, h, gate, ovt, bdiag, bfar,
          proj_a, w_out):
    B, S, D = h.shape
    TQ = ATT_ROWS
    CK = ATT_CHUNK
    assert TQ == CK and WINDOW % CK == 0 and S % TQ == 0

    def const(a):
        nd = a.ndim
        return pl.BlockSpec(a.shape, lambda b, i: (0,) * nd)

    def per_batch(a):
        nd = a.ndim
        return pl.BlockSpec((1,) + a.shape[1:], lambda b, i: (b,) + (0,) * (nd - 1))

    row = lambda w: pl.BlockSpec((1, TQ, w), lambda b, i: (b, i, 0))
    return pl.pallas_call(
        _attn_kernel,
        grid=(B, S // TQ),
        in_specs=[
            pl.BlockSpec((1, NSA_HEADS, HEAD_DIM, TQ), lambda b, i: (b, 0, 0, i)),
            per_batch(kselx), per_batch(vselt), per_batch(kwinx), per_batch(vwint),
            per_batch(kc), per_batch(vct),
            pl.BlockSpec((1, GATE_ROWS, TQ), lambda b, i: (b, 0, i)),
            row(D), row(D), row(D),
            pl.BlockSpec((1, 1, D), lambda b, i: (b, 0, 0)),
            const(ovt), const(bdiag), const(bfar), const(proj_a), const(w_out)],
        out_specs=row(D),
        out_shape=jax.ShapeDtypeStruct((B, S, D), F32),
        scratch_shapes=[
            pltpu.VMEM((HPG, LANE, TQ), BF16),
            pltpu.VMEM((S // CK, HPG, CK, TQ), F32),
            pltpu.VMEM((HPG, HEAD_DIM, TQ), F32),
            pltpu.VMEM((NSA_WIDTH, TQ), F32)],
        compiler_params=pltpu.CompilerParams(
            dimension_semantics=("parallel", "parallel"), vmem_limit_bytes=VMEM_LIMIT),
        name="attn",
    )(qt, kselx, vselt, kwinx, vwint, kc, vct, gt, ga, mb, h, gate, ovt, bdiag, bfar,
      proj_a, w_out)


def _final_kernel(h_ref, g_ref, o_ref):
    x = h_ref[0]
    o_ref[0] = x * lax.rsqrt(jnp.mean(x * x, axis=-1, keepdims=True) + EPS) * g_ref[...]


def _final(h, g):
    B, S, D = h.shape
    tm = FFN_ROWS
    row = pl.BlockSpec((1, tm, D), lambda b, i: (b, i, 0))
    return pl.pallas_call(
        _final_kernel,
        grid=(B, S // tm),
        in_specs=[row, pl.BlockSpec((1, D), lambda b, i: (0, 0))],
        out_specs=row,
        out_shape=jax.ShapeDtypeStruct((B, S, D), F32),
        compiler_params=pltpu.CompilerParams(
            dimension_semantics=("parallel", "parallel"), vmem_limit_bytes=VMEM_LIMIT),
        name="final_norm",
    )(h, g)


def _rope_tables(pos):
    inv = 1.0 / (ROPE_THETA ** (np.arange(0, HEAD_DIM, 2, dtype=np.float64) / HEAD_DIM))
    ang = np.asarray(pos, np.float64)[:, None] * inv[None, :]
    cos = np.concatenate([np.cos(ang), np.cos(ang)], axis=1)
    sin = np.concatenate([-np.sin(ang), np.sin(ang)], axis=1)
    return cos.astype(np.float32), sin.astype(np.float32)


def _tables(S):
    cos, sin = _rope_tables(np.arange(S))
    cos_t = np.concatenate([cos, cos], axis=1)
    sin_t = np.concatenate([sin, sin], axis=1)
    n_cmp_pad = S // CMP_STRIDE
    starts = np.arange(n_cmp_pad) * CMP_STRIDE
    cos_c, sin_c = _rope_tables(starts + CMP_BLOCK - 1)
    n_sel = S // SEL_BLOCK
    sel_start = np.arange(n_sel) * SEL_BLOCK
    overlap = np.clip(np.minimum(starts[:, None] + CMP_BLOCK, sel_start[None, :] + SEL_BLOCK)
                      - np.maximum(starts[:, None], sel_start[None, :]), 0, None) / CMP_BLOCK
    ovt = np.ascontiguousarray(overlap.T).astype(np.float32)
    oh = np.zeros((S, LANE), np.float32)
    oh[np.arange(S), HEAD_DIM + np.arange(S) // SEL_BLOCK] = 1.0
    j = np.arange(ATT_CHUNK)[:, None]
    i = np.arange(ATT_ROWS)[None, :]
    bdiag = np.where(j <= i, 0.0, NEG).astype(np.float32)
    bfar = np.where(j > i, 0.0, NEG).astype(np.float32)
    return tuple(jnp.asarray(a) for a in (cos_t, sin_t, cos_c, sin_c, ovt, oh, bdiag, bfar))


def _compress_weights(w1):
    eye = jnp.eye(KV_GROUPS, dtype=w1.dtype)
    half = CMP_BLOCK // 2

    def one(w):
        return jnp.einsum('ldf,gh->lgdhf', w, eye).reshape(half * KV_GROUPS * HEAD_DIM,
                                                           KV_GROUPS * CMP_HIDDEN)

    return jnp.stack([one(w1[:half]), one(w1[half:])]).astype(BF16)


def _compress_pe(pe):
    half = CMP_BLOCK // 2

    def one(p):
        return jnp.broadcast_to(p[:, None, :], (half, KV_GROUPS, HEAD_DIM)).reshape(1, -1)

    return one(pe[:half]), one(pe[half:])


def kernel(x, c, ada_w, ada_b, norm_g, ffn_w_in, ffn_w_out, mix_w_in, cmp_pe, cmp_w1, cmp_w2,
           gm_ln_g, gm_ln_b, gm_ws, gm_bs, proj_a, proj_b, w_out, final_g):
    B, S, D = x.shape
    L = ada_w.shape[0]
    assert S // SEL_BLOCK + HEAD_DIM <= LANE and 3 * NSA_HEADS <= GATE_ROWS
    cos_t, sin_t, cos_c, sin_c, ovt, oh_t, bdiag, bfar = _tables(S)
    swap = np.concatenate([np.arange(HEAD_DIM // 2, HEAD_DIM), np.arange(HEAD_DIM // 2)])

    mods = _ada(c, ada_w, ada_b).reshape(L, B, 3, 3, 1, D)
    h = x
    for l in range(L):
        mod = mods[l]
        h = _ffn(h, mod[:, 0, 0], mod[:, 0, 1], mod[:, 0, 2], norm_g[l, 0][None],
                 ffn_w_in[l, 0].astype(BF16), ffn_w_out[l, 0].astype(BF16))

        w = mix_w_in[l]
        o_kv = NSA_WIDTH
        o_g = o_kv + 6 * KV_GROUPS * HEAD_DIM
        o_uv = o_g + 3 * NSA_HEADS
        o_ga = o_uv + 2 * GM_WIDTH
        o_gb = o_ga + D
        wq = (w[:, :o_kv] * (HEAD_DIM ** -0.5 * LOG2E)).astype(BF16)
        wkv = w[:, o_kv:o_g].astype(BF16)
        wg = jnp.pad(w[:, o_g:o_uv], ((0, 0), (0, LANE - 3 * NSA_HEADS))).astype(BF16)
        wuv = w[:, o_uv:o_ga].astype(BF16)
        wga = w[:, o_ga:o_gb].astype(BF16)
        wgb = w[:, o_gb:].astype(BF16)
        bsx = jnp.repeat(gm_bs[l].T, HEAD_DIM, axis=1)
        qt, kcmp, vcmp, kselx, vselt, kwinx, vwint, gt, ga, mb = _mix_in(
            h, mod[:, 1, 0], mod[:, 1, 1], norm_g[l, 1][None], wq, wkv, wg, wuv, wga, wgb,
            cos_t, sin_t, oh_t, gm_ln_g[l][None], gm_ln_b[l][None], gm_ws[l], bsx,
            proj_b[l].astype(BF16))

        pe_ka, pe_kb = _compress_pe(cmp_pe[l, 0])
        pe_va, pe_vb = _compress_pe(cmp_pe[l, 1])
        pe = jnp.stack([pe_ka, pe_kb, pe_va, pe_vb])
        w2k = jnp.concatenate([cmp_w2[l, 0], cmp_w2[l, 0][:, swap]], axis=1).astype(BF16)
        w2v = jnp.pad(cmp_w2[l, 1], ((0, 0), (0, LANE - HEAD_DIM))).astype(BF16)
        nc = S // CMP_STRIDE
        kc, vct = _compress(kcmp.reshape(B, nc, CMP_STRIDE * LANE),
                            vcmp.reshape(B, nc, CMP_STRIDE * LANE), pe,
                            _compress_weights(cmp_w1[l, 0]), _compress_weights(cmp_w1[l, 1]),
                            w2k, w2v, cos_c, sin_c)

        h = _attn(qt, kselx, vselt, kwinx, vwint, kc, vct, gt, ga, mb, h, mod[:, 1, 2], ovt,
                  bdiag, bfar, proj_a[l].astype(BF16), w_out[l].astype(BF16))

        h = _ffn(h, mod[:, 2, 0], mod[:, 2, 1], mod[:, 2, 2], norm_g[l, 2][None],
                 ffn_w_in[l, 1].astype(BF16), ffn_w_out[l, 1].astype(BF16))
    return _final(h, final_g[None])
```

```python
import numpy as np
import jax
import jax.numpy as jnp
from jax import lax
from jax.experimental import pallas as pl
from jax.experimental.pallas import tpu as pltpu

F32 = jnp.float32
BF16 = jnp.bfloat16

HEAD_DIM = 64
NSA_HEADS = 8
KV_GROUPS = 2
HPG = NSA_HEADS // KV_GROUPS
NSA_WIDTH = NSA_HEADS * HEAD_DIM
CMP_BLOCK = 32
CMP_STRIDE = 16
CMP_HIDDEN = 128
SEL_BLOCK = 64
SEL_TOPK = 16
WINDOW = 512
GM_GROUPS = 8
GM_CHUNK = 128
GM_WIDTH = GM_GROUPS * HEAD_DIM
ROPE_THETA = 10000.0
EPS = 1e-6
NEG = -1e30
FORCE = 1e4
LANE = 128
GELU_C = float(np.sqrt(2.0 / np.pi))
LOG2E = float(np.log2(np.e))
GATE_ROWS = 32

FFN_ROWS = 512
MIX_ROWS = 512
ATT_ROWS = 256
ATT_CHUNK = 256
VMEM_LIMIT = 56 * 1024 * 1024


def _sigmoid(x):
    return 1.0 / (1.0 + jnp.exp(-x))


def _silu(x):
    return x * _sigmoid(x)


def _rms_mod(x, g, shift, scale):
    y = x * lax.rsqrt(jnp.mean(x * x, axis=-1, keepdims=True) + EPS) * g
    return y * (1.0 + scale) + shift


def _ada_kernel(c_ref, w_ref, b_ref, o_ref):
    s = _silu(c_ref[...])
    o_ref[0] = jnp.dot(s, w_ref[0], preferred_element_type=F32,
                       precision=lax.Precision.HIGHEST) + b_ref[0]


def _ada(c, ada_w, ada_b):
    L, D, N = ada_w.shape
    B = c.shape[0]
    tn = 1024
    return pl.pallas_call(
        _ada_kernel,
        grid=(L, N // tn),
        in_specs=[
            pl.BlockSpec((B, D), lambda l, j: (0, 0)),
            pl.BlockSpec((1, D, tn), lambda l, j: (l, 0, j)),
            pl.BlockSpec((1, 1, tn), lambda l, j: (l, 0, j)),
        ],
        out_specs=pl.BlockSpec((1, B, tn), lambda l, j: (l, 0, j)),
        out_shape=jax.ShapeDtypeStruct((L, B, N), F32),
        compiler_params=pltpu.CompilerParams(
            dimension_semantics=("parallel", "parallel"), vmem_limit_bytes=VMEM_LIMIT),
        name="ada",
    )(c, ada_w, ada_b.reshape(L, 1, N))


def _ffn_kernel(h_ref, shift_ref, scale_ref, gate_ref, g_ref, wa_ref, wb_ref, wo_ref,
                o_ref, n_scr, acc_scr):
    j = pl.program_id(2)

    @pl.when(j == 0)
    def _():
        n = _rms_mod(h_ref[0], g_ref[...], shift_ref[0], scale_ref[0])
        n_scr[...] = n.astype(BF16)
        acc_scr[...] = jnp.zeros_like(acc_scr)

    n = n_scr[...]
    a = jnp.dot(n, wa_ref[...], preferred_element_type=F32)
    b = jnp.dot(n, wb_ref[...], preferred_element_type=F32)
    hm = (_silu(a) * b).astype(BF16)
    acc_scr[...] += jnp.dot(hm, wo_ref[...], preferred_element_type=F32)

    @pl.when(j == pl.num_programs(2) - 1)
    def _():
        o_ref[0] = h_ref[0] + (0.5 * gate_ref[0]) * acc_scr[...]


def _ffn(h, shift, scale, gate, g, w_in, w_out):
    B, S, D = h.shape
    F = w_out.shape[0]
    tm = FFN_ROWS
    nf = 2
    tf = F // nf
    row = pl.BlockSpec((1, tm, D), lambda b, i, j: (b, i, 0))
    vec = pl.BlockSpec((1, 1, D), lambda b, i, j: (b, 0, 0))
    return pl.pallas_call(
        _ffn_kernel,
        grid=(B, S // tm, nf),
        in_specs=[
            row, vec, vec, vec,
            pl.BlockSpec((1, D), lambda b, i, j: (0, 0)),
            pl.BlockSpec((D, tf), lambda b, i, j: (0, j)),
            pl.BlockSpec((D, tf), lambda b, i, j: (0, j + nf)),
            pl.BlockSpec((tf, D), lambda b, i, j: (j, 0)),
        ],
        out_specs=row,
        out_shape=jax.ShapeDtypeStruct((B, S, D), F32),
        scratch_shapes=[pltpu.VMEM((tm, D), BF16), pltpu.VMEM((tm, D), F32)],
        compiler_params=pltpu.CompilerParams(
            dimension_semantics=("parallel", "parallel", "arbitrary"),
            vmem_limit_bytes=VMEM_LIMIT),
        name="ffn",
    )(h, shift, scale, gate, g, w_in, w_in, w_out)


def _rope_pair(x, c, s):
    lane = lax.broadcasted_iota(jnp.int32, x.shape, 1)
    first_half = (lane & (HEAD_DIM - 1)) < HEAD_DIM // 2
    swapped = jnp.where(first_half, pltpu.roll(x, LANE - HEAD_DIM // 2, 1),
                        pltpu.roll(x, HEAD_DIM // 2, 1))
    return x * c + swapped * s


def _mix_in_kernel(h_ref, shift_ref, scale_ref, g_ref, wq_ref, wkv_ref, wg_ref, wuv_ref,
                   wga_ref, wgb_ref, cos_ref, sin_ref, oh_ref, lng_ref, lnb_ref, ws_ref,
                   bsx_ref, pb_ref,
                   qt_ref, kcmp_ref, vcmp_ref, kselx_ref, vselt_ref, kwinx_ref, vwint_ref,
                   gt_ref, ga_ref, mb_ref):
    tm = h_ref.shape[1]
    CK = vselt_ref.shape[4]
    n = _rms_mod(h_ref[0], g_ref[...], shift_ref[0], scale_ref[0]).astype(BF16)
    cos = cos_ref[...]
    sin = sin_ref[...]
    low_lanes = lax.broadcasted_iota(jnp.int32, (tm, LANE), 1) < HEAD_DIM

    q = jnp.dot(n, wq_ref[...], preferred_element_type=F32)
    for p in range(NSA_HEADS // 2):
        qp = jnp.transpose(_rope_pair(q[:, p * LANE:(p + 1) * LANE], cos, sin)).astype(BF16)
        qt_ref[0, 2 * p] = qp[:HEAD_DIM]
        qt_ref[0, 2 * p + 1] = qp[HEAD_DIM:]

    kv = jnp.dot(n, wkv_ref[...], preferred_element_type=F32)
    kcmp_ref[0] = kv[:, 0 * LANE:1 * LANE]
    vcmp_ref[0] = kv[:, 1 * LANE:2 * LANE]
    for idx, ref, ext in ((2, kselx_ref, oh_ref[...]), (4, kwinx_ref, 0.0)):
        k2 = _rope_pair(kv[:, idx * LANE:(idx + 1) * LANE], cos, sin)
        ref[0, 0] = jnp.where(low_lanes, k2, ext).astype(BF16)
        ref[0, 1] = jnp.where(low_lanes, pltpu.roll(k2, HEAD_DIM, 1), ext).astype(BF16)
    for idx, ref in ((3, vselt_ref), (5, vwint_ref)):
        vt = jnp.transpose(kv[:, idx * LANE:(idx + 1) * LANE]).astype(BF16)
        for g in range(KV_GROUPS):
            for r in range(tm // CK):
                ref[0, g, r] = vt[g * HEAD_DIM:(g + 1) * HEAD_DIM, r * CK:(r + 1) * CK]

    gates = _sigmoid(jnp.dot(n, wg_ref[...], preferred_element_type=F32))
    gt_ref[0] = jnp.transpose(gates)[:GATE_ROWS]
    ga_ref[0] = _sigmoid(jnp.dot(n, wga_ref[...], preferred_element_type=F32))

    uv = jnp.dot(n, wuv_ref[...], preferred_element_type=F32)
    ge = uv * (0.5 * (1.0 + jnp.tanh(GELU_C * (uv + 0.044715 * (uv * uv * uv)))))
    u = ge[:, :GM_WIDTH]
    v = ge[:, GM_WIDTH:]
    mu = jnp.mean(v, axis=-1, keepdims=True)
    var = jnp.mean(jnp.square(v - mu), axis=-1, keepdims=True)
    vln = ((v - mu) * lax.rsqrt(var + EPS) * lng_ref[...] + lnb_ref[...]).astype(BF16)

    ti = lax.broadcasted_iota(jnp.int32, (GM_CHUNK, GM_CHUNK), 0)
    si = lax.broadcasted_iota(jnp.int32, (GM_CHUNK, GM_CHUNK), 1)
    tril = si <= ti
    wm = [jnp.where(tril, ws_ref[gg], 0.0).astype(BF16) for gg in range(GM_GROUPS)]
    low = lax.broadcasted_iota(jnp.int32, (GM_CHUNK, LANE), 1) < HEAD_DIM
    bsx = bsx_ref[...]
    yb_rows = []
    for r in range(tm // GM_CHUNK):
        vch = vln[r * GM_CHUNK:(r + 1) * GM_CHUNK]
        pieces = []
        for p in range(GM_GROUPS // 2):
            vp = vch[:, p * LANE:(p + 1) * LANE]
            a0 = jnp.dot(wm[2 * p], vp, preferred_element_type=F32)
            a1 = jnp.dot(wm[2 * p + 1], vp, preferred_element_type=F32)
            pieces.append(jnp.where(low, a0, a1))
        sv = jnp.concatenate(pieces, axis=1) + bsx
        yb_rows.append(u[r * GM_CHUNK:(r + 1) * GM_CHUNK] * sv)
    yb = jnp.concatenate(yb_rows, axis=0).astype(BF16)
    gb = _sigmoid(jnp.dot(n, wgb_ref[...], preferred_element_type=F32))
    mb_ref[0] = gb * jnp.dot(yb, pb_ref[...], preferred_element_type=F32)


def _mix_in(h, shift, scale, g, wq, wkv, wg, wuv, wga, wgb, cos_t, sin_t, oh_t, ln_g, ln_b, ws,
            bsx, proj_b):
    B, S, D = h.shape
    tm = MIX_ROWS
    G = KV_GROUPS
    CK = ATT_CHUNK

    def const(a):
        nd = a.ndim
        return pl.BlockSpec(a.shape, lambda b, i: (0,) * nd)

    row = lambda w: pl.BlockSpec((1, tm, w), lambda b, i: (b, i, 0))
    vec = pl.BlockSpec((1, 1, D), lambda b, i: (b, 0, 0))
    tab = pl.BlockSpec((tm, LANE), lambda b, i: (i, 0))
    kspec = pl.BlockSpec((1, G, tm, LANE), lambda b, i: (b, 0, i, 0))
    kshape = jax.ShapeDtypeStruct((B, G, S, LANE), BF16)
    vspec = pl.BlockSpec((1, G, tm // CK, HEAD_DIM, CK), lambda b, i: (b, 0, i, 0, 0))
    vshape = jax.ShapeDtypeStruct((B, G, S // CK, HEAD_DIM, CK), BF16)
    return pl.pallas_call(
        _mix_in_kernel,
        grid=(B, S // tm),
        in_specs=[row(D), vec, vec, const(g), const(wq), const(wkv), const(wg), const(wuv),
                  const(wga), const(wgb), tab, tab, tab, const(ln_g), const(ln_b), const(ws),
                  const(bsx), const(proj_b)],
        out_specs=[
            pl.BlockSpec((1, NSA_HEADS, HEAD_DIM, tm), lambda b, i: (b, 0, 0, i)),
            row(LANE), row(LANE), kspec, vspec, kspec, vspec,
            pl.BlockSpec((1, GATE_ROWS, tm), lambda b, i: (b, 0, i)),
            row(D), row(D)],
        out_shape=[
            jax.ShapeDtypeStruct((B, NSA_HEADS, HEAD_DIM, S), BF16),
            jax.ShapeDtypeStruct((B, S, LANE), F32), jax.ShapeDtypeStruct((B, S, LANE), F32),
            kshape, vshape, kshape, vshape,
            jax.ShapeDtypeStruct((B, GATE_ROWS, S), F32),
            jax.ShapeDtypeStruct((B, S, D), F32), jax.ShapeDtypeStruct((B, S, D), F32)],
        compiler_params=pltpu.CompilerParams(
            dimension_semantics=("parallel", "parallel"), vmem_limit_bytes=VMEM_LIMIT),
        name="mix_in",
    )(h, shift, scale, g, wq, wkv, wg, wuv, wga, wgb, cos_t, sin_t, oh_t, ln_g, ln_b, ws, bsx,
      proj_b)


def _compress_kernel(k16_ref, v16_ref, pe_ref, wk_ref, wv_ref, w2k_ref, w2v_ref, cos_ref,
                     sin_ref, kc_ref, vct_ref):
    def hidden(x16, pe_a, pe_b, w_ref):
        ha = jnp.dot((x16 + pe_a).astype(BF16), w_ref[0], preferred_element_type=F32)
        hb = jnp.dot((x16 + pe_b).astype(BF16), w_ref[1], preferred_element_type=F32)
        return _silu(ha + pltpu.roll(hb, hb.shape[0] - 1, 0))

    hk = hidden(k16_ref[0], pe_ref[0], pe_ref[1], wk_ref)
    hv = hidden(v16_ref[0], pe_ref[2], pe_ref[3], wv_ref)
    for g in range(KV_GROUPS):
        k2 = jnp.dot(hk[:, g * CMP_HIDDEN:(g + 1) * CMP_HIDDEN].astype(BF16), w2k_ref[...],
                     preferred_element_type=F32)
        kc = k2[:, :HEAD_DIM] * cos_ref[...] + k2[:, HEAD_DIM:] * sin_ref[...]
        kc_ref[0, g] = kc.astype(BF16)
        vc = jnp.dot(hv[:, g * CMP_HIDDEN:(g + 1) * CMP_HIDDEN].astype(BF16), w2v_ref[...],
                     preferred_element_type=F32)
        vct_ref[0, g] = jnp.transpose(vc)[:HEAD_DIM].astype(BF16)


def _compress(k16, v16, pe, wk, wv, w2k, w2v, cos_c, sin_c):
    B, NC, W = k16.shape

    def const(a):
        nd = a.ndim
        return pl.BlockSpec(a.shape, lambda b: (0,) * nd)

    blk = pl.BlockSpec((1, NC, W), lambda b: (b, 0, 0))
    return pl.pallas_call(
        _compress_kernel,
        grid=(B,),
        in_specs=[blk, blk, const(pe), const(wk), const(wv), const(w2k), const(w2v),
                  const(cos_c), const(sin_c)],
        out_specs=[pl.BlockSpec((1, KV_GROUPS, NC, HEAD_DIM), lambda b: (b, 0, 0, 0)),
                   pl.BlockSpec((1, KV_GROUPS, HEAD_DIM, NC), lambda b: (b, 0, 0, 0))],
        out_shape=[jax.ShapeDtypeStruct((B, KV_GROUPS, NC, HEAD_DIM), BF16),
                   jax.ShapeDtypeStruct((B, KV_GROUPS, HEAD_DIM, NC), BF16)],
        compiler_params=pltpu.CompilerParams(
            dimension_semantics=("parallel",), vmem_limit_bytes=VMEM_LIMIT),
        name="compress",
    )(k16, v16, pe, wk, wv, w2k, w2v, cos_c, sin_c)


def _fold_rows(x, op):
    parts = [x[8 * i:8 * (i + 1)] for i in range(x.shape[0] // 8)]
    while len(parts) > 1:
        parts = [op(parts[i], parts[i + 1]) for i in range(0, len(parts), 2)]
    return parts[0]


def _attn_kernel(qt_ref, kselx_ref, vselt_ref, kwinx_ref, vwint_ref, kc_ref, vct_ref, gt_ref,
                 ga_ref, mb_ref, h_ref, gate_ref, ovt_ref, bias_ref, pa_ref, wo_ref,
                 o_ref, qx_scr, s_scr, acc_scr, yt_scr):
    TQ = h_ref.shape[1]
    CK = ATT_CHUNK
    NC = kc_ref.shape[2]
    NSEL = kselx_ref.shape[2] // SEL_BLOCK
    nwin = WINDOW // CK
    qi = pl.program_id(1)

    def chunk_loop(first, one, carry):
        pairs = jnp.right_shift(qi + 1 - first, 1)

        def two(i, car):
            c = first + 2 * i
            return one(c + 1, one(c, car))

        carry = lax.fori_loop(0, pairs, two, carry)
        return lax.fori_loop(first + 2 * pairs, qi + 1, one, carry)

    def branch(kx_ref, vt_ref, g, first, far, gate_row):
        def score(c, mparts):
            kx = kx_ref[0, g, pl.ds(pl.multiple_of(c * CK, CK), CK), :]
            bias = bias_ref[jnp.where(c == qi, 1, jnp.where(c == far, 2, 0))]
            out = []
            for hh in range(HPG):
                s = jnp.dot(kx, qx_scr[hh], preferred_element_type=F32) + bias
                s_scr[c, hh] = s
                out.append(jnp.maximum(mparts[hh], _fold_rows(s, jnp.maximum)))
            return tuple(out)

        mparts = chunk_loop(first, score, tuple(jnp.full((8, TQ), NEG, F32) for _ in range(HPG)))
        m = [jnp.max(mp, axis=0, keepdims=True) for mp in mparts]

        acc_scr[...] = jnp.zeros_like(acc_scr)

        def weigh(c, lparts):
            vt = vt_ref[0, g, c]
            out = []
            for hh in range(HPG):
                p = jnp.exp2(s_scr[c, hh] - m[hh])
                out.append(lparts[hh] + _fold_rows(p, jnp.add))
                acc_scr[hh] = acc_scr[hh] + jnp.dot(vt, p.astype(BF16),
                                                    preferred_element_type=F32)
            return tuple(out)

        lparts = chunk_loop(first, weigh, tuple(jnp.zeros((8, TQ), F32) for _ in range(HPG)))
        for hh in range(HPG):
            hd = g * HPG + hh
            w = gt_ref[0, 3 * hd + gate_row:3 * hd + gate_row + 1, :] / jnp.sum(
                lparts[hh], axis=0, keepdims=True)
            rows = slice(hd * HEAD_DIM, (hd + 1) * HEAD_DIM)
            yt_scr[rows, :] = yt_scr[rows, :] + w * acc_scr[hh]

    n_idx = lax.broadcasted_iota(jnp.int32, (NC, TQ), 0)
    t_cmp = qi * TQ + lax.broadcasted_iota(jnp.int32, (NC, TQ), 1)
    vis = (n_idx * CMP_STRIDE + (CMP_BLOCK - 1)) <= t_cmp
    visf = jnp.where(vis, 1.0, 0.0)
    j_idx = lax.broadcasted_iota(jnp.int32, (NSEL, TQ), 0)
    t_sel = qi * TQ + lax.broadcasted_iota(jnp.int32, (NSEL, TQ), 1)
    cur = jnp.right_shift(t_sel, SEL_BLOCK.bit_length() - 1)
    forced = (j_idx == 0) | (j_idx == cur) | (j_idx == cur - 1)
    valid = j_idx <= cur

    for g in range(KV_GROUPS):
        psum = jnp.zeros((NC, TQ), F32)
        for hh in range(HPG):
            hd = g * HPG + hh
            qt = qt_ref[0, hd]
            sc = jnp.where(vis, jnp.dot(kc_ref[0, g], qt, preferred_element_type=F32), NEG)
            ec = jnp.exp2(sc - jnp.max(sc, axis=0, keepdims=True)) * visf
            den = jnp.sum(ec, axis=0, keepdims=True)
            pc = ec / jnp.where(den > 0.0, den, 1.0)
            psum = psum + pc
            oc = jnp.dot(vct_ref[0, g], pc.astype(BF16), preferred_element_type=F32)
            yt_scr[hd * HEAD_DIM:(hd + 1) * HEAD_DIM, :] = gt_ref[0, 3 * hd:3 * hd + 1, :] * oc

        imp = jnp.dot(ovt_ref[...], psum, preferred_element_type=F32,
                      precision=lax.Precision.HIGHEST)
        imp = jnp.where(forced, FORCE, jnp.where(valid, imp, -FORCE))
        rank = jnp.zeros((NSEL, TQ), jnp.int32)
        for kk in range(NSEL):
            rk = imp[kk:kk + 1, :]
            ahead = (rk > imp) | ((rk == imp) & (j_idx > kk))
            rank = rank + jnp.where(ahead, 1, 0)
        selneg = jnp.where(rank < SEL_TOPK, 0.0, NEG).astype(BF16)
        pad = jnp.zeros((LANE - HEAD_DIM - NSEL, TQ), BF16)
        for hh in range(HPG):
            qx_scr[hh] = jnp.concatenate([qt_ref[0, g * HPG + hh], selneg, pad], axis=0)

        branch(kselx_ref, vselt_ref, g, 0, -1, 1)
        branch(kwinx_ref, vwint_ref, g, jnp.maximum(qi - nwin, 0), qi - nwin, 2)

    y = jnp.transpose(yt_scr[...]).astype(BF16)
    ya = jnp.dot(y, pa_ref[...], preferred_element_type=F32)
    merged = (ga_ref[0] * ya + mb_ref[0]).astype(BF16)
    o_ref[0] = h_ref[0] + gate_ref[0] * jnp.dot(merged, wo_ref[...],
                                                preferred_element_type=F32)


def _attn(qt, kselx, vselt, kwinx, vwint, kc, vct, gt, ga, mb, h, gate, ovt, bias,
          proj_a, w_out):
    B, S, D = h.shape
    TQ = ATT_ROWS
    CK = ATT_CHUNK
    assert TQ == CK and WINDOW % CK == 0 and S % TQ == 0

    def const(a):
        nd = a.ndim
        return pl.BlockSpec(a.shape, lambda b, i: (0,) * nd)

    def per_batch(a):
        nd = a.ndim
        return pl.BlockSpec((1,) + a.shape[1:], lambda b, i: (b,) + (0,) * (nd - 1))

    row = lambda w: pl.BlockSpec((1, TQ, w), lambda b, i: (b, i, 0))
    return pl.pallas_call(
        _attn_kernel,
        grid=(B, S // TQ),
        in_specs=[
            pl.BlockSpec((1, NSA_HEADS, HEAD_DIM, TQ), lambda b, i: (b, 0, 0, i)),
            per_batch(kselx), per_batch(vselt), per_batch(kwinx), per_batch(vwint),
            per_batch(kc), per_batch(vct),
            pl.BlockSpec((1, GATE_ROWS, TQ), lambda b, i: (b, 0, i)),
            row(D), row(D), row(D),
            pl.BlockSpec((1, 1, D), lambda b, i: (b, 0, 0)),
            const(ovt), const(bias), const(proj_a), const(w_out)],
        out_specs=row(D),
        out_shape=jax.ShapeDtypeStruct((B, S, D), F32),
        scratch_shapes=[
            pltpu.VMEM((HPG, LANE, TQ), BF16),
            pltpu.VMEM((S // CK, HPG, CK, TQ), F32),
            pltpu.VMEM((HPG, HEAD_DIM, TQ), F32),
            pltpu.VMEM((NSA_WIDTH, TQ), F32)],
        compiler_params=pltpu.CompilerParams(
            dimension_semantics=("parallel", "parallel"), vmem_limit_bytes=VMEM_LIMIT),
        name="attn",
    )(qt, kselx, vselt, kwinx, vwint, kc, vct, gt, ga, mb, h, gate, ovt, bias,
      proj_a, w_out)


def _final_kernel(h_ref, g_ref, o_ref):
    x = h_ref[0]
    o_ref[0] = x * lax.rsqrt(jnp.mean(x * x, axis=-1, keepdims=True) + EPS) * g_ref[...]


def _final(h, g):
    B, S, D = h.shape
    tm = FFN_ROWS
    row = pl.BlockSpec((1, tm, D), lambda b, i: (b, i, 0))
    return pl.pallas_call(
        _final_kernel,
        grid=(B, S // tm),
        in_specs=[row, pl.BlockSpec((1, D), lambda b, i: (0, 0))],
        out_specs=row,
        out_shape=jax.ShapeDtypeStruct((B, S, D), F32),
        compiler_params=pltpu.CompilerParams(
            dimension_semantics=("parallel", "parallel"), vmem_limit_bytes=VMEM_LIMIT),
        name="final_norm",
    )(h, g)


def _rope_tables(pos):
    inv = 1.0 / (ROPE_THETA ** (np.arange(0, HEAD_DIM, 2, dtype=np.float64) / HEAD_DIM))
    ang = np.asarray(pos, np.float64)[:, None] * inv[None, :]
    cos = np.concatenate([np.cos(ang), np.cos(ang)], axis=1)
    sin = np.concatenate([-np.sin(ang), np.sin(ang)], axis=1)
    return cos.astype(np.float32), sin.astype(np.float32)


def _tables(S):
    cos, sin = _rope_tables(np.arange(S))
    cos_t = np.concatenate([cos, cos], axis=1)
    sin_t = np.concatenate([sin, sin], axis=1)
    n_cmp_pad = S // CMP_STRIDE
    starts = np.arange(n_cmp_pad) * CMP_STRIDE
    cos_c, sin_c = _rope_tables(starts + CMP_BLOCK - 1)
    n_sel = S // SEL_BLOCK
    sel_start = np.arange(n_sel) * SEL_BLOCK
    overlap = np.clip(np.minimum(starts[:, None] + CMP_BLOCK, sel_start[None, :] + SEL_BLOCK)
                      - np.maximum(starts[:, None], sel_start[None, :]), 0, None) / CMP_BLOCK
    ovt = np.ascontiguousarray(overlap.T).astype(np.float32)
    oh = np.zeros((S, LANE), np.float32)
    oh[np.arange(S), HEAD_DIM + np.arange(S) // SEL_BLOCK] = 1.0
    j = np.arange(ATT_CHUNK)[:, None]
    i = np.arange(ATT_ROWS)[None, :]
    bias = np.stack([np.zeros((ATT_CHUNK, ATT_ROWS)),
                     np.where(j <= i, 0.0, NEG),
                     np.where(j > i, 0.0, NEG)]
                    ).astype(np.float32)
    return tuple(jnp.asarray(a) for a in (cos_t, sin_t, cos_c, sin_c, ovt, oh, bias))


def _compress_weights(w1):
    eye = jnp.eye(KV_GROUPS, dtype=w1.dtype)
    half = CMP_BLOCK // 2

    def one(w):
        return jnp.einsum('ldf,gh->lgdhf', w, eye).reshape(half * KV_GROUPS * HEAD_DIM,
                                                           KV_GROUPS * CMP_HIDDEN)

    return jnp.stack([one(w1[:half]), one(w1[half:])]).astype(BF16)


def _compress_pe(pe):
    half = CMP_BLOCK // 2

    def one(p):
        return jnp.broadcast_to(p[:, None, :], (half, KV_GROUPS, HEAD_DIM)).reshape(1, -1)

    return one(pe[:half]), one(pe[half:])


def kernel(x, c, ada_w, ada_b, norm_g, ffn_w_in, ffn_w_out, mix_w_in, cmp_pe, cmp_w1, cmp_w2,
           gm_ln_g, gm_ln_b, gm_ws, gm_bs, proj_a, proj_b, w_out, final_g):
    B, S, D = x.shape
    L = ada_w.shape[0]
    assert S // SEL_BLOCK + HEAD_DIM <= LANE and 3 * NSA_HEADS <= GATE_ROWS
    cos_t, sin_t, cos_c, sin_c, ovt, oh_t, bias = _tables(S)
    swap = np.concatenate([np.arange(HEAD_DIM // 2, HEAD_DIM), np.arange(HEAD_DIM // 2)])

    mods = _ada(c, ada_w, ada_b).reshape(L, B, 3, 3, 1, D)
    h = x
    for l in range(L):
        mod = mods[l]
        h = _ffn(h, mod[:, 0, 0], mod[:, 0, 1], mod[:, 0, 2], norm_g[l, 0][None],
                 ffn_w_in[l, 0].astype(BF16), ffn_w_out[l, 0].astype(BF16))

        w = mix_w_in[l]
        o_kv = NSA_WIDTH
        o_g = o_kv + 6 * KV_GROUPS * HEAD_DIM
        o_uv = o_g + 3 * NSA_HEADS
        o_ga = o_uv + 2 * GM_WIDTH
        o_gb = o_ga + D
        wq = (w[:, :o_kv] * (HEAD_DIM ** -0.5 * LOG2E)).astype(BF16)
        wkv = w[:, o_kv:o_g].astype(BF16)
        wg = jnp.pad(w[:, o_g:o_uv], ((0, 0), (0, LANE - 3 * NSA_HEADS))).astype(BF16)
        wuv = w[:, o_uv:o_ga].astype(BF16)
        wga = w[:, o_ga:o_gb].astype(BF16)
        wgb = w[:, o_gb:].astype(BF16)
        bsx = jnp.repeat(gm_bs[l].T, HEAD_DIM, axis=1)
        qt, kcmp, vcmp, kselx, vselt, kwinx, vwint, gt, ga, mb = _mix_in(
            h, mod[:, 1, 0], mod[:, 1, 1], norm_g[l, 1][None], wq, wkv, wg, wuv, wga, wgb,
            cos_t, sin_t, oh_t, gm_ln_g[l][None], gm_ln_b[l][None], gm_ws[l], bsx,
            proj_b[l].astype(BF16))

        pe_ka, pe_kb = _compress_pe(cmp_pe[l, 0])
        pe_va, pe_vb = _compress_pe(cmp_pe[l, 1])
        pe = jnp.stack([pe_ka, pe_kb, pe_va, pe_vb])
        w2k = jnp.concatenate([cmp_w2[l, 0], cmp_w2[l, 0][:, swap]], axis=1).astype(BF16)
        w2v = jnp.pad(cmp_w2[l, 1], ((0, 0), (0, LANE - HEAD_DIM))).astype(BF16)
        nc = S // CMP_STRIDE
        kc, vct = _compress(kcmp.reshape(B, nc, CMP_STRIDE * LANE),
                            vcmp.reshape(B, nc, CMP_STRIDE * LANE), pe,
                            _compress_weights(cmp_w1[l, 0]), _compress_weights(cmp_w1[l, 1]),
                            w2k, w2v, cos_c, sin_c)

        h = _attn(qt, kselx, vselt, kwinx, vwint, kc, vct, gt, ga, mb, h, mod[:, 1, 2], ovt,
                  bias, proj_a[l].astype(BF16), w_out[l].astype(BF16))

        h = _ffn(h, mod[:, 2, 0], mod[:, 2, 1], mod[:, 2, 2], norm_g[l, 2][None],
                 ffn_w_in[l, 1].astype(BF16), ffn_w_out[l, 1].astype(BF16))
    return _final(h, final_g[None])
```

```python
import numpy as np
import jax
import jax.numpy as jnp
from jax import lax
from jax.experimental import pallas as pl
from jax.experimental.pallas import tpu as pltpu

F32 = jnp.float32
BF16 = jnp.bfloat16

HEAD_DIM = 64
NSA_HEADS = 8
KV_GROUPS = 2
HPG = NSA_HEADS // KV_GROUPS
NSA_WIDTH = NSA_HEADS * HEAD_DIM
CMP_BLOCK = 32
CMP_STRIDE = 16
CMP_HIDDEN = 128
SEL_BLOCK = 64
SEL_TOPK = 16
WINDOW = 512
GM_GROUPS = 8
GM_CHUNK = 128
GM_WIDTH = GM_GROUPS * HEAD_DIM
ROPE_THETA = 10000.0
EPS = 1e-6
NEG = -1e30
FORCE = 1e4
LANE = 128
GELU_C = float(np.sqrt(2.0 / np.pi))
LOG2E = float(np.log2(np.e))
GATE_ROWS = 32
V_ROWS = HEAD_DIM + 16

FFN_ROWS = 512
MIX_ROWS = 512
ATT_ROWS = 256
ATT_CHUNK = 256
VMEM_LIMIT = 56 * 1024 * 1024


def _sigmoid(x):
    return 1.0 / (1.0 + jnp.exp(-x))


def _silu(x):
    return x * _sigmoid(x)


def _rms_mod(x, g, shift, scale):
    y = x * lax.rsqrt(jnp.mean(x * x, axis=-1, keepdims=True) + EPS) * g
    return y * (1.0 + scale) + shift


def _ada_kernel(c_ref, w_ref, b_ref, o_ref):
    s = _silu(c_ref[...])
    o_ref[0] = jnp.dot(s, w_ref[0], preferred_element_type=F32,
                       precision=lax.Precision.HIGHEST) + b_ref[0]


def _ada(c, ada_w, ada_b):
    L, D, N = ada_w.shape
    B = c.shape[0]
    tn = 1024
    return pl.pallas_call(
        _ada_kernel,
        grid=(L, N // tn),
        in_specs=[
            pl.BlockSpec((B, D), lambda l, j: (0, 0)),
            pl.BlockSpec((1, D, tn), lambda l, j: (l, 0, j)),
            pl.BlockSpec((1, 1, tn), lambda l, j: (l, 0, j)),
        ],
        out_specs=pl.BlockSpec((1, B, tn), lambda l, j: (l, 0, j)),
        out_shape=jax.ShapeDtypeStruct((L, B, N), F32),
        compiler_params=pltpu.CompilerParams(
            dimension_semantics=("parallel", "parallel"), vmem_limit_bytes=VMEM_LIMIT),
        name="ada",
    )(c, ada_w, ada_b.reshape(L, 1, N))


def _ffn_kernel(h_ref, shift_ref, scale_ref, gate_ref, g_ref, wa_ref, wb_ref, wo_ref,
                o_ref, n_scr, acc_scr):
    j = pl.program_id(2)

    @pl.when(j == 0)
    def _():
        n = _rms_mod(h_ref[0], g_ref[...], shift_ref[0], scale_ref[0])
        n_scr[...] = n.astype(BF16)
        acc_scr[...] = jnp.zeros_like(acc_scr)

    n = n_scr[...]
    a = jnp.dot(n, wa_ref[...], preferred_element_type=F32)
    b = jnp.dot(n, wb_ref[...], preferred_element_type=F32)
    hm = (_silu(a) * b).astype(BF16)
    acc_scr[...] += jnp.dot(hm, wo_ref[...], preferred_element_type=F32)

    @pl.when(j == pl.num_programs(2) - 1)
    def _():
        o_ref[0] = h_ref[0] + (0.5 * gate_ref[0]) * acc_scr[...]


def _ffn(h, shift, scale, gate, g, w_in, w_out):
    B, S, D = h.shape
    F = w_out.shape[0]
    tm = FFN_ROWS
    nf = 2
    tf = F // nf
    row = pl.BlockSpec((1, tm, D), lambda b, i, j: (b, i, 0))
    vec = pl.BlockSpec((1, 1, D), lambda b, i, j: (b, 0, 0))
    return pl.pallas_call(
        _ffn_kernel,
        grid=(B, S // tm, nf),
        in_specs=[
            row, vec, vec, vec,
            pl.BlockSpec((1, D), lambda b, i, j: (0, 0)),
            pl.BlockSpec((D, tf), lambda b, i, j: (0, j)),
            pl.BlockSpec((D, tf), lambda b, i, j: (0, j + nf)),
            pl.BlockSpec((tf, D), lambda b, i, j: (j, 0)),
        ],
        out_specs=row,
        out_shape=jax.ShapeDtypeStruct((B, S, D), F32),
        scratch_shapes=[pltpu.VMEM((tm, D), BF16), pltpu.VMEM((tm, D), F32)],
        compiler_params=pltpu.CompilerParams(
            dimension_semantics=("parallel", "parallel", "arbitrary"),
            vmem_limit_bytes=VMEM_LIMIT),
        name="ffn",
    )(h, shift, scale, gate, g, w_in, w_in, w_out)


def _rope_pair(x, c, s):
    lane = lax.broadcasted_iota(jnp.int32, x.shape, 1)
    first_half = (lane & (HEAD_DIM - 1)) < HEAD_DIM // 2
    swapped = jnp.where(first_half, pltpu.roll(x, LANE - HEAD_DIM // 2, 1),
                        pltpu.roll(x, HEAD_DIM // 2, 1))
    return x * c + swapped * s


def _mix_in_kernel(h_ref, shift_ref, scale_ref, g_ref, wq_ref, wkv_ref, wg_ref, wuv_ref,
                   wga_ref, wgb_ref, cos_ref, sin_ref, oh_ref, lng_ref, lnb_ref, ws_ref,
                   bsx_ref, pb_ref,
                   qt_ref, kcmp_ref, vcmp_ref, kselx_ref, vselt_ref, kwinx_ref, vwint_ref,
                   gt_ref, ga_ref, mb_ref):
    tm = h_ref.shape[1]
    CK = vselt_ref.shape[4]
    n = _rms_mod(h_ref[0], g_ref[...], shift_ref[0], scale_ref[0]).astype(BF16)
    cos = cos_ref[...]
    sin = sin_ref[...]
    low_lanes = lax.broadcasted_iota(jnp.int32, (tm, LANE), 1) < HEAD_DIM

    q = jnp.dot(n, wq_ref[...], preferred_element_type=F32)
    for p in range(NSA_HEADS // 2):
        qp = jnp.transpose(_rope_pair(q[:, p * LANE:(p + 1) * LANE], cos, sin)).astype(BF16)
        qt_ref[0, 2 * p] = qp[:HEAD_DIM]
        qt_ref[0, 2 * p + 1] = qp[HEAD_DIM:]

    kv = jnp.dot(n, wkv_ref[...], preferred_element_type=F32)
    kcmp_ref[0] = kv[:, 0 * LANE:1 * LANE]
    vcmp_ref[0] = kv[:, 1 * LANE:2 * LANE]
    for idx, ref, ext in ((2, kselx_ref, oh_ref[...]), (4, kwinx_ref, 0.0)):
        k2 = _rope_pair(kv[:, idx * LANE:(idx + 1) * LANE], cos, sin)
        ref[0, 0] = jnp.where(low_lanes, k2, ext).astype(BF16)
        ref[0, 1] = jnp.where(low_lanes, pltpu.roll(k2, HEAD_DIM, 1), ext).astype(BF16)
    ones_row = jnp.where(lax.broadcasted_iota(jnp.int32, (V_ROWS - HEAD_DIM, CK), 0) == 0,
                         1.0, 0.0).astype(BF16)
    for idx, ref in ((3, vselt_ref), (5, vwint_ref)):
        vt = jnp.transpose(kv[:, idx * LANE:(idx + 1) * LANE]).astype(BF16)
        for g in range(KV_GROUPS):
            for r in range(tm // CK):
                ref[0, g, r] = jnp.concatenate(
                    [vt[g * HEAD_DIM:(g + 1) * HEAD_DIM, r * CK:(r + 1) * CK], ones_row], axis=0)

    gates = _sigmoid(jnp.dot(n, wg_ref[...], preferred_element_type=F32))
    gt_ref[0] = jnp.transpose(gates)[:GATE_ROWS]
    ga_ref[0] = _sigmoid(jnp.dot(n, wga_ref[...], preferred_element_type=F32))

    uv = jnp.dot(n, wuv_ref[...], preferred_element_type=F32)
    ge = uv * (0.5 * (1.0 + jnp.tanh(GELU_C * (uv + 0.044715 * (uv * uv * uv)))))
    u = ge[:, :GM_WIDTH]
    v = ge[:, GM_WIDTH:]
    mu = jnp.mean(v, axis=-1, keepdims=True)
    var = jnp.mean(jnp.square(v - mu), axis=-1, keepdims=True)
    vln = ((v - mu) * lax.rsqrt(var + EPS) * lng_ref[...] + lnb_ref[...]).astype(BF16)

    ti = lax.broadcasted_iota(jnp.int32, (GM_CHUNK, GM_CHUNK), 0)
    si = lax.broadcasted_iota(jnp.int32, (GM_CHUNK, GM_CHUNK), 1)
    tril = si <= ti
    wm = [jnp.where(tril, ws_ref[gg], 0.0).astype(BF16) for gg in range(GM_GROUPS)]
    low = lax.broadcasted_iota(jnp.int32, (GM_CHUNK, LANE), 1) < HEAD_DIM
    bsx = bsx_ref[...]
    yb_rows = []
    for r in range(tm // GM_CHUNK):
        vch = vln[r * GM_CHUNK:(r + 1) * GM_CHUNK]
        pieces = []
        for p in range(GM_GROUPS // 2):
            vp = vch[:, p * LANE:(p + 1) * LANE]
            a0 = jnp.dot(wm[2 * p], vp, preferred_element_type=F32)
            a1 = jnp.dot(wm[2 * p + 1], vp, preferred_element_type=F32)
            pieces.append(jnp.where(low, a0, a1))
        sv = jnp.concatenate(pieces, axis=1) + bsx
        yb_rows.append(u[r * GM_CHUNK:(r + 1) * GM_CHUNK] * sv)
    yb = jnp.concatenate(yb_rows, axis=0).astype(BF16)
    gb = _sigmoid(jnp.dot(n, wgb_ref[...], preferred_element_type=F32))
    mb_ref[0] = gb * jnp.dot(yb, pb_ref[...], preferred_element_type=F32)


def _mix_in(h, shift, scale, g, wq, wkv, wg, wuv, wga, wgb, cos_t, sin_t, oh_t, ln_g, ln_b, ws,
            bsx, proj_b):
    B, S, D = h.shape
    tm = MIX_ROWS
    G = KV_GROUPS
    CK = ATT_CHUNK

    def const(a):
        nd = a.ndim
        return pl.BlockSpec(a.shape, lambda b, i: (0,) * nd)

    row = lambda w: pl.BlockSpec((1, tm, w), lambda b, i: (b, i, 0))
    vec = pl.BlockSpec((1, 1, D), lambda b, i: (b, 0, 0))
    tab = pl.BlockSpec((tm, LANE), lambda b, i: (i, 0))
    kspec = pl.BlockSpec((1, G, tm, LANE), lambda b, i: (b, 0, i, 0))
    kshape = jax.ShapeDtypeStruct((B, G, S, LANE), BF16)
    vspec = pl.BlockSpec((1, G, tm // CK, V_ROWS, CK), lambda b, i: (b, 0, i, 0, 0))
    vshape = jax.ShapeDtypeStruct((B, G, S // CK, V_ROWS, CK), BF16)
    return pl.pallas_call(
        _mix_in_kernel,
        grid=(B, S // tm),
        in_specs=[row(D), vec, vec, const(g), const(wq), const(wkv), const(wg), const(wuv),
                  const(wga), const(wgb), tab, tab, tab, const(ln_g), const(ln_b), const(ws),
                  const(bsx), const(proj_b)],
        out_specs=[
            pl.BlockSpec((1, NSA_HEADS, HEAD_DIM, tm), lambda b, i: (b, 0, 0, i)),
            row(LANE), row(LANE), kspec, vspec, kspec, vspec,
            pl.BlockSpec((1, GATE_ROWS, tm), lambda b, i: (b, 0, i)),
            row(D), row(D)],
        out_shape=[
            jax.ShapeDtypeStruct((B, NSA_HEADS, HEAD_DIM, S), BF16),
            jax.ShapeDtypeStruct((B, S, LANE), F32), jax.ShapeDtypeStruct((B, S, LANE), F32),
            kshape, vshape, kshape, vshape,
            jax.ShapeDtypeStruct((B, GATE_ROWS, S), F32),
            jax.ShapeDtypeStruct((B, S, D), F32), jax.ShapeDtypeStruct((B, S, D), F32)],
        compiler_params=pltpu.CompilerParams(
            dimension_semantics=("parallel", "parallel"), vmem_limit_bytes=VMEM_LIMIT),
        name="mix_in",
    )(h, shift, scale, g, wq, wkv, wg, wuv, wga, wgb, cos_t, sin_t, oh_t, ln_g, ln_b, ws, bsx,
      proj_b)


def _compress_kernel(k16_ref, v16_ref, pe_ref, wk_ref, wv_ref, w2k_ref, w2v_ref, cos_ref,
                     sin_ref, kc_ref, vct_ref):
    def hidden(x16, pe_a, pe_b, w_ref):
        ha = jnp.dot((x16 + pe_a).astype(BF16), w_ref[0], preferred_element_type=F32)
        hb = jnp.dot((x16 + pe_b).astype(BF16), w_ref[1], preferred_element_type=F32)
        return _silu(ha + pltpu.roll(hb, hb.shape[0] - 1, 0))

    hk = hidden(k16_ref[0], pe_ref[0], pe_ref[1], wk_ref)
    hv = hidden(v16_ref[0], pe_ref[2], pe_ref[3], wv_ref)
    for g in range(KV_GROUPS):
        k2 = jnp.dot(hk[:, g * CMP_HIDDEN:(g + 1) * CMP_HIDDEN].astype(BF16), w2k_ref[...],
                     preferred_element_type=F32)
        kc = k2[:, :HEAD_DIM] * cos_ref[...] + k2[:, HEAD_DIM:] * sin_ref[...]
        kc_ref[0, g] = kc.astype(BF16)
        vc = jnp.dot(hv[:, g * CMP_HIDDEN:(g + 1) * CMP_HIDDEN].astype(BF16), w2v_ref[...],
                     preferred_element_type=F32)
        vct_ref[0, g] = jnp.transpose(vc)[:HEAD_DIM].astype(BF16)


def _compress(k16, v16, pe, wk, wv, w2k, w2v, cos_c, sin_c):
    B, NC, W = k16.shape

    def const(a):
        nd = a.ndim
        return pl.BlockSpec(a.shape, lambda b: (0,) * nd)

    blk = pl.BlockSpec((1, NC, W), lambda b: (b, 0, 0))
    return pl.pallas_call(
        _compress_kernel,
        grid=(B,),
        in_specs=[blk, blk, const(pe), const(wk), const(wv), const(w2k), const(w2v),
                  const(cos_c), const(sin_c)],
        out_specs=[pl.BlockSpec((1, KV_GROUPS, NC, HEAD_DIM), lambda b: (b, 0, 0, 0)),
                   pl.BlockSpec((1, KV_GROUPS, HEAD_DIM, NC), lambda b: (b, 0, 0, 0))],
        out_shape=[jax.ShapeDtypeStruct((B, KV_GROUPS, NC, HEAD_DIM), BF16),
                   jax.ShapeDtypeStruct((B, KV_GROUPS, HEAD_DIM, NC), BF16)],
        compiler_params=pltpu.CompilerParams(
            dimension_semantics=("parallel",), vmem_limit_bytes=VMEM_LIMIT),
        name="compress",
    )(k16, v16, pe, wk, wv, w2k, w2v, cos_c, sin_c)


def _fold_rows(x, op):
    parts = [x[8 * i:8 * (i + 1)] for i in range(x.shape[0] // 8)]
    while len(parts) > 1:
        parts = [op(parts[i], parts[i + 1]) for i in range(0, len(parts), 2)]
    return parts[0]


def _attn_kernel(qt_ref, kselx_ref, vselt_ref, kwinx_ref, vwint_ref, kc_ref, vct_ref, gt_ref,
                 ga_ref, mb_ref, h_ref, gate_ref, ovt_ref, bias_ref, pa_ref, wo_ref,
                 o_ref, qx_scr, s_scr, m_scr, a_scr, acc_scr, yt_scr):
    TQ = h_ref.shape[1]
    CK = ATT_CHUNK
    NC = kc_ref.shape[2]
    NSEL = kselx_ref.shape[2] // SEL_BLOCK
    nwin = WINDOW // CK
    qi = pl.program_id(1)

    def branch(kx_ref, vt_ref, first, far, gate_row):
        def score(c, masked):
            kx = [kx_ref[0, g, pl.ds(pl.multiple_of(c * CK, CK), CK), :]
                  for g in range(KV_GROUPS)]
            if masked:
                bias = bias_ref[jnp.where(c == qi, 1, jnp.where(c == far, 2, 0))]
            for hd in range(NSA_HEADS):
                s = jnp.dot(kx[hd // HPG], qx_scr[hd], preferred_element_type=F32)
                if masked:
                    s = s + bias
                s_scr[hd] = s
                m_prev = m_scr[hd]
                m_new = jnp.maximum(m_prev, jnp.max(_fold_rows(s, jnp.maximum), axis=0,
                                                    keepdims=True))
                a_scr[hd] = jnp.exp2(m_prev - m_new)
                m_scr[hd] = m_new

        def weigh(c):
            vt = [vt_ref[0, g, c] for g in range(KV_GROUPS)]
            for hd in range(NSA_HEADS):
                p = jnp.exp2(s_scr[hd] - m_scr[hd]).astype(BF16)
                acc_scr[hd] = a_scr[hd] * acc_scr[hd] + jnp.dot(vt[hd // HPG], p,
                                                                preferred_element_type=F32)

        def loop(lo, hi, body):
            def wrapped(c, carry):
                body(c)
                return carry
            lax.fori_loop(lo, hi, wrapped, 0)

        def weigh_then_score(masked):
            def body(c):
                weigh(c)
                score(c + 1, masked)
            return body

        m_scr[...] = jnp.full_like(m_scr, NEG)
        acc_scr[...] = jnp.zeros_like(acc_scr)
        loop(first, first + 1, lambda c: score(c, True))
        loop(first, qi - 1, weigh_then_score(False))
        loop(jnp.maximum(qi - 1, first), qi, weigh_then_score(True))
        loop(qi, qi + 1, weigh)
        for hd in range(NSA_HEADS):
            w = gt_ref[0, 3 * hd + gate_row:3 * hd + gate_row + 1, :] / acc_scr[
                hd, HEAD_DIM:HEAD_DIM + 1, :]
            rows = slice(hd * HEAD_DIM, (hd + 1) * HEAD_DIM)
            yt_scr[rows, :] = yt_scr[rows, :] + w * acc_scr[hd, :HEAD_DIM, :]

    n_idx = lax.broadcasted_iota(jnp.int32, (NC, TQ), 0)
    t_cmp = qi * TQ + lax.broadcasted_iota(jnp.int32, (NC, TQ), 1)
    vis = (n_idx * CMP_STRIDE + (CMP_BLOCK - 1)) <= t_cmp
    visf = jnp.where(vis, 1.0, 0.0)
    j_idx = lax.broadcasted_iota(jnp.int32, (NSEL, TQ), 0)
    t_sel = qi * TQ + lax.broadcasted_iota(jnp.int32, (NSEL, TQ), 1)
    cur = jnp.right_shift(t_sel, SEL_BLOCK.bit_length() - 1)
    forced = (j_idx == 0) | (j_idx == cur) | (j_idx == cur - 1)
    valid = j_idx <= cur

    for g in range(KV_GROUPS):
        psum = jnp.zeros((NC, TQ), F32)
        for hh in range(HPG):
            hd = g * HPG + hh
            qt = qt_ref[0, hd]
            sc = jnp.where(vis, jnp.dot(kc_ref[0, g], qt, preferred_element_type=F32), NEG)
            ec = jnp.exp2(sc - jnp.max(sc, axis=0, keepdims=True)) * visf
            den = jnp.sum(ec, axis=0, keepdims=True)
            pc = ec / jnp.where(den > 0.0, den, 1.0)
            psum = psum + pc
            oc = jnp.dot(vct_ref[0, g], pc.astype(BF16), preferred_element_type=F32)
            yt_scr[hd * HEAD_DIM:(hd + 1) * HEAD_DIM, :] = gt_ref[0, 3 * hd:3 * hd + 1, :] * oc

        imp = jnp.dot(ovt_ref[...], psum, preferred_element_type=F32,
                      precision=lax.Precision.HIGHEST)
        imp = jnp.where(forced, FORCE, jnp.where(valid, imp, -FORCE))
        rank = jnp.zeros((NSEL, TQ), jnp.int32)
        for kk in range(NSEL):
            rk = imp[kk:kk + 1, :]
            ahead = (rk > imp) | ((rk == imp) & (j_idx > kk))
            rank = rank + jnp.where(ahead, 1, 0)
        selneg = jnp.where(rank < SEL_TOPK, 0.0, NEG).astype(BF16)
        pad = jnp.zeros((LANE - HEAD_DIM - NSEL, TQ), BF16)
        for hh in range(HPG):
            hd = g * HPG + hh
            qx_scr[hd] = jnp.concatenate([qt_ref[0, hd], selneg, pad], axis=0)

    branch(kselx_ref, vselt_ref, 0, -1, 1)
    branch(kwinx_ref, vwint_ref, jnp.maximum(qi - nwin, 0), qi - nwin, 2)

    y = jnp.transpose(yt_scr[...]).astype(BF16)
    ya = jnp.dot(y, pa_ref[...], preferred_element_type=F32)
    merged = (ga_ref[0] * ya + mb_ref[0]).astype(BF16)
    o_ref[0] = h_ref[0] + gate_ref[0] * jnp.dot(merged, wo_ref[...],
                                                preferred_element_type=F32)


def _attn(qt, kselx, vselt, kwinx, vwint, kc, vct, gt, ga, mb, h, gate, ovt, bias,
          proj_a, w_out):
    B, S, D = h.shape
    TQ = ATT_ROWS
    CK = ATT_CHUNK
    assert TQ == CK and WINDOW % CK == 0 and S % TQ == 0

    def const(a):
        nd = a.ndim
        return pl.BlockSpec(a.shape, lambda b, i: (0,) * nd)

    def per_batch(a):
        nd = a.ndim
        return pl.BlockSpec((1,) + a.shape[1:], lambda b, i: (b,) + (0,) * (nd - 1))

    row = lambda w: pl.BlockSpec((1, TQ, w), lambda b, i: (b, i, 0))
    return pl.pallas_call(
        _attn_kernel,
        grid=(B, S // TQ),
        in_specs=[
            pl.BlockSpec((1, NSA_HEADS, HEAD_DIM, TQ), lambda b, i: (b, 0, 0, i)),
            per_batch(kselx), per_batch(vselt), per_batch(kwinx), per_batch(vwint),
            per_batch(kc), per_batch(vct),
            pl.BlockSpec((1, GATE_ROWS, TQ), lambda b, i: (b, 0, i)),
            row(D), row(D), row(D),
            pl.BlockSpec((1, 1, D), lambda b, i: (b, 0, 0)),
            const(ovt), const(bias), const(proj_a), const(w_out)],
        out_specs=row(D),
        out_shape=jax.ShapeDtypeStruct((B, S, D), F32),
        scratch_shapes=[
            pltpu.VMEM((NSA_HEADS, LANE, TQ), BF16),
            pltpu.VMEM((NSA_HEADS, CK, TQ), F32),
            pltpu.VMEM((NSA_HEADS, 1, TQ), F32), pltpu.VMEM((NSA_HEADS, 1, TQ), F32),
            pltpu.VMEM((NSA_HEADS, V_ROWS, TQ), F32),
            pltpu.VMEM((NSA_WIDTH, TQ), F32)],
        compiler_params=pltpu.CompilerParams(
            dimension_semantics=("parallel", "parallel"), vmem_limit_bytes=VMEM_LIMIT),
        name="attn",
    )(qt, kselx, vselt, kwinx, vwint, kc, vct, gt, ga, mb, h, gate, ovt, bias,
      proj_a, w_out)


def _final_kernel(h_ref, g_ref, o_ref):
    x = h_ref[0]
    o_ref[0] = x * lax.rsqrt(jnp.mean(x * x, axis=-1, keepdims=True) + EPS) * g_ref[...]


def _final(h, g):
    B, S, D = h.shape
    tm = FFN_ROWS
    row = pl.BlockSpec((1, tm, D), lambda b, i: (b, i, 0))
    return pl.pallas_call(
        _final_kernel,
        grid=(B, S // tm),
        in_specs=[row, pl.BlockSpec((1, D), lambda b, i: (0, 0))],
        out_specs=row,
        out_shape=jax.ShapeDtypeStruct((B, S, D), F32),
        compiler_params=pltpu.CompilerParams(
            dimension_semantics=("parallel", "parallel"), vmem_limit_bytes=VMEM_LIMIT),
        name="final_norm",
    )(h, g)


def _rope_tables(pos):
    inv = 1.0 / (ROPE_THETA ** (np.arange(0, HEAD_DIM, 2, dtype=np.float64) / HEAD_DIM))
    ang = np.asarray(pos, np.float64)[:, None] * inv[None, :]
    cos = np.concatenate([np.cos(ang), np.cos(ang)], axis=1)
    sin = np.concatenate([-np.sin(ang), np.sin(ang)], axis=1)
    return cos.astype(np.float32), sin.astype(np.float32)


def _tables(S):
    cos, sin = _rope_tables(np.arange(S))
    cos_t = np.concatenate([cos, cos], axis=1)
    sin_t = np.concatenate([sin, sin], axis=1)
    n_cmp_pad = S // CMP_STRIDE
    starts = np.arange(n_cmp_pad) * CMP_STRIDE
    cos_c, sin_c = _rope_tables(starts + CMP_BLOCK - 1)
    n_sel = S // SEL_BLOCK
    sel_start = np.arange(n_sel) * SEL_BLOCK
    overlap = np.clip(np.minimum(starts[:, None] + CMP_BLOCK, sel_start[None, :] + SEL_BLOCK)
                      - np.maximum(starts[:, None], sel_start[None, :]), 0, None) / CMP_BLOCK
    ovt = np.ascontiguousarray(overlap.T).astype(np.float32)
    oh = np.zeros((S, LANE), np.float32)
    oh[np.arange(S), HEAD_DIM + np.arange(S) // SEL_BLOCK] = 1.0
    j = np.arange(ATT_CHUNK)[:, None]
    i = np.arange(ATT_ROWS)[None, :]
    bias = np.stack([np.zeros((ATT_CHUNK, ATT_ROWS)),
                     np.where(j <= i, 0.0, NEG),
                     np.where(j > i, 0.0, NEG)]
                    ).astype(np.float32)
    return tuple(jnp.asarray(a) for a in (cos_t, sin_t, cos_c, sin_c, ovt, oh, bias))


def _compress_weights(w1):
    eye = jnp.eye(KV_GROUPS, dtype=w1.dtype)
    half = CMP_BLOCK // 2

    def one(w):
        return jnp.einsum('ldf,gh->lgdhf', w, eye).reshape(half * KV_GROUPS * HEAD_DIM,
                                                           KV_GROUPS * CMP_HIDDEN)

    return jnp.stack([one(w1[:half]), one(w1[half:])]).astype(BF16)


def _compress_pe(pe):
    half = CMP_BLOCK // 2

    def one(p):
        return jnp.broadcast_to(p[:, None, :], (half, KV_GROUPS, HEAD_DIM)).reshape(1, -1)

    return one(pe[:half]), one(pe[half:])


def kernel(x, c, ada_w, ada_b, norm_g, ffn_w_in, ffn_w_out, mix_w_in, cmp_pe, cmp_w1, cmp_w2,
           gm_ln_g, gm_ln_b, gm_ws, gm_bs, proj_a, proj_b, w_out, final_g):
    B, S, D = x.shape
    L = ada_w.shape[0]
    assert S // SEL_BLOCK + HEAD_DIM <= LANE and 3 * NSA_HEADS <= GATE_ROWS
    cos_t, sin_t, cos_c, sin_c, ovt, oh_t, bias = _tables(S)
    swap = np.concatenate([np.arange(HEAD_DIM // 2, HEAD_DIM), np.arange(HEAD_DIM // 2)])

    mods = _ada(c, ada_w, ada_b).reshape(L, B, 3, 3, 1, D)
    h = x
    for l in range(L):
        mod = mods[l]
        h = _ffn(h, mod[:, 0, 0], mod[:, 0, 1], mod[:, 0, 2], norm_g[l, 0][None],
                 ffn_w_in[l, 0].astype(BF16), ffn_w_out[l, 0].astype(BF16))

        w = mix_w_in[l]
        o_kv = NSA_WIDTH
        o_g = o_kv + 6 * KV_GROUPS * HEAD_DIM
        o_uv = o_g + 3 * NSA_HEADS
        o_ga = o_uv + 2 * GM_WIDTH
        o_gb = o_ga + D
        wq = (w[:, :o_kv] * (HEAD_DIM ** -0.5 * LOG2E)).astype(BF16)
        wkv = w[:, o_kv:o_g].astype(BF16)
        wg = jnp.pad(w[:, o_g:o_uv], ((0, 0), (0, LANE - 3 * NSA_HEADS))).astype(BF16)
        wuv = w[:, o_uv:o_ga].astype(BF16)
        wga = w[:, o_ga:o_gb].astype(BF16)
        wgb = w[:, o_gb:].astype(BF16)
        bsx = jnp.repeat(gm_bs[l].T, HEAD_DIM, axis=1)
        qt, kcmp, vcmp, kselx, vselt, kwinx, vwint, gt, ga, mb = _mix_in(
            h, mod[:, 1, 0], mod[:, 1, 1], norm_g[l, 1][None], wq, wkv, wg, wuv, wga, wgb,
            cos_t, sin_t, oh_t, gm_ln_g[l][None], gm_ln_b[l][None], gm_ws[l], bsx,
            proj_b[l].astype(BF16))

        pe_ka, pe_kb = _compress_pe(cmp_pe[l, 0])
        pe_va, pe_vb = _compress_pe(cmp_pe[l, 1])
        pe = jnp.stack([pe_ka, pe_kb, pe_va, pe_vb])
        w2k = jnp.concatenate([cmp_w2[l, 0], cmp_w2[l, 0][:, swap]], axis=1).astype(BF16)
        w2v = jnp.pad(cmp_w2[l, 1], ((0, 0), (0, LANE - HEAD_DIM))).astype(BF16)
        nc = S // CMP_STRIDE
        kc, vct = _compress(kcmp.reshape(B, nc, CMP_STRIDE * LANE),
                            vcmp.reshape(B, nc, CMP_STRIDE * LANE), pe,
                            _compress_weights(cmp_w1[l, 0]), _compress_weights(cmp_w1[l, 1]),
                            w2k, w2v, cos_c, sin_c)

        h = _attn(qt, kselx, vselt, kwinx, vwint, kc, vct, gt, ga, mb, h, mod[:, 1, 2], ovt,
                  bias, proj_a[l].astype(BF16), w_out[l].astype(BF16))

        h = _ffn(h, mod[:, 2, 0], mod[:, 2, 1], mod[:, 2, 2], norm_g[l, 2][None],
                 ffn_w_in[l, 1].astype(BF16), ffn_w_out[l, 1].astype(BF16))
    return _final(h, final_g[None])
```

```python
import numpy as np
import jax
import jax.numpy as jnp
from jax import lax
from jax.experimental import pallas as pl
from jax.experimental.pallas import tpu as pltpu

F32 = jnp.float32
BF16 = jnp.bfloat16

HEAD_DIM = 64
NSA_HEADS = 8
KV_GROUPS = 2
HPG = NSA_HEADS // KV_GROUPS
NSA_WIDTH = NSA_HEADS * HEAD_DIM
CMP_BLOCK = 32
CMP_STRIDE = 16
CMP_HIDDEN = 128
SEL_BLOCK = 64
SEL_TOPK = 16
WINDOW = 512
GM_GROUPS = 8
GM_CHUNK = 128
GM_WIDTH = GM_GROUPS * HEAD_DIM
ROPE_THETA = 10000.0
EPS = 1e-6
NEG = -1e30
FORCE = 1e4
LANE = 128
GELU_C = float(np.sqrt(2.0 / np.pi))
LOG2E = float(np.log2(np.e))
GATE_ROWS = 32
V_ROWS = HEAD_DIM + 16

FFN_ROWS = 512
MIX_ROWS = 512
ATT_ROWS = 256
ATT_CHUNK = 256
VMEM_LIMIT = 56 * 1024 * 1024


def _sigmoid(x):
    return 1.0 / (1.0 + jnp.exp(-x))


def _silu(x):
    return x * _sigmoid(x)


def _rms_mod(x, g, shift, scale):
    y = x * lax.rsqrt(jnp.mean(x * x, axis=-1, keepdims=True) + EPS) * g
    return y * (1.0 + scale) + shift


def _ada_kernel(c_ref, w_ref, b_ref, o_ref):
    s = _silu(c_ref[...])
    o_ref[0] = jnp.dot(s, w_ref[0], preferred_element_type=F32,
                       precision=lax.Precision.HIGHEST) + b_ref[0]


def _ada(c, ada_w, ada_b):
    L, D, N = ada_w.shape
    B = c.shape[0]
    tn = 1024
    return pl.pallas_call(
        _ada_kernel,
        grid=(L, N // tn),
        in_specs=[
            pl.BlockSpec((B, D), lambda l, j: (0, 0)),
            pl.BlockSpec((1, D, tn), lambda l, j: (l, 0, j)),
            pl.BlockSpec((1, 1, tn), lambda l, j: (l, 0, j)),
        ],
        out_specs=pl.BlockSpec((1, B, tn), lambda l, j: (l, 0, j)),
        out_shape=jax.ShapeDtypeStruct((L, B, N), F32),
        compiler_params=pltpu.CompilerParams(
            dimension_semantics=("parallel", "parallel"), vmem_limit_bytes=VMEM_LIMIT),
        name="ada",
    )(c, ada_w, ada_b.reshape(L, 1, N))


def _ffn_kernel(h_ref, shift_ref, scale_ref, gate_ref, g_ref, wa_ref, wb_ref, wo_ref,
                o_ref, n_scr, acc_scr):
    j = pl.program_id(2)

    @pl.when(j == 0)
    def _():
        n = _rms_mod(h_ref[0], g_ref[...], shift_ref[0], scale_ref[0])
        n_scr[...] = n.astype(BF16)
        acc_scr[...] = jnp.zeros_like(acc_scr)

    n = n_scr[...]
    a = jnp.dot(n, wa_ref[...], preferred_element_type=F32)
    b = jnp.dot(n, wb_ref[...], preferred_element_type=F32)
    hm = (_silu(a) * b).astype(BF16)
    acc_scr[...] += jnp.dot(hm, wo_ref[...], preferred_element_type=F32)

    @pl.when(j == pl.num_programs(2) - 1)
    def _():
        o_ref[0] = h_ref[0] + (0.5 * gate_ref[0]) * acc_scr[...]


def _ffn(h, shift, scale, gate, g, w_in, w_out):
    B, S, D = h.shape
    F = w_out.shape[0]
    tm = FFN_ROWS
    nf = 2
    tf = F // nf
    row = pl.BlockSpec((1, tm, D), lambda b, i, j: (b, i, 0))
    vec = pl.BlockSpec((1, 1, D), lambda b, i, j: (b, 0, 0))
    return pl.pallas_call(
        _ffn_kernel,
        grid=(B, S // tm, nf),
        in_specs=[
            row, vec, vec, vec,
            pl.BlockSpec((1, D), lambda b, i, j: (0, 0)),
            pl.BlockSpec((D, tf), lambda b, i, j: (0, j)),
            pl.BlockSpec((D, tf), lambda b, i, j: (0, j + nf)),
            pl.BlockSpec((tf, D), lambda b, i, j: (j, 0)),
        ],
        out_specs=row,
        out_shape=jax.ShapeDtypeStruct((B, S, D), F32),
        scratch_shapes=[pltpu.VMEM((tm, D), BF16), pltpu.VMEM((tm, D), F32)],
        compiler_params=pltpu.CompilerParams(
            dimension_semantics=("parallel", "parallel", "arbitrary"),
            vmem_limit_bytes=VMEM_LIMIT),
        name="ffn",
    )(h, shift, scale, gate, g, w_in, w_in, w_out)


def _rope_pair(x, c, s):
    lane = lax.broadcasted_iota(jnp.int32, x.shape, 1)
    first_half = (lane & (HEAD_DIM - 1)) < HEAD_DIM // 2
    swapped = jnp.where(first_half, pltpu.roll(x, LANE - HEAD_DIM // 2, 1),
                        pltpu.roll(x, HEAD_DIM // 2, 1))
    return x * c + swapped * s


def _mix_in_kernel(h_ref, shift_ref, scale_ref, g_ref, wq_ref, wkv_ref, wg_ref, wuv_ref,
                   wga_ref, wgb_ref, cos_ref, sin_ref, oh_ref, lng_ref, lnb_ref, ws_ref,
                   bsx_ref, pb_ref,
                   qt_ref, kcmp_ref, vcmp_ref, kselx_ref, vselt_ref, kwinx_ref, vwint_ref,
                   gt_ref, ga_ref, mb_ref):
    tm = h_ref.shape[1]
    CK = vselt_ref.shape[4]
    n = _rms_mod(h_ref[0], g_ref[...], shift_ref[0], scale_ref[0]).astype(BF16)
    cos = cos_ref[...]
    sin = sin_ref[...]
    low_lanes = lax.broadcasted_iota(jnp.int32, (tm, LANE), 1) < HEAD_DIM

    q = jnp.dot(n, wq_ref[...], preferred_element_type=F32)
    for p in range(NSA_HEADS // 2):
        qp = jnp.transpose(_rope_pair(q[:, p * LANE:(p + 1) * LANE], cos, sin)).astype(BF16)
        qt_ref[0, 2 * p] = qp[:HEAD_DIM]
        qt_ref[0, 2 * p + 1] = qp[HEAD_DIM:]

    kv = jnp.dot(n, wkv_ref[...], preferred_element_type=F32)
    kcmp_ref[0] = kv[:, 0 * LANE:1 * LANE]
    vcmp_ref[0] = kv[:, 1 * LANE:2 * LANE]
    for idx, ref, ext in ((2, kselx_ref, oh_ref[...]), (4, kwinx_ref, 0.0)):
        k2 = _rope_pair(kv[:, idx * LANE:(idx + 1) * LANE], cos, sin)
        ref[0, 0] = jnp.where(low_lanes, k2, ext).astype(BF16)
        ref[0, 1] = jnp.where(low_lanes, pltpu.roll(k2, HEAD_DIM, 1), ext).astype(BF16)
    ones_row = jnp.where(lax.broadcasted_iota(jnp.int32, (V_ROWS - HEAD_DIM, CK), 0) == 0,
                         1.0, 0.0).astype(BF16)
    for idx, ref in ((3, vselt_ref), (5, vwint_ref)):
        vt = jnp.transpose(kv[:, idx * LANE:(idx + 1) * LANE]).astype(BF16)
        for g in range(KV_GROUPS):
            for r in range(tm // CK):
                ref[0, g, r] = jnp.concatenate(
                    [vt[g * HEAD_DIM:(g + 1) * HEAD_DIM, r * CK:(r + 1) * CK], ones_row], axis=0)

    gates = _sigmoid(jnp.dot(n, wg_ref[...], preferred_element_type=F32))
    gt_ref[0] = jnp.transpose(gates)[:GATE_ROWS]
    ga_ref[0] = _sigmoid(jnp.dot(n, wga_ref[...], preferred_element_type=F32))

    uv = jnp.dot(n, wuv_ref[...], preferred_element_type=F32)
    ge = uv * (0.5 * (1.0 + jnp.tanh(GELU_C * (uv + 0.044715 * (uv * uv * uv)))))
    u = ge[:, :GM_WIDTH]
    v = ge[:, GM_WIDTH:]
    mu = jnp.mean(v, axis=-1, keepdims=True)
    var = jnp.mean(jnp.square(v - mu), axis=-1, keepdims=True)
    vln = ((v - mu) * lax.rsqrt(var + EPS) * lng_ref[...] + lnb_ref[...]).astype(BF16)

    ti = lax.broadcasted_iota(jnp.int32, (GM_CHUNK, GM_CHUNK), 0)
    si = lax.broadcasted_iota(jnp.int32, (GM_CHUNK, GM_CHUNK), 1)
    tril = si <= ti
    wm = [jnp.where(tril, ws_ref[gg], 0.0).astype(BF16) for gg in range(GM_GROUPS)]
    low = lax.broadcasted_iota(jnp.int32, (GM_CHUNK, LANE), 1) < HEAD_DIM
    bsx = bsx_ref[...]
    yb_rows = []
    for r in range(tm // GM_CHUNK):
        vch = vln[r * GM_CHUNK:(r + 1) * GM_CHUNK]
        pieces = []
        for p in range(GM_GROUPS // 2):
            vp = vch[:, p * LANE:(p + 1) * LANE]
            a0 = jnp.dot(wm[2 * p], vp, preferred_element_type=F32)
            a1 = jnp.dot(wm[2 * p + 1], vp, preferred_element_type=F32)
            pieces.append(jnp.where(low, a0, a1))
        sv = jnp.concatenate(pieces, axis=1) + bsx
        yb_rows.append(u[r * GM_CHUNK:(r + 1) * GM_CHUNK] * sv)
    yb = jnp.concatenate(yb_rows, axis=0).astype(BF16)
    gb = _sigmoid(jnp.dot(n, wgb_ref[...], preferred_element_type=F32))
    mb_ref[0] = gb * jnp.dot(yb, pb_ref[...], preferred_element_type=F32)


def _mix_in(h, shift, scale, g, wq, wkv, wg, wuv, wga, wgb, cos_t, sin_t, oh_t, ln_g, ln_b, ws,
            bsx, proj_b):
    B, S, D = h.shape
    tm = MIX_ROWS
    G = KV_GROUPS
    CK = ATT_CHUNK

    def const(a):
        nd = a.ndim
        return pl.BlockSpec(a.shape, lambda b, i: (0,) * nd)

    row = lambda w: pl.BlockSpec((1, tm, w), lambda b, i: (b, i, 0))
    vec = pl.BlockSpec((1, 1, D), lambda b, i: (b, 0, 0))
    tab = pl.BlockSpec((tm, LANE), lambda b, i: (i, 0))
    kspec = pl.BlockSpec((1, G, tm, LANE), lambda b, i: (b, 0, i, 0))
    kshape = jax.ShapeDtypeStruct((B, G, S, LANE), BF16)
    vspec = pl.BlockSpec((1, G, tm // CK, V_ROWS, CK), lambda b, i: (b, 0, i, 0, 0))
    vshape = jax.ShapeDtypeStruct((B, G, S // CK, V_ROWS, CK), BF16)
    return pl.pallas_call(
        _mix_in_kernel,
        grid=(B, S // tm),
        in_specs=[row(D), vec, vec, const(g), const(wq), const(wkv), const(wg), const(wuv),
                  const(wga), const(wgb), tab, tab, tab, const(ln_g), const(ln_b), const(ws),
                  const(bsx), const(proj_b)],
        out_specs=[
            pl.BlockSpec((1, NSA_HEADS, HEAD_DIM, tm), lambda b, i: (b, 0, 0, i)),
            row(LANE), row(LANE), kspec, vspec, kspec, vspec,
            pl.BlockSpec((1, GATE_ROWS, tm), lambda b, i: (b, 0, i)),
            row(D), row(D)],
        out_shape=[
            jax.ShapeDtypeStruct((B, NSA_HEADS, HEAD_DIM, S), BF16),
            jax.ShapeDtypeStruct((B, S, LANE), F32), jax.ShapeDtypeStruct((B, S, LANE), F32),
            kshape, vshape, kshape, vshape,
            jax.ShapeDtypeStruct((B, GATE_ROWS, S), F32),
            jax.ShapeDtypeStruct((B, S, D), F32), jax.ShapeDtypeStruct((B, S, D), F32)],
        compiler_params=pltpu.CompilerParams(
            dimension_semantics=("parallel", "parallel"), vmem_limit_bytes=VMEM_LIMIT),
        name="mix_in",
    )(h, shift, scale, g, wq, wkv, wg, wuv, wga, wgb, cos_t, sin_t, oh_t, ln_g, ln_b, ws, bsx,
      proj_b)


def _compress_kernel(k16_ref, v16_ref, pe_ref, wk_ref, wv_ref, w2k_ref, w2v_ref, cos_ref,
                     sin_ref, kc_ref, vct_ref):
    def hidden(x16, pe_a, pe_b, w_ref):
        ha = jnp.dot((x16 + pe_a).astype(BF16), w_ref[0], preferred_element_type=F32)
        hb = jnp.dot((x16 + pe_b).astype(BF16), w_ref[1], preferred_element_type=F32)
        return _silu(ha + pltpu.roll(hb, hb.shape[0] - 1, 0))

    hk = hidden(k16_ref[0], pe_ref[0], pe_ref[1], wk_ref)
    hv = hidden(v16_ref[0], pe_ref[2], pe_ref[3], wv_ref)
    for g in range(KV_GROUPS):
        k2 = jnp.dot(hk[:, g * CMP_HIDDEN:(g + 1) * CMP_HIDDEN].astype(BF16), w2k_ref[...],
                     preferred_element_type=F32)
        kc = k2[:, :HEAD_DIM] * cos_ref[...] + k2[:, HEAD_DIM:] * sin_ref[...]
        kc_ref[0, g] = kc.astype(BF16)
        vc = jnp.dot(hv[:, g * CMP_HIDDEN:(g + 1) * CMP_HIDDEN].astype(BF16), w2v_ref[...],
                     preferred_element_type=F32)
        vct_ref[0, g] = jnp.transpose(vc)[:HEAD_DIM].astype(BF16)


def _compress(k16, v16, pe, wk, wv, w2k, w2v, cos_c, sin_c):
    B, NC, W = k16.shape

    def const(a):
        nd = a.ndim
        return pl.BlockSpec(a.shape, lambda b: (0,) * nd)

    blk = pl.BlockSpec((1, NC, W), lambda b: (b, 0, 0))
    return pl.pallas_call(
        _compress_kernel,
        grid=(B,),
        in_specs=[blk, blk, const(pe), const(wk), const(wv), const(w2k), const(w2v),
                  const(cos_c), const(sin_c)],
        out_specs=[pl.BlockSpec((1, KV_GROUPS, NC, HEAD_DIM), lambda b: (b, 0, 0, 0)),
                   pl.BlockSpec((1, KV_GROUPS, HEAD_DIM, NC), lambda b: (b, 0, 0, 0))],
        out_shape=[jax.ShapeDtypeStruct((B, KV_GROUPS, NC, HEAD_DIM), BF16),
                   jax.ShapeDtypeStruct((B, KV_GROUPS, HEAD_DIM, NC), BF16)],
        compiler_params=pltpu.CompilerParams(
            dimension_semantics=("parallel",), vmem_limit_bytes=VMEM_LIMIT),
        name="compress",
    )(k16, v16, pe, wk, wv, w2k, w2v, cos_c, sin_c)


def _fold_rows(x, op):
    parts = [x[8 * i:8 * (i + 1)] for i in range(x.shape[0] // 8)]
    while len(parts) > 1:
        parts = [op(parts[i], parts[i + 1]) for i in range(0, len(parts), 2)]
    return parts[0]


def _attn_kernel(qt_ref, kselx_ref, vselt_ref, kwinx_ref, vwint_ref, kc_ref, vct_ref, gt_ref,
                 ga_ref, mb_ref, h_ref, gate_ref, ovt_ref, bias_ref, pa_ref, wo_ref,
                 o_ref, qx_scr, s_scr, m_scr, a_scr, acc_scr, yt_scr):
    TQ = h_ref.shape[1]
    CK = ATT_CHUNK
    NC = kc_ref.shape[2]
    NSEL = kselx_ref.shape[2] // SEL_BLOCK
    nwin = WINDOW // CK
    qi = pl.program_id(1)

    def branch(kx_ref, vt_ref, first, far, gate_row):
        def score(c, masked, dst):
            kx = [kx_ref[0, g, pl.ds(pl.multiple_of(c * CK, CK), CK), :]
                  for g in range(KV_GROUPS)]
            if masked:
                bias = bias_ref[jnp.where(c == qi, 1, jnp.where(c == far, 2, 0))]
            for hd in range(NSA_HEADS):
                s = jnp.dot(kx[hd // HPG], qx_scr[hd], preferred_element_type=F32)
                if masked:
                    s = s + bias
                s_scr[dst, hd] = s
                m_prev = m_scr[1 - dst, hd]
                m_new = jnp.maximum(m_prev, jnp.max(_fold_rows(s, jnp.maximum), axis=0,
                                                    keepdims=True))
                a_scr[dst, hd] = jnp.exp2(m_prev - m_new)
                m_scr[dst, hd] = m_new

        def weigh(c, src):
            vt = [vt_ref[0, g, c] for g in range(KV_GROUPS)]
            for hd in range(NSA_HEADS):
                p = jnp.exp2(s_scr[src, hd] - m_scr[src, hd]).astype(BF16)
                acc_scr[hd] = a_scr[src, hd] * acc_scr[hd] + jnp.dot(
                    vt[hd // HPG], p, preferred_element_type=F32)

        def loop(lo, hi, body):
            def wrapped(c, carry):
                body(c)
                return carry
            lax.fori_loop(lo, hi, wrapped, 0)

        def two_chunks(i):
            c = first + 2 * i
            score(c + 1, False, 1)
            weigh(c, 0)
            score(c + 2, True, 0)
            weigh(c + 1, 1)

        def last_two(c):
            score(c + 1, True, 1)
            weigh(c, 0)

        m_scr[...] = jnp.full_like(m_scr, NEG)
        acc_scr[...] = jnp.zeros_like(acc_scr)
        loop(first, first + 1, lambda c: score(c, True, 0))
        rest = qi - first
        pairs = jnp.right_shift(rest, 1)
        odd = rest & 1
        loop(0, pairs, two_chunks)
        loop(qi - 1, qi - 1 + odd, last_two)
        loop(qi, qi + odd, lambda c: weigh(c, 1))
        loop(qi, qi + 1 - odd, lambda c: weigh(c, 0))
        for hd in range(NSA_HEADS):
            w = gt_ref[0, 3 * hd + gate_row:3 * hd + gate_row + 1, :] / acc_scr[
                hd, HEAD_DIM:HEAD_DIM + 1, :]
            rows = slice(hd * HEAD_DIM, (hd + 1) * HEAD_DIM)
            yt_scr[rows, :] = yt_scr[rows, :] + w * acc_scr[hd, :HEAD_DIM, :]

    n_idx = lax.broadcasted_iota(jnp.int32, (NC, TQ), 0)
    t_cmp = qi * TQ + lax.broadcasted_iota(jnp.int32, (NC, TQ), 1)
    vis = (n_idx * CMP_STRIDE + (CMP_BLOCK - 1)) <= t_cmp
    visf = jnp.where(vis, 1.0, 0.0)
    j_idx = lax.broadcasted_iota(jnp.int32, (NSEL, TQ), 0)
    t_sel = qi * TQ + lax.broadcasted_iota(jnp.int32, (NSEL, TQ), 1)
    cur = jnp.right_shift(t_sel, SEL_BLOCK.bit_length() - 1)
    forced = (j_idx == 0) | (j_idx == cur) | (j_idx == cur - 1)
    valid = j_idx <= cur

    for g in range(KV_GROUPS):
        psum = jnp.zeros((NC, TQ), F32)
        for hh in range(HPG):
            hd = g * HPG + hh
            qt = qt_ref[0, hd]
            sc = jnp.where(vis, jnp.dot(kc_ref[0, g], qt, preferred_element_type=F32), NEG)
            ec = jnp.exp2(sc - jnp.max(sc, axis=0, keepdims=True)) * visf
            den = jnp.sum(ec, axis=0, keepdims=True)
            pc = ec / jnp.where(den > 0.0, den, 1.0)
            psum = psum + pc
            oc = jnp.dot(vct_ref[0, g], pc.astype(BF16), preferred_element_type=F32)
            yt_scr[hd * HEAD_DIM:(hd + 1) * HEAD_DIM, :] = gt_ref[0, 3 * hd:3 * hd + 1, :] * oc

        imp = jnp.dot(ovt_ref[...], psum, preferred_element_type=F32,
                      precision=lax.Precision.HIGHEST)
        imp = jnp.where(forced, FORCE, jnp.where(valid, imp, -FORCE))
        rank = jnp.zeros((NSEL, TQ), jnp.int32)
        for kk in range(NSEL):
            rk = imp[kk:kk + 1, :]
            ahead = (rk > imp) | ((rk == imp) & (j_idx > kk))
            rank = rank + jnp.where(ahead, 1, 0)
        selneg = jnp.where(rank < SEL_TOPK, 0.0, NEG).astype(BF16)
        pad = jnp.zeros((LANE - HEAD_DIM - NSEL, TQ), BF16)
        for hh in range(HPG):
            hd = g * HPG + hh
            qx_scr[hd] = jnp.concatenate([qt_ref[0, hd], selneg, pad], axis=0)

    branch(kselx_ref, vselt_ref, 0, -1, 1)
    branch(kwinx_ref, vwint_ref, jnp.maximum(qi - nwin, 0), qi - nwin, 2)

    y = jnp.transpose(yt_scr[...]).astype(BF16)
    ya = jnp.dot(y, pa_ref[...], preferred_element_type=F32)
    merged = (ga_ref[0] * ya + mb_ref[0]).astype(BF16)
    o_ref[0] = h_ref[0] + gate_ref[0] * jnp.dot(merged, wo_ref[...],
                                                preferred_element_type=F32)


def _attn(qt, kselx, vselt, kwinx, vwint, kc, vct, gt, ga, mb, h, gate, ovt, bias,
          proj_a, w_out):
    B, S, D = h.shape
    TQ = ATT_ROWS
    CK = ATT_CHUNK
    assert TQ == CK and WINDOW % CK == 0 and S % TQ == 0

    def const(a):
        nd = a.ndim
        return pl.BlockSpec(a.shape, lambda b, i: (0,) * nd)

    def per_batch(a):
        nd = a.ndim
        return pl.BlockSpec((1,) + a.shape[1:], lambda b, i: (b,) + (0,) * (nd - 1))

    row = lambda w: pl.BlockSpec((1, TQ, w), lambda b, i: (b, i, 0))
    return pl.pallas_call(
        _attn_kernel,
        grid=(B, S // TQ),
        in_specs=[
            pl.BlockSpec((1, NSA_HEADS, HEAD_DIM, TQ), lambda b, i: (b, 0, 0, i)),
            per_batch(kselx), per_batch(vselt), per_batch(kwinx), per_batch(vwint),
            per_batch(kc), per_batch(vct),
            pl.BlockSpec((1, GATE_ROWS, TQ), lambda b, i: (b, 0, i)),
            row(D), row(D), row(D),
            pl.BlockSpec((1, 1, D), lambda b, i: (b, 0, 0)),
            const(ovt), const(bias), const(proj_a), const(w_out)],
        out_specs=row(D),
        out_shape=jax.ShapeDtypeStruct((B, S, D), F32),
        scratch_shapes=[
            pltpu.VMEM((NSA_HEADS, LANE, TQ), BF16),
            pltpu.VMEM((2, NSA_HEADS, CK, TQ), F32),
            pltpu.VMEM((2, NSA_HEADS, 1, TQ), F32), pltpu.VMEM((2, NSA_HEADS, 1, TQ), F32),
            pltpu.VMEM((NSA_HEADS, V_ROWS, TQ), F32),
            pltpu.VMEM((NSA_WIDTH, TQ), F32)],
        compiler_params=pltpu.CompilerParams(
            dimension_semantics=("parallel", "parallel"), vmem_limit_bytes=VMEM_LIMIT),
        name="attn",
    )(qt, kselx, vselt, kwinx, vwint, kc, vct, gt, ga, mb, h, gate, ovt, bias,
      proj_a, w_out)


def _final_kernel(h_ref, g_ref, o_ref):
    x = h_ref[0]
    o_ref[0] = x * lax.rsqrt(jnp.mean(x * x, axis=-1, keepdims=True) + EPS) * g_ref[...]


def _final(h, g):
    B, S, D = h.shape
    tm = FFN_ROWS
    row = pl.BlockSpec((1, tm, D), lambda b, i: (b, i, 0))
    return pl.pallas_call(
        _final_kernel,
        grid=(B, S // tm),
        in_specs=[row, pl.BlockSpec((1, D), lambda b, i: (0, 0))],
        out_specs=row,
        out_shape=jax.ShapeDtypeStruct((B, S, D), F32),
        compiler_params=pltpu.CompilerParams(
            dimension_semantics=("parallel", "parallel"), vmem_limit_bytes=VMEM_LIMIT),
        name="final_norm",
    )(h, g)


def _rope_tables(pos):
    inv = 1.0 / (ROPE_THETA ** (np.arange(0, HEAD_DIM, 2, dtype=np.float64) / HEAD_DIM))
    ang = np.asarray(pos, np.float64)[:, None] * inv[None, :]
    cos = np.concatenate([np.cos(ang), np.cos(ang)], axis=1)
    sin = np.concatenate([-np.sin(ang), np.sin(ang)], axis=1)
    return cos.astype(np.float32), sin.astype(np.float32)


def _tables(S):
    cos, sin = _rope_tables(np.arange(S))
    cos_t = np.concatenate([cos, cos], axis=1)
    sin_t = np.concatenate([sin, sin], axis=1)
    n_cmp_pad = S // CMP_STRIDE
    starts = np.arange(n_cmp_pad) * CMP_STRIDE
    cos_c, sin_c = _rope_tables(starts + CMP_BLOCK - 1)
    n_sel = S // SEL_BLOCK
    sel_start = np.arange(n_sel) * SEL_BLOCK
    overlap = np.clip(np.minimum(starts[:, None] + CMP_BLOCK, sel_start[None, :] + SEL_BLOCK)
                      - np.maximum(starts[:, None], sel_start[None, :]), 0, None) / CMP_BLOCK
    ovt = np.ascontiguousarray(overlap.T).astype(np.float32)
    oh = np.zeros((S, LANE), np.float32)
    oh[np.arange(S), HEAD_DIM + np.arange(S) // SEL_BLOCK] = 1.0
    j = np.arange(ATT_CHUNK)[:, None]
    i = np.arange(ATT_ROWS)[None, :]
    bias = np.stack([np.zeros((ATT_CHUNK, ATT_ROWS)),
                     np.where(j <= i, 0.0, NEG),
                     np.where(j > i, 0.0, NEG)]
                    ).astype(np.float32)
    return tuple(jnp.asarray(a) for a in (cos_t, sin_t, cos_c, sin_c, ovt, oh, bias))


def _compress_weights(w1):
    eye = jnp.eye(KV_GROUPS, dtype=w1.dtype)
    half = CMP_BLOCK // 2

    def one(w):
        return jnp.einsum('ldf,gh->lgdhf', w, eye).reshape(half * KV_GROUPS * HEAD_DIM,
                                                           KV_GROUPS * CMP_HIDDEN)

    return jnp.stack([one(w1[:half]), one(w1[half:])]).astype(BF16)


def _compress_pe(pe):
    half = CMP_BLOCK // 2

    def one(p):
        return jnp.broadcast_to(p[:, None, :], (half, KV_GROUPS, HEAD_DIM)).reshape(1, -1)

    return one(pe[:half]), one(pe[half:])


def kernel(x, c, ada_w, ada_b, norm_g, ffn_w_in, ffn_w_out, mix_w_in, cmp_pe, cmp_w1, cmp_w2,
           gm_ln_g, gm_ln_b, gm_ws, gm_bs, proj_a, proj_b, w_out, final_g):
    B, S, D = x.shape
    L = ada_w.shape[0]
    assert S // SEL_BLOCK + HEAD_DIM <= LANE and 3 * NSA_HEADS <= GATE_ROWS
    cos_t, sin_t, cos_c, sin_c, ovt, oh_t, bias = _tables(S)
    swap = np.concatenate([np.arange(HEAD_DIM // 2, HEAD_DIM), np.arange(HEAD_DIM // 2)])

    mods = _ada(c, ada_w, ada_b).reshape(L, B, 3, 3, 1, D)
    h = x
    for l in range(L):
        mod = mods[l]
        h = _ffn(h, mod[:, 0, 0], mod[:, 0, 1], mod[:, 0, 2], norm_g[l, 0][None],
                 ffn_w_in[l, 0].astype(BF16), ffn_w_out[l, 0].astype(BF16))

        w = mix_w_in[l]
        o_kv = NSA_WIDTH
        o_g = o_kv + 6 * KV_GROUPS * HEAD_DIM
        o_uv = o_g + 3 * NSA_HEADS
        o_ga = o_uv + 2 * GM_WIDTH
        o_gb = o_ga + D
        wq = (w[:, :o_kv] * (HEAD_DIM ** -0.5 * LOG2E)).astype(BF16)
        wkv = w[:, o_kv:o_g].astype(BF16)
        wg = jnp.pad(w[:, o_g:o_uv], ((0, 0), (0, LANE - 3 * NSA_HEADS))).astype(BF16)
        wuv = w[:, o_uv:o_ga].astype(BF16)
        wga = w[:, o_ga:o_gb].astype(BF16)
        wgb = w[:, o_gb:].astype(BF16)
        bsx = jnp.repeat(gm_bs[l].T, HEAD_DIM, axis=1)
        qt, kcmp, vcmp, kselx, vselt, kwinx, vwint, gt, ga, mb = _mix_in(
            h, mod[:, 1, 0], mod[:, 1, 1], norm_g[l, 1][None], wq, wkv, wg, wuv, wga, wgb,
            cos_t, sin_t, oh_t, gm_ln_g[l][None], gm_ln_b[l][None], gm_ws[l], bsx,
            proj_b[l].astype(BF16))

        pe_ka, pe_kb = _compress_pe(cmp_pe[l, 0])
        pe_va, pe_vb = _compress_pe(cmp_pe[l, 1])
        pe = jnp.stack([pe_ka, pe_kb, pe_va, pe_vb])
        w2k = jnp.concatenate([cmp_w2[l, 0], cmp_w2[l, 0][:, swap]], axis=1).astype(BF16)
        w2v = jnp.pad(cmp_w2[l, 1], ((0, 0), (0, LANE - HEAD_DIM))).astype(BF16)
        nc = S // CMP_STRIDE
        kc, vct = _compress(kcmp.reshape(B, nc, CMP_STRIDE * LANE),
                            vcmp.reshape(B, nc, CMP_STRIDE * LANE), pe,
                            _compress_weights(cmp_w1[l, 0]), _compress_weights(cmp_w1[l, 1]),
                            w2k, w2v, cos_c, sin_c)

        h = _attn(qt, kselx, vselt, kwinx, vwint, kc, vct, gt, ga, mb, h, mod[:, 1, 2], ovt,
                  bias, proj_a[l].astype(BF16), w_out[l].astype(BF16))

        h = _ffn(h, mod[:, 2, 0], mod[:, 2, 1], mod[:, 2, 2], norm_g[l, 2][None],
                 ffn_w_in[l, 1].astype(BF16), ffn_w_out[l, 1].astype(BF16))
    return _final(h, final_g[None])
```

```python
import functools

import numpy as np
import jax
import jax.numpy as jnp
from jax import lax
from jax.experimental import pallas as pl
from jax.experimental.pallas import tpu as pltpu

F32 = jnp.float32
BF16 = jnp.bfloat16

HEAD_DIM = 64
NSA_HEADS = 8
KV_GROUPS = 2
HPG = NSA_HEADS // KV_GROUPS
NSA_WIDTH = NSA_HEADS * HEAD_DIM
CMP_BLOCK = 32
CMP_STRIDE = 16
CMP_HIDDEN = 128
SEL_BLOCK = 64
SEL_TOPK = 16
WINDOW = 512
GM_GROUPS = 8
GM_CHUNK = 128
GM_WIDTH = GM_GROUPS * HEAD_DIM
ROPE_THETA = 10000.0
EPS = 1e-6
NEG = -1e30
FORCE = 1e4
LANE = 128
GELU_C = float(np.sqrt(2.0 / np.pi))
LOG2E = float(np.log2(np.e))
GATE_ROWS = 32
V_ROWS = HEAD_DIM + 16

FFN_ROWS = 512
FFN_SPLIT = 2
MIX_ROWS = 512
ATT_ROWS = 256
ATT_CHUNK = 256
VMEM_LIMIT = 56 * 1024 * 1024


def _sigmoid(x):
    return 1.0 / (1.0 + jnp.exp(-x))


def _silu(x):
    return x * _sigmoid(x)


def _rms_mod(x, g, shift, scale):
    y = x * lax.rsqrt(jnp.mean(x * x, axis=-1, keepdims=True) + EPS) * g
    return y * (1.0 + scale) + shift


def _ada_kernel(c_ref, w_ref, b_ref, o_ref):
    s = _silu(c_ref[...])
    o_ref[0] = jnp.dot(s, w_ref[0], preferred_element_type=F32,
                       precision=lax.Precision.HIGHEST) + b_ref[0]


def _ada(c, ada_w, ada_b):
    L, D, N = ada_w.shape
    B = c.shape[0]
    tn = 1024
    return pl.pallas_call(
        _ada_kernel,
        grid=(L, N // tn),
        in_specs=[
            pl.BlockSpec((B, D), lambda l, j: (0, 0)),
            pl.BlockSpec((1, D, tn), lambda l, j: (l, 0, j)),
            pl.BlockSpec((1, 1, tn), lambda l, j: (l, 0, j)),
        ],
        out_specs=pl.BlockSpec((1, B, tn), lambda l, j: (l, 0, j)),
        out_shape=jax.ShapeDtypeStruct((L, B, N), F32),
        compiler_params=pltpu.CompilerParams(
            dimension_semantics=("parallel", "parallel"), vmem_limit_bytes=VMEM_LIMIT),
        name="ada",
    )(c, ada_w, ada_b.reshape(L, 1, N))


def _ffn_kernel(h_ref, shift_ref, scale_ref, gate_ref, g_ref, wi_ref, wo_ref, *rest,
                final_norm):
    fg_ref = rest[0] if final_norm else None
    o_ref = rest[-1]
    F = wo_ref.shape[0]
    tf = F // FFN_SPLIT
    x = h_ref[0]
    n = _rms_mod(x, g_ref[...], shift_ref[0], scale_ref[0]).astype(BF16)
    acc = None
    for j in range(FFN_SPLIT):
        a = jnp.dot(n, wi_ref[:, j * tf:(j + 1) * tf], preferred_element_type=F32)
        b = jnp.dot(n, wi_ref[:, F + j * tf:F + (j + 1) * tf], preferred_element_type=F32)
        hm = (_silu(a) * b).astype(BF16)
        part = jnp.dot(hm, wo_ref[j * tf:(j + 1) * tf, :], preferred_element_type=F32)
        acc = part if acc is None else acc + part
    y = x + (0.5 * gate_ref[0]) * acc
    if final_norm:
        y = y * lax.rsqrt(jnp.mean(y * y, axis=-1, keepdims=True) + EPS) * fg_ref[...]
    o_ref[0] = y


def _ffn(h, shift, scale, gate, g, w_in, w_out, final_g=None):
    B, S, D = h.shape
    tm = FFN_ROWS
    row = pl.BlockSpec((1, tm, D), lambda b, i: (b, i, 0))
    vec = pl.BlockSpec((1, 1, D), lambda b, i: (b, 0, 0))
    gain = pl.BlockSpec((1, D), lambda b, i: (0, 0))

    def resident(a):
        return pl.BlockSpec(a.shape, lambda b, i: (0, 0), pipeline_mode=pl.Buffered(1))

    extra = () if final_g is None else (final_g,)
    return pl.pallas_call(
        functools.partial(_ffn_kernel, final_norm=final_g is not None),
        grid=(B, S // tm),
        in_specs=[row, vec, vec, vec, gain, resident(w_in), resident(w_out)]
        + [gain] * len(extra),
        out_specs=row,
        out_shape=jax.ShapeDtypeStruct((B, S, D), F32),
        compiler_params=pltpu.CompilerParams(
            dimension_semantics=("parallel", "parallel"), vmem_limit_bytes=VMEM_LIMIT),
        name="ffn",
    )(h, shift, scale, gate, g, w_in, w_out, *extra)


def _rope_pair(x, c, s):
    lane = lax.broadcasted_iota(jnp.int32, x.shape, 1)
    first_half = (lane & (HEAD_DIM - 1)) < HEAD_DIM // 2
    swapped = jnp.where(first_half, pltpu.roll(x, LANE - HEAD_DIM // 2, 1),
                        pltpu.roll(x, HEAD_DIM // 2, 1))
    return x * c + swapped * s


def _mix_in_kernel(h_ref, shift_ref, scale_ref, g_ref, wq_ref, wkv_ref, wg_ref, wuv_ref,
                   wga_ref, wgb_ref, cos_ref, sin_ref, oh_ref, lng_ref, lnb_ref, ws_ref,
                   bsx_ref, pb_ref,
                   qt_ref, kcmp_ref, vcmp_ref, kselx_ref, vselt_ref, kwinx_ref, vwint_ref,
                   gt_ref, ga_ref, mb_ref):
    tm = h_ref.shape[1]
    CK = vselt_ref.shape[4]
    n = _rms_mod(h_ref[0], g_ref[...], shift_ref[0], scale_ref[0]).astype(BF16)
    cos = cos_ref[...]
    sin = sin_ref[...]
    low_lanes = lax.broadcasted_iota(jnp.int32, (tm, LANE), 1) < HEAD_DIM

    q = jnp.dot(n, wq_ref[...], preferred_element_type=F32)
    for p in range(NSA_HEADS // 2):
        qp = jnp.transpose(_rope_pair(q[:, p * LANE:(p + 1) * LANE], cos, sin)).astype(BF16)
        qt_ref[0, 2 * p] = qp[:HEAD_DIM]
        qt_ref[0, 2 * p + 1] = qp[HEAD_DIM:]

    kv = jnp.dot(n, wkv_ref[...], preferred_element_type=F32)
    kcmp_ref[0] = kv[:, 0 * LANE:1 * LANE]
    vcmp_ref[0] = kv[:, 1 * LANE:2 * LANE]
    for idx, ref, ext in ((2, kselx_ref, oh_ref[...]), (4, kwinx_ref, 0.0)):
        k2 = _rope_pair(kv[:, idx * LANE:(idx + 1) * LANE], cos, sin)
        ref[0, 0] = jnp.where(low_lanes, k2, ext).astype(BF16)
        ref[0, 1] = jnp.where(low_lanes, pltpu.roll(k2, HEAD_DIM, 1), ext).astype(BF16)
    ones_row = jnp.where(lax.broadcasted_iota(jnp.int32, (V_ROWS - HEAD_DIM, CK), 0) == 0,
                         1.0, 0.0).astype(BF16)
    for idx, ref in ((3, vselt_ref), (5, vwint_ref)):
        vt = jnp.transpose(kv[:, idx * LANE:(idx + 1) * LANE]).astype(BF16)
        for g in range(KV_GROUPS):
            for r in range(tm // CK):
                ref[0, g, r] = jnp.concatenate(
                    [vt[g * HEAD_DIM:(g + 1) * HEAD_DIM, r * CK:(r + 1) * CK], ones_row], axis=0)

    gates = _sigmoid(jnp.dot(n, wg_ref[...], preferred_element_type=F32))
    gt_ref[0] = jnp.transpose(gates)[:GATE_ROWS]
    ga_ref[0] = _sigmoid(jnp.dot(n, wga_ref[...], preferred_element_type=F32))

    uv = jnp.dot(n, wuv_ref[...], preferred_element_type=F32)
    ge = uv * (0.5 * (1.0 + jnp.tanh(GELU_C * (uv + 0.044715 * (uv * uv * uv)))))
    u = ge[:, :GM_WIDTH]
    v = ge[:, GM_WIDTH:]
    mu = jnp.mean(v, axis=-1, keepdims=True)
    var = jnp.mean(jnp.square(v - mu), axis=-1, keepdims=True)
    vln = ((v - mu) * lax.rsqrt(var + EPS) * lng_ref[...] + lnb_ref[...]).astype(BF16)

    ti = lax.broadcasted_iota(jnp.int32, (GM_CHUNK, GM_CHUNK), 0)
    si = lax.broadcasted_iota(jnp.int32, (GM_CHUNK, GM_CHUNK), 1)
    tril = si <= ti
    wm = [jnp.where(tril, ws_ref[gg], 0.0).astype(BF16) for gg in range(GM_GROUPS)]
    low = lax.broadcasted_iota(jnp.int32, (GM_CHUNK, LANE), 1) < HEAD_DIM
    bsx = bsx_ref[...]
    yb_rows = []
    for r in range(tm // GM_CHUNK):
        vch = vln[r * GM_CHUNK:(r + 1) * GM_CHUNK]
        pieces = []
        for p in range(GM_GROUPS // 2):
            vp = vch[:, p * LANE:(p + 1) * LANE]
            a0 = jnp.dot(wm[2 * p], vp, preferred_element_type=F32)
            a1 = jnp.dot(wm[2 * p + 1], vp, preferred_element_type=F32)
            pieces.append(jnp.where(low, a0, a1))
        sv = jnp.concatenate(pieces, axis=1) + bsx
        yb_rows.append(u[r * GM_CHUNK:(r + 1) * GM_CHUNK] * sv)
    yb = jnp.concatenate(yb_rows, axis=0).astype(BF16)
    gb = _sigmoid(jnp.dot(n, wgb_ref[...], preferred_element_type=F32))
    mb_ref[0] = gb * jnp.dot(yb, pb_ref[...], preferred_element_type=F32)


def _mix_in(h, shift, scale, g, wq, wkv, wg, wuv, wga, wgb, cos_t, sin_t, oh_t, ln_g, ln_b, ws,
            bsx, proj_b):
    B, S, D = h.shape
    tm = MIX_ROWS
    G = KV_GROUPS
    CK = ATT_CHUNK

    def const(a):
        nd = a.ndim
        return pl.BlockSpec(a.shape, lambda b, i: (0,) * nd)

    row = lambda w: pl.BlockSpec((1, tm, w), lambda b, i: (b, i, 0))
    vec = pl.BlockSpec((1, 1, D), lambda b, i: (b, 0, 0))
    tab = pl.BlockSpec((tm, LANE), lambda b, i: (i, 0))
    kspec = pl.BlockSpec((1, G, tm, LANE), lambda b, i: (b, 0, i, 0))
    kshape = jax.ShapeDtypeStruct((B, G, S, LANE), BF16)
    vspec = pl.BlockSpec((1, G, tm // CK, V_ROWS, CK), lambda b, i: (b, 0, i, 0, 0))
    vshape = jax.ShapeDtypeStruct((B, G, S // CK, V_ROWS, CK), BF16)
    return pl.pallas_call(
        _mix_in_kernel,
        grid=(B, S // tm),
        in_specs=[row(D), vec, vec, const(g), const(wq), const(wkv), const(wg), const(wuv),
                  const(wga), const(wgb), tab, tab, tab, const(ln_g), const(ln_b), const(ws),
                  const(bsx), const(proj_b)],
        out_specs=[
            pl.BlockSpec((1, NSA_HEADS, HEAD_DIM, tm), lambda b, i: (b, 0, 0, i)),
            row(LANE), row(LANE), kspec, vspec, kspec, vspec,
            pl.BlockSpec((1, GATE_ROWS, tm), lambda b, i: (b, 0, i)),
            row(D), row(D)],
        out_shape=[
            jax.ShapeDtypeStruct((B, NSA_HEADS, HEAD_DIM, S), BF16),
            jax.ShapeDtypeStruct((B, S, LANE), F32), jax.ShapeDtypeStruct((B, S, LANE), F32),
            kshape, vshape, kshape, vshape,
            jax.ShapeDtypeStruct((B, GATE_ROWS, S), F32),
            jax.ShapeDtypeStruct((B, S, D), F32), jax.ShapeDtypeStruct((B, S, D), F32)],
        compiler_params=pltpu.CompilerParams(
            dimension_semantics=("parallel", "parallel"), vmem_limit_bytes=VMEM_LIMIT),
        name="mix_in",
    )(h, shift, scale, g, wq, wkv, wg, wuv, wga, wgb, cos_t, sin_t, oh_t, ln_g, ln_b, ws, bsx,
      proj_b)


def _compress_kernel(k16_ref, v16_ref, pe_ref, wk_ref, wv_ref, w2k_ref, w2v_ref, cos_ref,
                     sin_ref, kc_ref, vct_ref):
    def hidden(x16, pe_a, pe_b, w_ref):
        ha = jnp.dot((x16 + pe_a).astype(BF16), w_ref[0], preferred_element_type=F32)
        hb = jnp.dot((x16 + pe_b).astype(BF16), w_ref[1], preferred_element_type=F32)
        return _silu(ha + pltpu.roll(hb, hb.shape[0] - 1, 0))

    hk = hidden(k16_ref[0], pe_ref[0], pe_ref[1], wk_ref)
    hv = hidden(v16_ref[0], pe_ref[2], pe_ref[3], wv_ref)
    for g in range(KV_GROUPS):
        k2 = jnp.dot(hk[:, g * CMP_HIDDEN:(g + 1) * CMP_HIDDEN].astype(BF16), w2k_ref[...],
                     preferred_element_type=F32)
        kc = k2[:, :HEAD_DIM] * cos_ref[...] + k2[:, HEAD_DIM:] * sin_ref[...]
        kc_ref[0, g] = kc.astype(BF16)
        vc = jnp.dot(hv[:, g * CMP_HIDDEN:(g + 1) * CMP_HIDDEN].astype(BF16), w2v_ref[...],
                     preferred_element_type=F32)
        vct_ref[0, g] = jnp.transpose(vc)[:HEAD_DIM].astype(BF16)


def _compress(k16, v16, pe, wk, wv, w2k, w2v, cos_c, sin_c):
    B, NC, W = k16.shape

    def const(a):
        nd = a.ndim
        return pl.BlockSpec(a.shape, lambda b: (0,) * nd)

    blk = pl.BlockSpec((1, NC, W), lambda b: (b, 0, 0))
    return pl.pallas_call(
        _compress_kernel,
        grid=(B,),
        in_specs=[blk, blk, const(pe), const(wk), const(wv), const(w2k), const(w2v),
                  const(cos_c), const(sin_c)],
        out_specs=[pl.BlockSpec((1, KV_GROUPS, NC, HEAD_DIM), lambda b: (b, 0, 0, 0)),
                   pl.BlockSpec((1, KV_GROUPS, HEAD_DIM, NC), lambda b: (b, 0, 0, 0))],
        out_shape=[jax.ShapeDtypeStruct((B, KV_GROUPS, NC, HEAD_DIM), BF16),
                   jax.ShapeDtypeStruct((B, KV_GROUPS, HEAD_DIM, NC), BF16)],
        compiler_params=pltpu.CompilerParams(
            dimension_semantics=("parallel",), vmem_limit_bytes=VMEM_LIMIT),
        name="compress",
    )(k16, v16, pe, wk, wv, w2k, w2v, cos_c, sin_c)


def _fold_rows(x, op):
    parts = [x[8 * i:8 * (i + 1)] for i in range(x.shape[0] // 8)]
    while len(parts) > 1:
        parts = [op(parts[i], parts[i + 1]) for i in range(0, len(parts), 2)]
    return parts[0]


def _attn_kernel(qt_ref, kselx_ref, vselt_ref, kwinx_ref, vwint_ref, kc_ref, vct_ref, gt_ref,
                 ga_ref, mb_ref, h_ref, gate_ref, ovt_ref, bias_ref, pa_ref, wo_ref,
                 o_ref, qx_scr, s_scr, m_scr, a_scr, acc_scr, yt_scr):
    TQ = h_ref.shape[1]
    CK = ATT_CHUNK
    NC = kc_ref.shape[2]
    NSEL = kselx_ref.shape[2] // SEL_BLOCK
    nwin = WINDOW // CK
    qi = pl.program_id(1)

    def branch(kx_ref, vt_ref, first, far, gate_row):
        def score(c, masked, dst):
            kx = [kx_ref[0, g, pl.ds(pl.multiple_of(c * CK, CK), CK), :]
                  for g in range(KV_GROUPS)]
            if masked:
                bias = bias_ref[jnp.where(c == qi, 1, jnp.where(c == far, 2, 0))]
            for hd in range(NSA_HEADS):
                s = jnp.dot(kx[hd // HPG], qx_scr[hd], preferred_element_type=F32)
                if masked:
                    s = s + bias
                s_scr[dst, hd] = s
                m_prev = m_scr[1 - dst, hd]
                m_new = jnp.maximum(m_prev, jnp.max(_fold_rows(s, jnp.maximum), axis=0,
                                                    keepdims=True))
                a_scr[dst, hd] = jnp.exp2(m_prev - m_new)
                m_scr[dst, hd] = m_new

        def weigh(c, src):
            vt = [vt_ref[0, g, c] for g in range(KV_GROUPS)]
            for hd in range(NSA_HEADS):
                p = jnp.exp2(s_scr[src, hd] - m_scr[src, hd]).astype(BF16)
                acc_scr[hd] = a_scr[src, hd] * acc_scr[hd] + jnp.dot(
                    vt[hd // HPG], p, preferred_element_type=F32)

        def loop(lo, hi, body):
            def wrapped(c, carry):
                body(c)
                return carry
            lax.fori_loop(lo, hi, wrapped, 0)

        def two_chunks(i):
            c = first + 2 * i
            score(c + 1, False, 1)
            weigh(c, 0)
            score(c + 2, True, 0)
            weigh(c + 1, 1)

        def last_two(c):
            score(c + 1, True, 1)
            weigh(c, 0)

        m_scr[...] = jnp.full_like(m_scr, NEG)
        acc_scr[...] = jnp.zeros_like(acc_scr)
        loop(first, first + 1, lambda c: score(c, True, 0))
        rest = qi - first
        pairs = jnp.right_shift(rest, 1)
        odd = rest & 1
        loop(0, pairs, two_chunks)
        loop(qi - 1, qi - 1 + odd, last_two)
        loop(qi, qi + odd, lambda c: weigh(c, 1))
        loop(qi, qi + 1 - odd, lambda c: weigh(c, 0))
        for hd in range(NSA_HEADS):
            w = gt_ref[0, 3 * hd + gate_row:3 * hd + gate_row + 1, :] / acc_scr[
                hd, HEAD_DIM:HEAD_DIM + 1, :]
            rows = slice(hd * HEAD_DIM, (hd + 1) * HEAD_DIM)
            yt_scr[rows, :] = yt_scr[rows, :] + w * acc_scr[hd, :HEAD_DIM, :]

    n_idx = lax.broadcasted_iota(jnp.int32, (NC, TQ), 0)
    t_cmp = qi * TQ + lax.broadcasted_iota(jnp.int32, (NC, TQ), 1)
    vis = (n_idx * CMP_STRIDE + (CMP_BLOCK - 1)) <= t_cmp
    visf = jnp.where(vis, 1.0, 0.0)
    j_idx = lax.broadcasted_iota(jnp.int32, (NSEL, TQ), 0)
    t_sel = qi * TQ + lax.broadcasted_iota(jnp.int32, (NSEL, TQ), 1)
    cur = jnp.right_shift(t_sel, SEL_BLOCK.bit_length() - 1)
    forced = (j_idx == 0) | (j_idx == cur) | (j_idx == cur - 1)
    valid = j_idx <= cur

    for g in range(KV_GROUPS):
        psum = jnp.zeros((NC, TQ), F32)
        for hh in range(HPG):
            hd = g * HPG + hh
            qt = qt_ref[0, hd]
            sc = jnp.where(vis, jnp.dot(kc_ref[0, g], qt, preferred_element_type=F32), NEG)
            ec = jnp.exp2(sc - jnp.max(sc, axis=0, keepdims=True)) * visf
            den = jnp.sum(ec, axis=0, keepdims=True)
            pc = ec / jnp.where(den > 0.0, den, 1.0)
            psum = psum + pc
            oc = jnp.dot(vct_ref[0, g], pc.astype(BF16), preferred_element_type=F32)
            yt_scr[hd * HEAD_DIM:(hd + 1) * HEAD_DIM, :] = gt_ref[0, 3 * hd:3 * hd + 1, :] * oc

        imp = jnp.dot(ovt_ref[...], psum, preferred_element_type=F32,
                      precision=lax.Precision.HIGHEST)
        imp = jnp.where(forced, FORCE, jnp.where(valid, imp, -FORCE))
        rank = jnp.zeros((NSEL, TQ), jnp.int32)
        for kk in range(NSEL):
            rk = imp[kk:kk + 1, :]
            ahead = (rk > imp) | ((rk == imp) & (j_idx > kk))
            rank = rank + jnp.where(ahead, 1, 0)
        selneg = jnp.where(rank < SEL_TOPK, 0.0, NEG).astype(BF16)
        pad = jnp.zeros((LANE - HEAD_DIM - NSEL, TQ), BF16)
        for hh in range(HPG):
            hd = g * HPG + hh
            qx_scr[hd] = jnp.concatenate([qt_ref[0, hd], selneg, pad], axis=0)

    branch(kselx_ref, vselt_ref, 0, -1, 1)
    branch(kwinx_ref, vwint_ref, jnp.maximum(qi - nwin, 0), qi - nwin, 2)

    y = jnp.transpose(yt_scr[...]).astype(BF16)
    ya = jnp.dot(y, pa_ref[...], preferred_element_type=F32)
    merged = (ga_ref[0] * ya + mb_ref[0]).astype(BF16)
    o_ref[0] = h_ref[0] + gate_ref[0] * jnp.dot(merged, wo_ref[...],
                                                preferred_element_type=F32)


def _attn(qt, kselx, vselt, kwinx, vwint, kc, vct, gt, ga, mb, h, gate, ovt, bias,
          proj_a, w_out):
    B, S, D = h.shape
    TQ = ATT_ROWS
    CK = ATT_CHUNK
    assert TQ == CK and WINDOW % CK == 0 and S % TQ == 0

    def const(a):
        nd = a.ndim
        return pl.BlockSpec(a.shape, lambda b, i: (0,) * nd)

    def per_batch(a):
        nd = a.ndim
        return pl.BlockSpec((1,) + a.shape[1:], lambda b, i: (b,) + (0,) * (nd - 1))

    row = lambda w: pl.BlockSpec((1, TQ, w), lambda b, i: (b, i, 0))
    return pl.pallas_call(
        _attn_kernel,
        grid=(B, S // TQ),
        in_specs=[
            pl.BlockSpec((1, NSA_HEADS, HEAD_DIM, TQ), lambda b, i: (b, 0, 0, i)),
            per_batch(kselx), per_batch(vselt), per_batch(kwinx), per_batch(vwint),
            per_batch(kc), per_batch(vct),
            pl.BlockSpec((1, GATE_ROWS, TQ), lambda b, i: (b, 0, i)),
            row(D), row(D), row(D),
            pl.BlockSpec((1, 1, D), lambda b, i: (b, 0, 0)),
            const(ovt), const(bias), const(proj_a), const(w_out)],
        out_specs=row(D),
        out_shape=jax.ShapeDtypeStruct((B, S, D), F32),
        scratch_shapes=[
            pltpu.VMEM((NSA_HEADS, LANE, TQ), BF16),
            pltpu.VMEM((2, NSA_HEADS, CK, TQ), F32),
            pltpu.VMEM((2, NSA_HEADS, 1, TQ), F32), pltpu.VMEM((2, NSA_HEADS, 1, TQ), F32),
            pltpu.VMEM((NSA_HEADS, V_ROWS, TQ), F32),
            pltpu.VMEM((NSA_WIDTH, TQ), F32)],
        compiler_params=pltpu.CompilerParams(
            dimension_semantics=("parallel", "parallel"), vmem_limit_bytes=VMEM_LIMIT),
        name="attn",
    )(qt, kselx, vselt, kwinx, vwint, kc, vct, gt, ga, mb, h, gate, ovt, bias,
      proj_a, w_out)


def _rope_tables(pos):
    inv = 1.0 / (ROPE_THETA ** (np.arange(0, HEAD_DIM, 2, dtype=np.float64) / HEAD_DIM))
    ang = np.asarray(pos, np.float64)[:, None] * inv[None, :]
    cos = np.concatenate([np.cos(ang), np.cos(ang)], axis=1)
    sin = np.concatenate([-np.sin(ang), np.sin(ang)], axis=1)
    return cos.astype(np.float32), sin.astype(np.float32)


def _tables(S):
    cos, sin = _rope_tables(np.arange(S))
    cos_t = np.concatenate([cos, cos], axis=1)
    sin_t = np.concatenate([sin, sin], axis=1)
    n_cmp_pad = S // CMP_STRIDE
    starts = np.arange(n_cmp_pad) * CMP_STRIDE
    cos_c, sin_c = _rope_tables(starts + CMP_BLOCK - 1)
    n_sel = S // SEL_BLOCK
    sel_start = np.arange(n_sel) * SEL_BLOCK
    overlap = np.clip(np.minimum(starts[:, None] + CMP_BLOCK, sel_start[None, :] + SEL_BLOCK)
                      - np.maximum(starts[:, None], sel_start[None, :]), 0, None) / CMP_BLOCK
    ovt = np.ascontiguousarray(overlap.T).astype(np.float32)
    oh = np.zeros((S, LANE), np.float32)
    oh[np.arange(S), HEAD_DIM + np.arange(S) // SEL_BLOCK] = 1.0
    j = np.arange(ATT_CHUNK)[:, None]
    i = np.arange(ATT_ROWS)[None, :]
    bias = np.stack([np.zeros((ATT_CHUNK, ATT_ROWS)),
                     np.where(j <= i, 0.0, NEG),
                     np.where(j > i, 0.0, NEG)]
                    ).astype(np.float32)
    return tuple(jnp.asarray(a) for a in (cos_t, sin_t, cos_c, sin_c, ovt, oh, bias))


def _compress_weights(w1):
    eye = jnp.eye(KV_GROUPS, dtype=w1.dtype)
    half = CMP_BLOCK // 2

    def one(w):
        return jnp.einsum('ldf,gh->lgdhf', w, eye).reshape(half * KV_GROUPS * HEAD_DIM,
                                                           KV_GROUPS * CMP_HIDDEN)

    return jnp.stack([one(w1[:half]), one(w1[half:])]).astype(BF16)


def _compress_pe(pe):
    half = CMP_BLOCK // 2

    def one(p):
        return jnp.broadcast_to(p[:, None, :], (half, KV_GROUPS, HEAD_DIM)).reshape(1, -1)

    return one(pe[:half]), one(pe[half:])


def kernel(x, c, ada_w, ada_b, norm_g, ffn_w_in, ffn_w_out, mix_w_in, cmp_pe, cmp_w1, cmp_w2,
           gm_ln_g, gm_ln_b, gm_ws, gm_bs, proj_a, proj_b, w_out, final_g):
    B, S, D = x.shape
    L = ada_w.shape[0]
    assert S // SEL_BLOCK + HEAD_DIM <= LANE and 3 * NSA_HEADS <= GATE_ROWS
    cos_t, sin_t, cos_c, sin_c, ovt, oh_t, bias = _tables(S)
    swap = np.concatenate([np.arange(HEAD_DIM // 2, HEAD_DIM), np.arange(HEAD_DIM // 2)])

    mods = _ada(c, ada_w, ada_b).reshape(L, B, 3, 3, 1, D)
    h = x
    for l in range(L):
        mod = mods[l]
        h = _ffn(h, mod[:, 0, 0], mod[:, 0, 1], mod[:, 0, 2], norm_g[l, 0][None],
                 ffn_w_in[l, 0].astype(BF16), ffn_w_out[l, 0].astype(BF16))

        w = mix_w_in[l]
        o_kv = NSA_WIDTH
        o_g = o_kv + 6 * KV_GROUPS * HEAD_DIM
        o_uv = o_g + 3 * NSA_HEADS
        o_ga = o_uv + 2 * GM_WIDTH
        o_gb = o_ga + D
        wq = (w[:, :o_kv] * (HEAD_DIM ** -0.5 * LOG2E)).astype(BF16)
        wkv = w[:, o_kv:o_g].astype(BF16)
        wg = jnp.pad(w[:, o_g:o_uv], ((0, 0), (0, LANE - 3 * NSA_HEADS))).astype(BF16)
        wuv = w[:, o_uv:o_ga].astype(BF16)
        wga = w[:, o_ga:o_gb].astype(BF16)
        wgb = w[:, o_gb:].astype(BF16)
        bsx = jnp.repeat(gm_bs[l].T, HEAD_DIM, axis=1)
        qt, kcmp, vcmp, kselx, vselt, kwinx, vwint, gt, ga, mb = _mix_in(
            h, mod[:, 1, 0], mod[:, 1, 1], norm_g[l, 1][None], wq, wkv, wg, wuv, wga, wgb,
            cos_t, sin_t, oh_t, gm_ln_g[l][None], gm_ln_b[l][None], gm_ws[l], bsx,
            proj_b[l].astype(BF16))

        pe_ka, pe_kb = _compress_pe(cmp_pe[l, 0])
        pe_va, pe_vb = _compress_pe(cmp_pe[l, 1])
        pe = jnp.stack([pe_ka, pe_kb, pe_va, pe_vb])
        w2k = jnp.concatenate([cmp_w2[l, 0], cmp_w2[l, 0][:, swap]], axis=1).astype(BF16)
        w2v = jnp.pad(cmp_w2[l, 1], ((0, 0), (0, LANE - HEAD_DIM))).astype(BF16)
        nc = S // CMP_STRIDE
        kc, vct = _compress(kcmp.reshape(B, nc, CMP_STRIDE * LANE),
                            vcmp.reshape(B, nc, CMP_STRIDE * LANE), pe,
                            _compress_weights(cmp_w1[l, 0]), _compress_weights(cmp_w1[l, 1]),
                            w2k, w2v, cos_c, sin_c)

        h = _attn(qt, kselx, vselt, kwinx, vwint, kc, vct, gt, ga, mb, h, mod[:, 1, 2], ovt,
                  bias, proj_a[l].astype(BF16), w_out[l].astype(BF16))

        h = _ffn(h, mod[:, 2, 0], mod[:, 2, 1], mod[:, 2, 2], norm_g[l, 2][None],
                 ffn_w_in[l, 1].astype(BF16), ffn_w_out[l, 1].astype(BF16),
                 final_g=final_g[None] if l == L - 1 else None)
    return h
```

```python
import functools

import numpy as np
import jax
import jax.numpy as jnp
from jax import lax
from jax.experimental import pallas as pl
from jax.experimental.pallas import tpu as pltpu

F32 = jnp.float32
BF16 = jnp.bfloat16

HEAD_DIM = 64
NSA_HEADS = 8
KV_GROUPS = 2
HPG = NSA_HEADS // KV_GROUPS
NSA_WIDTH = NSA_HEADS * HEAD_DIM
CMP_BLOCK = 32
CMP_STRIDE = 16
CMP_HIDDEN = 128
SEL_BLOCK = 64
SEL_TOPK = 16
WINDOW = 512
GM_GROUPS = 8
GM_CHUNK = 128
GM_WIDTH = GM_GROUPS * HEAD_DIM
ROPE_THETA = 10000.0
EPS = 1e-6
NEG = -1e30
FORCE = 1e4
LANE = 128
GELU_C = float(np.sqrt(2.0 / np.pi))
LOG2E = float(np.log2(np.e))
GATE_ROWS = 32
V_ROWS = HEAD_DIM + 16

FFN_ROWS = 512
FFN_SPLIT = 2
MIX_ROWS = 512
ATT_ROWS = 256
ATT_CHUNK = 256
VMEM_LIMIT = 56 * 1024 * 1024


def _sigmoid(x):
    return 1.0 / (1.0 + jnp.exp(-x))


def _silu(x):
    return x * _sigmoid(x)


def _rms_mod(x, g, shift, scale):
    y = x * lax.rsqrt(jnp.mean(x * x, axis=-1, keepdims=True) + EPS) * g
    return y * (1.0 + scale) + shift


def _ada_kernel(c_ref, w_ref, b_ref, o_ref):
    s = _silu(c_ref[...])
    o_ref[0] = jnp.dot(s, w_ref[0], preferred_element_type=F32,
                       precision=lax.Precision.HIGHEST) + b_ref[0]


def _ada(c, ada_w, ada_b):
    L, D, N = ada_w.shape
    B = c.shape[0]
    tn = 1024
    return pl.pallas_call(
        _ada_kernel,
        grid=(L, N // tn),
        in_specs=[
            pl.BlockSpec((B, D), lambda l, j: (0, 0)),
            pl.BlockSpec((1, D, tn), lambda l, j: (l, 0, j)),
            pl.BlockSpec((1, 1, tn), lambda l, j: (l, 0, j)),
        ],
        out_specs=pl.BlockSpec((1, B, tn), lambda l, j: (l, 0, j)),
        out_shape=jax.ShapeDtypeStruct((L, B, N), F32),
        compiler_params=pltpu.CompilerParams(
            dimension_semantics=("parallel", "parallel"), vmem_limit_bytes=VMEM_LIMIT),
        name="ada",
    )(c, ada_w, ada_b.reshape(L, 1, N))


def _ffn_kernel(h_ref, shift_ref, scale_ref, gate_ref, g_ref, wi_ref, wo_ref, *rest,
                final_norm):
    fg_ref = rest[0] if final_norm else None
    o_ref = rest[-1]
    F = wo_ref.shape[0]
    tf = F // FFN_SPLIT
    x = h_ref[0]
    n = _rms_mod(x, g_ref[...], shift_ref[0], scale_ref[0]).astype(BF16)
    acc = None
    for j in range(FFN_SPLIT):
        a = jnp.dot(n, wi_ref[:, j * tf:(j + 1) * tf], preferred_element_type=F32)
        b = jnp.dot(n, wi_ref[:, F + j * tf:F + (j + 1) * tf], preferred_element_type=F32)
        hm = (_silu(a) * b).astype(BF16)
        part = jnp.dot(hm, wo_ref[j * tf:(j + 1) * tf, :], preferred_element_type=F32)
        acc = part if acc is None else acc + part
    y = x + (0.5 * gate_ref[0]) * acc
    if final_norm:
        y = y * lax.rsqrt(jnp.mean(y * y, axis=-1, keepdims=True) + EPS) * fg_ref[...]
    o_ref[0] = y


def _ffn(h, shift, scale, gate, g, w_in, w_out, final_g=None):
    B, S, D = h.shape
    tm = FFN_ROWS
    row = pl.BlockSpec((1, tm, D), lambda b, i: (b, i, 0))
    vec = pl.BlockSpec((1, 1, D), lambda b, i: (b, 0, 0))
    gain = pl.BlockSpec((1, D), lambda b, i: (0, 0))

    def resident(a):
        return pl.BlockSpec(a.shape, lambda b, i: (0, 0), pipeline_mode=pl.Buffered(1))

    extra = () if final_g is None else (final_g,)
    return pl.pallas_call(
        functools.partial(_ffn_kernel, final_norm=final_g is not None),
        grid=(B, S // tm),
        in_specs=[row, vec, vec, vec, gain, resident(w_in), resident(w_out)]
        + [gain] * len(extra),
        out_specs=row,
        out_shape=jax.ShapeDtypeStruct((B, S, D), F32),
        compiler_params=pltpu.CompilerParams(
            dimension_semantics=("parallel", "parallel"), vmem_limit_bytes=VMEM_LIMIT),
        name="ffn",
    )(h, shift, scale, gate, g, w_in, w_out, *extra)


def _rope_pair(x, c, s):
    lane = lax.broadcasted_iota(jnp.int32, x.shape, 1)
    first_half = (lane & (HEAD_DIM - 1)) < HEAD_DIM // 2
    swapped = jnp.where(first_half, pltpu.roll(x, LANE - HEAD_DIM // 2, 1),
                        pltpu.roll(x, HEAD_DIM // 2, 1))
    return x * c + swapped * s


def _mix_in_kernel(h_ref, shift_ref, scale_ref, g_ref, wq_ref, wkv_ref, wg_ref, wuv_ref,
                   wga_ref, wgb_ref, cos_ref, sin_ref, oh_ref, lng_ref, lnb_ref, ws_ref,
                   bsx_ref, pb_ref,
                   qt_ref, kcmp_ref, vcmp_ref, kselx_ref, vselt_ref, kwinx_ref, vwint_ref,
                   gt_ref, ga_ref, mb_ref):
    tm = h_ref.shape[1]
    CK = vselt_ref.shape[4]
    n = _rms_mod(h_ref[0], g_ref[...], shift_ref[0], scale_ref[0]).astype(BF16)
    cos = cos_ref[...]
    sin = sin_ref[...]
    low_lanes = lax.broadcasted_iota(jnp.int32, (tm, LANE), 1) < HEAD_DIM

    q = jnp.dot(n, wq_ref[...], preferred_element_type=F32)
    for p in range(NSA_HEADS // 2):
        qp = jnp.transpose(_rope_pair(q[:, p * LANE:(p + 1) * LANE], cos, sin)).astype(BF16)
        qt_ref[0, 2 * p] = qp[:HEAD_DIM]
        qt_ref[0, 2 * p + 1] = qp[HEAD_DIM:]

    kv = jnp.dot(n, wkv_ref[...], preferred_element_type=F32)
    kcmp_ref[0] = kv[:, 0 * LANE:1 * LANE]
    vcmp_ref[0] = kv[:, 1 * LANE:2 * LANE]
    for idx, ref, ext in ((2, kselx_ref, oh_ref[...]), (4, kwinx_ref, 0.0)):
        k2 = _rope_pair(kv[:, idx * LANE:(idx + 1) * LANE], cos, sin)
        ref[0, 0] = jnp.where(low_lanes, k2, ext).astype(BF16)
        ref[0, 1] = jnp.where(low_lanes, pltpu.roll(k2, HEAD_DIM, 1), ext).astype(BF16)
    ones_row = jnp.where(lax.broadcasted_iota(jnp.int32, (V_ROWS - HEAD_DIM, CK), 0) == 0,
                         1.0, 0.0).astype(BF16)
    for idx, ref in ((3, vselt_ref), (5, vwint_ref)):
        vt = jnp.transpose(kv[:, idx * LANE:(idx + 1) * LANE]).astype(BF16)
        for g in range(KV_GROUPS):
            for r in range(tm // CK):
                ref[0, g, r] = jnp.concatenate(
                    [vt[g * HEAD_DIM:(g + 1) * HEAD_DIM, r * CK:(r + 1) * CK], ones_row], axis=0)

    gates = _sigmoid(jnp.dot(n, wg_ref[...], preferred_element_type=F32))
    gt_ref[0] = jnp.transpose(gates)[:GATE_ROWS]
    ga_ref[0] = _sigmoid(jnp.dot(n, wga_ref[...], preferred_element_type=F32))

    uv = jnp.dot(n, wuv_ref[...], preferred_element_type=F32)
    ge = uv * (0.5 * (1.0 + jnp.tanh(GELU_C * (uv + 0.044715 * (uv * uv * uv)))))
    u = ge[:, :GM_WIDTH]
    v = ge[:, GM_WIDTH:]
    mu = jnp.mean(v, axis=-1, keepdims=True)
    var = jnp.mean(jnp.square(v - mu), axis=-1, keepdims=True)
    vln = ((v - mu) * lax.rsqrt(var + EPS) * lng_ref[...] + lnb_ref[...]).astype(BF16)

    ti = lax.broadcasted_iota(jnp.int32, (GM_CHUNK, GM_CHUNK), 0)
    si = lax.broadcasted_iota(jnp.int32, (GM_CHUNK, GM_CHUNK), 1)
    tril = si <= ti
    wm = [jnp.where(tril, ws_ref[gg], 0.0).astype(BF16) for gg in range(GM_GROUPS)]
    low = lax.broadcasted_iota(jnp.int32, (GM_CHUNK, LANE), 1) < HEAD_DIM
    bsx = bsx_ref[...]
    yb_rows = []
    for r in range(tm // GM_CHUNK):
        vch = vln[r * GM_CHUNK:(r + 1) * GM_CHUNK]
        pieces = []
        for p in range(GM_GROUPS // 2):
            vp = vch[:, p * LANE:(p + 1) * LANE]
            a0 = jnp.dot(wm[2 * p], vp, preferred_element_type=F32)
            a1 = jnp.dot(wm[2 * p + 1], vp, preferred_element_type=F32)
            pieces.append(jnp.where(low, a0, a1))
        sv = jnp.concatenate(pieces, axis=1) + bsx
        yb_rows.append(u[r * GM_CHUNK:(r + 1) * GM_CHUNK] * sv)
    yb = jnp.concatenate(yb_rows, axis=0).astype(BF16)
    gb = _sigmoid(jnp.dot(n, wgb_ref[...], preferred_element_type=F32))
    mb_ref[0] = gb * jnp.dot(yb, pb_ref[...], preferred_element_type=F32)


def _mix_in(h, shift, scale, g, wq, wkv, wg, wuv, wga, wgb, cos_t, sin_t, oh_t, ln_g, ln_b, ws,
            bsx, proj_b):
    B, S, D = h.shape
    tm = MIX_ROWS
    G = KV_GROUPS
    CK = ATT_CHUNK

    def const(a):
        nd = a.ndim
        return pl.BlockSpec(a.shape, lambda b, i: (0,) * nd)

    row = lambda w: pl.BlockSpec((1, tm, w), lambda b, i: (b, i, 0))
    vec = pl.BlockSpec((1, 1, D), lambda b, i: (b, 0, 0))
    tab = pl.BlockSpec((tm, LANE), lambda b, i: (i, 0))
    kspec = pl.BlockSpec((1, G, tm, LANE), lambda b, i: (b, 0, i, 0))
    kshape = jax.ShapeDtypeStruct((B, G, S, LANE), BF16)
    vspec = pl.BlockSpec((1, G, tm // CK, V_ROWS, CK), lambda b, i: (b, 0, i, 0, 0))
    vshape = jax.ShapeDtypeStruct((B, G, S // CK, V_ROWS, CK), BF16)
    return pl.pallas_call(
        _mix_in_kernel,
        grid=(B, S // tm),
        in_specs=[row(D), vec, vec, const(g), const(wq), const(wkv), const(wg), const(wuv),
                  const(wga), const(wgb), tab, tab, tab, const(ln_g), const(ln_b), const(ws),
                  const(bsx), const(proj_b)],
        out_specs=[
            pl.BlockSpec((1, NSA_HEADS, HEAD_DIM, tm), lambda b, i: (b, 0, 0, i)),
            row(LANE), row(LANE), kspec, vspec, kspec, vspec,
            pl.BlockSpec((1, GATE_ROWS, tm), lambda b, i: (b, 0, i)),
            row(D), row(D)],
        out_shape=[
            jax.ShapeDtypeStruct((B, NSA_HEADS, HEAD_DIM, S), BF16),
            jax.ShapeDtypeStruct((B, S, LANE), F32), jax.ShapeDtypeStruct((B, S, LANE), F32),
            kshape, vshape, kshape, vshape,
            jax.ShapeDtypeStruct((B, GATE_ROWS, S), F32),
            jax.ShapeDtypeStruct((B, S, D), F32), jax.ShapeDtypeStruct((B, S, D), F32)],
        compiler_params=pltpu.CompilerParams(
            dimension_semantics=("parallel", "parallel"), vmem_limit_bytes=VMEM_LIMIT),
        name="mix_in",
    )(h, shift, scale, g, wq, wkv, wg, wuv, wga, wgb, cos_t, sin_t, oh_t, ln_g, ln_b, ws, bsx,
      proj_b)


def _compress_kernel(k16_ref, v16_ref, pe_ref, wk_ref, wv_ref, w2k_ref, w2v_ref, cos_ref,
                     sin_ref, kc_ref, vct_ref):
    def hidden(x16, pe_a, pe_b, w_ref):
        ha = jnp.dot((x16 + pe_a).astype(BF16), w_ref[0], preferred_element_type=F32)
        hb = jnp.dot((x16 + pe_b).astype(BF16), w_ref[1], preferred_element_type=F32)
        return _silu(ha + pltpu.roll(hb, hb.shape[0] - 1, 0))

    hk = hidden(k16_ref[0], pe_ref[0], pe_ref[1], wk_ref)
    hv = hidden(v16_ref[0], pe_ref[2], pe_ref[3], wv_ref)
    for g in range(KV_GROUPS):
        k2 = jnp.dot(hk[:, g * CMP_HIDDEN:(g + 1) * CMP_HIDDEN].astype(BF16), w2k_ref[...],
                     preferred_element_type=F32)
        kc = k2[:, :HEAD_DIM] * cos_ref[...] + k2[:, HEAD_DIM:] * sin_ref[...]
        kc_ref[0, g] = kc.astype(BF16)
        vc = jnp.dot(hv[:, g * CMP_HIDDEN:(g + 1) * CMP_HIDDEN].astype(BF16), w2v_ref[...],
                     preferred_element_type=F32)
        vct_ref[0, g] = jnp.transpose(vc)[:HEAD_DIM].astype(BF16)


def _compress(k16, v16, pe, wk, wv, w2k, w2v, cos_c, sin_c):
    B, NC, W = k16.shape

    def const(a):
        nd = a.ndim
        return pl.BlockSpec(a.shape, lambda b: (0,) * nd)

    blk = pl.BlockSpec((1, NC, W), lambda b: (b, 0, 0))
    return pl.pallas_call(
        _compress_kernel,
        grid=(B,),
        in_specs=[blk, blk, const(pe), const(wk), const(wv), const(w2k), const(w2v),
                  const(cos_c), const(sin_c)],
        out_specs=[pl.BlockSpec((1, KV_GROUPS, NC, HEAD_DIM), lambda b: (b, 0, 0, 0)),
                   pl.BlockSpec((1, KV_GROUPS, HEAD_DIM, NC), lambda b: (b, 0, 0, 0))],
        out_shape=[jax.ShapeDtypeStruct((B, KV_GROUPS, NC, HEAD_DIM), BF16),
                   jax.ShapeDtypeStruct((B, KV_GROUPS, HEAD_DIM, NC), BF16)],
        compiler_params=pltpu.CompilerParams(
            dimension_semantics=("parallel",), vmem_limit_bytes=VMEM_LIMIT),
        name="compress",
    )(k16, v16, pe, wk, wv, w2k, w2v, cos_c, sin_c)


def _fold_rows(x, op):
    parts = [x[8 * i:8 * (i + 1)] for i in range(x.shape[0] // 8)]
    while len(parts) > 1:
        parts = [op(parts[i], parts[i + 1]) for i in range(0, len(parts), 2)]
    return parts[0]


def _attn_kernel(qt_ref, kselx_ref, vselt_ref, kwinx_ref, vwint_ref, kc_ref, vct_ref, gt_ref,
                 ga_ref, mb_ref, h_ref, gate_ref, ovt_ref, bias_ref, pa_ref, wo_ref,
                 o_ref, qx_scr, s_scr, m_scr, a_scr, acc_scr, yt_scr):
    TQ = h_ref.shape[1]
    CK = ATT_CHUNK
    NC = kc_ref.shape[2]
    NSEL = kselx_ref.shape[2] // SEL_BLOCK
    nwin = WINDOW // CK
    qi = pl.program_id(1)

    def branch(kx_ref, vt_ref, first, far, gate_row):
        def score(c, masked, dst):
            kx = [kx_ref[0, g, pl.ds(pl.multiple_of(c * CK, CK), CK), :]
                  for g in range(KV_GROUPS)]
            if masked:
                bias = bias_ref[jnp.where(c == qi, 1, jnp.where(c == far, 2, 0))]
            for hd in range(NSA_HEADS):
                s = jnp.dot(kx[hd // HPG], qx_scr[hd], preferred_element_type=F32)
                if masked:
                    s = s + bias
                s_scr[dst, hd] = s
                m_prev = m_scr[1 - dst, hd]
                m_new = jnp.maximum(m_prev, jnp.max(_fold_rows(s, jnp.maximum), axis=0,
                                                    keepdims=True))
                a_scr[dst, hd] = jnp.exp2(m_prev - m_new)
                m_scr[dst, hd] = m_new

        def weigh(c, src):
            vt = [vt_ref[0, g, c] for g in range(KV_GROUPS)]
            for hd in range(NSA_HEADS):
                p = jnp.exp2(s_scr[src, hd] - m_scr[src, hd]).astype(BF16)
                acc_scr[hd] = a_scr[src, hd] * acc_scr[hd] + jnp.dot(
                    vt[hd // HPG], p, preferred_element_type=F32)

        def loop(lo, hi, body):
            def wrapped(c, carry):
                body(c)
                return carry
            lax.fori_loop(lo, hi, wrapped, 0)

        def two_chunks(i):
            c = first + 2 * i
            score(c + 1, False, 1)
            weigh(c, 0)
            score(c + 2, True, 0)
            weigh(c + 1, 1)

        def last_two(c):
            score(c + 1, True, 1)
            weigh(c, 0)

        m_scr[...] = jnp.full_like(m_scr, NEG)
        acc_scr[...] = jnp.zeros_like(acc_scr)
        loop(first, first + 1, lambda c: score(c, True, 0))
        rest = qi - first
        pairs = jnp.right_shift(rest, 1)
        odd = rest & 1
        loop(0, pairs, two_chunks)
        loop(qi - 1, qi - 1 + odd, last_two)
        loop(qi, qi + odd, lambda c: weigh(c, 1))
        loop(qi, qi + 1 - odd, lambda c: weigh(c, 0))
        for hd in range(NSA_HEADS):
            w = gt_ref[0, 3 * hd + gate_row:3 * hd + gate_row + 1, :] / acc_scr[
                hd, HEAD_DIM:HEAD_DIM + 1, :]
            rows = slice(hd * HEAD_DIM, (hd + 1) * HEAD_DIM)
            yt_scr[rows, :] = yt_scr[rows, :] + w * acc_scr[hd, :HEAD_DIM, :]

    n_idx = lax.broadcasted_iota(jnp.int32, (NC, TQ), 0)
    t_cmp = qi * TQ + lax.broadcasted_iota(jnp.int32, (NC, TQ), 1)
    vis = (n_idx * CMP_STRIDE + (CMP_BLOCK - 1)) <= t_cmp
    visf = jnp.where(vis, 1.0, 0.0)
    j_idx = lax.broadcasted_iota(jnp.int32, (NSEL, TQ), 0)
    t_sel = qi * TQ + lax.broadcasted_iota(jnp.int32, (NSEL, TQ), 1)
    cur = jnp.right_shift(t_sel, SEL_BLOCK.bit_length() - 1)
    forced = (j_idx == 0) | (j_idx == cur) | (j_idx == cur - 1)
    valid = j_idx <= cur

    heads = range(NSA_HEADS)
    m_c = []
    for hd in heads:
        sc = jnp.where(vis, jnp.dot(kc_ref[0, hd // HPG], qt_ref[0, hd],
                                    preferred_element_type=F32), NEG)
        s_scr[0, hd, :NC, :] = sc
        m_c.append(jnp.max(_fold_rows(sc, jnp.maximum), axis=0, keepdims=True))
    inv = []
    for hd in heads:
        ec = jnp.exp2(s_scr[0, hd, :NC, :] - m_c[hd]) * visf
        s_scr[0, hd, :NC, :] = ec
        den = jnp.sum(_fold_rows(ec, jnp.add), axis=0, keepdims=True)
        inv.append(1.0 / jnp.where(den > 0.0, den, 1.0))
    psum = [jnp.zeros((NC, TQ), F32) for _ in range(KV_GROUPS)]
    for hd in heads:
        g = hd // HPG
        pc = s_scr[0, hd, :NC, :] * inv[hd]
        psum[g] = psum[g] + pc
        oc = jnp.dot(vct_ref[0, g], pc.astype(BF16), preferred_element_type=F32)
        yt_scr[hd * HEAD_DIM:(hd + 1) * HEAD_DIM, :] = gt_ref[0, 3 * hd:3 * hd + 1, :] * oc

    row8 = lax.broadcasted_iota(jnp.int32, (8, TQ), 0)
    pad = jnp.zeros((LANE - HEAD_DIM - NSEL, TQ), BF16)
    for g in range(KV_GROUPS):
        imp = jnp.dot(ovt_ref[...], psum[g], preferred_element_type=F32,
                      precision=lax.Precision.HIGHEST)
        imp = jnp.where(forced, FORCE, jnp.where(valid, imp, -FORCE))
        tiles = [imp[8 * r:8 * (r + 1)] for r in range(NSEL // 8)]
        rank = [jnp.zeros((8, TQ), jnp.int32) for _ in tiles]
        for kk in range(NSEL):
            rk = imp[kk:kk + 1, :]
            for r, tile in enumerate(tiles):
                if 8 * r > kk:
                    ahead = rk >= tile
                elif 8 * r + 7 < kk:
                    ahead = rk > tile
                else:
                    ahead = (rk > tile) | ((rk == tile) & (row8 > kk - 8 * r))
                rank[r] = rank[r] + jnp.where(ahead, 1, 0)
        selneg = jnp.where(jnp.concatenate(rank, axis=0) < SEL_TOPK, 0.0, NEG).astype(BF16)
        for hh in range(HPG):
            hd = g * HPG + hh
            qx_scr[hd] = jnp.concatenate([qt_ref[0, hd], selneg, pad], axis=0)

    branch(kselx_ref, vselt_ref, 0, -1, 1)
    branch(kwinx_ref, vwint_ref, jnp.maximum(qi - nwin, 0), qi - nwin, 2)

    y = jnp.transpose(yt_scr[...]).astype(BF16)
    ya = jnp.dot(y, pa_ref[...], preferred_element_type=F32)
    merged = (ga_ref[0] * ya + mb_ref[0]).astype(BF16)
    o_ref[0] = h_ref[0] + gate_ref[0] * jnp.dot(merged, wo_ref[...],
                                                preferred_element_type=F32)


def _attn(qt, kselx, vselt, kwinx, vwint, kc, vct, gt, ga, mb, h, gate, ovt, bias,
          proj_a, w_out):
    B, S, D = h.shape
    TQ = ATT_ROWS
    CK = ATT_CHUNK
    assert TQ == CK and WINDOW % CK == 0 and S % TQ == 0

    def const(a):
        nd = a.ndim
        return pl.BlockSpec(a.shape, lambda b, i: (0,) * nd)

    def per_batch(a):
        nd = a.ndim
        return pl.BlockSpec((1,) + a.shape[1:], lambda b, i: (b,) + (0,) * (nd - 1))

    row = lambda w: pl.BlockSpec((1, TQ, w), lambda b, i: (b, i, 0))
    return pl.pallas_call(
        _attn_kernel,
        grid=(B, S // TQ),
        in_specs=[
            pl.BlockSpec((1, NSA_HEADS, HEAD_DIM, TQ), lambda b, i: (b, 0, 0, i)),
            per_batch(kselx), per_batch(vselt), per_batch(kwinx), per_batch(vwint),
            per_batch(kc), per_batch(vct),
            pl.BlockSpec((1, GATE_ROWS, TQ), lambda b, i: (b, 0, i)),
            row(D), row(D), row(D),
            pl.BlockSpec((1, 1, D), lambda b, i: (b, 0, 0)),
            const(ovt), const(bias), const(proj_a), const(w_out)],
        out_specs=row(D),
        out_shape=jax.ShapeDtypeStruct((B, S, D), F32),
        scratch_shapes=[
            pltpu.VMEM((NSA_HEADS, LANE, TQ), BF16),
            pltpu.VMEM((2, NSA_HEADS, CK, TQ), F32),
            pltpu.VMEM((2, NSA_HEADS, 1, TQ), F32), pltpu.VMEM((2, NSA_HEADS, 1, TQ), F32),
            pltpu.VMEM((NSA_HEADS, V_ROWS, TQ), F32),
            pltpu.VMEM((NSA_WIDTH, TQ), F32)],
        compiler_params=pltpu.CompilerParams(
            dimension_semantics=("parallel", "parallel"), vmem_limit_bytes=VMEM_LIMIT),
        name="attn",
    )(qt, kselx, vselt, kwinx, vwint, kc, vct, gt, ga, mb, h, gate, ovt, bias,
      proj_a, w_out)


def _rope_tables(pos):
    inv = 1.0 / (ROPE_THETA ** (np.arange(0, HEAD_DIM, 2, dtype=np.float64) / HEAD_DIM))
    ang = np.asarray(pos, np.float64)[:, None] * inv[None, :]
    cos = np.concatenate([np.cos(ang), np.cos(ang)], axis=1)
    sin = np.concatenate([-np.sin(ang), np.sin(ang)], axis=1)
    return cos.astype(np.float32), sin.astype(np.float32)


def _tables(S):
    cos, sin = _rope_tables(np.arange(S))
    cos_t = np.concatenate([cos, cos], axis=1)
    sin_t = np.concatenate([sin, sin], axis=1)
    n_cmp_pad = S // CMP_STRIDE
    starts = np.arange(n_cmp_pad) * CMP_STRIDE
    cos_c, sin_c = _rope_tables(starts + CMP_BLOCK - 1)
    n_sel = S // SEL_BLOCK
    sel_start = np.arange(n_sel) * SEL_BLOCK
    overlap = np.clip(np.minimum(starts[:, None] + CMP_BLOCK, sel_start[None, :] + SEL_BLOCK)
                      - np.maximum(starts[:, None], sel_start[None, :]), 0, None) / CMP_BLOCK
    ovt = np.ascontiguousarray(overlap.T).astype(np.float32)
    oh = np.zeros((S, LANE), np.float32)
    oh[np.arange(S), HEAD_DIM + np.arange(S) // SEL_BLOCK] = 1.0
    j = np.arange(ATT_CHUNK)[:, None]
    i = np.arange(ATT_ROWS)[None, :]
    bias = np.stack([np.zeros((ATT_CHUNK, ATT_ROWS)),
                     np.where(j <= i, 0.0, NEG),
                     np.where(j > i, 0.0, NEG)]
                    ).astype(np.float32)
    return tuple(jnp.asarray(a) for a in (cos_t, sin_t, cos_c, sin_c, ovt, oh, bias))


def _compress_weights(w1):
    eye = jnp.eye(KV_GROUPS, dtype=w1.dtype)
    half = CMP_BLOCK // 2

    def one(w):
        return jnp.einsum('ldf,gh->lgdhf', w, eye).reshape(half * KV_GROUPS * HEAD_DIM,
                                                           KV_GROUPS * CMP_HIDDEN)

    return jnp.stack([one(w1[:half]), one(w1[half:])]).astype(BF16)


def _compress_pe(pe):
    half = CMP_BLOCK // 2

    def one(p):
        return jnp.broadcast_to(p[:, None, :], (half, KV_GROUPS, HEAD_DIM)).reshape(1, -1)

    return one(pe[:half]), one(pe[half:])


def kernel(x, c, ada_w, ada_b, norm_g, ffn_w_in, ffn_w_out, mix_w_in, cmp_pe, cmp_w1, cmp_w2,
           gm_ln_g, gm_ln_b, gm_ws, gm_bs, proj_a, proj_b, w_out, final_g):
    B, S, D = x.shape
    L = ada_w.shape[0]
    assert S // SEL_BLOCK + HEAD_DIM <= LANE and 3 * NSA_HEADS <= GATE_ROWS
    cos_t, sin_t, cos_c, sin_c, ovt, oh_t, bias = _tables(S)
    swap = np.concatenate([np.arange(HEAD_DIM // 2, HEAD_DIM), np.arange(HEAD_DIM // 2)])

    mods = _ada(c, ada_w, ada_b).reshape(L, B, 3, 3, 1, D)
    h = x
    for l in range(L):
        mod = mods[l]
        h = _ffn(h, mod[:, 0, 0], mod[:, 0, 1], mod[:, 0, 2], norm_g[l, 0][None],
                 ffn_w_in[l, 0].astype(BF16), ffn_w_out[l, 0].astype(BF16))

        w = mix_w_in[l]
        o_kv = NSA_WIDTH
        o_g = o_kv + 6 * KV_GROUPS * HEAD_DIM
        o_uv = o_g + 3 * NSA_HEADS
        o_ga = o_uv + 2 * GM_WIDTH
        o_gb = o_ga + D
        wq = (w[:, :o_kv] * (HEAD_DIM ** -0.5 * LOG2E)).astype(BF16)
        wkv = w[:, o_kv:o_g].astype(BF16)
        wg = jnp.pad(w[:, o_g:o_uv], ((0, 0), (0, LANE - 3 * NSA_HEADS))).astype(BF16)
        wuv = w[:, o_uv:o_ga].astype(BF16)
        wga = w[:, o_ga:o_gb].astype(BF16)
        wgb = w[:, o_gb:].astype(BF16)
        bsx = jnp.repeat(gm_bs[l].T, HEAD_DIM, axis=1)
        qt, kcmp, vcmp, kselx, vselt, kwinx, vwint, gt, ga, mb = _mix_in(
            h, mod[:, 1, 0], mod[:, 1, 1], norm_g[l, 1][None], wq, wkv, wg, wuv, wga, wgb,
            cos_t, sin_t, oh_t, gm_ln_g[l][None], gm_ln_b[l][None], gm_ws[l], bsx,
            proj_b[l].astype(BF16))

        pe_ka, pe_kb = _compress_pe(cmp_pe[l, 0])
        pe_va, pe_vb = _compress_pe(cmp_pe[l, 1])
        pe = jnp.stack([pe_ka, pe_kb, pe_va, pe_vb])
        w2k = jnp.concatenate([cmp_w2[l, 0], cmp_w2[l, 0][:, swap]], axis=1).astype(BF16)
        w2v = jnp.pad(cmp_w2[l, 1], ((0, 0), (0, LANE - HEAD_DIM))).astype(BF16)
        nc = S // CMP_STRIDE
        kc, vct = _compress(kcmp.reshape(B, nc, CMP_STRIDE * LANE),
                            vcmp.reshape(B, nc, CMP_STRIDE * LANE), pe,
                            _compress_weights(cmp_w1[l, 0]), _compress_weights(cmp_w1[l, 1]),
                            w2k, w2v, cos_c, sin_c)

        h = _attn(qt, kselx, vselt, kwinx, vwint, kc, vct, gt, ga, mb, h, mod[:, 1, 2], ovt,
                  bias, proj_a[l].astype(BF16), w_out[l].astype(BF16))

        h = _ffn(h, mod[:, 2, 0], mod[:, 2, 1], mod[:, 2, 2], norm_g[l, 2][None],
                 ffn_w_in[l, 1].astype(BF16), ffn_w_out[l, 1].astype(BF16),
                 final_g=final_g[None] if l == L - 1 else None)
    return h
```

```python
import functools

import numpy as np
import jax
import jax.numpy as jnp
from jax import lax
from jax.experimental import pallas as pl
from jax.experimental.pallas import tpu as pltpu

F32 = jnp.float32
BF16 = jnp.bfloat16

HEAD_DIM = 64
NSA_HEADS = 8
KV_GROUPS = 2
HPG = NSA_HEADS // KV_GROUPS
NSA_WIDTH = NSA_HEADS * HEAD_DIM
CMP_BLOCK = 32
CMP_STRIDE = 16
CMP_HIDDEN = 128
SEL_BLOCK = 64
SEL_TOPK = 16
WINDOW = 512
GM_GROUPS = 8
GM_CHUNK = 128
GM_WIDTH = GM_GROUPS * HEAD_DIM
ROPE_THETA = 10000.0
EPS = 1e-6
NEG = -1e30
FORCE = 1e4
LANE = 128
GELU_C = float(np.sqrt(2.0 / np.pi))
LOG2E = float(np.log2(np.e))
GATE_ROWS = 32
V_ROWS = HEAD_DIM + 16

FFN_ROWS = 512
FFN_SPLIT = 2
MIX_ROWS = 512
ATT_ROWS = 256
ATT_CHUNK = 256
VMEM_LIMIT = 56 * 1024 * 1024


def _sigmoid(x):
    return 1.0 / (1.0 + jnp.exp(-x))


def _silu(x):
    return x * _sigmoid(x)


def _rms_mod(x, g, shift, scale):
    y = x * lax.rsqrt(jnp.mean(x * x, axis=-1, keepdims=True) + EPS) * g
    return y * (1.0 + scale) + shift


def _ada_kernel(c_ref, w_ref, b_ref, o_ref):
    s = _silu(c_ref[...])
    o_ref[0] = jnp.dot(s, w_ref[0], preferred_element_type=F32,
                       precision=lax.Precision.HIGHEST) + b_ref[0]


def _ada(c, ada_w, ada_b):
    L, D, N = ada_w.shape
    B = c.shape[0]
    tn = 1024
    return pl.pallas_call(
        _ada_kernel,
        grid=(L, N // tn),
        in_specs=[
            pl.BlockSpec((B, D), lambda l, j: (0, 0)),
            pl.BlockSpec((1, D, tn), lambda l, j: (l, 0, j)),
            pl.BlockSpec((1, 1, tn), lambda l, j: (l, 0, j)),
        ],
        out_specs=pl.BlockSpec((1, B, tn), lambda l, j: (l, 0, j)),
        out_shape=jax.ShapeDtypeStruct((L, B, N), F32),
        compiler_params=pltpu.CompilerParams(
            dimension_semantics=("parallel", "parallel"), vmem_limit_bytes=VMEM_LIMIT),
        name="ada",
    )(c, ada_w, ada_b.reshape(L, 1, N))


def _ffn_kernel(h_ref, shift_ref, scale_ref, gate_ref, g_ref, wi_ref, wo_ref, *rest,
                final_norm):
    fg_ref = rest[0] if final_norm else None
    o_ref = rest[-1]
    F = wo_ref.shape[0]
    tf = F // FFN_SPLIT
    x = h_ref[0]
    n = _rms_mod(x, g_ref[...], shift_ref[0], scale_ref[0]).astype(BF16)
    acc = None
    for j in range(FFN_SPLIT):
        a = jnp.dot(n, wi_ref[:, j * tf:(j + 1) * tf], preferred_element_type=F32)
        b = jnp.dot(n, wi_ref[:, F + j * tf:F + (j + 1) * tf], preferred_element_type=F32)
        hm = (_silu(a) * b).astype(BF16)
        part = jnp.dot(hm, wo_ref[j * tf:(j + 1) * tf, :], preferred_element_type=F32)
        acc = part if acc is None else acc + part
    y = x + (0.5 * gate_ref[0]) * acc
    if final_norm:
        y = y * lax.rsqrt(jnp.mean(y * y, axis=-1, keepdims=True) + EPS) * fg_ref[...]
    o_ref[0] = y


def _ffn(h, shift, scale, gate, g, w_in, w_out, final_g=None):
    B, S, D = h.shape
    tm = FFN_ROWS
    row = pl.BlockSpec((1, tm, D), lambda b, i: (b, i, 0))
    vec = pl.BlockSpec((1, 1, D), lambda b, i: (b, 0, 0))
    gain = pl.BlockSpec((1, D), lambda b, i: (0, 0))

    def resident(a):
        return pl.BlockSpec(a.shape, lambda b, i: (0, 0), pipeline_mode=pl.Buffered(1))

    extra = () if final_g is None else (final_g,)
    return pl.pallas_call(
        functools.partial(_ffn_kernel, final_norm=final_g is not None),
        grid=(B, S // tm),
        in_specs=[row, vec, vec, vec, gain, resident(w_in), resident(w_out)]
        + [gain] * len(extra),
        out_specs=row,
        out_shape=jax.ShapeDtypeStruct((B, S, D), F32),
        compiler_params=pltpu.CompilerParams(
            dimension_semantics=("parallel", "parallel"), vmem_limit_bytes=VMEM_LIMIT),
        name="ffn",
    )(h, shift, scale, gate, g, w_in, w_out, *extra)


def _rope_pair(x, c, s):
    lane = lax.broadcasted_iota(jnp.int32, x.shape, 1)
    first_half = (lane & (HEAD_DIM - 1)) < HEAD_DIM // 2
    swapped = jnp.where(first_half, pltpu.roll(x, LANE - HEAD_DIM // 2, 1),
                        pltpu.roll(x, HEAD_DIM // 2, 1))
    return x * c + swapped * s


def _mix_cols(D):
    widths = (NSA_WIDTH, 6 * KV_GROUPS * HEAD_DIM, LANE, 2 * GM_WIDTH, D, D)
    edges = np.cumsum((0,) + widths)
    return [slice(int(a), int(b)) for a, b in zip(edges[:-1], edges[1:])]


def _mix_in_kernel(h_ref, shift_ref, scale_ref, g_ref, w_ref, cos_ref, sin_ref, oh_ref,
                   lng_ref, lnb_ref, ws_ref, bsx_ref, pb_ref,
                   qt_ref, kcmp_ref, vcmp_ref, kselx_ref, vselt_ref, kwinx_ref, vwint_ref,
                   gt_ref, ga_ref, mb_ref):
    tm = h_ref.shape[1]
    CK = vselt_ref.shape[4]
    c_q, c_kv, c_g, c_uv, c_ga, c_gb = _mix_cols(h_ref.shape[2])
    n = _rms_mod(h_ref[0], g_ref[...], shift_ref[0], scale_ref[0]).astype(BF16)
    cos = cos_ref[...]
    sin = sin_ref[...]
    low_lanes = lax.broadcasted_iota(jnp.int32, (tm, LANE), 1) < HEAD_DIM

    q = jnp.dot(n, w_ref[:, c_q], preferred_element_type=F32)
    for p in range(NSA_HEADS // 2):
        qp = jnp.transpose(_rope_pair(q[:, p * LANE:(p + 1) * LANE], cos, sin)).astype(BF16)
        qt_ref[0, 2 * p] = qp[:HEAD_DIM]
        qt_ref[0, 2 * p + 1] = qp[HEAD_DIM:]

    kv = jnp.dot(n, w_ref[:, c_kv], preferred_element_type=F32)
    kcmp_ref[0] = kv[:, 0 * LANE:1 * LANE]
    vcmp_ref[0] = kv[:, 1 * LANE:2 * LANE]
    for idx, ref, ext in ((2, kselx_ref, oh_ref[...]), (4, kwinx_ref, 0.0)):
        k2 = _rope_pair(kv[:, idx * LANE:(idx + 1) * LANE], cos, sin)
        ref[0, 0] = jnp.where(low_lanes, k2, ext).astype(BF16)
        ref[0, 1] = jnp.where(low_lanes, pltpu.roll(k2, HEAD_DIM, 1), ext).astype(BF16)
    ones_row = jnp.where(lax.broadcasted_iota(jnp.int32, (V_ROWS - HEAD_DIM, CK), 0) == 0,
                         1.0, 0.0).astype(BF16)
    for idx, ref in ((3, vselt_ref), (5, vwint_ref)):
        vt = jnp.transpose(kv[:, idx * LANE:(idx + 1) * LANE]).astype(BF16)
        for g in range(KV_GROUPS):
            for r in range(tm // CK):
                ref[0, g, r] = jnp.concatenate(
                    [vt[g * HEAD_DIM:(g + 1) * HEAD_DIM, r * CK:(r + 1) * CK], ones_row], axis=0)

    gates = _sigmoid(jnp.dot(n, w_ref[:, c_g], preferred_element_type=F32))
    gt_ref[0] = jnp.transpose(gates)[:GATE_ROWS]
    ga_ref[0] = _sigmoid(jnp.dot(n, w_ref[:, c_ga], preferred_element_type=F32))

    uv = jnp.dot(n, w_ref[:, c_uv], preferred_element_type=F32)
    ge = uv * (0.5 * (1.0 + jnp.tanh(GELU_C * (uv + 0.044715 * (uv * uv * uv)))))
    u = ge[:, :GM_WIDTH]
    v = ge[:, GM_WIDTH:]
    mu = jnp.mean(v, axis=-1, keepdims=True)
    var = jnp.mean(jnp.square(v - mu), axis=-1, keepdims=True)
    vln = ((v - mu) * lax.rsqrt(var + EPS) * lng_ref[...] + lnb_ref[...]).astype(BF16)

    ti = lax.broadcasted_iota(jnp.int32, (GM_CHUNK, GM_CHUNK), 0)
    si = lax.broadcasted_iota(jnp.int32, (GM_CHUNK, GM_CHUNK), 1)
    tril = si <= ti
    wm = [jnp.where(tril, ws_ref[gg], 0.0).astype(BF16) for gg in range(GM_GROUPS)]
    low = lax.broadcasted_iota(jnp.int32, (GM_CHUNK, LANE), 1) < HEAD_DIM
    bsx = bsx_ref[...]
    yb_rows = []
    for r in range(tm // GM_CHUNK):
        vch = vln[r * GM_CHUNK:(r + 1) * GM_CHUNK]
        pieces = []
        for p in range(GM_GROUPS // 2):
            vp = vch[:, p * LANE:(p + 1) * LANE]
            a0 = jnp.dot(wm[2 * p], vp, preferred_element_type=F32)
            a1 = jnp.dot(wm[2 * p + 1], vp, preferred_element_type=F32)
            pieces.append(jnp.where(low, a0, a1))
        sv = jnp.concatenate(pieces, axis=1) + bsx
        yb_rows.append(u[r * GM_CHUNK:(r + 1) * GM_CHUNK] * sv)
    yb = jnp.concatenate(yb_rows, axis=0).astype(BF16)
    gb = _sigmoid(jnp.dot(n, w_ref[:, c_gb], preferred_element_type=F32))
    mb_ref[0] = gb * jnp.dot(yb, pb_ref[...], preferred_element_type=F32)


def _mix_in(h, shift, scale, g, w, cos_t, sin_t, oh_t, ln_g, ln_b, ws, bsx, proj_b):
    B, S, D = h.shape
    assert w.shape == (D, _mix_cols(D)[-1].stop)
    tm = MIX_ROWS
    G = KV_GROUPS
    CK = ATT_CHUNK

    def const(a):
        nd = a.ndim
        return pl.BlockSpec(a.shape, lambda b, i: (0,) * nd)

    row = lambda w: pl.BlockSpec((1, tm, w), lambda b, i: (b, i, 0))
    vec = pl.BlockSpec((1, 1, D), lambda b, i: (b, 0, 0))
    tab = pl.BlockSpec((tm, LANE), lambda b, i: (i, 0))
    kspec = pl.BlockSpec((1, G, tm, LANE), lambda b, i: (b, 0, i, 0))
    kshape = jax.ShapeDtypeStruct((B, G, S, LANE), BF16)
    vspec = pl.BlockSpec((1, G, tm // CK, V_ROWS, CK), lambda b, i: (b, 0, i, 0, 0))
    vshape = jax.ShapeDtypeStruct((B, G, S // CK, V_ROWS, CK), BF16)
    return pl.pallas_call(
        _mix_in_kernel,
        grid=(B, S // tm),
        in_specs=[row(D), vec, vec, const(g),
                  pl.BlockSpec(w.shape, lambda b, i: (0, 0), pipeline_mode=pl.Buffered(1)),
                  tab, tab, tab, const(ln_g), const(ln_b), const(ws), const(bsx),
                  const(proj_b)],
        out_specs=[
            pl.BlockSpec((1, NSA_HEADS, HEAD_DIM, tm), lambda b, i: (b, 0, 0, i)),
            row(LANE), row(LANE), kspec, vspec, kspec, vspec,
            pl.BlockSpec((1, GATE_ROWS, tm), lambda b, i: (b, 0, i)),
            row(D), row(D)],
        out_shape=[
            jax.ShapeDtypeStruct((B, NSA_HEADS, HEAD_DIM, S), BF16),
            jax.ShapeDtypeStruct((B, S, LANE), F32), jax.ShapeDtypeStruct((B, S, LANE), F32),
            kshape, vshape, kshape, vshape,
            jax.ShapeDtypeStruct((B, GATE_ROWS, S), F32),
            jax.ShapeDtypeStruct((B, S, D), F32), jax.ShapeDtypeStruct((B, S, D), F32)],
        compiler_params=pltpu.CompilerParams(
            dimension_semantics=("parallel", "parallel"), vmem_limit_bytes=VMEM_LIMIT),
        name="mix_in",
    )(h, shift, scale, g, w, cos_t, sin_t, oh_t, ln_g, ln_b, ws, bsx, proj_b)


def _compress_kernel(k_ref, v_ref, pe_ref, w1_ref, w2k_ref, w2v_ref, cos_ref, sin_ref,
                     kc_ref, vct_ref):
    NC = kc_ref.shape[2]
    half = CMP_BLOCK // 2

    def hidden(x_ref, j):
        ha = [None] * KV_GROUPS
        hb = [None] * KV_GROUPS
        for l in range(half):
            x = x_ref[0, pl.ds(l, NC, stride=CMP_STRIDE), :]
            for g in range(KV_GROUPS):
                xg = x[:, g * HEAD_DIM:(g + 1) * HEAD_DIM]
                a = jnp.dot((xg + pe_ref[j, l]).astype(BF16), w1_ref[j, l],
                            preferred_element_type=F32)
                b = jnp.dot((xg + pe_ref[j, half + l]).astype(BF16), w1_ref[j, half + l],
                            preferred_element_type=F32)
                ha[g] = a if ha[g] is None else ha[g] + a
                hb[g] = b if hb[g] is None else hb[g] + b
        return [_silu(ha[g] + pltpu.roll(hb[g], NC - 1, 0)) for g in range(KV_GROUPS)]

    hk = hidden(k_ref, 0)
    hv = hidden(v_ref, 1)
    for g in range(KV_GROUPS):
        k2 = jnp.dot(hk[g].astype(BF16), w2k_ref[...], preferred_element_type=F32)
        kc = k2[:, :HEAD_DIM] * cos_ref[...] + k2[:, HEAD_DIM:] * sin_ref[...]
        kc_ref[0, g] = kc.astype(BF16)
        vc = jnp.dot(hv[g].astype(BF16), w2v_ref[...],
                     preferred_element_type=F32)
        vct_ref[0, g] = jnp.transpose(vc)[:HEAD_DIM].astype(BF16)


def _compress(kcmp, vcmp, pe, w1, w2k, w2v, cos_c, sin_c):
    B, S, W = kcmp.shape
    NC = S // CMP_STRIDE

    def const(a):
        nd = a.ndim
        return pl.BlockSpec(a.shape, lambda b: (0,) * nd)

    blk = pl.BlockSpec((1, S, W), lambda b: (b, 0, 0))
    return pl.pallas_call(
        _compress_kernel,
        grid=(B,),
        in_specs=[blk, blk, const(pe), const(w1), const(w2k), const(w2v),
                  const(cos_c), const(sin_c)],
        out_specs=[pl.BlockSpec((1, KV_GROUPS, NC, HEAD_DIM), lambda b: (b, 0, 0, 0)),
                   pl.BlockSpec((1, KV_GROUPS, HEAD_DIM, NC), lambda b: (b, 0, 0, 0))],
        out_shape=[jax.ShapeDtypeStruct((B, KV_GROUPS, NC, HEAD_DIM), BF16),
                   jax.ShapeDtypeStruct((B, KV_GROUPS, HEAD_DIM, NC), BF16)],
        compiler_params=pltpu.CompilerParams(
            dimension_semantics=("parallel",), vmem_limit_bytes=VMEM_LIMIT),
        name="compress",
    )(kcmp, vcmp, pe, w1, w2k, w2v, cos_c, sin_c)


def _fold_rows(x, op):
    parts = [x[8 * i:8 * (i + 1)] for i in range(x.shape[0] // 8)]
    while len(parts) > 1:
        parts = [op(parts[i], parts[i + 1]) for i in range(0, len(parts), 2)]
    return parts[0]


def _attn_kernel(qt_ref, kselx_ref, vselt_ref, kwinx_ref, vwint_ref, kc_ref, vct_ref, gt_ref,
                 ga_ref, mb_ref, h_ref, gate_ref, ovt_ref, bias_ref, pa_ref, wo_ref,
                 o_ref, qx_scr, s_scr, m_scr, a_scr, acc_scr, yt_scr):
    TQ = h_ref.shape[1]
    CK = ATT_CHUNK
    NC = kc_ref.shape[2]
    NSEL = kselx_ref.shape[2] // SEL_BLOCK
    nwin = WINDOW // CK
    qi = pl.program_id(1)

    def branch(kx_ref, vt_ref, first, far, gate_row):
        def score(c, masked, dst):
            kx = [kx_ref[0, g, pl.ds(pl.multiple_of(c * CK, CK), CK), :]
                  for g in range(KV_GROUPS)]
            if masked:
                bias = bias_ref[jnp.where(c == qi, 1, jnp.where(c == far, 2, 0))]
            for hd in range(NSA_HEADS):
                s = jnp.dot(kx[hd // HPG], qx_scr[hd], preferred_element_type=F32)
                if masked:
                    s = s + bias
                s_scr[dst, hd] = s
                m_prev = m_scr[1 - dst, hd]
                m_new = jnp.maximum(m_prev, jnp.max(_fold_rows(s, jnp.maximum), axis=0,
                                                    keepdims=True))
                a_scr[dst, hd] = jnp.exp2(m_prev - m_new)
                m_scr[dst, hd] = m_new

        def weigh(c, src):
            vt = [vt_ref[0, g, c] for g in range(KV_GROUPS)]
            for hd in range(NSA_HEADS):
                p = jnp.exp2(s_scr[src, hd] - m_scr[src, hd]).astype(BF16)
                acc_scr[hd] = a_scr[src, hd] * acc_scr[hd] + jnp.dot(
                    vt[hd // HPG], p, preferred_element_type=F32)

        def loop(lo, hi, body):
            def wrapped(c, carry):
                body(c)
                return carry
            lax.fori_loop(lo, hi, wrapped, 0)

        def two_chunks(i):
            c = first + 2 * i
            score(c + 1, False, 1)
            weigh(c, 0)
            score(c + 2, True, 0)
            weigh(c + 1, 1)

        def last_two(c):
            score(c + 1, True, 1)
            weigh(c, 0)

        m_scr[...] = jnp.full_like(m_scr, NEG)
        acc_scr[...] = jnp.zeros_like(acc_scr)
        loop(first, first + 1, lambda c: score(c, True, 0))
        rest = qi - first
        pairs = jnp.right_shift(rest, 1)
        odd = rest & 1
        loop(0, pairs, two_chunks)
        loop(qi - 1, qi - 1 + odd, last_two)
        loop(qi, qi + odd, lambda c: weigh(c, 1))
        loop(qi, qi + 1 - odd, lambda c: weigh(c, 0))
        for hd in range(NSA_HEADS):
            w = gt_ref[0, 3 * hd + gate_row:3 * hd + gate_row + 1, :] / acc_scr[
                hd, HEAD_DIM:HEAD_DIM + 1, :]
            rows = slice(hd * HEAD_DIM, (hd + 1) * HEAD_DIM)
            yt_scr[rows, :] = yt_scr[rows, :] + w * acc_scr[hd, :HEAD_DIM, :]

    n_idx = lax.broadcasted_iota(jnp.int32, (NC, TQ), 0)
    t_cmp = qi * TQ + lax.broadcasted_iota(jnp.int32, (NC, TQ), 1)
    vis = (n_idx * CMP_STRIDE + (CMP_BLOCK - 1)) <= t_cmp
    visf = jnp.where(vis, 1.0, 0.0)
    j_idx = lax.broadcasted_iota(jnp.int32, (NSEL, TQ), 0)
    t_sel = qi * TQ + lax.broadcasted_iota(jnp.int32, (NSEL, TQ), 1)
    cur = jnp.right_shift(t_sel, SEL_BLOCK.bit_length() - 1)
    forced = (j_idx == 0) | (j_idx == cur) | (j_idx == cur - 1)
    valid = j_idx <= cur

    heads = range(NSA_HEADS)
    m_c = []
    for hd in heads:
        sc = jnp.where(vis, jnp.dot(kc_ref[0, hd // HPG], qt_ref[0, hd],
                                    preferred_element_type=F32), NEG)
        s_scr[0, hd, :NC, :] = sc
        m_c.append(jnp.max(_fold_rows(sc, jnp.maximum), axis=0, keepdims=True))
    inv = []
    for hd in heads:
        ec = jnp.exp2(s_scr[0, hd, :NC, :] - m_c[hd]) * visf
        s_scr[0, hd, :NC, :] = ec
        den = jnp.sum(_fold_rows(ec, jnp.add), axis=0, keepdims=True)
        inv.append(1.0 / jnp.where(den > 0.0, den, 1.0))
    psum = [jnp.zeros((NC, TQ), F32) for _ in range(KV_GROUPS)]
    for hd in heads:
        g = hd // HPG
        pc = s_scr[0, hd, :NC, :] * inv[hd]
        psum[g] = psum[g] + pc
        oc = jnp.dot(vct_ref[0, g], pc.astype(BF16), preferred_element_type=F32)
        yt_scr[hd * HEAD_DIM:(hd + 1) * HEAD_DIM, :] = gt_ref[0, 3 * hd:3 * hd + 1, :] * oc

    row8 = lax.broadcasted_iota(jnp.int32, (8, TQ), 0)
    pad = jnp.zeros((LANE - HEAD_DIM - NSEL, TQ), BF16)
    for g in range(KV_GROUPS):
        imp = jnp.dot(ovt_ref[...], psum[g], preferred_element_type=F32,
                      precision=lax.Precision.HIGHEST)
        imp = jnp.where(forced, FORCE, jnp.where(valid, imp, -FORCE))
        tiles = [imp[8 * r:8 * (r + 1)] for r in range(NSEL // 8)]
        rank = [jnp.zeros((8, TQ), jnp.int32) for _ in tiles]
        for kk in range(NSEL):
            rk = imp[kk:kk + 1, :]
            for r, tile in enumerate(tiles):
                if 8 * r > kk:
                    ahead = rk >= tile
                elif 8 * r + 7 < kk:
                    ahead = rk > tile
                else:
                    ahead = (rk > tile) | ((rk == tile) & (row8 > kk - 8 * r))
                rank[r] = rank[r] + jnp.where(ahead, 1, 0)
        selneg = jnp.where(jnp.concatenate(rank, axis=0) < SEL_TOPK, 0.0, NEG).astype(BF16)
        for hh in range(HPG):
            hd = g * HPG + hh
            qx_scr[hd] = jnp.concatenate([qt_ref[0, hd], selneg, pad], axis=0)

    branch(kselx_ref, vselt_ref, 0, -1, 1)
    branch(kwinx_ref, vwint_ref, jnp.maximum(qi - nwin, 0), qi - nwin, 2)

    y = jnp.transpose(yt_scr[...]).astype(BF16)
    ya = jnp.dot(y, pa_ref[...], preferred_element_type=F32)
    merged = (ga_ref[0] * ya + mb_ref[0]).astype(BF16)
    o_ref[0] = h_ref[0] + gate_ref[0] * jnp.dot(merged, wo_ref[...],
                                                preferred_element_type=F32)


def _attn(qt, kselx, vselt, kwinx, vwint, kc, vct, gt, ga, mb, h, gate, ovt, bias,
          proj_a, w_out):
    B, S, D = h.shape
    TQ = ATT_ROWS
    CK = ATT_CHUNK
    assert TQ == CK and WINDOW % CK == 0 and S % TQ == 0

    def const(a):
        nd = a.ndim
        return pl.BlockSpec(a.shape, lambda b, i: (0,) * nd)

    def per_batch(a):
        nd = a.ndim
        return pl.BlockSpec((1,) + a.shape[1:], lambda b, i: (b,) + (0,) * (nd - 1))

    row = lambda w: pl.BlockSpec((1, TQ, w), lambda b, i: (b, i, 0))
    return pl.pallas_call(
        _attn_kernel,
        grid=(B, S // TQ),
        in_specs=[
            pl.BlockSpec((1, NSA_HEADS, HEAD_DIM, TQ), lambda b, i: (b, 0, 0, i)),
            per_batch(kselx), per_batch(vselt), per_batch(kwinx), per_batch(vwint),
            per_batch(kc), per_batch(vct),
            pl.BlockSpec((1, GATE_ROWS, TQ), lambda b, i: (b, 0, i)),
            row(D), row(D), row(D),
            pl.BlockSpec((1, 1, D), lambda b, i: (b, 0, 0)),
            const(ovt), const(bias), const(proj_a), const(w_out)],
        out_specs=row(D),
        out_shape=jax.ShapeDtypeStruct((B, S, D), F32),
        scratch_shapes=[
            pltpu.VMEM((NSA_HEADS, LANE, TQ), BF16),
            pltpu.VMEM((2, NSA_HEADS, CK, TQ), F32),
            pltpu.VMEM((2, NSA_HEADS, 1, TQ), F32), pltpu.VMEM((2, NSA_HEADS, 1, TQ), F32),
            pltpu.VMEM((NSA_HEADS, V_ROWS, TQ), F32),
            pltpu.VMEM((NSA_WIDTH, TQ), F32)],
        compiler_params=pltpu.CompilerParams(
            dimension_semantics=("parallel", "parallel"), vmem_limit_bytes=VMEM_LIMIT),
        name="attn",
    )(qt, kselx, vselt, kwinx, vwint, kc, vct, gt, ga, mb, h, gate, ovt, bias,
      proj_a, w_out)


def _rope_tables(pos):
    inv = 1.0 / (ROPE_THETA ** (np.arange(0, HEAD_DIM, 2, dtype=np.float64) / HEAD_DIM))
    ang = np.asarray(pos, np.float64)[:, None] * inv[None, :]
    cos = np.concatenate([np.cos(ang), np.cos(ang)], axis=1)
    sin = np.concatenate([-np.sin(ang), np.sin(ang)], axis=1)
    return cos.astype(np.float32), sin.astype(np.float32)


def _tables(S):
    cos, sin = _rope_tables(np.arange(S))
    cos_t = np.concatenate([cos, cos], axis=1)
    sin_t = np.concatenate([sin, sin], axis=1)
    n_cmp_pad = S // CMP_STRIDE
    starts = np.arange(n_cmp_pad) * CMP_STRIDE
    cos_c, sin_c = _rope_tables(starts + CMP_BLOCK - 1)
    n_sel = S // SEL_BLOCK
    sel_start = np.arange(n_sel) * SEL_BLOCK
    overlap = np.clip(np.minimum(starts[:, None] + CMP_BLOCK, sel_start[None, :] + SEL_BLOCK)
                      - np.maximum(starts[:, None], sel_start[None, :]), 0, None) / CMP_BLOCK
    ovt = np.ascontiguousarray(overlap.T).astype(np.float32)
    oh = np.zeros((S, LANE), np.float32)
    oh[np.arange(S), HEAD_DIM + np.arange(S) // SEL_BLOCK] = 1.0
    j = np.arange(ATT_CHUNK)[:, None]
    i = np.arange(ATT_ROWS)[None, :]
    bias = np.stack([np.zeros((ATT_CHUNK, ATT_ROWS)),
                     np.where(j <= i, 0.0, NEG),
                     np.where(j > i, 0.0, NEG)]
                    ).astype(np.float32)
    return tuple(jnp.asarray(a) for a in (cos_t, sin_t, cos_c, sin_c, ovt, oh, bias))


def _pack_mix_weights(mix_w_in):
    o_g = NSA_WIDTH + 6 * KV_GROUPS * HEAD_DIM + 3 * NSA_HEADS
    q = mix_w_in[..., :NSA_WIDTH] * (HEAD_DIM ** -0.5 * LOG2E)
    pad = jnp.zeros(mix_w_in.shape[:-1] + (LANE - 3 * NSA_HEADS,), mix_w_in.dtype)
    return jnp.concatenate([q, mix_w_in[..., NSA_WIDTH:o_g], pad, mix_w_in[..., o_g:]],
                           axis=-1).astype(BF16)


def kernel(x, c, ada_w, ada_b, norm_g, ffn_w_in, ffn_w_out, mix_w_in, cmp_pe, cmp_w1, cmp_w2,
           gm_ln_g, gm_ln_b, gm_ws, gm_bs, proj_a, proj_b, w_out, final_g):
    B, S, D = x.shape
    L = ada_w.shape[0]
    assert S // SEL_BLOCK + HEAD_DIM <= LANE and 3 * NSA_HEADS <= GATE_ROWS
    cos_t, sin_t, cos_c, sin_c, ovt, oh_t, bias = _tables(S)
    swap = np.concatenate([np.arange(HEAD_DIM // 2, HEAD_DIM), np.arange(HEAD_DIM // 2)])

    w_fi, w_fo = ffn_w_in.astype(BF16), ffn_w_out.astype(BF16)
    w_mix = _pack_mix_weights(mix_w_in)
    w_pa, w_pb, w_o = proj_a.astype(BF16), proj_b.astype(BF16), w_out.astype(BF16)
    w1 = cmp_w1.astype(BF16)
    pe = cmp_pe[:, :, :, None, :]
    w2k = jnp.concatenate([cmp_w2[:, 0], cmp_w2[:, 0][..., swap]], axis=-1).astype(BF16)
    w2v = jnp.pad(cmp_w2[:, 1], ((0, 0), (0, 0), (0, LANE - HEAD_DIM))).astype(BF16)
    bsx = jnp.repeat(jnp.swapaxes(gm_bs, 1, 2), HEAD_DIM, axis=2)

    mods = _ada(c, ada_w, ada_b).reshape(L, B, 3, 3, 1, D)
    h = x
    for l in range(L):
        mod = mods[l]
        h = _ffn(h, mod[:, 0, 0], mod[:, 0, 1], mod[:, 0, 2], norm_g[l, 0][None],
                 w_fi[l, 0], w_fo[l, 0])
        qt, kcmp, vcmp, kselx, vselt, kwinx, vwint, gt, ga, mb = _mix_in(
            h, mod[:, 1, 0], mod[:, 1, 1], norm_g[l, 1][None], w_mix[l], cos_t, sin_t, oh_t,
            gm_ln_g[l][None], gm_ln_b[l][None], gm_ws[l], bsx[l], w_pb[l])
        kc, vct = _compress(kcmp, vcmp, pe[l], w1[l], w2k[l], w2v[l], cos_c, sin_c)
        h = _attn(qt, kselx, vselt, kwinx, vwint, kc, vct, gt, ga, mb, h, mod[:, 1, 2], ovt,
                  bias, w_pa[l], w_o[l])
        h = _ffn(h, mod[:, 2, 0], mod[:, 2, 1], mod[:, 2, 2], norm_g[l, 2][None],
                 w_fi[l, 1], w_fo[l, 1], final_g=final_g[None] if l == L - 1 else None)
    return h
```

```python
import functools

import numpy as np
import jax
import jax.numpy as jnp
from jax import lax
from jax.experimental import pallas as pl
from jax.experimental.pallas import tpu as pltpu

F32 = jnp.float32
BF16 = jnp.bfloat16

HEAD_DIM = 64
NSA_HEADS = 8
KV_GROUPS = 2
HPG = NSA_HEADS // KV_GROUPS
NSA_WIDTH = NSA_HEADS * HEAD_DIM
CMP_BLOCK = 32
CMP_STRIDE = 16
CMP_HIDDEN = 128
SEL_BLOCK = 64
SEL_TOPK = 16
WINDOW = 512
GM_GROUPS = 8
GM_CHUNK = 128
GM_WIDTH = GM_GROUPS * HEAD_DIM
ROPE_THETA = 10000.0
EPS = 1e-6
NEG = -1e30
FORCE = 1e4
LANE = 128
GELU_C = float(np.sqrt(2.0 / np.pi))
LOG2E = float(np.log2(np.e))
GATE_ROWS = 32
V_ROWS = HEAD_DIM + 16

FFN_ROWS = 512
FFN_SPLIT = 2
MIX_ROWS = 512
ATT_ROWS = 256
ATT_CHUNK = 256
VMEM_LIMIT = 56 * 1024 * 1024


def _sigmoid(x):
    return 1.0 / (1.0 + jnp.exp(-x))


def _silu(x):
    return x * _sigmoid(x)


def _rms_mod(x, g, shift, scale):
    y = x * lax.rsqrt(jnp.mean(x * x, axis=-1, keepdims=True) + EPS) * g
    return y * (1.0 + scale) + shift


def _ada_kernel(c_ref, w_ref, b_ref, o_ref):
    s = _silu(c_ref[...])
    o_ref[0] = jnp.dot(s, w_ref[0], preferred_element_type=F32,
                       precision=lax.Precision.HIGHEST) + b_ref[0]


def _ada(c, ada_w, ada_b):
    L, D, N = ada_w.shape
    B = c.shape[0]
    tn = 1024
    return pl.pallas_call(
        _ada_kernel,
        grid=(L, N // tn),
        in_specs=[
            pl.BlockSpec((B, D), lambda l, j: (0, 0)),
            pl.BlockSpec((1, D, tn), lambda l, j: (l, 0, j)),
            pl.BlockSpec((1, 1, tn), lambda l, j: (l, 0, j)),
        ],
        out_specs=pl.BlockSpec((1, B, tn), lambda l, j: (l, 0, j)),
        out_shape=jax.ShapeDtypeStruct((L, B, N), F32),
        compiler_params=pltpu.CompilerParams(
            dimension_semantics=("parallel", "parallel"), vmem_limit_bytes=VMEM_LIMIT),
        name="ada",
    )(c, ada_w, ada_b.reshape(L, 1, N))


def _ffn_kernel(h_ref, shift_ref, scale_ref, gate_ref, g_ref, wi_ref, wo_ref, *rest,
                final_norm):
    fg_ref = rest[0] if final_norm else None
    o_ref = rest[-1]
    F = wo_ref.shape[0]
    tf = F // FFN_SPLIT
    x = h_ref[0]
    n = _rms_mod(x, g_ref[...], shift_ref[0], scale_ref[0]).astype(BF16)
    acc = None
    for j in range(FFN_SPLIT):
        a = jnp.dot(n, wi_ref[:, j * tf:(j + 1) * tf], preferred_element_type=F32)
        b = jnp.dot(n, wi_ref[:, F + j * tf:F + (j + 1) * tf], preferred_element_type=F32)
        hm = (_silu(a) * b).astype(BF16)
        part = jnp.dot(hm, wo_ref[j * tf:(j + 1) * tf, :], preferred_element_type=F32)
        acc = part if acc is None else acc + part
    y = x + (0.5 * gate_ref[0]) * acc
    if final_norm:
        y = y * lax.rsqrt(jnp.mean(y * y, axis=-1, keepdims=True) + EPS) * fg_ref[...]
    o_ref[0] = y


def _ffn(h, shift, scale, gate, g, w_in, w_out, final_g=None):
    B, S, D = h.shape
    tm = FFN_ROWS
    row = pl.BlockSpec((1, tm, D), lambda b, i: (b, i, 0))
    vec = pl.BlockSpec((1, 1, D), lambda b, i: (b, 0, 0))
    gain = pl.BlockSpec((1, D), lambda b, i: (0, 0))

    def resident(a):
        return pl.BlockSpec(a.shape, lambda b, i: (0, 0), pipeline_mode=pl.Buffered(1))

    extra = () if final_g is None else (final_g,)
    return pl.pallas_call(
        functools.partial(_ffn_kernel, final_norm=final_g is not None),
        grid=(B, S // tm),
        in_specs=[row, vec, vec, vec, gain, resident(w_in), resident(w_out)]
        + [gain] * len(extra),
        out_specs=row,
        out_shape=jax.ShapeDtypeStruct((B, S, D), F32),
        compiler_params=pltpu.CompilerParams(
            dimension_semantics=("parallel", "parallel"), vmem_limit_bytes=VMEM_LIMIT),
        name="ffn",
    )(h, shift, scale, gate, g, w_in, w_out, *extra)


def _rope_pair(x, c, s):
    lane = lax.broadcasted_iota(jnp.int32, x.shape, 1)
    first_half = (lane & (HEAD_DIM - 1)) < HEAD_DIM // 2
    swapped = jnp.where(first_half, pltpu.roll(x, LANE - HEAD_DIM // 2, 1),
                        pltpu.roll(x, HEAD_DIM // 2, 1))
    return x * c + swapped * s


def _mix_cols(D):
    kv0 = NSA_WIDTH
    g0 = kv0 + 6 * KV_GROUPS * HEAD_DIM
    tail0 = g0 + 3 * NSA_HEADS
    edges = np.cumsum((0, 2 * GM_WIDTH, D, D))
    tail = [slice(int(a), int(b)) for a, b in zip(edges[:-1], edges[1:])]
    return slice(0, kv0), slice(kv0, g0), slice(g0, g0 + LANE), tail0, tail


def _mix_in_kernel(h_ref, shift_ref, scale_ref, g_ref, w_ref, cos_ref, sin_ref, oh_ref,
                   lng_ref, lnb_ref, ws_ref, bsx_ref, pb_ref,
                   qt_ref, kcmp_ref, vcmp_ref, kselx_ref, vselt_ref, kwinx_ref, vwint_ref,
                   gt_ref, ga_ref, mb_ref, wt_scr):
    tm = h_ref.shape[1]
    CK = vselt_ref.shape[4]
    c_q, c_kv, c_g, tail0, (c_uv, c_ga, c_gb) = _mix_cols(h_ref.shape[2])

    @pl.when((pl.program_id(0) == 0) & (pl.program_id(1) == 0))
    def _():
        wt_scr[...] = w_ref[:, tail0:tail0 + wt_scr.shape[1]]

    n = _rms_mod(h_ref[0], g_ref[...], shift_ref[0], scale_ref[0]).astype(BF16)
    cos = cos_ref[...]
    sin = sin_ref[...]
    low_lanes = lax.broadcasted_iota(jnp.int32, (tm, LANE), 1) < HEAD_DIM

    q = jnp.dot(n, w_ref[:, c_q], preferred_element_type=F32)
    for p in range(NSA_HEADS // 2):
        qp = jnp.transpose(_rope_pair(q[:, p * LANE:(p + 1) * LANE], cos, sin)).astype(BF16)
        qt_ref[0, 2 * p] = qp[:HEAD_DIM]
        qt_ref[0, 2 * p + 1] = qp[HEAD_DIM:]

    kv = jnp.dot(n, w_ref[:, c_kv], preferred_element_type=F32)
    kcmp_ref[0] = kv[:, 0 * LANE:1 * LANE]
    vcmp_ref[0] = kv[:, 1 * LANE:2 * LANE]
    for idx, ref, ext in ((2, kselx_ref, oh_ref[...]), (4, kwinx_ref, 0.0)):
        k2 = _rope_pair(kv[:, idx * LANE:(idx + 1) * LANE], cos, sin)
        ref[0, 0] = jnp.where(low_lanes, k2, ext).astype(BF16)
        ref[0, 1] = jnp.where(low_lanes, pltpu.roll(k2, HEAD_DIM, 1), ext).astype(BF16)
    ones_row = jnp.where(lax.broadcasted_iota(jnp.int32, (V_ROWS - HEAD_DIM, CK), 0) == 0,
                         1.0, 0.0).astype(BF16)
    for idx, ref in ((3, vselt_ref), (5, vwint_ref)):
        vt = jnp.transpose(kv[:, idx * LANE:(idx + 1) * LANE]).astype(BF16)
        for g in range(KV_GROUPS):
            for r in range(tm // CK):
                ref[0, g, r] = jnp.concatenate(
                    [vt[g * HEAD_DIM:(g + 1) * HEAD_DIM, r * CK:(r + 1) * CK], ones_row], axis=0)

    gates = _sigmoid(jnp.dot(n, w_ref[:, c_g], preferred_element_type=F32))
    gt_ref[0] = jnp.transpose(gates)[:GATE_ROWS]
    ga_ref[0] = _sigmoid(jnp.dot(n, wt_scr[:, c_ga], preferred_element_type=F32))

    uv = jnp.dot(n, wt_scr[:, c_uv], preferred_element_type=F32)
    ge = uv * (0.5 * (1.0 + jnp.tanh(GELU_C * (uv + 0.044715 * (uv * uv * uv)))))
    u = ge[:, :GM_WIDTH]
    v = ge[:, GM_WIDTH:]
    mu = jnp.mean(v, axis=-1, keepdims=True)
    var = jnp.mean(jnp.square(v - mu), axis=-1, keepdims=True)
    vln = ((v - mu) * lax.rsqrt(var + EPS) * lng_ref[...] + lnb_ref[...]).astype(BF16)

    ti = lax.broadcasted_iota(jnp.int32, (GM_CHUNK, GM_CHUNK), 0)
    si = lax.broadcasted_iota(jnp.int32, (GM_CHUNK, GM_CHUNK), 1)
    tril = si <= ti
    wm = [jnp.where(tril, ws_ref[gg], 0.0).astype(BF16) for gg in range(GM_GROUPS)]
    low = lax.broadcasted_iota(jnp.int32, (GM_CHUNK, LANE), 1) < HEAD_DIM
    bsx = bsx_ref[...]
    yb_rows = []
    for r in range(tm // GM_CHUNK):
        vch = vln[r * GM_CHUNK:(r + 1) * GM_CHUNK]
        pieces = []
        for p in range(GM_GROUPS // 2):
            vp = vch[:, p * LANE:(p + 1) * LANE]
            a0 = jnp.dot(wm[2 * p], vp, preferred_element_type=F32)
            a1 = jnp.dot(wm[2 * p + 1], vp, preferred_element_type=F32)
            pieces.append(jnp.where(low, a0, a1))
        sv = jnp.concatenate(pieces, axis=1) + bsx
        yb_rows.append(u[r * GM_CHUNK:(r + 1) * GM_CHUNK] * sv)
    yb = jnp.concatenate(yb_rows, axis=0).astype(BF16)
    gb = _sigmoid(jnp.dot(n, wt_scr[:, c_gb], preferred_element_type=F32))
    mb_ref[0] = gb * jnp.dot(yb, pb_ref[...], preferred_element_type=F32)


def _mix_in(h, shift, scale, g, w, cos_t, sin_t, oh_t, ln_g, ln_b, ws, bsx, proj_b):
    B, S, D = h.shape
    tail0, tail = _mix_cols(D)[3:]
    assert w.shape == (D, tail0 + tail[-1].stop)
    tm = MIX_ROWS
    G = KV_GROUPS
    CK = ATT_CHUNK

    def const(a):
        nd = a.ndim
        return pl.BlockSpec(a.shape, lambda b, i: (0,) * nd)

    row = lambda w: pl.BlockSpec((1, tm, w), lambda b, i: (b, i, 0))
    vec = pl.BlockSpec((1, 1, D), lambda b, i: (b, 0, 0))
    tab = pl.BlockSpec((tm, LANE), lambda b, i: (i, 0))
    kspec = pl.BlockSpec((1, G, tm, LANE), lambda b, i: (b, 0, i, 0))
    kshape = jax.ShapeDtypeStruct((B, G, S, LANE), BF16)
    vspec = pl.BlockSpec((1, G, tm // CK, V_ROWS, CK), lambda b, i: (b, 0, i, 0, 0))
    vshape = jax.ShapeDtypeStruct((B, G, S // CK, V_ROWS, CK), BF16)
    return pl.pallas_call(
        _mix_in_kernel,
        grid=(B, S // tm),
        in_specs=[row(D), vec, vec, const(g),
                  pl.BlockSpec(w.shape, lambda b, i: (0, 0), pipeline_mode=pl.Buffered(1)),
                  tab, tab, tab, const(ln_g), const(ln_b), const(ws), const(bsx),
                  const(proj_b)],
        out_specs=[
            pl.BlockSpec((1, NSA_HEADS, HEAD_DIM, tm), lambda b, i: (b, 0, 0, i)),
            row(LANE), row(LANE), kspec, vspec, kspec, vspec,
            pl.BlockSpec((1, GATE_ROWS, tm), lambda b, i: (b, 0, i)),
            row(D), row(D)],
        out_shape=[
            jax.ShapeDtypeStruct((B, NSA_HEADS, HEAD_DIM, S), BF16),
            jax.ShapeDtypeStruct((B, S, LANE), F32), jax.ShapeDtypeStruct((B, S, LANE), F32),
            kshape, vshape, kshape, vshape,
            jax.ShapeDtypeStruct((B, GATE_ROWS, S), F32),
            jax.ShapeDtypeStruct((B, S, D), F32), jax.ShapeDtypeStruct((B, S, D), F32)],
        scratch_shapes=[pltpu.VMEM((D, tail[-1].stop), BF16)],
        compiler_params=pltpu.CompilerParams(
            dimension_semantics=("arbitrary", "arbitrary"), vmem_limit_bytes=VMEM_LIMIT),
        name="mix_in",
    )(h, shift, scale, g, w, cos_t, sin_t, oh_t, ln_g, ln_b, ws, bsx, proj_b)


def _compress_kernel(k_ref, v_ref, pe_ref, w1_ref, w2k_ref, w2v_ref, cos_ref, sin_ref,
                     kc_ref, vct_ref):
    NC = kc_ref.shape[2]
    half = CMP_BLOCK // 2

    def hidden(x_ref, j):
        ha = [None] * KV_GROUPS
        hb = [None] * KV_GROUPS
        for l in range(half):
            x = x_ref[0, pl.ds(l, NC, stride=CMP_STRIDE), :]
            for g in range(KV_GROUPS):
                xg = x[:, g * HEAD_DIM:(g + 1) * HEAD_DIM]
                a = jnp.dot((xg + pe_ref[j, l]).astype(BF16), w1_ref[j, l],
                            preferred_element_type=F32)
                b = jnp.dot((xg + pe_ref[j, half + l]).astype(BF16), w1_ref[j, half + l],
                            preferred_element_type=F32)
                ha[g] = a if ha[g] is None else ha[g] + a
                hb[g] = b if hb[g] is None else hb[g] + b
        return [_silu(ha[g] + pltpu.roll(hb[g], NC - 1, 0)) for g in range(KV_GROUPS)]

    hk = hidden(k_ref, 0)
    hv = hidden(v_ref, 1)
    for g in range(KV_GROUPS):
        k2 = jnp.dot(hk[g].astype(BF16), w2k_ref[...], preferred_element_type=F32)
        kc = k2[:, :HEAD_DIM] * cos_ref[...] + k2[:, HEAD_DIM:] * sin_ref[...]
        kc_ref[0, g] = kc.astype(BF16)
        vc = jnp.dot(hv[g].astype(BF16), w2v_ref[...],
                     preferred_element_type=F32)
        vct_ref[0, g] = jnp.transpose(vc)[:HEAD_DIM].astype(BF16)


def _compress(kcmp, vcmp, pe, w1, w2k, w2v, cos_c, sin_c):
    B, S, W = kcmp.shape
    NC = S // CMP_STRIDE

    def const(a):
        nd = a.ndim
        return pl.BlockSpec(a.shape, lambda b: (0,) * nd)

    blk = pl.BlockSpec((1, S, W), lambda b: (b, 0, 0))
    return pl.pallas_call(
        _compress_kernel,
        grid=(B,),
        in_specs=[blk, blk, const(pe), const(w1), const(w2k), const(w2v),
                  const(cos_c), const(sin_c)],
        out_specs=[pl.BlockSpec((1, KV_GROUPS, NC, HEAD_DIM), lambda b: (b, 0, 0, 0)),
                   pl.BlockSpec((1, KV_GROUPS, HEAD_DIM, NC), lambda b: (b, 0, 0, 0))],
        out_shape=[jax.ShapeDtypeStruct((B, KV_GROUPS, NC, HEAD_DIM), BF16),
                   jax.ShapeDtypeStruct((B, KV_GROUPS, HEAD_DIM, NC), BF16)],
        compiler_params=pltpu.CompilerParams(
            dimension_semantics=("parallel",), vmem_limit_bytes=VMEM_LIMIT),
        name="compress",
    )(kcmp, vcmp, pe, w1, w2k, w2v, cos_c, sin_c)


def _fold_rows(x, op):
    parts = [x[8 * i:8 * (i + 1)] for i in range(x.shape[0] // 8)]
    while len(parts) > 1:
        parts = [op(parts[i], parts[i + 1]) for i in range(0, len(parts), 2)]
    return parts[0]


def _attn_kernel(qt_ref, kselx_ref, vselt_ref, kwinx_ref, vwint_ref, kc_ref, vct_ref, gt_ref,
                 ga_ref, mb_ref, h_ref, gate_ref, ovt_ref, bias_ref, pa_ref, wo_ref,
                 o_ref, qx_scr, s_scr, m_scr, a_scr, acc_scr, yt_scr):
    TQ = h_ref.shape[1]
    CK = ATT_CHUNK
    NC = kc_ref.shape[2]
    NSEL = kselx_ref.shape[2] // SEL_BLOCK
    nwin = WINDOW // CK
    qi = pl.program_id(1)

    def branch(kx_ref, vt_ref, first, far, gate_row):
        def score(c, masked, dst):
            kx = [kx_ref[0, g, pl.ds(pl.multiple_of(c * CK, CK), CK), :]
                  for g in range(KV_GROUPS)]
            if masked:
                bias = bias_ref[jnp.where(c == qi, 1, jnp.where(c == far, 2, 0))]
            for hd in range(NSA_HEADS):
                s = jnp.dot(kx[hd // HPG], qx_scr[hd], preferred_element_type=F32)
                if masked:
                    s = s + bias
                s_scr[dst, hd] = s
                m_prev = m_scr[1 - dst, hd]
                m_new = jnp.maximum(m_prev, jnp.max(_fold_rows(s, jnp.maximum), axis=0,
                                                    keepdims=True))
                a_scr[dst, hd] = jnp.exp2(m_prev - m_new)
                m_scr[dst, hd] = m_new

        def weigh(c, src):
            vt = [vt_ref[0, g, c] for g in range(KV_GROUPS)]
            for hd in range(NSA_HEADS):
                p = jnp.exp2(s_scr[src, hd] - m_scr[src, hd]).astype(BF16)
                acc_scr[hd] = a_scr[src, hd] * acc_scr[hd] + jnp.dot(
                    vt[hd // HPG], p, preferred_element_type=F32)

        def loop(lo, hi, body):
            def wrapped(c, carry):
                body(c)
                return carry
            lax.fori_loop(lo, hi, wrapped, 0)

        def two_chunks(i):
            c = first + 2 * i
            score(c + 1, False, 1)
            weigh(c, 0)
            score(c + 2, True, 0)
            weigh(c + 1, 1)

        def last_two(c):
            score(c + 1, True, 1)
            weigh(c, 0)

        m_scr[...] = jnp.full_like(m_scr, NEG)
        acc_scr[...] = jnp.zeros_like(acc_scr)
        loop(first, first + 1, lambda c: score(c, True, 0))
        rest = qi - first
        pairs = jnp.right_shift(rest, 1)
        odd = rest & 1
        loop(0, pairs, two_chunks)
        loop(qi - 1, qi - 1 + odd, last_two)
        loop(qi, qi + odd, lambda c: weigh(c, 1))
        loop(qi, qi + 1 - odd, lambda c: weigh(c, 0))
        for hd in range(NSA_HEADS):
            w = gt_ref[0, 3 * hd + gate_row:3 * hd + gate_row + 1, :] / acc_scr[
                hd, HEAD_DIM:HEAD_DIM + 1, :]
            rows = slice(hd * HEAD_DIM, (hd + 1) * HEAD_DIM)
            yt_scr[rows, :] = yt_scr[rows, :] + w * acc_scr[hd, :HEAD_DIM, :]

    n_idx = lax.broadcasted_iota(jnp.int32, (NC, TQ), 0)
    t_cmp = qi * TQ + lax.broadcasted_iota(jnp.int32, (NC, TQ), 1)
    vis = (n_idx * CMP_STRIDE + (CMP_BLOCK - 1)) <= t_cmp
    visf = jnp.where(vis, 1.0, 0.0)
    j_idx = lax.broadcasted_iota(jnp.int32, (NSEL, TQ), 0)
    t_sel = qi * TQ + lax.broadcasted_iota(jnp.int32, (NSEL, TQ), 1)
    cur = jnp.right_shift(t_sel, SEL_BLOCK.bit_length() - 1)
    forced = (j_idx == 0) | (j_idx == cur) | (j_idx == cur - 1)
    valid = j_idx <= cur

    heads = range(NSA_HEADS)
    m_c = []
    for hd in heads:
        sc = jnp.where(vis, jnp.dot(kc_ref[0, hd // HPG], qt_ref[0, hd],
                                    preferred_element_type=F32), NEG)
        s_scr[0, hd, :NC, :] = sc
        m_c.append(jnp.max(_fold_rows(sc, jnp.maximum), axis=0, keepdims=True))
    inv = []
    for hd in heads:
        ec = jnp.exp2(s_scr[0, hd, :NC, :] - m_c[hd]) * visf
        s_scr[0, hd, :NC, :] = ec
        den = jnp.sum(_fold_rows(ec, jnp.add), axis=0, keepdims=True)
        inv.append(1.0 / jnp.where(den > 0.0, den, 1.0))
    psum = [jnp.zeros((NC, TQ), F32) for _ in range(KV_GROUPS)]
    for hd in heads:
        g = hd // HPG
        pc = s_scr[0, hd, :NC, :] * inv[hd]
        psum[g] = psum[g] + pc
        oc = jnp.dot(vct_ref[0, g], pc.astype(BF16), preferred_element_type=F32)
        yt_scr[hd * HEAD_DIM:(hd + 1) * HEAD_DIM, :] = gt_ref[0, 3 * hd:3 * hd + 1, :] * oc

    row8 = lax.broadcasted_iota(jnp.int32, (8, TQ), 0)
    pad = jnp.zeros((LANE - HEAD_DIM - NSEL, TQ), BF16)
    for g in range(KV_GROUPS):
        imp = jnp.dot(ovt_ref[...], psum[g], preferred_element_type=F32,
                      precision=lax.Precision.HIGHEST)
        imp = jnp.where(forced, FORCE, jnp.where(valid, imp, -FORCE))
        tiles = [imp[8 * r:8 * (r + 1)] for r in range(NSEL // 8)]
        rank = [jnp.zeros((8, TQ), jnp.int32) for _ in tiles]
        for kk in range(NSEL):
            rk = imp[kk:kk + 1, :]
            for r, tile in enumerate(tiles):
                if 8 * r > kk:
                    ahead = rk >= tile
                elif 8 * r + 7 < kk:
                    ahead = rk > tile
                else:
                    ahead = (rk > tile) | ((rk == tile) & (row8 > kk - 8 * r))
                rank[r] = rank[r] + jnp.where(ahead, 1, 0)
        selneg = jnp.where(jnp.concatenate(rank, axis=0) < SEL_TOPK, 0.0, NEG).astype(BF16)
        for hh in range(HPG):
            hd = g * HPG + hh
            qx_scr[hd] = jnp.concatenate([qt_ref[0, hd], selneg, pad], axis=0)

    branch(kselx_ref, vselt_ref, 0, -1, 1)
    branch(kwinx_ref, vwint_ref, jnp.maximum(qi - nwin, 0), qi - nwin, 2)

    y = jnp.transpose(yt_scr[...]).astype(BF16)
    ya = jnp.dot(y, pa_ref[...], preferred_element_type=F32)
    merged = (ga_ref[0] * ya + mb_ref[0]).astype(BF16)
    o_ref[0] = h_ref[0] + gate_ref[0] * jnp.dot(merged, wo_ref[...],
                                                preferred_element_type=F32)


def _attn(qt, kselx, vselt, kwinx, vwint, kc, vct, gt, ga, mb, h, gate, ovt, bias,
          proj_a, w_out):
    B, S, D = h.shape
    TQ = ATT_ROWS
    CK = ATT_CHUNK
    assert TQ == CK and WINDOW % CK == 0 and S % TQ == 0

    def const(a):
        nd = a.ndim
        return pl.BlockSpec(a.shape, lambda b, i: (0,) * nd)

    def per_batch(a):
        nd = a.ndim
        return pl.BlockSpec((1,) + a.shape[1:], lambda b, i: (b,) + (0,) * (nd - 1))

    row = lambda w: pl.BlockSpec((1, TQ, w), lambda b, i: (b, i, 0))
    return pl.pallas_call(
        _attn_kernel,
        grid=(B, S // TQ),
        in_specs=[
            pl.BlockSpec((1, NSA_HEADS, HEAD_DIM, TQ), lambda b, i: (b, 0, 0, i)),
            per_batch(kselx), per_batch(vselt), per_batch(kwinx), per_batch(vwint),
            per_batch(kc), per_batch(vct),
            pl.BlockSpec((1, GATE_ROWS, TQ), lambda b, i: (b, 0, i)),
            row(D), row(D), row(D),
            pl.BlockSpec((1, 1, D), lambda b, i: (b, 0, 0)),
            const(ovt), const(bias), const(proj_a), const(w_out)],
        out_specs=row(D),
        out_shape=jax.ShapeDtypeStruct((B, S, D), F32),
        scratch_shapes=[
            pltpu.VMEM((NSA_HEADS, LANE, TQ), BF16),
            pltpu.VMEM((2, NSA_HEADS, CK, TQ), F32),
            pltpu.VMEM((2, NSA_HEADS, 1, TQ), F32), pltpu.VMEM((2, NSA_HEADS, 1, TQ), F32),
            pltpu.VMEM((NSA_HEADS, V_ROWS, TQ), F32),
            pltpu.VMEM((NSA_WIDTH, TQ), F32)],
        compiler_params=pltpu.CompilerParams(
            dimension_semantics=("parallel", "parallel"), vmem_limit_bytes=VMEM_LIMIT),
        name="attn",
    )(qt, kselx, vselt, kwinx, vwint, kc, vct, gt, ga, mb, h, gate, ovt, bias,
      proj_a, w_out)


def _rope_tables(pos):
    inv = 1.0 / (ROPE_THETA ** (np.arange(0, HEAD_DIM, 2, dtype=np.float64) / HEAD_DIM))
    ang = np.asarray(pos, np.float64)[:, None] * inv[None, :]
    cos = np.concatenate([np.cos(ang), np.cos(ang)], axis=1)
    sin = np.concatenate([-np.sin(ang), np.sin(ang)], axis=1)
    return cos.astype(np.float32), sin.astype(np.float32)


def _tables(S):
    cos, sin = _rope_tables(np.arange(S))
    cos_t = np.concatenate([cos, cos], axis=1)
    sin_t = np.concatenate([sin, sin], axis=1)
    n_cmp_pad = S // CMP_STRIDE
    starts = np.arange(n_cmp_pad) * CMP_STRIDE
    cos_c, sin_c = _rope_tables(starts + CMP_BLOCK - 1)
    n_sel = S // SEL_BLOCK
    sel_start = np.arange(n_sel) * SEL_BLOCK
    overlap = np.clip(np.minimum(starts[:, None] + CMP_BLOCK, sel_start[None, :] + SEL_BLOCK)
                      - np.maximum(starts[:, None], sel_start[None, :]), 0, None) / CMP_BLOCK
    ovt = np.ascontiguousarray(overlap.T).astype(np.float32)
    oh = np.zeros((S, LANE), np.float32)
    oh[np.arange(S), HEAD_DIM + np.arange(S) // SEL_BLOCK] = 1.0
    j = np.arange(ATT_CHUNK)[:, None]
    i = np.arange(ATT_ROWS)[None, :]
    bias = np.stack([np.zeros((ATT_CHUNK, ATT_ROWS)),
                     np.where(j <= i, 0.0, NEG),
                     np.where(j > i, 0.0, NEG)]
                    ).astype(np.float32)
    return tuple(jnp.asarray(a) for a in (cos_t, sin_t, cos_c, sin_c, ovt, oh, bias))


def _cast_mix_weights(mix_w_in):
    col = np.ones((mix_w_in.shape[-1],), np.float32)
    col[:NSA_WIDTH] = HEAD_DIM ** -0.5 * LOG2E
    return (mix_w_in * col).astype(BF16)


def kernel(x, c, ada_w, ada_b, norm_g, ffn_w_in, ffn_w_out, mix_w_in, cmp_pe, cmp_w1, cmp_w2,
           gm_ln_g, gm_ln_b, gm_ws, gm_bs, proj_a, proj_b, w_out, final_g):
    B, S, D = x.shape
    L = ada_w.shape[0]
    assert S // SEL_BLOCK + HEAD_DIM <= LANE and 3 * NSA_HEADS <= GATE_ROWS
    cos_t, sin_t, cos_c, sin_c, ovt, oh_t, bias = _tables(S)
    swap = np.concatenate([np.arange(HEAD_DIM // 2, HEAD_DIM), np.arange(HEAD_DIM // 2)])

    w_fi, w_fo = ffn_w_in.astype(BF16), ffn_w_out.astype(BF16)
    w_mix = _cast_mix_weights(mix_w_in)
    w_pa, w_pb, w_o = proj_a.astype(BF16), proj_b.astype(BF16), w_out.astype(BF16)
    w1 = cmp_w1.astype(BF16)
    pe = cmp_pe[:, :, :, None, :]
    w2k = jnp.concatenate([cmp_w2[:, 0], cmp_w2[:, 0][..., swap]], axis=-1).astype(BF16)
    w2v = jnp.pad(cmp_w2[:, 1], ((0, 0), (0, 0), (0, LANE - HEAD_DIM))).astype(BF16)
    bsx = jnp.repeat(jnp.swapaxes(gm_bs, 1, 2), HEAD_DIM, axis=2)

    mods = _ada(c, ada_w, ada_b).reshape(L, B, 3, 3, 1, D)
    h = x
    for l in range(L):
        mod = mods[l]
        h = _ffn(h, mod[:, 0, 0], mod[:, 0, 1], mod[:, 0, 2], norm_g[l, 0][None],
                 w_fi[l, 0], w_fo[l, 0])
        qt, kcmp, vcmp, kselx, vselt, kwinx, vwint, gt, ga, mb = _mix_in(
            h, mod[:, 1, 0], mod[:, 1, 1], norm_g[l, 1][None], w_mix[l], cos_t, sin_t, oh_t,
            gm_ln_g[l][None], gm_ln_b[l][None], gm_ws[l], bsx[l], w_pb[l])
        kc, vct = _compress(kcmp, vcmp, pe[l], w1[l], w2k[l], w2v[l], cos_c, sin_c)
        h = _attn(qt, kselx, vselt, kwinx, vwint, kc, vct, gt, ga, mb, h, mod[:, 1, 2], ovt,
                  bias, w_pa[l], w_o[l])
        h = _ffn(h, mod[:, 2, 0], mod[:, 2, 1], mod[:, 2, 2], norm_g[l, 2][None],
                 w_fi[l, 1], w_fo[l, 1], final_g=final_g[None] if l == L - 1 else None)
    return h
```

```python
import functools
from typing import NamedTuple

import numpy as np
import jax
import jax.numpy as jnp
from jax import lax
from jax.experimental import pallas as pl
from jax.experimental.pallas import tpu as pltpu

F32 = jnp.float32
BF16 = jnp.bfloat16

HEAD_DIM = 64
NSA_HEADS = 8
KV_GROUPS = 2
HPG = NSA_HEADS // KV_GROUPS
NSA_WIDTH = NSA_HEADS * HEAD_DIM
CMP_BLOCK = 32
CMP_STRIDE = 16
CMP_HIDDEN = 128
SEL_BLOCK = 64
SEL_TOPK = 16
WINDOW = 512
GM_GROUPS = 8
GM_CHUNK = 128
GM_WIDTH = GM_GROUPS * HEAD_DIM
ROPE_THETA = 10000.0
EPS = 1e-6
NEG = -1e30
FORCE = 1e4
LANE = 128
GELU_C = float(np.sqrt(2.0 / np.pi))
LOG2E = float(np.log2(np.e))
GATE_ROWS = 32
V_ROWS = HEAD_DIM + 16

FFN_ROWS = 512
FFN_SPLIT = 2
MIX_ROWS = 512
ATT_ROWS = 256
ATT_CHUNK = 256
VMEM_LIMIT = 56 * 1024 * 1024


def _sigmoid(x):
    return 1.0 / (1.0 + jnp.exp(-x))


def _silu(x):
    return x * _sigmoid(x)


def _rms_mod(x, g, shift, scale):
    y = x * lax.rsqrt(jnp.mean(x * x, axis=-1, keepdims=True) + EPS) * g
    return y * (1.0 + scale) + shift


class _Pick(NamedTuple):
    array: jax.Array
    idx: tuple


def _arr(x):
    return x.array if isinstance(x, _Pick) else x


def _whole(x, single_buffer=False):
    idx = x.idx if isinstance(x, _Pick) else ()
    shape = _arr(x).shape
    mode = dict(pipeline_mode=pl.Buffered(1)) if single_buffer else {}
    return pl.BlockSpec((None,) * len(idx) + shape[len(idx):],
                        lambda *_: idx + (0,) * (len(shape) - len(idx)), **mode)


def _mod_vec(mods, l, sub, k):
    return pl.BlockSpec((None, 1, None, None, 1, mods.shape[-1]),
                        lambda b, *_: (l, b, sub, k, 0, 0))


def _ada_kernel(c_ref, w_ref, b_ref, o_ref):
    s = _silu(c_ref[...])
    o_ref[0] = jnp.dot(s, w_ref[0], preferred_element_type=F32,
                       precision=lax.Precision.HIGHEST) + b_ref[0]


def _ada(c, ada_w, ada_b):
    L, D, N = ada_w.shape
    B = c.shape[0]
    tn = 1024
    return pl.pallas_call(
        _ada_kernel,
        grid=(L, N // tn),
        in_specs=[
            pl.BlockSpec((B, D), lambda l, j: (0, 0)),
            pl.BlockSpec((1, D, tn), lambda l, j: (l, 0, j)),
            pl.BlockSpec((1, 1, tn), lambda l, j: (l, 0, j)),
        ],
        out_specs=pl.BlockSpec((1, B, tn), lambda l, j: (l, 0, j)),
        out_shape=jax.ShapeDtypeStruct((L, B, N), F32),
        compiler_params=pltpu.CompilerParams(
            dimension_semantics=("parallel", "parallel"), vmem_limit_bytes=VMEM_LIMIT),
        name="ada",
    )(c, ada_w, ada_b.reshape(L, 1, N))


def _ffn_kernel(h_ref, shift_ref, scale_ref, gate_ref, g_ref, wi_ref, wo_ref, *rest,
                final_norm):
    fg_ref = rest[0] if final_norm else None
    o_ref = rest[-1]
    F = wo_ref.shape[0]
    tf = F // FFN_SPLIT
    x = h_ref[0]
    n = _rms_mod(x, g_ref[...], shift_ref[0], scale_ref[0]).astype(BF16)
    acc = None
    for j in range(FFN_SPLIT):
        a = jnp.dot(n, wi_ref[:, j * tf:(j + 1) * tf], preferred_element_type=F32)
        b = jnp.dot(n, wi_ref[:, F + j * tf:F + (j + 1) * tf], preferred_element_type=F32)
        hm = (_silu(a) * b).astype(BF16)
        part = jnp.dot(hm, wo_ref[j * tf:(j + 1) * tf, :], preferred_element_type=F32)
        acc = part if acc is None else acc + part
    y = x + (0.5 * gate_ref[0]) * acc
    if final_norm:
        y = y * lax.rsqrt(jnp.mean(y * y, axis=-1, keepdims=True) + EPS) * fg_ref[...]
    o_ref[0] = y


def _ffn(h, mods, l, sub, g, w_in, w_out, final_g=None):
    B, S, D = h.shape
    tm = FFN_ROWS
    row = pl.BlockSpec((1, tm, D), lambda b, i: (b, i, 0))
    extra = () if final_g is None else (final_g,)
    return pl.pallas_call(
        functools.partial(_ffn_kernel, final_norm=final_g is not None),
        grid=(B, S // tm),
        in_specs=[row] + [_mod_vec(mods, l, sub, k) for k in range(3)]
        + [_whole(g), _whole(w_in, True), _whole(w_out, True)] + [_whole(e) for e in extra],
        out_specs=row,
        out_shape=jax.ShapeDtypeStruct((B, S, D), F32),
        compiler_params=pltpu.CompilerParams(
            dimension_semantics=("parallel", "parallel"), vmem_limit_bytes=VMEM_LIMIT),
        name="ffn",
    )(h, mods, mods, mods, _arr(g), _arr(w_in), _arr(w_out), *extra)


def _rope_pair(x, c, s):
    lane = lax.broadcasted_iota(jnp.int32, x.shape, 1)
    first_half = (lane & (HEAD_DIM - 1)) < HEAD_DIM // 2
    swapped = jnp.where(first_half, pltpu.roll(x, LANE - HEAD_DIM // 2, 1),
                        pltpu.roll(x, HEAD_DIM // 2, 1))
    return x * c + swapped * s


def _mix_cols(D):
    kv0 = NSA_WIDTH
    g0 = kv0 + 6 * KV_GROUPS * HEAD_DIM
    tail0 = g0 + 3 * NSA_HEADS
    edges = np.cumsum((0, 2 * GM_WIDTH, D, D))
    tail = [slice(int(a), int(b)) for a, b in zip(edges[:-1], edges[1:])]
    return slice(0, kv0), slice(kv0, g0), slice(g0, g0 + LANE), tail0, tail


def _mix_in_kernel(h_ref, shift_ref, scale_ref, g_ref, w_ref, cos_ref, sin_ref, oh_ref,
                   lng_ref, lnb_ref, ws_ref, bsx_ref, pb_ref,
                   qt_ref, kcmp_ref, vcmp_ref, kselx_ref, vselt_ref, kwinx_ref, vwint_ref,
                   gt_ref, ga_ref, mb_ref, wt_scr):
    tm = h_ref.shape[1]
    CK = vselt_ref.shape[4]
    c_q, c_kv, c_g, tail0, (c_uv, c_ga, c_gb) = _mix_cols(h_ref.shape[2])

    @pl.when((pl.program_id(0) == 0) & (pl.program_id(1) == 0))
    def _():
        wt_scr[...] = w_ref[:, tail0:tail0 + wt_scr.shape[1]]

    n = _rms_mod(h_ref[0], g_ref[...], shift_ref[0], scale_ref[0]).astype(BF16)
    cos = cos_ref[...]
    sin = sin_ref[...]
    low_lanes = lax.broadcasted_iota(jnp.int32, (tm, LANE), 1) < HEAD_DIM

    q = jnp.dot(n, w_ref[:, c_q], preferred_element_type=F32)
    for p in range(NSA_HEADS // 2):
        qp = jnp.transpose(_rope_pair(q[:, p * LANE:(p + 1) * LANE], cos, sin)).astype(BF16)
        qt_ref[0, 2 * p] = qp[:HEAD_DIM]
        qt_ref[0, 2 * p + 1] = qp[HEAD_DIM:]

    kv = jnp.dot(n, w_ref[:, c_kv], preferred_element_type=F32)
    kcmp_ref[0] = kv[:, 0 * LANE:1 * LANE]
    vcmp_ref[0] = kv[:, 1 * LANE:2 * LANE]
    for idx, ref, ext in ((2, kselx_ref, oh_ref[...]), (4, kwinx_ref, 0.0)):
        k2 = _rope_pair(kv[:, idx * LANE:(idx + 1) * LANE], cos, sin)
        ref[0, 0] = jnp.where(low_lanes, k2, ext).astype(BF16)
        ref[0, 1] = jnp.where(low_lanes, pltpu.roll(k2, HEAD_DIM, 1), ext).astype(BF16)
    ones_row = jnp.where(lax.broadcasted_iota(jnp.int32, (V_ROWS - HEAD_DIM, CK), 0) == 0,
                         1.0, 0.0).astype(BF16)
    for idx, ref in ((3, vselt_ref), (5, vwint_ref)):
        vt = jnp.transpose(kv[:, idx * LANE:(idx + 1) * LANE]).astype(BF16)
        for g in range(KV_GROUPS):
            for r in range(tm // CK):
                ref[0, g, r] = jnp.concatenate(
                    [vt[g * HEAD_DIM:(g + 1) * HEAD_DIM, r * CK:(r + 1) * CK], ones_row], axis=0)

    gates = _sigmoid(jnp.dot(n, w_ref[:, c_g], preferred_element_type=F32))
    gt_ref[0] = jnp.transpose(gates)[:GATE_ROWS]
    ga_ref[0] = _sigmoid(jnp.dot(n, wt_scr[:, c_ga], preferred_element_type=F32))

    uv = jnp.dot(n, wt_scr[:, c_uv], preferred_element_type=F32)
    ge = uv * (0.5 * (1.0 + jnp.tanh(GELU_C * (uv + 0.044715 * (uv * uv * uv)))))
    u = ge[:, :GM_WIDTH]
    v = ge[:, GM_WIDTH:]
    mu = jnp.mean(v, axis=-1, keepdims=True)
    var = jnp.mean(jnp.square(v - mu), axis=-1, keepdims=True)
    vln = ((v - mu) * lax.rsqrt(var + EPS) * lng_ref[...] + lnb_ref[...]).astype(BF16)

    ti = lax.broadcasted_iota(jnp.int32, (GM_CHUNK, GM_CHUNK), 0)
    si = lax.broadcasted_iota(jnp.int32, (GM_CHUNK, GM_CHUNK), 1)
    tril = si <= ti
    wm = [jnp.where(tril, ws_ref[gg], 0.0).astype(BF16) for gg in range(GM_GROUPS)]
    low = lax.broadcasted_iota(jnp.int32, (GM_CHUNK, LANE), 1) < HEAD_DIM
    bsx = bsx_ref[...]
    yb_rows = []
    for r in range(tm // GM_CHUNK):
        vch = vln[r * GM_CHUNK:(r + 1) * GM_CHUNK]
        pieces = []
        for p in range(GM_GROUPS // 2):
            vp = vch[:, p * LANE:(p + 1) * LANE]
            a0 = jnp.dot(wm[2 * p], vp, preferred_element_type=F32)
            a1 = jnp.dot(wm[2 * p + 1], vp, preferred_element_type=F32)
            pieces.append(jnp.where(low, a0, a1))
        sv = jnp.concatenate(pieces, axis=1) + bsx
        yb_rows.append(u[r * GM_CHUNK:(r + 1) * GM_CHUNK] * sv)
    yb = jnp.concatenate(yb_rows, axis=0).astype(BF16)
    gb = _sigmoid(jnp.dot(n, wt_scr[:, c_gb], preferred_element_type=F32))
    mb_ref[0] = gb * jnp.dot(yb, pb_ref[...], preferred_element_type=F32)


def _mix_in(h, mods, l, g, w, cos_t, sin_t, oh_t, ln_g, ln_b, ws, bsx, proj_b):
    B, S, D = h.shape
    tail0, tail = _mix_cols(D)[3:]
    assert _arr(w).shape[-2:] == (D, tail0 + tail[-1].stop)
    tm = MIX_ROWS
    G = KV_GROUPS
    CK = ATT_CHUNK
    const = _whole
    row = lambda w: pl.BlockSpec((1, tm, w), lambda b, i: (b, i, 0))
    tab = pl.BlockSpec((tm, LANE), lambda b, i: (i, 0))
    kspec = pl.BlockSpec((1, G, tm, LANE), lambda b, i: (b, 0, i, 0))
    kshape = jax.ShapeDtypeStruct((B, G, S, LANE), BF16)
    vspec = pl.BlockSpec((1, G, tm // CK, V_ROWS, CK), lambda b, i: (b, 0, i, 0, 0))
    vshape = jax.ShapeDtypeStruct((B, G, S // CK, V_ROWS, CK), BF16)
    return pl.pallas_call(
        _mix_in_kernel,
        grid=(B, S // tm),
        in_specs=[row(D), _mod_vec(mods, l, 1, 0), _mod_vec(mods, l, 1, 1), const(g),
                  _whole(w, True), tab, tab, tab, const(ln_g), const(ln_b), const(ws),
                  const(bsx), const(proj_b)],
        out_specs=[
            pl.BlockSpec((1, NSA_HEADS, HEAD_DIM, tm), lambda b, i: (b, 0, 0, i)),
            row(LANE), row(LANE), kspec, vspec, kspec, vspec,
            pl.BlockSpec((1, GATE_ROWS, tm), lambda b, i: (b, 0, i)),
            row(D), row(D)],
        out_shape=[
            jax.ShapeDtypeStruct((B, NSA_HEADS, HEAD_DIM, S), BF16),
            jax.ShapeDtypeStruct((B, S, LANE), F32), jax.ShapeDtypeStruct((B, S, LANE), F32),
            kshape, vshape, kshape, vshape,
            jax.ShapeDtypeStruct((B, GATE_ROWS, S), F32),
            jax.ShapeDtypeStruct((B, S, D), F32), jax.ShapeDtypeStruct((B, S, D), F32)],
        scratch_shapes=[pltpu.VMEM((D, tail[-1].stop), BF16)],
        compiler_params=pltpu.CompilerParams(
            dimension_semantics=("arbitrary", "arbitrary"), vmem_limit_bytes=VMEM_LIMIT),
        name="mix_in",
    )(h, mods, mods, *map(_arr, (g, w, cos_t, sin_t, oh_t, ln_g, ln_b, ws, bsx, proj_b)))


def _compress_kernel(k_ref, v_ref, pe_ref, w1_ref, w2k_ref, w2v_ref, cos_ref, sin_ref,
                     kc_ref, vct_ref):
    NC = kc_ref.shape[2]
    half = CMP_BLOCK // 2

    def hidden(x_ref, j):
        ha = [None] * KV_GROUPS
        hb = [None] * KV_GROUPS
        for l in range(half):
            x = x_ref[0, pl.ds(l, NC, stride=CMP_STRIDE), :]
            for g in range(KV_GROUPS):
                xg = x[:, g * HEAD_DIM:(g + 1) * HEAD_DIM]
                a = jnp.dot((xg + pe_ref[j, l]).astype(BF16), w1_ref[j, l],
                            preferred_element_type=F32)
                b = jnp.dot((xg + pe_ref[j, half + l]).astype(BF16), w1_ref[j, half + l],
                            preferred_element_type=F32)
                ha[g] = a if ha[g] is None else ha[g] + a
                hb[g] = b if hb[g] is None else hb[g] + b
        return [_silu(ha[g] + pltpu.roll(hb[g], NC - 1, 0)) for g in range(KV_GROUPS)]

    hk = hidden(k_ref, 0)
    hv = hidden(v_ref, 1)
    for g in range(KV_GROUPS):
        k2 = jnp.dot(hk[g].astype(BF16), w2k_ref[...], preferred_element_type=F32)
        kc = k2[:, :HEAD_DIM] * cos_ref[...] + k2[:, HEAD_DIM:] * sin_ref[...]
        kc_ref[0, g] = kc.astype(BF16)
        vc = jnp.dot(hv[g].astype(BF16), w2v_ref[...],
                     preferred_element_type=F32)
        vct_ref[0, g] = jnp.transpose(vc)[:HEAD_DIM].astype(BF16)


def _compress(kcmp, vcmp, pe, w1, w2k, w2v, cos_c, sin_c):
    B, S, W = kcmp.shape
    NC = S // CMP_STRIDE
    const = _whole
    blk = pl.BlockSpec((1, S, W), lambda b: (b, 0, 0))
    return pl.pallas_call(
        _compress_kernel,
        grid=(B,),
        in_specs=[blk, blk, const(pe), const(w1), const(w2k), const(w2v),
                  const(cos_c), const(sin_c)],
        out_specs=[pl.BlockSpec((1, KV_GROUPS, NC, HEAD_DIM), lambda b: (b, 0, 0, 0)),
                   pl.BlockSpec((1, KV_GROUPS, HEAD_DIM, NC), lambda b: (b, 0, 0, 0))],
        out_shape=[jax.ShapeDtypeStruct((B, KV_GROUPS, NC, HEAD_DIM), BF16),
                   jax.ShapeDtypeStruct((B, KV_GROUPS, HEAD_DIM, NC), BF16)],
        compiler_params=pltpu.CompilerParams(
            dimension_semantics=("parallel",), vmem_limit_bytes=VMEM_LIMIT),
        name="compress",
    )(kcmp, vcmp, *map(_arr, (pe, w1, w2k, w2v, cos_c, sin_c)))


def _fold_rows(x, op):
    parts = [x[8 * i:8 * (i + 1)] for i in range(x.shape[0] // 8)]
    while len(parts) > 1:
        parts = [op(parts[i], parts[i + 1]) for i in range(0, len(parts), 2)]
    return parts[0]


def _attn_kernel(qt_ref, kselx_ref, vselt_ref, kwinx_ref, vwint_ref, kc_ref, vct_ref, gt_ref,
                 ga_ref, mb_ref, h_ref, gate_ref, ovt_ref, bias_ref, pa_ref, wo_ref,
                 o_ref, qx_scr, s_scr, m_scr, a_scr, acc_scr, yt_scr):
    TQ = h_ref.shape[1]
    CK = ATT_CHUNK
    NC = kc_ref.shape[2]
    NSEL = kselx_ref.shape[2] // SEL_BLOCK
    nwin = WINDOW // CK
    qi = pl.program_id(1)

    def branch(kx_ref, vt_ref, first, far, gate_row):
        def score(c, masked, dst):
            kx = [kx_ref[0, g, pl.ds(pl.multiple_of(c * CK, CK), CK), :]
                  for g in range(KV_GROUPS)]
            if masked:
                bias = bias_ref[jnp.where(c == qi, 1, jnp.where(c == far, 2, 0))]
            for hd in range(NSA_HEADS):
                s = jnp.dot(kx[hd // HPG], qx_scr[hd], preferred_element_type=F32)
                if masked:
                    s = s + bias
                s_scr[dst, hd] = s
                m_prev = m_scr[1 - dst, hd]
                m_new = jnp.maximum(m_prev, jnp.max(_fold_rows(s, jnp.maximum), axis=0,
                                                    keepdims=True))
                a_scr[dst, hd] = jnp.exp2(m_prev - m_new)
                m_scr[dst, hd] = m_new

        def weigh(c, src):
            vt = [vt_ref[0, g, c] for g in range(KV_GROUPS)]
            for hd in range(NSA_HEADS):
                p = jnp.exp2(s_scr[src, hd] - m_scr[src, hd]).astype(BF16)
                acc_scr[hd] = a_scr[src, hd] * acc_scr[hd] + jnp.dot(
                    vt[hd // HPG], p, preferred_element_type=F32)

        def loop(lo, hi, body):
            def wrapped(c, carry):
                body(c)
                return carry
            lax.fori_loop(lo, hi, wrapped, 0)

        def two_chunks(i):
            c = first + 2 * i
            score(c + 1, False, 1)
            weigh(c, 0)
            score(c + 2, True, 0)
            weigh(c + 1, 1)

        def last_two(c):
            score(c + 1, True, 1)
            weigh(c, 0)

        m_scr[...] = jnp.full_like(m_scr, NEG)
        acc_scr[...] = jnp.zeros_like(acc_scr)
        loop(first, first + 1, lambda c: score(c, True, 0))
        rest = qi - first
        pairs = jnp.right_shift(rest, 1)
        odd = rest & 1
        loop(0, pairs, two_chunks)
        loop(qi - 1, qi - 1 + odd, last_two)
        loop(qi, qi + odd, lambda c: weigh(c, 1))
        loop(qi, qi + 1 - odd, lambda c: weigh(c, 0))
        for hd in range(NSA_HEADS):
            w = gt_ref[0, 3 * hd + gate_row:3 * hd + gate_row + 1, :] / acc_scr[
                hd, HEAD_DIM:HEAD_DIM + 1, :]
            rows = slice(hd * HEAD_DIM, (hd + 1) * HEAD_DIM)
            yt_scr[rows, :] = yt_scr[rows, :] + w * acc_scr[hd, :HEAD_DIM, :]

    n_idx = lax.broadcasted_iota(jnp.int32, (NC, TQ), 0)
    t_cmp = qi * TQ + lax.broadcasted_iota(jnp.int32, (NC, TQ), 1)
    vis = (n_idx * CMP_STRIDE + (CMP_BLOCK - 1)) <= t_cmp
    visf = jnp.where(vis, 1.0, 0.0)
    j_idx = lax.broadcasted_iota(jnp.int32, (NSEL, TQ), 0)
    t_sel = qi * TQ + lax.broadcasted_iota(jnp.int32, (NSEL, TQ), 1)
    cur = jnp.right_shift(t_sel, SEL_BLOCK.bit_length() - 1)
    forced = (j_idx == 0) | (j_idx == cur) | (j_idx == cur - 1)
    valid = j_idx <= cur

    heads = range(NSA_HEADS)
    m_c = []
    for hd in heads:
        sc = jnp.where(vis, jnp.dot(kc_ref[0, hd // HPG], qt_ref[0, hd],
                                    preferred_element_type=F32), NEG)
        s_scr[0, hd, :NC, :] = sc
        m_c.append(jnp.max(_fold_rows(sc, jnp.maximum), axis=0, keepdims=True))
    inv = []
    for hd in heads:
        ec = jnp.exp2(s_scr[0, hd, :NC, :] - m_c[hd]) * visf
        s_scr[0, hd, :NC, :] = ec
        den = jnp.sum(_fold_rows(ec, jnp.add), axis=0, keepdims=True)
        inv.append(1.0 / jnp.where(den > 0.0, den, 1.0))
    psum = [jnp.zeros((NC, TQ), F32) for _ in range(KV_GROUPS)]
    for hd in heads:
        g = hd // HPG
        pc = s_scr[0, hd, :NC, :] * inv[hd]
        psum[g] = psum[g] + pc
        oc = jnp.dot(vct_ref[0, g], pc.astype(BF16), preferred_element_type=F32)
        yt_scr[hd * HEAD_DIM:(hd + 1) * HEAD_DIM, :] = gt_ref[0, 3 * hd:3 * hd + 1, :] * oc

    row8 = lax.broadcasted_iota(jnp.int32, (8, TQ), 0)
    pad = jnp.zeros((LANE - HEAD_DIM - NSEL, TQ), BF16)
    for g in range(KV_GROUPS):
        imp = jnp.dot(ovt_ref[...], psum[g], preferred_element_type=F32,
                      precision=lax.Precision.HIGHEST)
        imp = jnp.where(forced, FORCE, jnp.where(valid, imp, -FORCE))
        tiles = [imp[8 * r:8 * (r + 1)] for r in range(NSEL // 8)]
        rank = [jnp.zeros((8, TQ), jnp.int32) for _ in tiles]
        for kk in range(NSEL):
            rk = imp[kk:kk + 1, :]
            for r, tile in enumerate(tiles):
                if 8 * r > kk:
                    ahead = rk >= tile
                elif 8 * r + 7 < kk:
                    ahead = rk > tile
                else:
                    ahead = (rk > tile) | ((rk == tile) & (row8 > kk - 8 * r))
                rank[r] = rank[r] + jnp.where(ahead, 1, 0)
        selneg = jnp.where(jnp.concatenate(rank, axis=0) < SEL_TOPK, 0.0, NEG).astype(BF16)
        for hh in range(HPG):
            hd = g * HPG + hh
            qx_scr[hd] = jnp.concatenate([qt_ref[0, hd], selneg, pad], axis=0)

    branch(kselx_ref, vselt_ref, 0, -1, 1)
    branch(kwinx_ref, vwint_ref, jnp.maximum(qi - nwin, 0), qi - nwin, 2)

    y = jnp.transpose(yt_scr[...]).astype(BF16)
    ya = jnp.dot(y, pa_ref[...], preferred_element_type=F32)
    merged = (ga_ref[0] * ya + mb_ref[0]).astype(BF16)
    o_ref[0] = h_ref[0] + gate_ref[0] * jnp.dot(merged, wo_ref[...],
                                                preferred_element_type=F32)


def _attn(qt, kselx, vselt, kwinx, vwint, kc, vct, gt, ga, mb, h, mods, l, ovt, bias,
          proj_a, w_out):
    B, S, D = h.shape
    TQ = ATT_ROWS
    CK = ATT_CHUNK
    assert TQ == CK and WINDOW % CK == 0 and S % TQ == 0
    const = _whole

    def per_batch(a):
        nd = a.ndim
        return pl.BlockSpec((1,) + a.shape[1:], lambda b, i: (b,) + (0,) * (nd - 1))

    row = lambda w: pl.BlockSpec((1, TQ, w), lambda b, i: (b, i, 0))
    return pl.pallas_call(
        _attn_kernel,
        grid=(B, S // TQ),
        in_specs=[
            pl.BlockSpec((1, NSA_HEADS, HEAD_DIM, TQ), lambda b, i: (b, 0, 0, i)),
            per_batch(kselx), per_batch(vselt), per_batch(kwinx), per_batch(vwint),
            per_batch(kc), per_batch(vct),
            pl.BlockSpec((1, GATE_ROWS, TQ), lambda b, i: (b, 0, i)),
            row(D), row(D), row(D), _mod_vec(mods, l, 1, 2),
            const(ovt), const(bias), const(proj_a), const(w_out)],
        out_specs=row(D),
        out_shape=jax.ShapeDtypeStruct((B, S, D), F32),
        scratch_shapes=[
            pltpu.VMEM((NSA_HEADS, LANE, TQ), BF16),
            pltpu.VMEM((2, NSA_HEADS, CK, TQ), F32),
            pltpu.VMEM((2, NSA_HEADS, 1, TQ), F32), pltpu.VMEM((2, NSA_HEADS, 1, TQ), F32),
            pltpu.VMEM((NSA_HEADS, V_ROWS, TQ), F32),
            pltpu.VMEM((NSA_WIDTH, TQ), F32)],
        compiler_params=pltpu.CompilerParams(
            dimension_semantics=("parallel", "parallel"), vmem_limit_bytes=VMEM_LIMIT),
        name="attn",
    )(qt, kselx, vselt, kwinx, vwint, kc, vct, gt, ga, mb, h, mods, ovt, bias,
      _arr(proj_a), _arr(w_out))


def _rope_tables(pos):
    inv = 1.0 / (ROPE_THETA ** (np.arange(0, HEAD_DIM, 2, dtype=np.float64) / HEAD_DIM))
    ang = np.asarray(pos, np.float64)[:, None] * inv[None, :]
    cos = np.concatenate([np.cos(ang), np.cos(ang)], axis=1)
    sin = np.concatenate([-np.sin(ang), np.sin(ang)], axis=1)
    return cos.astype(np.float32), sin.astype(np.float32)


def _tables(S):
    cos, sin = _rope_tables(np.arange(S))
    cos_t = np.concatenate([cos, cos], axis=1)
    sin_t = np.concatenate([sin, sin], axis=1)
    n_cmp_pad = S // CMP_STRIDE
    starts = np.arange(n_cmp_pad) * CMP_STRIDE
    cos_c, sin_c = _rope_tables(starts + CMP_BLOCK - 1)
    n_sel = S // SEL_BLOCK
    sel_start = np.arange(n_sel) * SEL_BLOCK
    overlap = np.clip(np.minimum(starts[:, None] + CMP_BLOCK, sel_start[None, :] + SEL_BLOCK)
                      - np.maximum(starts[:, None], sel_start[None, :]), 0, None) / CMP_BLOCK
    ovt = np.ascontiguousarray(overlap.T).astype(np.float32)
    oh = np.zeros((S, LANE), np.float32)
    oh[np.arange(S), HEAD_DIM + np.arange(S) // SEL_BLOCK] = 1.0
    j = np.arange(ATT_CHUNK)[:, None]
    i = np.arange(ATT_ROWS)[None, :]
    bias = np.stack([np.zeros((ATT_CHUNK, ATT_ROWS)),
                     np.where(j <= i, 0.0, NEG),
                     np.where(j > i, 0.0, NEG)]
                    ).astype(np.float32)
    return tuple(jnp.asarray(a) for a in (cos_t, sin_t, cos_c, sin_c, ovt, oh, bias))


def _cast_mix_weights(mix_w_in):
    col = np.ones((mix_w_in.shape[-1],), np.float32)
    col[:NSA_WIDTH] = HEAD_DIM ** -0.5 * LOG2E
    return (mix_w_in * col).astype(BF16)


def kernel(x, c, ada_w, ada_b, norm_g, ffn_w_in, ffn_w_out, mix_w_in, cmp_pe, cmp_w1, cmp_w2,
           gm_ln_g, gm_ln_b, gm_ws, gm_bs, proj_a, proj_b, w_out, final_g):
    B, S, D = x.shape
    L = ada_w.shape[0]
    assert S // SEL_BLOCK + HEAD_DIM <= LANE and 3 * NSA_HEADS <= GATE_ROWS
    cos_t, sin_t, cos_c, sin_c, ovt, oh_t, bias = _tables(S)
    swap = np.concatenate([np.arange(HEAD_DIM // 2, HEAD_DIM), np.arange(HEAD_DIM // 2)])

    w_fi, w_fo = ffn_w_in.astype(BF16), ffn_w_out.astype(BF16)
    w_mix = _cast_mix_weights(mix_w_in)
    w_pa, w_pb, w_o = proj_a.astype(BF16), proj_b.astype(BF16), w_out.astype(BF16)
    w1 = cmp_w1.astype(BF16)
    pe = cmp_pe[:, :, :, None, :]
    w2k = jnp.concatenate([cmp_w2[:, 0], cmp_w2[:, 0][..., swap]], axis=-1).astype(BF16)
    w2v = jnp.pad(cmp_w2[:, 1], ((0, 0), (0, 0), (0, LANE - HEAD_DIM))).astype(BF16)
    bsx = jnp.repeat(jnp.swapaxes(gm_bs, 1, 2), HEAD_DIM, axis=2)

    gains = norm_g[:, :, None, :]
    ln_g, ln_b = gm_ln_g[:, None, :], gm_ln_b[:, None, :]

    mods = _ada(c, ada_w, ada_b).reshape(L, B, 3, 3, 1, D)
    h = x
    for l in range(L):
        h = _ffn(h, mods, l, 0, _Pick(gains, (l, 0)), _Pick(w_fi, (l, 0)), _Pick(w_fo, (l, 0)))
        qt, kcmp, vcmp, kselx, vselt, kwinx, vwint, gt, ga, mb = _mix_in(
            h, mods, l, _Pick(gains, (l, 1)), _Pick(w_mix, (l,)), cos_t, sin_t, oh_t,
            _Pick(ln_g, (l,)), _Pick(ln_b, (l,)), _Pick(gm_ws, (l,)), _Pick(bsx, (l,)),
            _Pick(w_pb, (l,)))
        kc, vct = _compress(kcmp, vcmp, _Pick(pe, (l,)), _Pick(w1, (l,)), _Pick(w2k, (l,)),
                            _Pick(w2v, (l,)), cos_c, sin_c)
        h = _attn(qt, kselx, vselt, kwinx, vwint, kc, vct, gt, ga, mb, h, mods, l, ovt, bias,
                  _Pick(w_pa, (l,)), _Pick(w_o, (l,)))
        h = _ffn(h, mods, l, 2, _Pick(gains, (l, 2)), _Pick(w_fi, (l, 1)), _Pick(w_fo, (l, 1)),
                 final_g=final_g[None] if l == L - 1 else None)
    return h
```

```python
import functools
from typing import NamedTuple

import numpy as np
import jax
import jax.numpy as jnp
from jax import lax
from jax.experimental import pallas as pl
from jax.experimental.pallas import tpu as pltpu

F32 = jnp.float32
BF16 = jnp.bfloat16

HEAD_DIM = 64
NSA_HEADS = 8
KV_GROUPS = 2
HPG = NSA_HEADS // KV_GROUPS
NSA_WIDTH = NSA_HEADS * HEAD_DIM
CMP_BLOCK = 32
CMP_STRIDE = 16
CMP_HIDDEN = 128
SEL_BLOCK = 64
SEL_TOPK = 16
WINDOW = 512
GM_GROUPS = 8
GM_CHUNK = 128
GM_WIDTH = GM_GROUPS * HEAD_DIM
ROPE_THETA = 10000.0
EPS = 1e-6
NEG = -1e30
FORCE = 1e4
LANE = 128
GELU_C = float(np.sqrt(2.0 / np.pi))
LOG2E = float(np.log2(np.e))
GATE_ROWS = 32
V_ROWS = HEAD_DIM + 16

FFN_ROWS = 512
FFN_SPLIT = 2
MIX_ROWS = 512
ATT_ROWS = 256
ATT_CHUNK = 256
VMEM_LIMIT = 56 * 1024 * 1024


def _sigmoid(x):
    return 1.0 / (1.0 + jnp.exp(-x))


def _silu(x):
    return x * _sigmoid(x)


def _rms_mod(x, g, shift, scale):
    y = x * lax.rsqrt(jnp.mean(x * x, axis=-1, keepdims=True) + EPS) * g
    return y * (1.0 + scale) + shift


class _Pick(NamedTuple):
    array: jax.Array
    idx: tuple


def _arr(x):
    return x.array if isinstance(x, _Pick) else x


def _whole(x, single_buffer=False):
    idx = x.idx if isinstance(x, _Pick) else ()
    shape = _arr(x).shape
    mode = dict(pipeline_mode=pl.Buffered(1)) if single_buffer else {}
    return pl.BlockSpec((None,) * len(idx) + shape[len(idx):],
                        lambda *_: idx + (0,) * (len(shape) - len(idx)), **mode)


def _mod_vec(mods, l, sub, k):
    return pl.BlockSpec((None, 1, None, None, 1, mods.shape[-1]),
                        lambda b, *_: (l, b, sub, k, 0, 0))


def _ada_kernel(c_ref, w_ref, b_ref, o_ref):
    s = _silu(c_ref[...])
    o_ref[0] = jnp.dot(s, w_ref[0], preferred_element_type=F32,
                       precision=lax.Precision.HIGHEST) + b_ref[0]


def _ada(c, ada_w, ada_b):
    L, D, N = ada_w.shape
    B = c.shape[0]
    tn = 1024
    return pl.pallas_call(
        _ada_kernel,
        grid=(L, N // tn),
        in_specs=[
            pl.BlockSpec((B, D), lambda l, j: (0, 0)),
            pl.BlockSpec((1, D, tn), lambda l, j: (l, 0, j)),
            pl.BlockSpec((1, 1, tn), lambda l, j: (l, 0, j)),
        ],
        out_specs=pl.BlockSpec((1, B, tn), lambda l, j: (l, 0, j)),
        out_shape=jax.ShapeDtypeStruct((L, B, N), F32),
        compiler_params=pltpu.CompilerParams(
            dimension_semantics=("parallel", "parallel"), vmem_limit_bytes=VMEM_LIMIT),
        name="ada",
    )(c, ada_w, ada_b.reshape(L, 1, N))


def _ffn_kernel(h_ref, shift_ref, scale_ref, gate_ref, g_ref, wi_ref, wo_ref, *rest,
                final_norm):
    fg_ref = rest[0] if final_norm else None
    o_ref = rest[-1]
    F = wo_ref.shape[0]
    tf = F // FFN_SPLIT
    x = h_ref[0]
    n = _rms_mod(x, g_ref[...], shift_ref[0], scale_ref[0]).astype(BF16)
    acc = None
    for j in range(FFN_SPLIT):
        a = jnp.dot(n, wi_ref[:, j * tf:(j + 1) * tf], preferred_element_type=F32)
        b = jnp.dot(n, wi_ref[:, F + j * tf:F + (j + 1) * tf], preferred_element_type=F32)
        hm = (_silu(a) * b).astype(BF16)
        part = jnp.dot(hm, wo_ref[j * tf:(j + 1) * tf, :], preferred_element_type=F32)
        acc = part if acc is None else acc + part
    y = x + (0.5 * gate_ref[0]) * acc
    if final_norm:
        y = y * lax.rsqrt(jnp.mean(y * y, axis=-1, keepdims=True) + EPS) * fg_ref[...]
    o_ref[0] = y


def _ffn(h, mods, l, sub, g, w_in, w_out, final_g=None):
    B, S, D = h.shape
    tm = FFN_ROWS
    row = pl.BlockSpec((1, tm, D), lambda b, i: (b, i, 0))
    extra = () if final_g is None else (final_g,)
    return pl.pallas_call(
        functools.partial(_ffn_kernel, final_norm=final_g is not None),
        grid=(B, S // tm),
        in_specs=[row] + [_mod_vec(mods, l, sub, k) for k in range(3)]
        + [_whole(g), _whole(w_in, True), _whole(w_out, True)] + [_whole(e) for e in extra],
        out_specs=row,
        out_shape=jax.ShapeDtypeStruct((B, S, D), F32),
        compiler_params=pltpu.CompilerParams(
            dimension_semantics=("parallel", "parallel"), vmem_limit_bytes=VMEM_LIMIT),
        name="ffn",
    )(h, mods, mods, mods, _arr(g), _arr(w_in), _arr(w_out), *extra)


def _rope_pair(x, c, s):
    lane = lax.broadcasted_iota(jnp.int32, x.shape, 1)
    first_half = (lane & (HEAD_DIM - 1)) < HEAD_DIM // 2
    swapped = jnp.where(first_half, pltpu.roll(x, LANE - HEAD_DIM // 2, 1),
                        pltpu.roll(x, HEAD_DIM // 2, 1))
    return x * c + swapped * s


def _mix_cols(D):
    kv0 = NSA_WIDTH
    g0 = kv0 + 6 * KV_GROUPS * HEAD_DIM
    tail0 = g0 + 3 * NSA_HEADS
    edges = np.cumsum((0, 2 * GM_WIDTH, D, D))
    tail = [slice(int(a), int(b)) for a, b in zip(edges[:-1], edges[1:])]
    return slice(0, kv0), slice(kv0, g0), slice(g0, g0 + LANE), tail0, tail


def _mix_in_kernel(h_ref, shift_ref, scale_ref, g_ref, w_ref, cos_ref, sin_ref, oh_ref,
                   lng_ref, lnb_ref, ws_ref, bsx_ref, pb_ref,
                   qt_ref, kcmp_ref, vcmp_ref, kselx_ref, vselt_ref, kwinx_ref, vwint_ref,
                   gt_ref, ga_ref, mb_ref, wt_scr):
    tm = h_ref.shape[1]
    CK = vselt_ref.shape[4]
    c_q, c_kv, c_g, tail0, (c_uv, c_ga, c_gb) = _mix_cols(h_ref.shape[2])

    @pl.when((pl.program_id(0) == 0) & (pl.program_id(1) == 0))
    def _():
        wt_scr[...] = w_ref[:, tail0:tail0 + wt_scr.shape[1]]

    n = _rms_mod(h_ref[0], g_ref[...], shift_ref[0], scale_ref[0]).astype(BF16)
    cos = cos_ref[...]
    sin = sin_ref[...]
    low_lanes = lax.broadcasted_iota(jnp.int32, (tm, LANE), 1) < HEAD_DIM

    q = jnp.dot(n, w_ref[:, c_q], preferred_element_type=F32) * (HEAD_DIM ** -0.5 * LOG2E)
    for p in range(NSA_HEADS // 2):
        qp = jnp.transpose(_rope_pair(q[:, p * LANE:(p + 1) * LANE], cos, sin)).astype(BF16)
        qt_ref[0, 2 * p] = qp[:HEAD_DIM]
        qt_ref[0, 2 * p + 1] = qp[HEAD_DIM:]

    kv = jnp.dot(n, w_ref[:, c_kv], preferred_element_type=F32)
    kcmp_ref[0] = kv[:, 0 * LANE:1 * LANE]
    vcmp_ref[0] = kv[:, 1 * LANE:2 * LANE]
    for idx, ref, ext in ((2, kselx_ref, oh_ref[...]), (4, kwinx_ref, 0.0)):
        k2 = _rope_pair(kv[:, idx * LANE:(idx + 1) * LANE], cos, sin)
        ref[0, 0] = jnp.where(low_lanes, k2, ext).astype(BF16)
        ref[0, 1] = jnp.where(low_lanes, pltpu.roll(k2, HEAD_DIM, 1), ext).astype(BF16)
    ones_row = jnp.where(lax.broadcasted_iota(jnp.int32, (V_ROWS - HEAD_DIM, CK), 0) == 0,
                         1.0, 0.0).astype(BF16)
    for idx, ref in ((3, vselt_ref), (5, vwint_ref)):
        vt = jnp.transpose(kv[:, idx * LANE:(idx + 1) * LANE]).astype(BF16)
        for g in range(KV_GROUPS):
            for r in range(tm // CK):
                ref[0, g, r] = jnp.concatenate(
                    [vt[g * HEAD_DIM:(g + 1) * HEAD_DIM, r * CK:(r + 1) * CK], ones_row], axis=0)

    gates = _sigmoid(jnp.dot(n, w_ref[:, c_g], preferred_element_type=F32))
    gt_ref[0] = jnp.transpose(gates)[:GATE_ROWS]
    ga_ref[0] = _sigmoid(jnp.dot(n, wt_scr[:, c_ga], preferred_element_type=F32))

    uv = jnp.dot(n, wt_scr[:, c_uv], preferred_element_type=F32)
    ge = uv * (0.5 * (1.0 + jnp.tanh(GELU_C * (uv + 0.044715 * (uv * uv * uv)))))
    u = ge[:, :GM_WIDTH]
    v = ge[:, GM_WIDTH:]
    mu = jnp.mean(v, axis=-1, keepdims=True)
    var = jnp.mean(jnp.square(v - mu), axis=-1, keepdims=True)
    vln = ((v - mu) * lax.rsqrt(var + EPS) * lng_ref[...] + lnb_ref[...]).astype(BF16)

    ti = lax.broadcasted_iota(jnp.int32, (GM_CHUNK, GM_CHUNK), 0)
    si = lax.broadcasted_iota(jnp.int32, (GM_CHUNK, GM_CHUNK), 1)
    tril = si <= ti
    wm = [jnp.where(tril, ws_ref[gg], 0.0).astype(BF16) for gg in range(GM_GROUPS)]
    low = lax.broadcasted_iota(jnp.int32, (GM_CHUNK, LANE), 1) < HEAD_DIM
    bsx = bsx_ref[...]
    yb_rows = []
    for r in range(tm // GM_CHUNK):
        vch = vln[r * GM_CHUNK:(r + 1) * GM_CHUNK]
        pieces = []
        for p in range(GM_GROUPS // 2):
            vp = vch[:, p * LANE:(p + 1) * LANE]
            a0 = jnp.dot(wm[2 * p], vp, preferred_element_type=F32)
            a1 = jnp.dot(wm[2 * p + 1], vp, preferred_element_type=F32)
            pieces.append(jnp.where(low, a0, a1))
        sv = jnp.concatenate(pieces, axis=1) + bsx
        yb_rows.append(u[r * GM_CHUNK:(r + 1) * GM_CHUNK] * sv)
    yb = jnp.concatenate(yb_rows, axis=0).astype(BF16)
    gb = _sigmoid(jnp.dot(n, wt_scr[:, c_gb], preferred_element_type=F32))
    mb_ref[0] = gb * jnp.dot(yb, pb_ref[...], preferred_element_type=F32)


def _mix_in(h, mods, l, g, w, cos_t, sin_t, oh_t, ln_g, ln_b, ws, bsx, proj_b):
    B, S, D = h.shape
    tail0, tail = _mix_cols(D)[3:]
    assert _arr(w).shape[-2:] == (D, tail0 + tail[-1].stop)
    tm = MIX_ROWS
    G = KV_GROUPS
    CK = ATT_CHUNK
    const = _whole
    row = lambda w: pl.BlockSpec((1, tm, w), lambda b, i: (b, i, 0))
    tab = pl.BlockSpec((tm, LANE), lambda b, i: (i, 0))
    kspec = pl.BlockSpec((1, G, tm, LANE), lambda b, i: (b, 0, i, 0))
    kshape = jax.ShapeDtypeStruct((B, G, S, LANE), BF16)
    vspec = pl.BlockSpec((1, G, tm // CK, V_ROWS, CK), lambda b, i: (b, 0, i, 0, 0))
    vshape = jax.ShapeDtypeStruct((B, G, S // CK, V_ROWS, CK), BF16)
    return pl.pallas_call(
        _mix_in_kernel,
        grid=(B, S // tm),
        in_specs=[row(D), _mod_vec(mods, l, 1, 0), _mod_vec(mods, l, 1, 1), const(g),
                  _whole(w, True), tab, tab, tab, const(ln_g), const(ln_b), const(ws),
                  const(bsx), const(proj_b)],
        out_specs=[
            pl.BlockSpec((1, NSA_HEADS, HEAD_DIM, tm), lambda b, i: (b, 0, 0, i)),
            row(LANE), row(LANE), kspec, vspec, kspec, vspec,
            pl.BlockSpec((1, GATE_ROWS, tm), lambda b, i: (b, 0, i)),
            row(D), row(D)],
        out_shape=[
            jax.ShapeDtypeStruct((B, NSA_HEADS, HEAD_DIM, S), BF16),
            jax.ShapeDtypeStruct((B, S, LANE), F32), jax.ShapeDtypeStruct((B, S, LANE), F32),
            kshape, vshape, kshape, vshape,
            jax.ShapeDtypeStruct((B, GATE_ROWS, S), F32),
            jax.ShapeDtypeStruct((B, S, D), F32), jax.ShapeDtypeStruct((B, S, D), F32)],
        scratch_shapes=[pltpu.VMEM((D, tail[-1].stop), BF16)],
        compiler_params=pltpu.CompilerParams(
            dimension_semantics=("arbitrary", "arbitrary"), vmem_limit_bytes=VMEM_LIMIT),
        name="mix_in",
    )(h, mods, mods, *map(_arr, (g, w, cos_t, sin_t, oh_t, ln_g, ln_b, ws, bsx, proj_b)))


def _compress_kernel(k_ref, v_ref, pe_ref, w1_ref, w2k_ref, w2v_ref, cos_ref, sin_ref,
                     kc_ref, vct_ref):
    NC = kc_ref.shape[2]
    half = CMP_BLOCK // 2

    def hidden(x_ref, j):
        ha = [None] * KV_GROUPS
        hb = [None] * KV_GROUPS
        for l in range(half):
            x = x_ref[0, pl.ds(l, NC, stride=CMP_STRIDE), :]
            for g in range(KV_GROUPS):
                xg = x[:, g * HEAD_DIM:(g + 1) * HEAD_DIM]
                a = jnp.dot((xg + pe_ref[j, l]).astype(BF16), w1_ref[j, l],
                            preferred_element_type=F32)
                b = jnp.dot((xg + pe_ref[j, half + l]).astype(BF16), w1_ref[j, half + l],
                            preferred_element_type=F32)
                ha[g] = a if ha[g] is None else ha[g] + a
                hb[g] = b if hb[g] is None else hb[g] + b
        return [_silu(ha[g] + pltpu.roll(hb[g], NC - 1, 0)) for g in range(KV_GROUPS)]

    hk = hidden(k_ref, 0)
    hv = hidden(v_ref, 1)
    for g in range(KV_GROUPS):
        k2 = jnp.dot(hk[g].astype(BF16), w2k_ref[...], preferred_element_type=F32)
        kc = k2[:, :HEAD_DIM] * cos_ref[...] + k2[:, HEAD_DIM:] * sin_ref[...]
        kc_ref[0, g] = kc.astype(BF16)
        vc = jnp.dot(hv[g].astype(BF16), w2v_ref[...],
                     preferred_element_type=F32)
        vct_ref[0, g] = jnp.transpose(vc)[:HEAD_DIM].astype(BF16)


def _compress(kcmp, vcmp, pe, w1, w2k, w2v, cos_c, sin_c):
    B, S, W = kcmp.shape
    NC = S // CMP_STRIDE
    const = _whole
    blk = pl.BlockSpec((1, S, W), lambda b: (b, 0, 0))
    return pl.pallas_call(
        _compress_kernel,
        grid=(B,),
        in_specs=[blk, blk, const(pe), const(w1), const(w2k), const(w2v),
                  const(cos_c), const(sin_c)],
        out_specs=[pl.BlockSpec((1, KV_GROUPS, NC, HEAD_DIM), lambda b: (b, 0, 0, 0)),
                   pl.BlockSpec((1, KV_GROUPS, HEAD_DIM, NC), lambda b: (b, 0, 0, 0))],
        out_shape=[jax.ShapeDtypeStruct((B, KV_GROUPS, NC, HEAD_DIM), BF16),
                   jax.ShapeDtypeStruct((B, KV_GROUPS, HEAD_DIM, NC), BF16)],
        compiler_params=pltpu.CompilerParams(
            dimension_semantics=("parallel",), vmem_limit_bytes=VMEM_LIMIT),
        name="compress",
    )(kcmp, vcmp, *map(_arr, (pe, w1, w2k, w2v, cos_c, sin_c)))


def _fold_rows(x, op):
    parts = [x[8 * i:8 * (i + 1)] for i in range(x.shape[0] // 8)]
    while len(parts) > 1:
        parts = [op(parts[i], parts[i + 1]) for i in range(0, len(parts), 2)]
    return parts[0]


def _attn_kernel(qt_ref, kselx_ref, vselt_ref, kwinx_ref, vwint_ref, kc_ref, vct_ref, gt_ref,
                 ga_ref, mb_ref, h_ref, gate_ref, ovt_ref, bias_ref, pa_ref, wo_ref,
                 o_ref, qx_scr, s_scr, m_scr, a_scr, acc_scr, yt_scr):
    TQ = h_ref.shape[1]
    CK = ATT_CHUNK
    NC = kc_ref.shape[2]
    NSEL = kselx_ref.shape[2] // SEL_BLOCK
    nwin = WINDOW // CK
    qi = pl.program_id(1)

    def branch(kx_ref, vt_ref, first, far, gate_row):
        def score(c, masked, dst):
            kx = [kx_ref[0, g, pl.ds(pl.multiple_of(c * CK, CK), CK), :]
                  for g in range(KV_GROUPS)]
            if masked:
                bias = bias_ref[jnp.where(c == qi, 1, jnp.where(c == far, 2, 0))]
            for hd in range(NSA_HEADS):
                s = jnp.dot(kx[hd // HPG], qx_scr[hd], preferred_element_type=F32)
                if masked:
                    s = s + bias
                s_scr[dst, hd] = s
                m_prev = m_scr[1 - dst, hd]
                m_new = jnp.maximum(m_prev, jnp.max(_fold_rows(s, jnp.maximum), axis=0,
                                                    keepdims=True))
                a_scr[dst, hd] = jnp.exp2(m_prev - m_new)
                m_scr[dst, hd] = m_new

        def weigh(c, src):
            vt = [vt_ref[0, g, c] for g in range(KV_GROUPS)]
            for hd in range(NSA_HEADS):
                p = jnp.exp2(s_scr[src, hd] - m_scr[src, hd]).astype(BF16)
                acc_scr[hd] = a_scr[src, hd] * acc_scr[hd] + jnp.dot(
                    vt[hd // HPG], p, preferred_element_type=F32)

        def loop(lo, hi, body):
            def wrapped(c, carry):
                body(c)
                return carry
            lax.fori_loop(lo, hi, wrapped, 0)

        def two_chunks(i):
            c = first + 2 * i
            score(c + 1, False, 1)
            weigh(c, 0)
            score(c + 2, True, 0)
            weigh(c + 1, 1)

        def last_two(c):
            score(c + 1, True, 1)
            weigh(c, 0)

        m_scr[...] = jnp.full_like(m_scr, NEG)
        acc_scr[...] = jnp.zeros_like(acc_scr)
        loop(first, first + 1, lambda c: score(c, True, 0))
        rest = qi - first
        pairs = jnp.right_shift(rest, 1)
        odd = rest & 1
        loop(0, pairs, two_chunks)
        loop(qi - 1, qi - 1 + odd, last_two)
        loop(qi, qi + odd, lambda c: weigh(c, 1))
        loop(qi, qi + 1 - odd, lambda c: weigh(c, 0))
        for hd in range(NSA_HEADS):
            w = gt_ref[0, 3 * hd + gate_row:3 * hd + gate_row + 1, :] / acc_scr[
                hd, HEAD_DIM:HEAD_DIM + 1, :]
            rows = slice(hd * HEAD_DIM, (hd + 1) * HEAD_DIM)
            yt_scr[rows, :] = yt_scr[rows, :] + w * acc_scr[hd, :HEAD_DIM, :]

    n_idx = lax.broadcasted_iota(jnp.int32, (NC, TQ), 0)
    t_cmp = qi * TQ + lax.broadcasted_iota(jnp.int32, (NC, TQ), 1)
    vis = (n_idx * CMP_STRIDE + (CMP_BLOCK - 1)) <= t_cmp
    visf = jnp.where(vis, 1.0, 0.0)
    j_idx = lax.broadcasted_iota(jnp.int32, (NSEL, TQ), 0)
    t_sel = qi * TQ + lax.broadcasted_iota(jnp.int32, (NSEL, TQ), 1)
    cur = jnp.right_shift(t_sel, SEL_BLOCK.bit_length() - 1)
    forced = (j_idx == 0) | (j_idx == cur) | (j_idx == cur - 1)
    valid = j_idx <= cur

    heads = range(NSA_HEADS)
    m_c = []
    for hd in heads:
        sc = jnp.where(vis, jnp.dot(kc_ref[0, hd // HPG], qt_ref[0, hd],
                                    preferred_element_type=F32), NEG)
        s_scr[0, hd, :NC, :] = sc
        m_c.append(jnp.max(_fold_rows(sc, jnp.maximum), axis=0, keepdims=True))
    inv = []
    for hd in heads:
        ec = jnp.exp2(s_scr[0, hd, :NC, :] - m_c[hd]) * visf
        s_scr[0, hd, :NC, :] = ec
        den = jnp.sum(_fold_rows(ec, jnp.add), axis=0, keepdims=True)
        inv.append(1.0 / jnp.where(den > 0.0, den, 1.0))
    psum = [jnp.zeros((NC, TQ), F32) for _ in range(KV_GROUPS)]
    for hd in heads:
        g = hd // HPG
        pc = s_scr[0, hd, :NC, :] * inv[hd]
        psum[g] = psum[g] + pc
        oc = jnp.dot(vct_ref[0, g], pc.astype(BF16), preferred_element_type=F32)
        yt_scr[hd * HEAD_DIM:(hd + 1) * HEAD_DIM, :] = gt_ref[0, 3 * hd:3 * hd + 1, :] * oc

    row8 = lax.broadcasted_iota(jnp.int32, (8, TQ), 0)
    pad = jnp.zeros((LANE - HEAD_DIM - NSEL, TQ), BF16)
    for g in range(KV_GROUPS):
        imp = jnp.dot(ovt_ref[...], psum[g], preferred_element_type=F32,
                      precision=lax.Precision.HIGHEST)
        imp = jnp.where(forced, FORCE, jnp.where(valid, imp, -FORCE))
        tiles = [imp[8 * r:8 * (r + 1)] for r in range(NSEL // 8)]
        rank = [jnp.zeros((8, TQ), jnp.int32) for _ in tiles]
        for kk in range(NSEL):
            rk = imp[kk:kk + 1, :]
            for r, tile in enumerate(tiles):
                if 8 * r > kk:
                    ahead = rk >= tile
                elif 8 * r + 7 < kk:
                    ahead = rk > tile
                else:
                    ahead = (rk > tile) | ((rk == tile) & (row8 > kk - 8 * r))
                rank[r] = rank[r] + jnp.where(ahead, 1, 0)
        selneg = jnp.where(jnp.concatenate(rank, axis=0) < SEL_TOPK, 0.0, NEG).astype(BF16)
        for hh in range(HPG):
            hd = g * HPG + hh
            qx_scr[hd] = jnp.concatenate([qt_ref[0, hd], selneg, pad], axis=0)

    branch(kselx_ref, vselt_ref, 0, -1, 1)
    branch(kwinx_ref, vwint_ref, jnp.maximum(qi - nwin, 0), qi - nwin, 2)

    y = jnp.transpose(yt_scr[...]).astype(BF16)
    ya = jnp.dot(y, pa_ref[...], preferred_element_type=F32)
    merged = (ga_ref[0] * ya + mb_ref[0]).astype(BF16)
    o_ref[0] = h_ref[0] + gate_ref[0] * jnp.dot(merged, wo_ref[...],
                                                preferred_element_type=F32)


def _attn(qt, kselx, vselt, kwinx, vwint, kc, vct, gt, ga, mb, h, mods, l, ovt, bias,
          proj_a, w_out):
    B, S, D = h.shape
    TQ = ATT_ROWS
    CK = ATT_CHUNK
    assert TQ == CK and WINDOW % CK == 0 and S % TQ == 0
    const = _whole

    def per_batch(a):
        nd = a.ndim
        return pl.BlockSpec((1,) + a.shape[1:], lambda b, i: (b,) + (0,) * (nd - 1))

    row = lambda w: pl.BlockSpec((1, TQ, w), lambda b, i: (b, i, 0))
    return pl.pallas_call(
        _attn_kernel,
        grid=(B, S // TQ),
        in_specs=[
            pl.BlockSpec((1, NSA_HEADS, HEAD_DIM, TQ), lambda b, i: (b, 0, 0, i)),
            per_batch(kselx), per_batch(vselt), per_batch(kwinx), per_batch(vwint),
            per_batch(kc), per_batch(vct),
            pl.BlockSpec((1, GATE_ROWS, TQ), lambda b, i: (b, 0, i)),
            row(D), row(D), row(D), _mod_vec(mods, l, 1, 2),
            const(ovt), const(bias), const(proj_a), const(w_out)],
        out_specs=row(D),
        out_shape=jax.ShapeDtypeStruct((B, S, D), F32),
        scratch_shapes=[
            pltpu.VMEM((NSA_HEADS, LANE, TQ), BF16),
            pltpu.VMEM((2, NSA_HEADS, CK, TQ), F32),
            pltpu.VMEM((2, NSA_HEADS, 1, TQ), F32), pltpu.VMEM((2, NSA_HEADS, 1, TQ), F32),
            pltpu.VMEM((NSA_HEADS, V_ROWS, TQ), F32),
            pltpu.VMEM((NSA_WIDTH, TQ), F32)],
        compiler_params=pltpu.CompilerParams(
            dimension_semantics=("parallel", "parallel"), vmem_limit_bytes=VMEM_LIMIT),
        name="attn",
    )(qt, kselx, vselt, kwinx, vwint, kc, vct, gt, ga, mb, h, mods, ovt, bias,
      _arr(proj_a), _arr(w_out))


def _rope_tables(pos):
    inv = 1.0 / (ROPE_THETA ** (np.arange(0, HEAD_DIM, 2, dtype=np.float64) / HEAD_DIM))
    ang = np.asarray(pos, np.float64)[:, None] * inv[None, :]
    cos = np.concatenate([np.cos(ang), np.cos(ang)], axis=1)
    sin = np.concatenate([-np.sin(ang), np.sin(ang)], axis=1)
    return cos.astype(np.float32), sin.astype(np.float32)


def _tables(S):
    cos, sin = _rope_tables(np.arange(S))
    cos_t = np.concatenate([cos, cos], axis=1)
    sin_t = np.concatenate([sin, sin], axis=1)
    n_cmp_pad = S // CMP_STRIDE
    starts = np.arange(n_cmp_pad) * CMP_STRIDE
    cos_c, sin_c = _rope_tables(starts + CMP_BLOCK - 1)
    n_sel = S // SEL_BLOCK
    sel_start = np.arange(n_sel) * SEL_BLOCK
    overlap = np.clip(np.minimum(starts[:, None] + CMP_BLOCK, sel_start[None, :] + SEL_BLOCK)
                      - np.maximum(starts[:, None], sel_start[None, :]), 0, None) / CMP_BLOCK
    ovt = np.ascontiguousarray(overlap.T).astype(np.float32)
    oh = np.zeros((S, LANE), np.float32)
    oh[np.arange(S), HEAD_DIM + np.arange(S) // SEL_BLOCK] = 1.0
    j = np.arange(ATT_CHUNK)[:, None]
    i = np.arange(ATT_ROWS)[None, :]
    bias = np.stack([np.zeros((ATT_CHUNK, ATT_ROWS)),
                     np.where(j <= i, 0.0, NEG),
                     np.where(j > i, 0.0, NEG)]
                    ).astype(np.float32)
    return tuple(jnp.asarray(a) for a in (cos_t, sin_t, cos_c, sin_c, ovt, oh, bias))


def kernel(x, c, ada_w, ada_b, norm_g, ffn_w_in, ffn_w_out, mix_w_in, cmp_pe, cmp_w1, cmp_w2,
           gm_ln_g, gm_ln_b, gm_ws, gm_bs, proj_a, proj_b, w_out, final_g):
    B, S, D = x.shape
    L = ada_w.shape[0]
    assert S // SEL_BLOCK + HEAD_DIM <= LANE and 3 * NSA_HEADS <= GATE_ROWS
    cos_t, sin_t, cos_c, sin_c, ovt, oh_t, bias = _tables(S)
    swap = np.concatenate([np.arange(HEAD_DIM // 2, HEAD_DIM), np.arange(HEAD_DIM // 2)])

    w_fi, w_fo = ffn_w_in.astype(BF16), ffn_w_out.astype(BF16)
    w_mix = mix_w_in.astype(BF16)
    w_pa, w_pb, w_o = proj_a.astype(BF16), proj_b.astype(BF16), w_out.astype(BF16)
    w1 = cmp_w1.astype(BF16)
    pe = cmp_pe[:, :, :, None, :]
    w2k = jnp.concatenate([cmp_w2[:, 0], cmp_w2[:, 0][..., swap]], axis=-1).astype(BF16)
    w2v = jnp.pad(cmp_w2[:, 1], ((0, 0), (0, 0), (0, LANE - HEAD_DIM))).astype(BF16)
    bsx = jnp.repeat(jnp.swapaxes(gm_bs, 1, 2), HEAD_DIM, axis=2)

    gains = norm_g[:, :, None, :]
    ln_g, ln_b = gm_ln_g[:, None, :], gm_ln_b[:, None, :]

    mods = _ada(c, ada_w, ada_b).reshape(L, B, 3, 3, 1, D)
    h = x
    for l in range(L):
        h = _ffn(h, mods, l, 0, _Pick(gains, (l, 0)), _Pick(w_fi, (l, 0)), _Pick(w_fo, (l, 0)))
        qt, kcmp, vcmp, kselx, vselt, kwinx, vwint, gt, ga, mb = _mix_in(
            h, mods, l, _Pick(gains, (l, 1)), _Pick(w_mix, (l,)), cos_t, sin_t, oh_t,
            _Pick(ln_g, (l,)), _Pick(ln_b, (l,)), _Pick(gm_ws, (l,)), _Pick(bsx, (l,)),
            _Pick(w_pb, (l,)))
        kc, vct = _compress(kcmp, vcmp, _Pick(pe, (l,)), _Pick(w1, (l,)), _Pick(w2k, (l,)),
                            _Pick(w2v, (l,)), cos_c, sin_c)
        h = _attn(qt, kselx, vselt, kwinx, vwint, kc, vct, gt, ga, mb, h, mods, l, ovt, bias,
                  _Pick(w_pa, (l,)), _Pick(w_o, (l,)))
        h = _ffn(h, mods, l, 2, _Pick(gains, (l, 2)), _Pick(w_fi, (l, 1)), _Pick(w_fo, (l, 1)),
                 final_g=final_g[None] if l == L - 1 else None)
    return h
```

```python
import functools
from typing import NamedTuple

import numpy as np
import jax
import jax.numpy as jnp
from jax import lax
from jax.experimental import pallas as pl
from jax.experimental.pallas import tpu as pltpu

F32 = jnp.float32
BF16 = jnp.bfloat16

HEAD_DIM = 64
NSA_HEADS = 8
KV_GROUPS = 2
HPG = NSA_HEADS // KV_GROUPS
NSA_WIDTH = NSA_HEADS * HEAD_DIM
CMP_BLOCK = 32
CMP_STRIDE = 16
CMP_HIDDEN = 128
SEL_BLOCK = 64
SEL_TOPK = 16
WINDOW = 512
GM_GROUPS = 8
GM_CHUNK = 128
GM_WIDTH = GM_GROUPS * HEAD_DIM
ROPE_THETA = 10000.0
EPS = 1e-6
NEG = -1e30
FORCE = 1e4
LANE = 128
GELU_C = float(np.sqrt(2.0 / np.pi))
LOG2E = float(np.log2(np.e))
GATE_ROWS = 32
V_ROWS = HEAD_DIM + 16

FFN_ROWS = 512
FFN_SPLIT = 11
MIX_ROWS = 512
ATT_ROWS = 256
ATT_CHUNK = 256
VMEM_LIMIT = 56 * 1024 * 1024


def _sigmoid(x):
    return 1.0 / (1.0 + jnp.exp(-x))


def _silu(x):
    return x * _sigmoid(x)


def _rms_mod(x, g, shift, scale):
    y = x * lax.rsqrt(jnp.mean(x * x, axis=-1, keepdims=True) + EPS) * g
    return y * (1.0 + scale) + shift


class _Pick(NamedTuple):
    array: jax.Array
    idx: tuple


def _arr(x):
    return x.array if isinstance(x, _Pick) else x


def _whole(x, single_buffer=False):
    idx = x.idx if isinstance(x, _Pick) else ()
    shape = _arr(x).shape
    mode = dict(pipeline_mode=pl.Buffered(1)) if single_buffer else {}
    return pl.BlockSpec((None,) * len(idx) + shape[len(idx):],
                        lambda *_: idx + (0,) * (len(shape) - len(idx)), **mode)


def _mod_vec(mods, l, sub, k):
    return pl.BlockSpec((None, 1, None, None, 1, mods.shape[-1]),
                        lambda b, *_: (l, b, sub, k, 0, 0))


def _ada_kernel(c_ref, w_ref, b_ref, o_ref):
    s = _silu(c_ref[...])
    o_ref[0] = jnp.dot(s, w_ref[0], preferred_element_type=F32,
                       precision=lax.Precision.HIGHEST) + b_ref[0]


def _ada(c, ada_w, ada_b):
    L, D, N = ada_w.shape
    B = c.shape[0]
    tn = 1024
    return pl.pallas_call(
        _ada_kernel,
        grid=(L, N // tn),
        in_specs=[
            pl.BlockSpec((B, D), lambda l, j: (0, 0)),
            pl.BlockSpec((1, D, tn), lambda l, j: (l, 0, j)),
            pl.BlockSpec((1, 1, tn), lambda l, j: (l, 0, j)),
        ],
        out_specs=pl.BlockSpec((1, B, tn), lambda l, j: (l, 0, j)),
        out_shape=jax.ShapeDtypeStruct((L, B, N), F32),
        compiler_params=pltpu.CompilerParams(
            dimension_semantics=("parallel", "parallel"), vmem_limit_bytes=VMEM_LIMIT),
        name="ada",
    )(c, ada_w, ada_b.reshape(L, 1, N))


def _ffn_kernel(h_ref, shift_ref, scale_ref, gate_ref, g_ref, wi_ref, wo_ref, *rest,
                final_norm):
    fg_ref = rest[0] if final_norm else None
    o_ref = rest[-1]
    F = wo_ref.shape[0]
    tf = F // FFN_SPLIT
    x = h_ref[0]
    n = _rms_mod(x, g_ref[...], shift_ref[0], scale_ref[0]).astype(BF16)
    acc = None
    for j in range(FFN_SPLIT):
        a = jnp.dot(n, wi_ref[:, j * tf:(j + 1) * tf], preferred_element_type=F32)
        b = jnp.dot(n, wi_ref[:, F + j * tf:F + (j + 1) * tf], preferred_element_type=F32)
        hm = (_silu(a) * b).astype(BF16)
        part = jnp.dot(hm, wo_ref[j * tf:(j + 1) * tf, :], preferred_element_type=F32)
        acc = part if acc is None else acc + part
    y = x + (0.5 * gate_ref[0]) * acc
    if final_norm:
        y = y * lax.rsqrt(jnp.mean(y * y, axis=-1, keepdims=True) + EPS) * fg_ref[...]
    o_ref[0] = y


def _ffn(h, mods, l, sub, g, w_in, w_out, final_g=None):
    B, S, D = h.shape
    tm = FFN_ROWS
    row = pl.BlockSpec((1, tm, D), lambda b, i: (b, i, 0))
    extra = () if final_g is None else (final_g,)
    return pl.pallas_call(
        functools.partial(_ffn_kernel, final_norm=final_g is not None),
        grid=(B, S // tm),
        in_specs=[row] + [_mod_vec(mods, l, sub, k) for k in range(3)]
        + [_whole(g), _whole(w_in, True), _whole(w_out, True)] + [_whole(e) for e in extra],
        out_specs=row,
        out_shape=jax.ShapeDtypeStruct((B, S, D), F32),
        compiler_params=pltpu.CompilerParams(
            dimension_semantics=("parallel", "parallel"), vmem_limit_bytes=VMEM_LIMIT),
        name="ffn",
    )(h, mods, mods, mods, _arr(g), _arr(w_in), _arr(w_out), *extra)


def _rope_pair(x, c, s):
    lane = lax.broadcasted_iota(jnp.int32, x.shape, 1)
    first_half = (lane & (HEAD_DIM - 1)) < HEAD_DIM // 2
    swapped = jnp.where(first_half, pltpu.roll(x, LANE - HEAD_DIM // 2, 1),
                        pltpu.roll(x, HEAD_DIM // 2, 1))
    return x * c + swapped * s


def _mix_cols(D):
    kv0 = NSA_WIDTH
    g0 = kv0 + 6 * KV_GROUPS * HEAD_DIM
    tail0 = g0 + 3 * NSA_HEADS
    edges = np.cumsum((0, 2 * GM_WIDTH, D, D))
    tail = [slice(int(a), int(b)) for a, b in zip(edges[:-1], edges[1:])]
    return slice(0, kv0), slice(kv0, g0), slice(g0, g0 + LANE), tail0, tail


def _mix_in_kernel(h_ref, shift_ref, scale_ref, g_ref, w_ref, cos_ref, sin_ref, oh_ref,
                   lng_ref, lnb_ref, ws_ref, bsx_ref, pb_ref,
                   qt_ref, kcmp_ref, vcmp_ref, kselx_ref, vselt_ref, kwinx_ref, vwint_ref,
                   gt_ref, ga_ref, mb_ref, wt_scr):
    tm = h_ref.shape[1]
    CK = vselt_ref.shape[4]
    c_q, c_kv, c_g, tail0, (c_uv, c_ga, c_gb) = _mix_cols(h_ref.shape[2])

    @pl.when((pl.program_id(0) == 0) & (pl.program_id(1) == 0))
    def _():
        wt_scr[...] = w_ref[:, tail0:tail0 + wt_scr.shape[1]]

    n = _rms_mod(h_ref[0], g_ref[...], shift_ref[0], scale_ref[0]).astype(BF16)
    cos = cos_ref[...]
    sin = sin_ref[...]
    low_lanes = lax.broadcasted_iota(jnp.int32, (tm, LANE), 1) < HEAD_DIM

    q = jnp.dot(n, w_ref[:, c_q], preferred_element_type=F32) * (HEAD_DIM ** -0.5 * LOG2E)
    for p in range(NSA_HEADS // 2):
        qp = jnp.transpose(_rope_pair(q[:, p * LANE:(p + 1) * LANE], cos, sin)).astype(BF16)
        qt_ref[0, 2 * p] = qp[:HEAD_DIM]
        qt_ref[0, 2 * p + 1] = qp[HEAD_DIM:]

    kv = jnp.dot(n, w_ref[:, c_kv], preferred_element_type=F32)
    kcmp_ref[0] = kv[:, 0 * LANE:1 * LANE]
    vcmp_ref[0] = kv[:, 1 * LANE:2 * LANE]
    for idx, ref, ext in ((2, kselx_ref, oh_ref[...]), (4, kwinx_ref, 0.0)):
        k2 = _rope_pair(kv[:, idx * LANE:(idx + 1) * LANE], cos, sin)
        ref[0, 0] = jnp.where(low_lanes, k2, ext).astype(BF16)
        ref[0, 1] = jnp.where(low_lanes, pltpu.roll(k2, HEAD_DIM, 1), ext).astype(BF16)
    ones_row = jnp.where(lax.broadcasted_iota(jnp.int32, (V_ROWS - HEAD_DIM, CK), 0) == 0,
                         1.0, 0.0).astype(BF16)
    for idx, ref in ((3, vselt_ref), (5, vwint_ref)):
        vt = jnp.transpose(kv[:, idx * LANE:(idx + 1) * LANE]).astype(BF16)
        for g in range(KV_GROUPS):
            for r in range(tm // CK):
                ref[0, g, r] = jnp.concatenate(
                    [vt[g * HEAD_DIM:(g + 1) * HEAD_DIM, r * CK:(r + 1) * CK], ones_row], axis=0)

    gates = _sigmoid(jnp.dot(n, w_ref[:, c_g], preferred_element_type=F32))
    gt_ref[0] = jnp.transpose(gates)[:GATE_ROWS]
    ga_ref[0] = _sigmoid(jnp.dot(n, wt_scr[:, c_ga], preferred_element_type=F32))

    uv = jnp.dot(n, wt_scr[:, c_uv], preferred_element_type=F32)
    ge = uv * (0.5 * (1.0 + jnp.tanh(GELU_C * (uv + 0.044715 * (uv * uv * uv)))))
    u = ge[:, :GM_WIDTH]
    v = ge[:, GM_WIDTH:]
    mu = jnp.mean(v, axis=-1, keepdims=True)
    var = jnp.mean(jnp.square(v - mu), axis=-1, keepdims=True)
    vln = ((v - mu) * lax.rsqrt(var + EPS) * lng_ref[...] + lnb_ref[...]).astype(BF16)

    ti = lax.broadcasted_iota(jnp.int32, (GM_CHUNK, GM_CHUNK), 0)
    si = lax.broadcasted_iota(jnp.int32, (GM_CHUNK, GM_CHUNK), 1)
    tril = si <= ti
    wm = [jnp.where(tril, ws_ref[gg], 0.0).astype(BF16) for gg in range(GM_GROUPS)]
    low = lax.broadcasted_iota(jnp.int32, (GM_CHUNK, LANE), 1) < HEAD_DIM
    bsx = bsx_ref[...]
    yb_rows = []
    for r in range(tm // GM_CHUNK):
        vch = vln[r * GM_CHUNK:(r + 1) * GM_CHUNK]
        pieces = []
        for p in range(GM_GROUPS // 2):
            vp = vch[:, p * LANE:(p + 1) * LANE]
            a0 = jnp.dot(wm[2 * p], vp, preferred_element_type=F32)
            a1 = jnp.dot(wm[2 * p + 1], vp, preferred_element_type=F32)
            pieces.append(jnp.where(low, a0, a1))
        sv = jnp.concatenate(pieces, axis=1) + bsx
        yb_rows.append(u[r * GM_CHUNK:(r + 1) * GM_CHUNK] * sv)
    yb = jnp.concatenate(yb_rows, axis=0).astype(BF16)
    gb = _sigmoid(jnp.dot(n, wt_scr[:, c_gb], preferred_element_type=F32))
    mb_ref[0] = gb * jnp.dot(yb, pb_ref[...], preferred_element_type=F32)


def _mix_in(h, mods, l, g, w, cos_t, sin_t, oh_t, ln_g, ln_b, ws, bsx, proj_b):
    B, S, D = h.shape
    tail0, tail = _mix_cols(D)[3:]
    assert _arr(w).shape[-2:] == (D, tail0 + tail[-1].stop)
    tm = MIX_ROWS
    G = KV_GROUPS
    CK = ATT_CHUNK
    const = _whole
    row = lambda w: pl.BlockSpec((1, tm, w), lambda b, i: (b, i, 0))
    tab = pl.BlockSpec((tm, LANE), lambda b, i: (i, 0))
    kspec = pl.BlockSpec((1, G, tm, LANE), lambda b, i: (b, 0, i, 0))
    kshape = jax.ShapeDtypeStruct((B, G, S, LANE), BF16)
    vspec = pl.BlockSpec((1, G, tm // CK, V_ROWS, CK), lambda b, i: (b, 0, i, 0, 0))
    vshape = jax.ShapeDtypeStruct((B, G, S // CK, V_ROWS, CK), BF16)
    return pl.pallas_call(
        _mix_in_kernel,
        grid=(B, S // tm),
        in_specs=[row(D), _mod_vec(mods, l, 1, 0), _mod_vec(mods, l, 1, 1), const(g),
                  _whole(w, True), tab, tab, tab, const(ln_g), const(ln_b), const(ws),
                  const(bsx), const(proj_b)],
        out_specs=[
            pl.BlockSpec((1, NSA_HEADS, HEAD_DIM, tm), lambda b, i: (b, 0, 0, i)),
            row(LANE), row(LANE), kspec, vspec, kspec, vspec,
            pl.BlockSpec((1, GATE_ROWS, tm), lambda b, i: (b, 0, i)),
            row(D), row(D)],
        out_shape=[
            jax.ShapeDtypeStruct((B, NSA_HEADS, HEAD_DIM, S), BF16),
            jax.ShapeDtypeStruct((B, S, LANE), F32), jax.ShapeDtypeStruct((B, S, LANE), F32),
            kshape, vshape, kshape, vshape,
            jax.ShapeDtypeStruct((B, GATE_ROWS, S), F32),
            jax.ShapeDtypeStruct((B, S, D), F32), jax.ShapeDtypeStruct((B, S, D), F32)],
        scratch_shapes=[pltpu.VMEM((D, tail[-1].stop), BF16)],
        compiler_params=pltpu.CompilerParams(
            dimension_semantics=("arbitrary", "arbitrary"), vmem_limit_bytes=VMEM_LIMIT),
        name="mix_in",
    )(h, mods, mods, *map(_arr, (g, w, cos_t, sin_t, oh_t, ln_g, ln_b, ws, bsx, proj_b)))


def _compress_kernel(k_ref, v_ref, pe_ref, w1_ref, w2k_ref, w2v_ref, cos_ref, sin_ref,
                     kc_ref, vct_ref):
    NC = kc_ref.shape[2]
    half = CMP_BLOCK // 2

    def hidden(x_ref, j):
        ha = [None] * KV_GROUPS
        hb = [None] * KV_GROUPS
        for l in range(half):
            x = x_ref[0, pl.ds(l, NC, stride=CMP_STRIDE), :]
            for g in range(KV_GROUPS):
                xg = x[:, g * HEAD_DIM:(g + 1) * HEAD_DIM]
                a = jnp.dot((xg + pe_ref[j, l]).astype(BF16), w1_ref[j, l],
                            preferred_element_type=F32)
                b = jnp.dot((xg + pe_ref[j, half + l]).astype(BF16), w1_ref[j, half + l],
                            preferred_element_type=F32)
                ha[g] = a if ha[g] is None else ha[g] + a
                hb[g] = b if hb[g] is None else hb[g] + b
        return [_silu(ha[g] + pltpu.roll(hb[g], NC - 1, 0)) for g in range(KV_GROUPS)]

    hk = hidden(k_ref, 0)
    hv = hidden(v_ref, 1)
    for g in range(KV_GROUPS):
        k2 = jnp.dot(hk[g].astype(BF16), w2k_ref[...], preferred_element_type=F32)
        kc = k2[:, :HEAD_DIM] * cos_ref[...] + k2[:, HEAD_DIM:] * sin_ref[...]
        kc_ref[0, g] = kc.astype(BF16)
        vc = jnp.dot(hv[g].astype(BF16), w2v_ref[...],
                     preferred_element_type=F32)
        vct_ref[0, g] = jnp.transpose(vc)[:HEAD_DIM].astype(BF16)


def _compress(kcmp, vcmp, pe, w1, w2k, w2v, cos_c, sin_c):
    B, S, W = kcmp.shape
    NC = S // CMP_STRIDE
    const = _whole
    blk = pl.BlockSpec((1, S, W), lambda b: (b, 0, 0))
    return pl.pallas_call(
        _compress_kernel,
        grid=(B,),
        in_specs=[blk, blk, const(pe), const(w1), const(w2k), const(w2v),
                  const(cos_c), const(sin_c)],
        out_specs=[pl.BlockSpec((1, KV_GROUPS, NC, HEAD_DIM), lambda b: (b, 0, 0, 0)),
                   pl.BlockSpec((1, KV_GROUPS, HEAD_DIM, NC), lambda b: (b, 0, 0, 0))],
        out_shape=[jax.ShapeDtypeStruct((B, KV_GROUPS, NC, HEAD_DIM), BF16),
                   jax.ShapeDtypeStruct((B, KV_GROUPS, HEAD_DIM, NC), BF16)],
        compiler_params=pltpu.CompilerParams(
            dimension_semantics=("parallel",), vmem_limit_bytes=VMEM_LIMIT),
        name="compress",
    )(kcmp, vcmp, *map(_arr, (pe, w1, w2k, w2v, cos_c, sin_c)))


def _fold_rows(x, op):
    parts = [x[8 * i:8 * (i + 1)] for i in range(x.shape[0] // 8)]
    while len(parts) > 1:
        parts = [op(parts[i], parts[i + 1]) for i in range(0, len(parts), 2)]
    return parts[0]


def _attn_kernel(qt_ref, kselx_ref, vselt_ref, kwinx_ref, vwint_ref, kc_ref, vct_ref, gt_ref,
                 ga_ref, mb_ref, h_ref, gate_ref, ovt_ref, bias_ref, pa_ref, wo_ref,
                 o_ref, qx_scr, s_scr, m_scr, a_scr, acc_scr, yt_scr):
    TQ = h_ref.shape[1]
    CK = ATT_CHUNK
    NC = kc_ref.shape[2]
    NSEL = kselx_ref.shape[2] // SEL_BLOCK
    nwin = WINDOW // CK
    qi = pl.program_id(1)

    def branch(kx_ref, vt_ref, first, far, gate_row):
        def score(c, masked, dst):
            kx = [kx_ref[0, g, pl.ds(pl.multiple_of(c * CK, CK), CK), :]
                  for g in range(KV_GROUPS)]
            if masked:
                bias = bias_ref[jnp.where(c == qi, 1, jnp.where(c == far, 2, 0))]
            for hd in range(NSA_HEADS):
                s = jnp.dot(kx[hd // HPG], qx_scr[hd], preferred_element_type=F32)
                if masked:
                    s = s + bias
                s_scr[dst, hd] = s
                m_prev = m_scr[1 - dst, hd]
                m_new = jnp.maximum(m_prev, jnp.max(_fold_rows(s, jnp.maximum), axis=0,
                                                    keepdims=True))
                a_scr[dst, hd] = jnp.exp2(m_prev - m_new)
                m_scr[dst, hd] = m_new

        def weigh(c, src):
            vt = [vt_ref[0, g, c] for g in range(KV_GROUPS)]
            for hd in range(NSA_HEADS):
                p = jnp.exp2(s_scr[src, hd] - m_scr[src, hd]).astype(BF16)
                acc_scr[hd] = a_scr[src, hd] * acc_scr[hd] + jnp.dot(
                    vt[hd // HPG], p, preferred_element_type=F32)

        def loop(lo, hi, body):
            def wrapped(c, carry):
                body(c)
                return carry
            lax.fori_loop(lo, hi, wrapped, 0)

        def two_chunks(i):
            c = first + 2 * i
            score(c + 1, False, 1)
            weigh(c, 0)
            score(c + 2, True, 0)
            weigh(c + 1, 1)

        def last_two(c):
            score(c + 1, True, 1)
            weigh(c, 0)

        m_scr[...] = jnp.full_like(m_scr, NEG)
        acc_scr[...] = jnp.zeros_like(acc_scr)
        loop(first, first + 1, lambda c: score(c, True, 0))
        rest = qi - first
        pairs = jnp.right_shift(rest, 1)
        odd = rest & 1
        loop(0, pairs, two_chunks)
        loop(qi - 1, qi - 1 + odd, last_two)
        loop(qi, qi + odd, lambda c: weigh(c, 1))
        loop(qi, qi + 1 - odd, lambda c: weigh(c, 0))
        for hd in range(NSA_HEADS):
            w = gt_ref[0, 3 * hd + gate_row:3 * hd + gate_row + 1, :] / acc_scr[
                hd, HEAD_DIM:HEAD_DIM + 1, :]
            rows = slice(hd * HEAD_DIM, (hd + 1) * HEAD_DIM)
            yt_scr[rows, :] = yt_scr[rows, :] + w * acc_scr[hd, :HEAD_DIM, :]

    n_idx = lax.broadcasted_iota(jnp.int32, (NC, TQ), 0)
    t_cmp = qi * TQ + lax.broadcasted_iota(jnp.int32, (NC, TQ), 1)
    vis = (n_idx * CMP_STRIDE + (CMP_BLOCK - 1)) <= t_cmp
    visf = jnp.where(vis, 1.0, 0.0)
    j_idx = lax.broadcasted_iota(jnp.int32, (NSEL, TQ), 0)
    t_sel = qi * TQ + lax.broadcasted_iota(jnp.int32, (NSEL, TQ), 1)
    cur = jnp.right_shift(t_sel, SEL_BLOCK.bit_length() - 1)
    forced = (j_idx == 0) | (j_idx == cur) | (j_idx == cur - 1)
    valid = j_idx <= cur

    heads = range(NSA_HEADS)
    m_c = []
    for hd in heads:
        sc = jnp.where(vis, jnp.dot(kc_ref[0, hd // HPG], qt_ref[0, hd],
                                    preferred_element_type=F32), NEG)
        s_scr[0, hd, :NC, :] = sc
        m_c.append(jnp.max(_fold_rows(sc, jnp.maximum), axis=0, keepdims=True))
    inv = []
    for hd in heads:
        ec = jnp.exp2(s_scr[0, hd, :NC, :] - m_c[hd]) * visf
        s_scr[0, hd, :NC, :] = ec
        den = jnp.sum(_fold_rows(ec, jnp.add), axis=0, keepdims=True)
        inv.append(1.0 / jnp.where(den > 0.0, den, 1.0))
    psum = [jnp.zeros((NC, TQ), F32) for _ in range(KV_GROUPS)]
    for hd in heads:
        g = hd // HPG
        pc = s_scr[0, hd, :NC, :] * inv[hd]
        psum[g] = psum[g] + pc
        oc = jnp.dot(vct_ref[0, g], pc.astype(BF16), preferred_element_type=F32)
        yt_scr[hd * HEAD_DIM:(hd + 1) * HEAD_DIM, :] = gt_ref[0, 3 * hd:3 * hd + 1, :] * oc

    row8 = lax.broadcasted_iota(jnp.int32, (8, TQ), 0)
    pad = jnp.zeros((LANE - HEAD_DIM - NSEL, TQ), BF16)
    for g in range(KV_GROUPS):
        imp = jnp.dot(ovt_ref[...], psum[g], preferred_element_type=F32,
                      precision=lax.Precision.HIGHEST)
        imp = jnp.where(forced, FORCE, jnp.where(valid, imp, -FORCE))
        tiles = [imp[8 * r:8 * (r + 1)] for r in range(NSEL // 8)]
        rank = [jnp.zeros((8, TQ), jnp.int32) for _ in tiles]
        for kk in range(NSEL):
            rk = imp[kk:kk + 1, :]
            for r, tile in enumerate(tiles):
                if 8 * r > kk:
                    ahead = rk >= tile
                elif 8 * r + 7 < kk:
                    ahead = rk > tile
                else:
                    ahead = (rk > tile) | ((rk == tile) & (row8 > kk - 8 * r))
                rank[r] = rank[r] + jnp.where(ahead, 1, 0)
        selneg = jnp.where(jnp.concatenate(rank, axis=0) < SEL_TOPK, 0.0, NEG).astype(BF16)
        for hh in range(HPG):
            hd = g * HPG + hh
            qx_scr[hd] = jnp.concatenate([qt_ref[0, hd], selneg, pad], axis=0)

    branch(kselx_ref, vselt_ref, 0, -1, 1)
    branch(kwinx_ref, vwint_ref, jnp.maximum(qi - nwin, 0), qi - nwin, 2)

    y = jnp.transpose(yt_scr[...]).astype(BF16)
    ya = jnp.dot(y, pa_ref[...], preferred_element_type=F32)
    merged = (ga_ref[0] * ya + mb_ref[0]).astype(BF16)
    o_ref[0] = h_ref[0] + gate_ref[0] * jnp.dot(merged, wo_ref[...],
                                                preferred_element_type=F32)


def _attn(qt, kselx, vselt, kwinx, vwint, kc, vct, gt, ga, mb, h, mods, l, ovt, bias,
          proj_a, w_out):
    B, S, D = h.shape
    TQ = ATT_ROWS
    CK = ATT_CHUNK
    assert TQ == CK and WINDOW % CK == 0 and S % TQ == 0
    const = _whole

    def per_batch(a):
        nd = a.ndim
        return pl.BlockSpec((1,) + a.shape[1:], lambda b, i: (b,) + (0,) * (nd - 1))

    row = lambda w: pl.BlockSpec((1, TQ, w), lambda b, i: (b, i, 0))
    return pl.pallas_call(
        _attn_kernel,
        grid=(B, S // TQ),
        in_specs=[
            pl.BlockSpec((1, NSA_HEADS, HEAD_DIM, TQ), lambda b, i: (b, 0, 0, i)),
            per_batch(kselx), per_batch(vselt), per_batch(kwinx), per_batch(vwint),
            per_batch(kc), per_batch(vct),
            pl.BlockSpec((1, GATE_ROWS, TQ), lambda b, i: (b, 0, i)),
            row(D), row(D), row(D), _mod_vec(mods, l, 1, 2),
            const(ovt), const(bias), const(proj_a), const(w_out)],
        out_specs=row(D),
        out_shape=jax.ShapeDtypeStruct((B, S, D), F32),
        scratch_shapes=[
            pltpu.VMEM((NSA_HEADS, LANE, TQ), BF16),
            pltpu.VMEM((2, NSA_HEADS, CK, TQ), F32),
            pltpu.VMEM((2, NSA_HEADS, 1, TQ), F32), pltpu.VMEM((2, NSA_HEADS, 1, TQ), F32),
            pltpu.VMEM((NSA_HEADS, V_ROWS, TQ), F32),
            pltpu.VMEM((NSA_WIDTH, TQ), F32)],
        compiler_params=pltpu.CompilerParams(
            dimension_semantics=("parallel", "parallel"), vmem_limit_bytes=VMEM_LIMIT),
        name="attn",
    )(qt, kselx, vselt, kwinx, vwint, kc, vct, gt, ga, mb, h, mods, ovt, bias,
      _arr(proj_a), _arr(w_out))


def _rope_tables(pos):
    inv = 1.0 / (ROPE_THETA ** (np.arange(0, HEAD_DIM, 2, dtype=np.float64) / HEAD_DIM))
    ang = np.asarray(pos, np.float64)[:, None] * inv[None, :]
    cos = np.concatenate([np.cos(ang), np.cos(ang)], axis=1)
    sin = np.concatenate([-np.sin(ang), np.sin(ang)], axis=1)
    return cos.astype(np.float32), sin.astype(np.float32)


def _tables(S):
    cos, sin = _rope_tables(np.arange(S))
    cos_t = np.concatenate([cos, cos], axis=1)
    sin_t = np.concatenate([sin, sin], axis=1)
    n_cmp_pad = S // CMP_STRIDE
    starts = np.arange(n_cmp_pad) * CMP_STRIDE
    cos_c, sin_c = _rope_tables(starts + CMP_BLOCK - 1)
    n_sel = S // SEL_BLOCK
    sel_start = np.arange(n_sel) * SEL_BLOCK
    overlap = np.clip(np.minimum(starts[:, None] + CMP_BLOCK, sel_start[None, :] + SEL_BLOCK)
                      - np.maximum(starts[:, None], sel_start[None, :]), 0, None) / CMP_BLOCK
    ovt = np.ascontiguousarray(overlap.T).astype(np.float32)
    oh = np.zeros((S, LANE), np.float32)
    oh[np.arange(S), HEAD_DIM + np.arange(S) // SEL_BLOCK] = 1.0
    j = np.arange(ATT_CHUNK)[:, None]
    i = np.arange(ATT_ROWS)[None, :]
    bias = np.stack([np.zeros((ATT_CHUNK, ATT_ROWS)),
                     np.where(j <= i, 0.0, NEG),
                     np.where(j > i, 0.0, NEG)]
                    ).astype(np.float32)
    return tuple(jnp.asarray(a) for a in (cos_t, sin_t, cos_c, sin_c, ovt, oh, bias))


def kernel(x, c, ada_w, ada_b, norm_g, ffn_w_in, ffn_w_out, mix_w_in, cmp_pe, cmp_w1, cmp_w2,
           gm_ln_g, gm_ln_b, gm_ws, gm_bs, proj_a, proj_b, w_out, final_g):
    B, S, D = x.shape
    L = ada_w.shape[0]
    assert S // SEL_BLOCK + HEAD_DIM <= LANE and 3 * NSA_HEADS <= GATE_ROWS
    cos_t, sin_t, cos_c, sin_c, ovt, oh_t, bias = _tables(S)
    swap = np.concatenate([np.arange(HEAD_DIM // 2, HEAD_DIM), np.arange(HEAD_DIM // 2)])

    w_fi, w_fo = ffn_w_in.astype(BF16), ffn_w_out.astype(BF16)
    w_mix = mix_w_in.astype(BF16)
    w_pa, w_pb, w_o = proj_a.astype(BF16), proj_b.astype(BF16), w_out.astype(BF16)
    w1 = cmp_w1.astype(BF16)
    pe = cmp_pe[:, :, :, None, :]
    w2k = jnp.concatenate([cmp_w2[:, 0], cmp_w2[:, 0][..., swap]], axis=-1).astype(BF16)
    w2v = jnp.pad(cmp_w2[:, 1], ((0, 0), (0, 0), (0, LANE - HEAD_DIM))).astype(BF16)
    bsx = jnp.repeat(jnp.swapaxes(gm_bs, 1, 2), HEAD_DIM, axis=2)

    gains = norm_g[:, :, None, :]
    ln_g, ln_b = gm_ln_g[:, None, :], gm_ln_b[:, None, :]

    mods = _ada(c, ada_w, ada_b).reshape(L, B, 3, 3, 1, D)
    h = x
    for l in range(L):
        h = _ffn(h, mods, l, 0, _Pick(gains, (l, 0)), _Pick(w_fi, (l, 0)), _Pick(w_fo, (l, 0)))
        qt, kcmp, vcmp, kselx, vselt, kwinx, vwint, gt, ga, mb = _mix_in(
            h, mods, l, _Pick(gains, (l, 1)), _Pick(w_mix, (l,)), cos_t, sin_t, oh_t,
            _Pick(ln_g, (l,)), _Pick(ln_b, (l,)), _Pick(gm_ws, (l,)), _Pick(bsx, (l,)),
            _Pick(w_pb, (l,)))
        kc, vct = _compress(kcmp, vcmp, _Pick(pe, (l,)), _Pick(w1, (l,)), _Pick(w2k, (l,)),
                            _Pick(w2v, (l,)), cos_c, sin_c)
        h = _attn(qt, kselx, vselt, kwinx, vwint, kc, vct, gt, ga, mb, h, mods, l, ovt, bias,
                  _Pick(w_pa, (l,)), _Pick(w_o, (l,)))
        h = _ffn(h, mods, l, 2, _Pick(gains, (l, 2)), _Pick(w_fi, (l, 1)), _Pick(w_fo, (l, 1)),
                 final_g=final_g[None] if l == L - 1 else None)
    return h
```

```python
import functools
from typing import NamedTuple

import numpy as np
import jax
import jax.numpy as jnp
from jax import lax
from jax.experimental import pallas as pl
from jax.experimental.pallas import tpu as pltpu

F32 = jnp.float32
BF16 = jnp.bfloat16

HEAD_DIM = 64
NSA_HEADS = 8
KV_GROUPS = 2
HPG = NSA_HEADS // KV_GROUPS
NSA_WIDTH = NSA_HEADS * HEAD_DIM
CMP_BLOCK = 32
CMP_STRIDE = 16
CMP_HIDDEN = 128
SEL_BLOCK = 64
SEL_TOPK = 16
WINDOW = 512
GM_GROUPS = 8
GM_CHUNK = 128
GM_WIDTH = GM_GROUPS * HEAD_DIM
ROPE_THETA = 10000.0
EPS = 1e-6
NEG = -1e30
FORCE = 1e4
LANE = 128
GELU_C = float(np.sqrt(2.0 / np.pi))
LOG2E = float(np.log2(np.e))
GATE_ROWS = 32
V_ROWS = HEAD_DIM + 16

FFN_ROWS = 512
FFN_SPLIT = 11
MIX_ROWS = 512
ATT_ROWS = 256
ATT_CHUNK = 256
VMEM_LIMIT = 56 * 1024 * 1024


def _sigmoid(x):
    return 1.0 / (1.0 + jnp.exp(-x))


def _silu(x):
    return x * _sigmoid(x)


def _rms_mod(x, g, shift, scale):
    y = x * lax.rsqrt(jnp.mean(x * x, axis=-1, keepdims=True) + EPS) * g
    return y * (1.0 + scale) + shift


class _Pick(NamedTuple):
    array: jax.Array
    idx: tuple


def _arr(x):
    return x.array if isinstance(x, _Pick) else x


def _whole(x, single_buffer=False):
    idx = x.idx if isinstance(x, _Pick) else ()
    shape = _arr(x).shape
    mode = dict(pipeline_mode=pl.Buffered(1)) if single_buffer else {}
    return pl.BlockSpec((None,) * len(idx) + shape[len(idx):],
                        lambda *_: idx + (0,) * (len(shape) - len(idx)), **mode)


def _mod_vec(mods, l, sub, k):
    return pl.BlockSpec((None, 1, None, None, 1, mods.shape[-1]),
                        lambda b, *_: (l, b, sub, k, 0, 0))


def _ada_kernel(c_ref, w_ref, b_ref, o_ref):
    s = _silu(c_ref[...])
    w = w_ref[0]
    s_hi = s.astype(BF16)
    s_lo = (s - s_hi.astype(F32)).astype(BF16)
    w_hi = w.astype(BF16)
    w_lo = (w - w_hi.astype(F32)).astype(BF16)
    nb = s.shape[0]
    both = jnp.dot(jnp.concatenate([s_hi, s_lo], axis=0), w_hi, preferred_element_type=F32)
    o_ref[0] = (both[:nb] + both[nb:] + jnp.dot(s_hi, w_lo, preferred_element_type=F32)
                + b_ref[0])


def _ada(c, ada_w, ada_b):
    L, D, N = ada_w.shape
    B = c.shape[0]
    tn = 1024
    return pl.pallas_call(
        _ada_kernel,
        grid=(L, N // tn),
        in_specs=[
            pl.BlockSpec((B, D), lambda l, j: (0, 0)),
            pl.BlockSpec((1, D, tn), lambda l, j: (l, 0, j)),
            pl.BlockSpec((1, 1, tn), lambda l, j: (l, 0, j)),
        ],
        out_specs=pl.BlockSpec((1, B, tn), lambda l, j: (l, 0, j)),
        out_shape=jax.ShapeDtypeStruct((L, B, N), F32),
        compiler_params=pltpu.CompilerParams(
            dimension_semantics=("parallel", "parallel"), vmem_limit_bytes=VMEM_LIMIT),
        name="ada",
    )(c, ada_w, ada_b.reshape(L, 1, N))


def _ffn_kernel(h_ref, shift_ref, scale_ref, gate_ref, g_ref, wi_ref, wo_ref, *rest,
                final_norm):
    fg_ref = rest[0] if final_norm else None
    o_ref = rest[-1]
    F = wo_ref.shape[0]
    tf = F // FFN_SPLIT
    x = h_ref[0]
    n = _rms_mod(x, g_ref[...], shift_ref[0], scale_ref[0]).astype(BF16)
    acc = None
    for j in range(FFN_SPLIT):
        a = jnp.dot(n, wi_ref[:, j * tf:(j + 1) * tf], preferred_element_type=F32)
        b = jnp.dot(n, wi_ref[:, F + j * tf:F + (j + 1) * tf], preferred_element_type=F32)
        hm = (_silu(a) * b).astype(BF16)
        part = jnp.dot(hm, wo_ref[j * tf:(j + 1) * tf, :], preferred_element_type=F32)
        acc = part if acc is None else acc + part
    y = x + (0.5 * gate_ref[0]) * acc
    if final_norm:
        y = y * lax.rsqrt(jnp.mean(y * y, axis=-1, keepdims=True) + EPS) * fg_ref[...]
    o_ref[0] = y


def _ffn(h, mods, l, sub, g, w_in, w_out, final_g=None):
    B, S, D = h.shape
    tm = FFN_ROWS
    row = pl.BlockSpec((1, tm, D), lambda b, i: (b, i, 0))
    extra = () if final_g is None else (final_g,)
    return pl.pallas_call(
        functools.partial(_ffn_kernel, final_norm=final_g is not None),
        grid=(B, S // tm),
        in_specs=[row] + [_mod_vec(mods, l, sub, k) for k in range(3)]
        + [_whole(g), _whole(w_in, True), _whole(w_out, True)] + [_whole(e) for e in extra],
        out_specs=row,
        out_shape=jax.ShapeDtypeStruct((B, S, D), F32),
        compiler_params=pltpu.CompilerParams(
            dimension_semantics=("parallel", "parallel"), vmem_limit_bytes=VMEM_LIMIT),
        name="ffn",
    )(h, mods, mods, mods, _arr(g), _arr(w_in), _arr(w_out), *extra)


def _rope_pair(x, c, s):
    lane = lax.broadcasted_iota(jnp.int32, x.shape, 1)
    first_half = (lane & (HEAD_DIM - 1)) < HEAD_DIM // 2
    swapped = jnp.where(first_half, pltpu.roll(x, LANE - HEAD_DIM // 2, 1),
                        pltpu.roll(x, HEAD_DIM // 2, 1))
    return x * c + swapped * s


def _mix_cols(D):
    kv0 = NSA_WIDTH
    g0 = kv0 + 6 * KV_GROUPS * HEAD_DIM
    tail0 = g0 + 3 * NSA_HEADS
    edges = np.cumsum((0, 2 * GM_WIDTH, D, D))
    tail = [slice(int(a), int(b)) for a, b in zip(edges[:-1], edges[1:])]
    return slice(0, kv0), slice(kv0, g0), slice(g0, g0 + LANE), tail0, tail


def _mix_in_kernel(h_ref, shift_ref, scale_ref, g_ref, w_ref, cos_ref, sin_ref, oh_ref,
                   lng_ref, lnb_ref, ws_ref, bsx_ref, pb_ref,
                   qt_ref, kcmp_ref, vcmp_ref, kselx_ref, vselt_ref, kwinx_ref, vwint_ref,
                   gt_ref, ga_ref, mb_ref, wh_scr, wt_scr):
    tm = h_ref.shape[1]
    CK = vselt_ref.shape[4]
    c_q, c_kv, c_g, tail0, (c_uv, c_ga, c_gb) = _mix_cols(h_ref.shape[2])

    @pl.when((pl.program_id(0) == 0) & (pl.program_id(1) == 0))
    def _():
        wh_scr[...] = w_ref[:, :wh_scr.shape[1]].astype(BF16)
        wt_scr[...] = w_ref[:, tail0:tail0 + wt_scr.shape[1]].astype(BF16)

    n = _rms_mod(h_ref[0], g_ref[...], shift_ref[0], scale_ref[0]).astype(BF16)
    cos = cos_ref[...]
    sin = sin_ref[...]
    low_lanes = lax.broadcasted_iota(jnp.int32, (tm, LANE), 1) < HEAD_DIM

    q = jnp.dot(n, wh_scr[:, c_q], preferred_element_type=F32) * (HEAD_DIM ** -0.5 * LOG2E)
    for p in range(NSA_HEADS // 2):
        qp = jnp.transpose(_rope_pair(q[:, p * LANE:(p + 1) * LANE], cos, sin)).astype(BF16)
        qt_ref[0, 2 * p] = qp[:HEAD_DIM]
        qt_ref[0, 2 * p + 1] = qp[HEAD_DIM:]

    kv = jnp.dot(n, wh_scr[:, c_kv], preferred_element_type=F32)
    kcmp_ref[0] = kv[:, 0 * LANE:1 * LANE]
    vcmp_ref[0] = kv[:, 1 * LANE:2 * LANE]
    for idx, ref, ext in ((2, kselx_ref, oh_ref[...]), (4, kwinx_ref, 0.0)):
        k2 = _rope_pair(kv[:, idx * LANE:(idx + 1) * LANE], cos, sin)
        ref[0, 0] = jnp.where(low_lanes, k2, ext).astype(BF16)
        ref[0, 1] = jnp.where(low_lanes, pltpu.roll(k2, HEAD_DIM, 1), ext).astype(BF16)
    ones_row = jnp.where(lax.broadcasted_iota(jnp.int32, (V_ROWS - HEAD_DIM, CK), 0) == 0,
                         1.0, 0.0).astype(BF16)
    for idx, ref in ((3, vselt_ref), (5, vwint_ref)):
        vt = jnp.transpose(kv[:, idx * LANE:(idx + 1) * LANE]).astype(BF16)
        for g in range(KV_GROUPS):
            for r in range(tm // CK):
                ref[0, g, r] = jnp.concatenate(
                    [vt[g * HEAD_DIM:(g + 1) * HEAD_DIM, r * CK:(r + 1) * CK], ones_row], axis=0)

    gates = _sigmoid(jnp.dot(n, wh_scr[:, c_g], preferred_element_type=F32))
    gt_ref[0] = jnp.transpose(gates)[:GATE_ROWS]
    ga_ref[0] = _sigmoid(jnp.dot(n, wt_scr[:, c_ga], preferred_element_type=F32))

    uv = jnp.dot(n, wt_scr[:, c_uv], preferred_element_type=F32)
    ge = uv * (0.5 * (1.0 + jnp.tanh(GELU_C * (uv + 0.044715 * (uv * uv * uv)))))
    u = ge[:, :GM_WIDTH]
    v = ge[:, GM_WIDTH:]
    mu = jnp.mean(v, axis=-1, keepdims=True)
    var = jnp.mean(jnp.square(v - mu), axis=-1, keepdims=True)
    vln = ((v - mu) * lax.rsqrt(var + EPS) * lng_ref[...] + lnb_ref[...]).astype(BF16)

    ti = lax.broadcasted_iota(jnp.int32, (GM_CHUNK, GM_CHUNK), 0)
    si = lax.broadcasted_iota(jnp.int32, (GM_CHUNK, GM_CHUNK), 1)
    tril = si <= ti
    wm = [jnp.where(tril, ws_ref[gg], 0.0).astype(BF16) for gg in range(GM_GROUPS)]
    wpair = [jnp.concatenate(wm[2 * p:2 * p + 2], axis=1) for p in range(GM_GROUPS // 2)]
    low = lax.broadcasted_iota(jnp.int32, (GM_CHUNK, LANE), 1) < HEAD_DIM
    bsx = bsx_ref[...]
    yb_rows = []
    for r in range(tm // GM_CHUNK):
        vch = vln[r * GM_CHUNK:(r + 1) * GM_CHUNK]
        pieces = []
        for p in range(GM_GROUPS // 2):
            vp = vch[:, p * LANE:(p + 1) * LANE]
            zero = jnp.zeros_like(vp)
            stacked = jnp.concatenate([jnp.where(low, vp, zero), jnp.where(low, zero, vp)],
                                      axis=0)
            pieces.append(jnp.dot(wpair[p], stacked, preferred_element_type=F32))
        sv = jnp.concatenate(pieces, axis=1) + bsx
        yb_rows.append(u[r * GM_CHUNK:(r + 1) * GM_CHUNK] * sv)
    yb = jnp.concatenate(yb_rows, axis=0).astype(BF16)
    gb = _sigmoid(jnp.dot(n, wt_scr[:, c_gb], preferred_element_type=F32))
    mb_ref[0] = gb * jnp.dot(yb, pb_ref[...], preferred_element_type=F32)


def _mix_in(h, mods, l, g, w, cos_t, sin_t, oh_t, ln_g, ln_b, ws, bsx, proj_b):
    B, S, D = h.shape
    tail0, tail = _mix_cols(D)[3:]
    assert _arr(w).shape[-2:] == (D, tail0 + tail[-1].stop)
    tm = MIX_ROWS
    G = KV_GROUPS
    CK = ATT_CHUNK
    const = _whole
    row = lambda w: pl.BlockSpec((1, tm, w), lambda b, i: (b, i, 0))
    tab = pl.BlockSpec((tm, LANE), lambda b, i: (i, 0))
    kspec = pl.BlockSpec((1, G, tm, LANE), lambda b, i: (b, 0, i, 0))
    kshape = jax.ShapeDtypeStruct((B, G, S, LANE), BF16)
    vspec = pl.BlockSpec((1, G, tm // CK, V_ROWS, CK), lambda b, i: (b, 0, i, 0, 0))
    vshape = jax.ShapeDtypeStruct((B, G, S // CK, V_ROWS, CK), BF16)
    return pl.pallas_call(
        _mix_in_kernel,
        grid=(B, S // tm),
        in_specs=[row(D), _mod_vec(mods, l, 1, 0), _mod_vec(mods, l, 1, 1), const(g),
                  _whole(w, True), tab, tab, tab, const(ln_g), const(ln_b), const(ws),
                  const(bsx), const(proj_b)],
        out_specs=[
            pl.BlockSpec((1, NSA_HEADS, HEAD_DIM, tm), lambda b, i: (b, 0, 0, i)),
            row(LANE), row(LANE), kspec, vspec, kspec, vspec,
            pl.BlockSpec((1, GATE_ROWS, tm), lambda b, i: (b, 0, i)),
            row(D), row(D)],
        out_shape=[
            jax.ShapeDtypeStruct((B, NSA_HEADS, HEAD_DIM, S), BF16),
            jax.ShapeDtypeStruct((B, S, LANE), F32), jax.ShapeDtypeStruct((B, S, LANE), F32),
            kshape, vshape, kshape, vshape,
            jax.ShapeDtypeStruct((B, GATE_ROWS, S), F32),
            jax.ShapeDtypeStruct((B, S, D), F32), jax.ShapeDtypeStruct((B, S, D), F32)],
        scratch_shapes=[pltpu.VMEM((D, _mix_cols(D)[2].stop), BF16),
                        pltpu.VMEM((D, tail[-1].stop), BF16)],
        compiler_params=pltpu.CompilerParams(
            dimension_semantics=("arbitrary", "arbitrary"), vmem_limit_bytes=VMEM_LIMIT),
        name="mix_in",
    )(h, mods, mods, *map(_arr, (g, w, cos_t, sin_t, oh_t, ln_g, ln_b, ws, bsx, proj_b)))


def _compress_kernel(k_ref, v_ref, pe_ref, w1_ref, w2k_ref, w2v_ref, cos_ref, sin_ref,
                     kc_ref, vct_ref):
    NC = kc_ref.shape[2]
    half = CMP_BLOCK // 2

    def hidden(x_ref, j):
        ha = [None] * KV_GROUPS
        hb = [None] * KV_GROUPS
        for l in range(half):
            x = x_ref[0, pl.ds(l, NC, stride=CMP_STRIDE), :]
            for g in range(KV_GROUPS):
                xg = x[:, g * HEAD_DIM:(g + 1) * HEAD_DIM]
                a = jnp.dot((xg + pe_ref[j, l]).astype(BF16), w1_ref[j, l],
                            preferred_element_type=F32)
                b = jnp.dot((xg + pe_ref[j, half + l]).astype(BF16), w1_ref[j, half + l],
                            preferred_element_type=F32)
                ha[g] = a if ha[g] is None else ha[g] + a
                hb[g] = b if hb[g] is None else hb[g] + b
        return [_silu(ha[g] + pltpu.roll(hb[g], NC - 1, 0)) for g in range(KV_GROUPS)]

    hk = hidden(k_ref, 0)
    hv = hidden(v_ref, 1)
    for g in range(KV_GROUPS):
        k2 = jnp.dot(hk[g].astype(BF16), w2k_ref[...], preferred_element_type=F32)
        kc = k2[:, :HEAD_DIM] * cos_ref[...] + k2[:, HEAD_DIM:] * sin_ref[...]
        kc_ref[0, g] = kc.astype(BF16)
        vc = jnp.dot(hv[g].astype(BF16), w2v_ref[...],
                     preferred_element_type=F32)
        vct_ref[0, g] = jnp.transpose(vc)[:HEAD_DIM].astype(BF16)


def _compress(kcmp, vcmp, pe, w1, w2k, w2v, cos_c, sin_c):
    B, S, W = kcmp.shape
    NC = S // CMP_STRIDE
    const = _whole
    blk = pl.BlockSpec((1, S, W), lambda b: (b, 0, 0))
    return pl.pallas_call(
        _compress_kernel,
        grid=(B,),
        in_specs=[blk, blk, const(pe), const(w1), const(w2k), const(w2v),
                  const(cos_c), const(sin_c)],
        out_specs=[pl.BlockSpec((1, KV_GROUPS, NC, HEAD_DIM), lambda b: (b, 0, 0, 0)),
                   pl.BlockSpec((1, KV_GROUPS, HEAD_DIM, NC), lambda b: (b, 0, 0, 0))],
        out_shape=[jax.ShapeDtypeStruct((B, KV_GROUPS, NC, HEAD_DIM), BF16),
                   jax.ShapeDtypeStruct((B, KV_GROUPS, HEAD_DIM, NC), BF16)],
        compiler_params=pltpu.CompilerParams(
            dimension_semantics=("parallel",), vmem_limit_bytes=VMEM_LIMIT),
        name="compress",
    )(kcmp, vcmp, *map(_arr, (pe, w1, w2k, w2v, cos_c, sin_c)))


def _fold_rows(x, op):
    parts = [x[8 * i:8 * (i + 1)] for i in range(x.shape[0] // 8)]
    while len(parts) > 1:
        parts = [op(parts[i], parts[i + 1]) for i in range(0, len(parts), 2)]
    return parts[0]


def _attn_kernel(qt_ref, kselx_ref, vselt_ref, kwinx_ref, vwint_ref, kc_ref, vct_ref, gt_ref,
                 ga_ref, mb_ref, h_ref, gate_ref, ovt_ref, bias_ref, pa_ref, wo_ref,
                 o_ref, qx_scr, s_scr, m_scr, a_scr, acc_scr, yt_scr):
    TQ = h_ref.shape[1]
    CK = ATT_CHUNK
    NC = kc_ref.shape[2]
    NSEL = kselx_ref.shape[2] // SEL_BLOCK
    nwin = WINDOW // CK
    qi = pl.program_id(1)

    def branch(kx_ref, vt_ref, first, far, gate_row):
        def score(c, masked, dst):
            kx = [kx_ref[0, g, pl.ds(pl.multiple_of(c * CK, CK), CK), :]
                  for g in range(KV_GROUPS)]
            if masked:
                bias = bias_ref[jnp.where(c == qi, 1, jnp.where(c == far, 2, 0))]
            for hd in range(NSA_HEADS):
                s = jnp.dot(kx[hd // HPG], qx_scr[hd], preferred_element_type=F32)
                if masked:
                    s = s + bias
                s_scr[dst, hd] = s
                m_prev = m_scr[1 - dst, hd]
                m_new = jnp.maximum(m_prev, jnp.max(_fold_rows(s, jnp.maximum), axis=0,
                                                    keepdims=True))
                a_scr[dst, hd] = jnp.exp2(m_prev - m_new)
                m_scr[dst, hd] = m_new

        def weigh(c, src):
            vt = [vt_ref[0, g, c] for g in range(KV_GROUPS)]
            for hd in range(NSA_HEADS):
                p = jnp.exp2(s_scr[src, hd] - m_scr[src, hd]).astype(BF16)
                acc_scr[hd] = a_scr[src, hd] * acc_scr[hd] + jnp.dot(
                    vt[hd // HPG], p, preferred_element_type=F32)

        def loop(lo, hi, body):
            def wrapped(c, carry):
                body(c)
                return carry
            lax.fori_loop(lo, hi, wrapped, 0)

        def two_chunks(i):
            c = first + 2 * i
            score(c + 1, False, 1)
            weigh(c, 0)
            score(c + 2, True, 0)
            weigh(c + 1, 1)

        def last_two(c):
            score(c + 1, True, 1)
            weigh(c, 0)

        m_scr[...] = jnp.full_like(m_scr, NEG)
        acc_scr[...] = jnp.zeros_like(acc_scr)
        loop(first, first + 1, lambda c: score(c, True, 0))
        rest = qi - first
        pairs = jnp.right_shift(rest, 1)
        odd = rest & 1
        loop(0, pairs, two_chunks)
        loop(qi - 1, qi - 1 + odd, last_two)
        loop(qi, qi + odd, lambda c: weigh(c, 1))
        loop(qi, qi + 1 - odd, lambda c: weigh(c, 0))
        for hd in range(NSA_HEADS):
            w = gt_ref[0, 3 * hd + gate_row:3 * hd + gate_row + 1, :] / acc_scr[
                hd, HEAD_DIM:HEAD_DIM + 1, :]
            rows = slice(hd * HEAD_DIM, (hd + 1) * HEAD_DIM)
            yt_scr[rows, :] = yt_scr[rows, :] + w * acc_scr[hd, :HEAD_DIM, :]

    n_idx = lax.broadcasted_iota(jnp.int32, (NC, TQ), 0)
    t_cmp = qi * TQ + lax.broadcasted_iota(jnp.int32, (NC, TQ), 1)
    vis = (n_idx * CMP_STRIDE + (CMP_BLOCK - 1)) <= t_cmp
    visf = jnp.where(vis, 1.0, 0.0)
    j_idx = lax.broadcasted_iota(jnp.int32, (NSEL, TQ), 0)
    t_sel = qi * TQ + lax.broadcasted_iota(jnp.int32, (NSEL, TQ), 1)
    cur = jnp.right_shift(t_sel, SEL_BLOCK.bit_length() - 1)
    forced = (j_idx == 0) | (j_idx == cur) | (j_idx == cur - 1)
    valid = j_idx <= cur

    heads = range(NSA_HEADS)
    m_c = []
    for hd in heads:
        sc = jnp.where(vis, jnp.dot(kc_ref[0, hd // HPG], qt_ref[0, hd],
                                    preferred_element_type=F32), NEG)
        s_scr[0, hd, :NC, :] = sc
        m_c.append(jnp.max(_fold_rows(sc, jnp.maximum), axis=0, keepdims=True))
    inv = []
    for hd in heads:
        ec = jnp.exp2(s_scr[0, hd, :NC, :] - m_c[hd]) * visf
        s_scr[0, hd, :NC, :] = ec
        den = jnp.sum(_fold_rows(ec, jnp.add), axis=0, keepdims=True)
        inv.append(1.0 / jnp.where(den > 0.0, den, 1.0))
    psum = [jnp.zeros((NC, TQ), F32) for _ in range(KV_GROUPS)]
    for hd in heads:
        g = hd // HPG
        pc = s_scr[0, hd, :NC, :] * inv[hd]
        psum[g] = psum[g] + pc
        oc = jnp.dot(vct_ref[0, g], pc.astype(BF16), preferred_element_type=F32)
        yt_scr[hd * HEAD_DIM:(hd + 1) * HEAD_DIM, :] = gt_ref[0, 3 * hd:3 * hd + 1, :] * oc

    row8 = lax.broadcasted_iota(jnp.int32, (8, TQ), 0)
    pad = jnp.zeros((LANE - HEAD_DIM - NSEL, TQ), BF16)
    for g in range(KV_GROUPS):
        imp = jnp.dot(ovt_ref[...], psum[g], preferred_element_type=F32,
                      precision=lax.Precision.HIGHEST)
        imp = jnp.where(forced, FORCE, jnp.where(valid, imp, -FORCE))
        tiles = [imp[8 * r:8 * (r + 1)] for r in range(NSEL // 8)]
        rank = [jnp.zeros((8, TQ), jnp.int32) for _ in tiles]
        for kk in range(NSEL):
            rk = imp[kk:kk + 1, :]
            for r, tile in enumerate(tiles):
                if 8 * r > kk:
                    ahead = rk >= tile
                elif 8 * r + 7 < kk:
                    ahead = rk > tile
                else:
                    ahead = (rk > tile) | ((rk == tile) & (row8 > kk - 8 * r))
                rank[r] = rank[r] + jnp.where(ahead, 1, 0)
        selneg = jnp.where(jnp.concatenate(rank, axis=0) < SEL_TOPK, 0.0, NEG).astype(BF16)
        for hh in range(HPG):
            hd = g * HPG + hh
            qx_scr[hd] = jnp.concatenate([qt_ref[0, hd], selneg, pad], axis=0)

    branch(kselx_ref, vselt_ref, 0, -1, 1)
    branch(kwinx_ref, vwint_ref, jnp.maximum(qi - nwin, 0), qi - nwin, 2)

    y = jnp.transpose(yt_scr[...]).astype(BF16)
    ya = jnp.dot(y, pa_ref[...], preferred_element_type=F32)
    merged = (ga_ref[0] * ya + mb_ref[0]).astype(BF16)
    o_ref[0] = h_ref[0] + gate_ref[0] * jnp.dot(merged, wo_ref[...],
                                                preferred_element_type=F32)


def _attn(qt, kselx, vselt, kwinx, vwint, kc, vct, gt, ga, mb, h, mods, l, ovt, bias,
          proj_a, w_out):
    B, S, D = h.shape
    TQ = ATT_ROWS
    CK = ATT_CHUNK
    assert TQ == CK and WINDOW % CK == 0 and S % TQ == 0
    const = _whole

    def per_batch(a):
        nd = a.ndim
        return pl.BlockSpec((1,) + a.shape[1:], lambda b, i: (b,) + (0,) * (nd - 1))

    row = lambda w: pl.BlockSpec((1, TQ, w), lambda b, i: (b, i, 0))
    return pl.pallas_call(
        _attn_kernel,
        grid=(B, S // TQ),
        in_specs=[
            pl.BlockSpec((1, NSA_HEADS, HEAD_DIM, TQ), lambda b, i: (b, 0, 0, i)),
            per_batch(kselx), per_batch(vselt), per_batch(kwinx), per_batch(vwint),
            per_batch(kc), per_batch(vct),
            pl.BlockSpec((1, GATE_ROWS, TQ), lambda b, i: (b, 0, i)),
            row(D), row(D), row(D), _mod_vec(mods, l, 1, 2),
            const(ovt), const(bias), const(proj_a), const(w_out)],
        out_specs=row(D),
        out_shape=jax.ShapeDtypeStruct((B, S, D), F32),
        scratch_shapes=[
            pltpu.VMEM((NSA_HEADS, LANE, TQ), BF16),
            pltpu.VMEM((2, NSA_HEADS, CK, TQ), F32),
            pltpu.VMEM((2, NSA_HEADS, 1, TQ), F32), pltpu.VMEM((2, NSA_HEADS, 1, TQ), F32),
            pltpu.VMEM((NSA_HEADS, V_ROWS, TQ), F32),
            pltpu.VMEM((NSA_WIDTH, TQ), F32)],
        compiler_params=pltpu.CompilerParams(
            dimension_semantics=("parallel", "parallel"), vmem_limit_bytes=VMEM_LIMIT),
        name="attn",
    )(qt, kselx, vselt, kwinx, vwint, kc, vct, gt, ga, mb, h, mods, ovt, bias,
      _arr(proj_a), _arr(w_out))


def _rope_tables(pos):
    inv = 1.0 / (ROPE_THETA ** (np.arange(0, HEAD_DIM, 2, dtype=np.float64) / HEAD_DIM))
    ang = np.asarray(pos, np.float64)[:, None] * inv[None, :]
    cos = np.concatenate([np.cos(ang), np.cos(ang)], axis=1)
    sin = np.concatenate([-np.sin(ang), np.sin(ang)], axis=1)
    return cos.astype(np.float32), sin.astype(np.float32)


def _tables(S):
    cos, sin = _rope_tables(np.arange(S))
    cos_t = np.concatenate([cos, cos], axis=1)
    sin_t = np.concatenate([sin, sin], axis=1)
    n_cmp_pad = S // CMP_STRIDE
    starts = np.arange(n_cmp_pad) * CMP_STRIDE
    cos_c, sin_c = _rope_tables(starts + CMP_BLOCK - 1)
    n_sel = S // SEL_BLOCK
    sel_start = np.arange(n_sel) * SEL_BLOCK
    overlap = np.clip(np.minimum(starts[:, None] + CMP_BLOCK, sel_start[None, :] + SEL_BLOCK)
                      - np.maximum(starts[:, None], sel_start[None, :]), 0, None) / CMP_BLOCK
    ovt = np.ascontiguousarray(overlap.T).astype(np.float32)
    oh = np.zeros((S, LANE), np.float32)
    oh[np.arange(S), HEAD_DIM + np.arange(S) // SEL_BLOCK] = 1.0
    j = np.arange(ATT_CHUNK)[:, None]
    i = np.arange(ATT_ROWS)[None, :]
    bias = np.stack([np.zeros((ATT_CHUNK, ATT_ROWS)),
                     np.where(j <= i, 0.0, NEG),
                     np.where(j > i, 0.0, NEG)]
                    ).astype(np.float32)
    return tuple(jnp.asarray(a) for a in (cos_t, sin_t, cos_c, sin_c, ovt, oh, bias))


def kernel(x, c, ada_w, ada_b, norm_g, ffn_w_in, ffn_w_out, mix_w_in, cmp_pe, cmp_w1, cmp_w2,
           gm_ln_g, gm_ln_b, gm_ws, gm_bs, proj_a, proj_b, w_out, final_g):
    B, S, D = x.shape
    L = ada_w.shape[0]
    assert S // SEL_BLOCK + HEAD_DIM <= LANE and 3 * NSA_HEADS <= GATE_ROWS
    cos_t, sin_t, cos_c, sin_c, ovt, oh_t, bias = _tables(S)
    swap = np.concatenate([np.arange(HEAD_DIM // 2, HEAD_DIM), np.arange(HEAD_DIM // 2)])

    w_fi, w_fo = ffn_w_in.astype(BF16), ffn_w_out.astype(BF16)
    w_pa, w_pb, w_o = proj_a.astype(BF16), proj_b.astype(BF16), w_out.astype(BF16)
    w1 = cmp_w1.astype(BF16)
    pe = cmp_pe[:, :, :, None, :]
    w2k = jnp.concatenate([cmp_w2[:, 0], cmp_w2[:, 0][..., swap]], axis=-1).astype(BF16)
    w2v = jnp.pad(cmp_w2[:, 1], ((0, 0), (0, 0), (0, LANE - HEAD_DIM))).astype(BF16)
    bsx = jnp.repeat(jnp.swapaxes(gm_bs, 1, 2), HEAD_DIM, axis=2)

    gains = norm_g[:, :, None, :]
    ln_g, ln_b = gm_ln_g[:, None, :], gm_ln_b[:, None, :]

    mods = _ada(c, ada_w, ada_b).reshape(L, B, 3, 3, 1, D)
    h = x
    for l in range(L):
        h = _ffn(h, mods, l, 0, _Pick(gains, (l, 0)), _Pick(w_fi, (l, 0)), _Pick(w_fo, (l, 0)))
        qt, kcmp, vcmp, kselx, vselt, kwinx, vwint, gt, ga, mb = _mix_in(
            h, mods, l, _Pick(gains, (l, 1)), _Pick(mix_w_in, (l,)), cos_t, sin_t, oh_t,
            _Pick(ln_g, (l,)), _Pick(ln_b, (l,)), _Pick(gm_ws, (l,)), _Pick(bsx, (l,)),
            _Pick(w_pb, (l,)))
        kc, vct = _compress(kcmp, vcmp, _Pick(pe, (l,)), _Pick(w1, (l,)), _Pick(w2k, (l,)),
                            _Pick(w2v, (l,)), cos_c, sin_c)
        h = _attn(qt, kselx, vselt, kwinx, vwint, kc, vct, gt, ga, mb, h, mods, l, ovt, bias,
                  _Pick(w_pa, (l,)), _Pick(w_o, (l,)))
        h = _ffn(h, mods, l, 2, _Pick(gains, (l, 2)), _Pick(w_fi, (l, 1)), _Pick(w_fo, (l, 1)),
                 final_g=final_g[None] if l == L - 1 else None)
    return h
```

```python
import functools
from typing import NamedTuple

import numpy as np
import jax
import jax.numpy as jnp
from jax import lax
from jax.experimental import pallas as pl
from jax.experimental.pallas import tpu as pltpu

F32 = jnp.float32
BF16 = jnp.bfloat16

HEAD_DIM = 64
NSA_HEADS = 8
KV_GROUPS = 2
HPG = NSA_HEADS // KV_GROUPS
NSA_WIDTH = NSA_HEADS * HEAD_DIM
CMP_BLOCK = 32
CMP_STRIDE = 16
CMP_HIDDEN = 128
SEL_BLOCK = 64
SEL_TOPK = 16
WINDOW = 512
GM_GROUPS = 8
GM_CHUNK = 128
GM_WIDTH = GM_GROUPS * HEAD_DIM
ROPE_THETA = 10000.0
EPS = 1e-6
NEG = -1e30
FORCE = 1e4
LANE = 128
GELU_C = float(np.sqrt(2.0 / np.pi))
LOG2E = float(np.log2(np.e))
GATE_ROWS = 32
V_ROWS = HEAD_DIM + 16

FFN_ROWS = 512
FFN_SPLIT = 11
MIX_ROWS = 512
ATT_ROWS = 256
ATT_CHUNK = 256
VMEM_LIMIT = 56 * 1024 * 1024


def _sigmoid(x):
    return 1.0 / (1.0 + jnp.exp(-x))


def _silu(x):
    return x * _sigmoid(x)


def _rms_mod(x, g, shift, scale):
    y = x * lax.rsqrt(jnp.mean(x * x, axis=-1, keepdims=True) + EPS) * g
    return y * (1.0 + scale) + shift


class _Pick(NamedTuple):
    array: jax.Array
    idx: tuple


def _arr(x):
    return x.array if isinstance(x, _Pick) else x


def _whole(x, single_buffer=False):
    idx = x.idx if isinstance(x, _Pick) else ()
    shape = _arr(x).shape
    mode = dict(pipeline_mode=pl.Buffered(1)) if single_buffer else {}
    return pl.BlockSpec((None,) * len(idx) + shape[len(idx):],
                        lambda *_: idx + (0,) * (len(shape) - len(idx)), **mode)


def _mod_vec(mods, l, sub, k):
    return pl.BlockSpec((None, 1, None, None, 1, mods.shape[-1]),
                        lambda b, *_: (l, b, sub, k, 0, 0))


def _ada_kernel(c_ref, w_ref, b_ref, o_ref):
    s = _silu(c_ref[...])
    w = w_ref[0]
    s_hi = s.astype(BF16)
    s_lo = (s - s_hi.astype(F32)).astype(BF16)
    w_hi = w.astype(BF16)
    w_lo = (w - w_hi.astype(F32)).astype(BF16)
    nb = s.shape[0]
    both = jnp.dot(jnp.concatenate([s_hi, s_lo], axis=0), w_hi, preferred_element_type=F32)
    o_ref[0] = (both[:nb] + both[nb:] + jnp.dot(s_hi, w_lo, preferred_element_type=F32)
                + b_ref[0])


def _ada(c, ada_w, ada_b):
    L, D, N = ada_w.shape
    B = c.shape[0]
    tn = 1024
    return pl.pallas_call(
        _ada_kernel,
        grid=(L, N // tn),
        in_specs=[
            pl.BlockSpec((B, D), lambda l, j: (0, 0)),
            pl.BlockSpec((1, D, tn), lambda l, j: (l, 0, j)),
            pl.BlockSpec((1, 1, tn), lambda l, j: (l, 0, j)),
        ],
        out_specs=pl.BlockSpec((1, B, tn), lambda l, j: (l, 0, j)),
        out_shape=jax.ShapeDtypeStruct((L, B, N), F32),
        compiler_params=pltpu.CompilerParams(
            dimension_semantics=("parallel", "parallel"), vmem_limit_bytes=VMEM_LIMIT),
        name="ada",
    )(c, ada_w, ada_b.reshape(L, 1, N))


def _ffn_kernel(h_ref, shift_ref, scale_ref, gate_ref, g_ref, wi_ref, wo_ref, *rest,
                final_norm):
    fg_ref = rest[0] if final_norm else None
    o_ref = rest[-1]
    F = wo_ref.shape[0]
    tf = F // FFN_SPLIT
    x = h_ref[0]
    n = _rms_mod(x, g_ref[...], shift_ref[0], scale_ref[0]).astype(BF16)
    acc = None
    for j in range(FFN_SPLIT):
        a = jnp.dot(n, wi_ref[:, j * tf:(j + 1) * tf], preferred_element_type=F32)
        b = jnp.dot(n, wi_ref[:, F + j * tf:F + (j + 1) * tf], preferred_element_type=F32)
        hm = (_silu(a) * b).astype(BF16)
        part = jnp.dot(hm, wo_ref[j * tf:(j + 1) * tf, :], preferred_element_type=F32)
        acc = part if acc is None else acc + part
    y = x + (0.5 * gate_ref[0]) * acc
    if final_norm:
        y = y * lax.rsqrt(jnp.mean(y * y, axis=-1, keepdims=True) + EPS) * fg_ref[...]
    o_ref[0] = y


def _ffn(h, mods, l, sub, g, w_in, w_out, final_g=None):
    B, S, D = h.shape
    tm = FFN_ROWS
    row = pl.BlockSpec((1, tm, D), lambda b, i: (b, i, 0))
    extra = () if final_g is None else (final_g,)
    return pl.pallas_call(
        functools.partial(_ffn_kernel, final_norm=final_g is not None),
        grid=(B, S // tm),
        in_specs=[row] + [_mod_vec(mods, l, sub, k) for k in range(3)]
        + [_whole(g), _whole(w_in, True), _whole(w_out, True)] + [_whole(e) for e in extra],
        out_specs=row,
        out_shape=jax.ShapeDtypeStruct((B, S, D), F32),
        compiler_params=pltpu.CompilerParams(
            dimension_semantics=("parallel", "parallel"), vmem_limit_bytes=VMEM_LIMIT),
        name="ffn",
    )(h, mods, mods, mods, _arr(g), _arr(w_in), _arr(w_out), *extra)


def _rope_pair(x, c, s):
    lane = lax.broadcasted_iota(jnp.int32, x.shape, 1)
    first_half = (lane & (HEAD_DIM - 1)) < HEAD_DIM // 2
    swapped = jnp.where(first_half, pltpu.roll(x, LANE - HEAD_DIM // 2, 1),
                        pltpu.roll(x, HEAD_DIM // 2, 1))
    return x * c + swapped * s


def _mix_cols(D):
    kv0 = NSA_WIDTH
    g0 = kv0 + 6 * KV_GROUPS * HEAD_DIM
    tail0 = g0 + 3 * NSA_HEADS
    edges = np.cumsum((0, 2 * GM_WIDTH, D, D))
    tail = [slice(int(a), int(b)) for a, b in zip(edges[:-1], edges[1:])]
    return slice(0, kv0), slice(kv0, g0), slice(g0, g0 + LANE), tail0, tail


def _mix_in_kernel(h_ref, shift_ref, scale_ref, g_ref, w_ref, cos_ref, sin_ref, oh_ref,
                   lng_ref, lnb_ref, ws_ref, bsx_ref, pb_ref,
                   qt_ref, kcmp_ref, vcmp_ref, kselx_ref, vselt_ref, kwinx_ref, vwint_ref,
                   gt_ref, ga_ref, mb_ref, wt_scr):
    tm = h_ref.shape[1]
    CK = vselt_ref.shape[4]
    c_q, c_kv, c_g, tail0, (c_uv, c_ga, c_gb) = _mix_cols(h_ref.shape[2])

    @pl.when((pl.program_id(0) == 0) & (pl.program_id(1) == 0))
    def _():
        wt_scr[...] = w_ref[:, tail0:tail0 + wt_scr.shape[1]]

    n = _rms_mod(h_ref[0], g_ref[...], shift_ref[0], scale_ref[0]).astype(BF16)
    cos = cos_ref[...]
    sin = sin_ref[...]
    low_lanes = lax.broadcasted_iota(jnp.int32, (tm, LANE), 1) < HEAD_DIM

    q = jnp.dot(n, w_ref[:, c_q], preferred_element_type=F32) * (HEAD_DIM ** -0.5 * LOG2E)
    for p in range(NSA_HEADS // 2):
        qp = jnp.transpose(_rope_pair(q[:, p * LANE:(p + 1) * LANE], cos, sin)).astype(BF16)
        qt_ref[0, 2 * p] = qp[:HEAD_DIM]
        qt_ref[0, 2 * p + 1] = qp[HEAD_DIM:]

    kv = jnp.dot(n, w_ref[:, c_kv], preferred_element_type=F32)
    kcmp_ref[0] = kv[:, 0 * LANE:1 * LANE]
    vcmp_ref[0] = kv[:, 1 * LANE:2 * LANE]
    for idx, ref, ext in ((2, kselx_ref, oh_ref[...]), (4, kwinx_ref, 0.0)):
        k2 = _rope_pair(kv[:, idx * LANE:(idx + 1) * LANE], cos, sin)
        ref[0, 0] = jnp.where(low_lanes, k2, ext).astype(BF16)
        ref[0, 1] = jnp.where(low_lanes, pltpu.roll(k2, HEAD_DIM, 1), ext).astype(BF16)
    ones_row = jnp.where(lax.broadcasted_iota(jnp.int32, (V_ROWS - HEAD_DIM, CK), 0) == 0,
                         1.0, 0.0).astype(BF16)
    for idx, ref in ((3, vselt_ref), (5, vwint_ref)):
        vt = jnp.transpose(kv[:, idx * LANE:(idx + 1) * LANE]).astype(BF16)
        for g in range(KV_GROUPS):
            for r in range(tm // CK):
                ref[0, g, r] = jnp.concatenate(
                    [vt[g * HEAD_DIM:(g + 1) * HEAD_DIM, r * CK:(r + 1) * CK], ones_row], axis=0)

    gates = _sigmoid(jnp.dot(n, w_ref[:, c_g], preferred_element_type=F32))
    gt_ref[0] = jnp.transpose(gates)[:GATE_ROWS]
    ga_ref[0] = _sigmoid(jnp.dot(n, wt_scr[:, c_ga], preferred_element_type=F32))

    uv = jnp.dot(n, wt_scr[:, c_uv], preferred_element_type=F32)
    ge = uv * (0.5 * (1.0 + jnp.tanh(GELU_C * (uv + 0.044715 * (uv * uv * uv)))))
    u = ge[:, :GM_WIDTH]
    v = ge[:, GM_WIDTH:]
    mu = jnp.mean(v, axis=-1, keepdims=True)
    var = jnp.mean(jnp.square(v - mu), axis=-1, keepdims=True)
    vln = ((v - mu) * lax.rsqrt(var + EPS) * lng_ref[...] + lnb_ref[...]).astype(BF16)

    ti = lax.broadcasted_iota(jnp.int32, (GM_CHUNK, GM_CHUNK), 0)
    si = lax.broadcasted_iota(jnp.int32, (GM_CHUNK, GM_CHUNK), 1)
    tril = si <= ti
    wm = [jnp.where(tril, ws_ref[gg], 0.0).astype(BF16) for gg in range(GM_GROUPS)]
    wpair = [jnp.concatenate(wm[2 * p:2 * p + 2], axis=1) for p in range(GM_GROUPS // 2)]
    low = lax.broadcasted_iota(jnp.int32, (GM_CHUNK, LANE), 1) < HEAD_DIM
    bsx = bsx_ref[...]
    yb_rows = []
    for r in range(tm // GM_CHUNK):
        vch = vln[r * GM_CHUNK:(r + 1) * GM_CHUNK]
        pieces = []
        for p in range(GM_GROUPS // 2):
            vp = vch[:, p * LANE:(p + 1) * LANE]
            zero = jnp.zeros_like(vp)
            stacked = jnp.concatenate([jnp.where(low, vp, zero), jnp.where(low, zero, vp)],
                                      axis=0)
            pieces.append(jnp.dot(wpair[p], stacked, preferred_element_type=F32))
        sv = jnp.concatenate(pieces, axis=1) + bsx
        yb_rows.append(u[r * GM_CHUNK:(r + 1) * GM_CHUNK] * sv)
    yb = jnp.concatenate(yb_rows, axis=0).astype(BF16)
    gb = _sigmoid(jnp.dot(n, wt_scr[:, c_gb], preferred_element_type=F32))
    mb_ref[0] = gb * jnp.dot(yb, pb_ref[...], preferred_element_type=F32)


def _mix_in(h, mods, l, g, w, cos_t, sin_t, oh_t, ln_g, ln_b, ws, bsx, proj_b):
    B, S, D = h.shape
    tail0, tail = _mix_cols(D)[3:]
    assert _arr(w).shape[-2] == D and _arr(w).shape[-1] >= tail0 + tail[-1].stop
    tm = MIX_ROWS
    G = KV_GROUPS
    CK = ATT_CHUNK
    const = _whole
    row = lambda w: pl.BlockSpec((1, tm, w), lambda b, i: (b, i, 0))
    tab = pl.BlockSpec((tm, LANE), lambda b, i: (i, 0))
    kspec = pl.BlockSpec((1, G, tm, LANE), lambda b, i: (b, 0, i, 0))
    kshape = jax.ShapeDtypeStruct((B, G, S, LANE), BF16)
    vspec = pl.BlockSpec((1, G, tm // CK, V_ROWS, CK), lambda b, i: (b, 0, i, 0, 0))
    vshape = jax.ShapeDtypeStruct((B, G, S // CK, V_ROWS, CK), BF16)
    return pl.pallas_call(
        _mix_in_kernel,
        grid=(B, S // tm),
        in_specs=[row(D), _mod_vec(mods, l, 1, 0), _mod_vec(mods, l, 1, 1), const(g),
                  _whole(w, True), tab, tab, tab, const(ln_g), const(ln_b), const(ws),
                  const(bsx), const(proj_b)],
        out_specs=[
            pl.BlockSpec((1, NSA_HEADS, HEAD_DIM, tm), lambda b, i: (b, 0, 0, i)),
            row(LANE), row(LANE), kspec, vspec, kspec, vspec,
            pl.BlockSpec((1, GATE_ROWS, tm), lambda b, i: (b, 0, i)),
            row(D), row(D)],
        out_shape=[
            jax.ShapeDtypeStruct((B, NSA_HEADS, HEAD_DIM, S), BF16),
            jax.ShapeDtypeStruct((B, S, LANE), F32), jax.ShapeDtypeStruct((B, S, LANE), F32),
            kshape, vshape, kshape, vshape,
            jax.ShapeDtypeStruct((B, GATE_ROWS, S), F32),
            jax.ShapeDtypeStruct((B, S, D), F32), jax.ShapeDtypeStruct((B, S, D), F32)],
        scratch_shapes=[pltpu.VMEM((D, tail[-1].stop), BF16)],
        compiler_params=pltpu.CompilerParams(
            dimension_semantics=("arbitrary", "arbitrary"), vmem_limit_bytes=VMEM_LIMIT),
        name="mix_in",
    )(h, mods, mods, *map(_arr, (g, w, cos_t, sin_t, oh_t, ln_g, ln_b, ws, bsx, proj_b)))


def _compress_kernel(k_ref, v_ref, pe_ref, w1_ref, w2k_ref, w2v_ref, cos_ref, sin_ref,
                     kc_ref, vct_ref):
    NC = kc_ref.shape[2]
    half = CMP_BLOCK // 2

    def hidden(x_ref, j):
        ha = [None] * KV_GROUPS
        hb = [None] * KV_GROUPS
        for l in range(half):
            x = x_ref[0, pl.ds(l, NC, stride=CMP_STRIDE), :]
            for g in range(KV_GROUPS):
                xg = x[:, g * HEAD_DIM:(g + 1) * HEAD_DIM]
                a = jnp.dot((xg + pe_ref[j, l]).astype(BF16), w1_ref[j, l],
                            preferred_element_type=F32)
                b = jnp.dot((xg + pe_ref[j, half + l]).astype(BF16), w1_ref[j, half + l],
                            preferred_element_type=F32)
                ha[g] = a if ha[g] is None else ha[g] + a
                hb[g] = b if hb[g] is None else hb[g] + b
        return [_silu(ha[g] + pltpu.roll(hb[g], NC - 1, 0)) for g in range(KV_GROUPS)]

    hk = hidden(k_ref, 0)
    hv = hidden(v_ref, 1)
    for g in range(KV_GROUPS):
        k2 = jnp.dot(hk[g].astype(BF16), w2k_ref[...], preferred_element_type=F32)
        kc = k2[:, :HEAD_DIM] * cos_ref[...] + k2[:, HEAD_DIM:] * sin_ref[...]
        kc_ref[0, g] = kc.astype(BF16)
        vc = jnp.dot(hv[g].astype(BF16), w2v_ref[...],
                     preferred_element_type=F32)
        vct_ref[0, g] = jnp.transpose(vc)[:HEAD_DIM].astype(BF16)


def _compress(kcmp, vcmp, pe, w1, w2k, w2v, cos_c, sin_c):
    B, S, W = kcmp.shape
    NC = S // CMP_STRIDE
    const = _whole
    blk = pl.BlockSpec((1, S, W), lambda b: (b, 0, 0))
    return pl.pallas_call(
        _compress_kernel,
        grid=(B,),
        in_specs=[blk, blk, const(pe), const(w1), const(w2k), const(w2v),
                  const(cos_c), const(sin_c)],
        out_specs=[pl.BlockSpec((1, KV_GROUPS, NC, HEAD_DIM), lambda b: (b, 0, 0, 0)),
                   pl.BlockSpec((1, KV_GROUPS, HEAD_DIM, NC), lambda b: (b, 0, 0, 0))],
        out_shape=[jax.ShapeDtypeStruct((B, KV_GROUPS, NC, HEAD_DIM), BF16),
                   jax.ShapeDtypeStruct((B, KV_GROUPS, HEAD_DIM, NC), BF16)],
        compiler_params=pltpu.CompilerParams(
            dimension_semantics=("parallel",), vmem_limit_bytes=VMEM_LIMIT),
        name="compress",
    )(kcmp, vcmp, *map(_arr, (pe, w1, w2k, w2v, cos_c, sin_c)))


def _fold_rows(x, op):
    parts = [x[8 * i:8 * (i + 1)] for i in range(x.shape[0] // 8)]
    while len(parts) > 1:
        parts = [op(parts[i], parts[i + 1]) for i in range(0, len(parts), 2)]
    return parts[0]


def _attn_kernel(qt_ref, kselx_ref, vselt_ref, kwinx_ref, vwint_ref, kc_ref, vct_ref, gt_ref,
                 ga_ref, mb_ref, h_ref, gate_ref, ovt_ref, bias_ref, pa_ref, wo_ref,
                 o_ref, qx_scr, s_scr, m_scr, a_scr, acc_scr, yt_scr):
    TQ = h_ref.shape[1]
    CK = ATT_CHUNK
    NC = kc_ref.shape[2]
    NSEL = kselx_ref.shape[2] // SEL_BLOCK
    nwin = WINDOW // CK
    qi = pl.program_id(1)

    def branch(kx_ref, vt_ref, first, far, gate_row):
        def score(c, masked, dst):
            kx = [kx_ref[0, g, pl.ds(pl.multiple_of(c * CK, CK), CK), :]
                  for g in range(KV_GROUPS)]
            if masked:
                bias = bias_ref[jnp.where(c == qi, 1, jnp.where(c == far, 2, 0))]
            for hd in range(NSA_HEADS):
                s = jnp.dot(kx[hd // HPG], qx_scr[hd], preferred_element_type=F32)
                if masked:
                    s = s + bias
                s_scr[dst, hd] = s
                m_prev = m_scr[1 - dst, hd]
                m_new = jnp.maximum(m_prev, jnp.max(_fold_rows(s, jnp.maximum), axis=0,
                                                    keepdims=True))
                a_scr[dst, hd] = jnp.exp2(m_prev - m_new)
                m_scr[dst, hd] = m_new

        def weigh(c, src):
            vt = [vt_ref[0, g, c] for g in range(KV_GROUPS)]
            for hd in range(NSA_HEADS):
                p = jnp.exp2(s_scr[src, hd] - m_scr[src, hd]).astype(BF16)
                acc_scr[hd] = a_scr[src, hd] * acc_scr[hd] + jnp.dot(
                    vt[hd // HPG], p, preferred_element_type=F32)

        def loop(lo, hi, body):
            def wrapped(c, carry):
                body(c)
                return carry
            lax.fori_loop(lo, hi, wrapped, 0)

        def two_chunks(i):
            c = first + 2 * i
            score(c + 1, False, 1)
            weigh(c, 0)
            score(c + 2, True, 0)
            weigh(c + 1, 1)

        def last_two(c):
            score(c + 1, True, 1)
            weigh(c, 0)

        m_scr[...] = jnp.full_like(m_scr, NEG)
        acc_scr[...] = jnp.zeros_like(acc_scr)
        loop(first, first + 1, lambda c: score(c, True, 0))
        rest = qi - first
        pairs = jnp.right_shift(rest, 1)
        odd = rest & 1
        loop(0, pairs, two_chunks)
        loop(qi - 1, qi - 1 + odd, last_two)
        loop(qi, qi + odd, lambda c: weigh(c, 1))
        loop(qi, qi + 1 - odd, lambda c: weigh(c, 0))
        for hd in range(NSA_HEADS):
            w = gt_ref[0, 3 * hd + gate_row:3 * hd + gate_row + 1, :] / acc_scr[
                hd, HEAD_DIM:HEAD_DIM + 1, :]
            rows = slice(hd * HEAD_DIM, (hd + 1) * HEAD_DIM)
            yt_scr[rows, :] = yt_scr[rows, :] + w * acc_scr[hd, :HEAD_DIM, :]

    n_idx = lax.broadcasted_iota(jnp.int32, (NC, TQ), 0)
    t_cmp = qi * TQ + lax.broadcasted_iota(jnp.int32, (NC, TQ), 1)
    vis = (n_idx * CMP_STRIDE + (CMP_BLOCK - 1)) <= t_cmp
    visf = jnp.where(vis, 1.0, 0.0)
    j_idx = lax.broadcasted_iota(jnp.int32, (NSEL, TQ), 0)
    t_sel = qi * TQ + lax.broadcasted_iota(jnp.int32, (NSEL, TQ), 1)
    cur = jnp.right_shift(t_sel, SEL_BLOCK.bit_length() - 1)
    forced = (j_idx == 0) | (j_idx == cur) | (j_idx == cur - 1)
    valid = j_idx <= cur

    heads = range(NSA_HEADS)
    m_c = []
    for hd in heads:
        sc = jnp.where(vis, jnp.dot(kc_ref[0, hd // HPG], qt_ref[0, hd],
                                    preferred_element_type=F32), NEG)
        s_scr[0, hd, :NC, :] = sc
        m_c.append(jnp.max(_fold_rows(sc, jnp.maximum), axis=0, keepdims=True))
    inv = []
    for hd in heads:
        ec = jnp.exp2(s_scr[0, hd, :NC, :] - m_c[hd]) * visf
        s_scr[0, hd, :NC, :] = ec
        den = jnp.sum(_fold_rows(ec, jnp.add), axis=0, keepdims=True)
        inv.append(1.0 / jnp.where(den > 0.0, den, 1.0))
    psum = [jnp.zeros((NC, TQ), F32) for _ in range(KV_GROUPS)]
    for hd in heads:
        g = hd // HPG
        pc = s_scr[0, hd, :NC, :] * inv[hd]
        psum[g] = psum[g] + pc
        oc = jnp.dot(vct_ref[0, g], pc.astype(BF16), preferred_element_type=F32)
        yt_scr[hd * HEAD_DIM:(hd + 1) * HEAD_DIM, :] = gt_ref[0, 3 * hd:3 * hd + 1, :] * oc

    row8 = lax.broadcasted_iota(jnp.int32, (8, TQ), 0)
    pad = jnp.zeros((LANE - HEAD_DIM - NSEL, TQ), BF16)
    for g in range(KV_GROUPS):
        imp = jnp.dot(ovt_ref[...], psum[g], preferred_element_type=F32,
                      precision=lax.Precision.HIGHEST)
        imp = jnp.where(forced, FORCE, jnp.where(valid, imp, -FORCE))
        tiles = [imp[8 * r:8 * (r + 1)] for r in range(NSEL // 8)]
        rank = [jnp.zeros((8, TQ), jnp.int32) for _ in tiles]
        for kk in range(NSEL):
            rk = imp[kk:kk + 1, :]
            for r, tile in enumerate(tiles):
                if 8 * r > kk:
                    ahead = rk >= tile
                elif 8 * r + 7 < kk:
                    ahead = rk > tile
                else:
                    ahead = (rk > tile) | ((rk == tile) & (row8 > kk - 8 * r))
                rank[r] = rank[r] + jnp.where(ahead, 1, 0)
        selneg = jnp.where(jnp.concatenate(rank, axis=0) < SEL_TOPK, 0.0, NEG).astype(BF16)
        for hh in range(HPG):
            hd = g * HPG + hh
            qx_scr[hd] = jnp.concatenate([qt_ref[0, hd], selneg, pad], axis=0)

    branch(kselx_ref, vselt_ref, 0, -1, 1)
    branch(kwinx_ref, vwint_ref, jnp.maximum(qi - nwin, 0), qi - nwin, 2)

    y = jnp.transpose(yt_scr[...]).astype(BF16)
    ya = jnp.dot(y, pa_ref[...], preferred_element_type=F32)
    merged = (ga_ref[0] * ya + mb_ref[0]).astype(BF16)
    o_ref[0] = h_ref[0] + gate_ref[0] * jnp.dot(merged, wo_ref[...],
                                                preferred_element_type=F32)


def _attn(qt, kselx, vselt, kwinx, vwint, kc, vct, gt, ga, mb, h, mods, l, ovt, bias,
          proj_a, w_out):
    B, S, D = h.shape
    TQ = ATT_ROWS
    CK = ATT_CHUNK
    assert TQ == CK and WINDOW % CK == 0 and S % TQ == 0
    const = _whole

    def per_batch(a):
        nd = a.ndim
        return pl.BlockSpec((1,) + a.shape[1:], lambda b, i: (b,) + (0,) * (nd - 1))

    row = lambda w: pl.BlockSpec((1, TQ, w), lambda b, i: (b, i, 0))
    return pl.pallas_call(
        _attn_kernel,
        grid=(B, S // TQ),
        in_specs=[
            pl.BlockSpec((1, NSA_HEADS, HEAD_DIM, TQ), lambda b, i: (b, 0, 0, i)),
            per_batch(kselx), per_batch(vselt), per_batch(kwinx), per_batch(vwint),
            per_batch(kc), per_batch(vct),
            pl.BlockSpec((1, GATE_ROWS, TQ), lambda b, i: (b, 0, i)),
            row(D), row(D), row(D), _mod_vec(mods, l, 1, 2),
            const(ovt), const(bias), const(proj_a), const(w_out)],
        out_specs=row(D),
        out_shape=jax.ShapeDtypeStruct((B, S, D), F32),
        scratch_shapes=[
            pltpu.VMEM((NSA_HEADS, LANE, TQ), BF16),
            pltpu.VMEM((2, NSA_HEADS, CK, TQ), F32),
            pltpu.VMEM((2, NSA_HEADS, 1, TQ), F32), pltpu.VMEM((2, NSA_HEADS, 1, TQ), F32),
            pltpu.VMEM((NSA_HEADS, V_ROWS, TQ), F32),
            pltpu.VMEM((NSA_WIDTH, TQ), F32)],
        compiler_params=pltpu.CompilerParams(
            dimension_semantics=("parallel", "parallel"), vmem_limit_bytes=VMEM_LIMIT),
        name="attn",
    )(qt, kselx, vselt, kwinx, vwint, kc, vct, gt, ga, mb, h, mods, ovt, bias,
      _arr(proj_a), _arr(w_out))


def _rope_tables(pos):
    inv = 1.0 / (ROPE_THETA ** (np.arange(0, HEAD_DIM, 2, dtype=np.float64) / HEAD_DIM))
    ang = np.asarray(pos, np.float64)[:, None] * inv[None, :]
    cos = np.concatenate([np.cos(ang), np.cos(ang)], axis=1)
    sin = np.concatenate([-np.sin(ang), np.sin(ang)], axis=1)
    return cos.astype(np.float32), sin.astype(np.float32)


def _tables(S):
    cos, sin = _rope_tables(np.arange(S))
    cos_t = np.concatenate([cos, cos], axis=1)
    sin_t = np.concatenate([sin, sin], axis=1)
    n_cmp_pad = S // CMP_STRIDE
    starts = np.arange(n_cmp_pad) * CMP_STRIDE
    cos_c, sin_c = _rope_tables(starts + CMP_BLOCK - 1)
    n_sel = S // SEL_BLOCK
    sel_start = np.arange(n_sel) * SEL_BLOCK
    overlap = np.clip(np.minimum(starts[:, None] + CMP_BLOCK, sel_start[None, :] + SEL_BLOCK)
                      - np.maximum(starts[:, None], sel_start[None, :]), 0, None) / CMP_BLOCK
    ovt = np.ascontiguousarray(overlap.T).astype(np.float32)
    oh = np.zeros((S, LANE), np.float32)
    oh[np.arange(S), HEAD_DIM + np.arange(S) // SEL_BLOCK] = 1.0
    j = np.arange(ATT_CHUNK)[:, None]
    i = np.arange(ATT_ROWS)[None, :]
    bias = np.stack([np.zeros((ATT_CHUNK, ATT_ROWS)),
                     np.where(j <= i, 0.0, NEG),
                     np.where(j > i, 0.0, NEG)]
                    ).astype(np.float32)
    return tuple(jnp.asarray(a) for a in (cos_t, sin_t, cos_c, sin_c, ovt, oh, bias))


def kernel(x, c, ada_w, ada_b, norm_g, ffn_w_in, ffn_w_out, mix_w_in, cmp_pe, cmp_w1, cmp_w2,
           gm_ln_g, gm_ln_b, gm_ws, gm_bs, proj_a, proj_b, w_out, final_g):
    B, S, D = x.shape
    L = ada_w.shape[0]
    assert S // SEL_BLOCK + HEAD_DIM <= LANE and 3 * NSA_HEADS <= GATE_ROWS
    cos_t, sin_t, cos_c, sin_c, ovt, oh_t, bias = _tables(S)
    swap = np.concatenate([np.arange(HEAD_DIM // 2, HEAD_DIM), np.arange(HEAD_DIM // 2)])

    w_fi, w_fo = ffn_w_in.astype(BF16), ffn_w_out.astype(BF16)
    w_mix = jnp.pad(mix_w_in, ((0, 0), (0, 0), (0, -mix_w_in.shape[-1] % LANE))).astype(BF16)
    w_pa, w_pb, w_o = proj_a.astype(BF16), proj_b.astype(BF16), w_out.astype(BF16)
    w1 = cmp_w1.astype(BF16)
    pe = cmp_pe[:, :, :, None, :]
    w2k = jnp.concatenate([cmp_w2[:, 0], cmp_w2[:, 0][..., swap]], axis=-1).astype(BF16)
    w2v = jnp.pad(cmp_w2[:, 1], ((0, 0), (0, 0), (0, LANE - HEAD_DIM))).astype(BF16)
    bsx = jnp.repeat(jnp.swapaxes(gm_bs, 1, 2), HEAD_DIM, axis=2)

    gains = norm_g[:, :, None, :]
    ln_g, ln_b = gm_ln_g[:, None, :], gm_ln_b[:, None, :]

    mods = _ada(c, ada_w, ada_b).reshape(L, B, 3, 3, 1, D)
    h = x
    for l in range(L):
        h = _ffn(h, mods, l, 0, _Pick(gains, (l, 0)), _Pick(w_fi, (l, 0)), _Pick(w_fo, (l, 0)))
        qt, kcmp, vcmp, kselx, vselt, kwinx, vwint, gt, ga, mb = _mix_in(
            h, mods, l, _Pick(gains, (l, 1)), _Pick(w_mix, (l,)), cos_t, sin_t, oh_t,
            _Pick(ln_g, (l,)), _Pick(ln_b, (l,)), _Pick(gm_ws, (l,)), _Pick(bsx, (l,)),
            _Pick(w_pb, (l,)))
        kc, vct = _compress(kcmp, vcmp, _Pick(pe, (l,)), _Pick(w1, (l,)), _Pick(w2k, (l,)),
                            _Pick(w2v, (l,)), cos_c, sin_c)
        h = _attn(qt, kselx, vselt, kwinx, vwint, kc, vct, gt, ga, mb, h, mods, l, ovt, bias,
                  _Pick(w_pa, (l,)), _Pick(w_o, (l,)))
        h = _ffn(h, mods, l, 2, _Pick(gains, (l, 2)), _Pick(w_fi, (l, 1)), _Pick(w_fo, (l, 1)),
                 final_g=final_g[None] if l == L - 1 else None)
    return h
```

```python
import functools
from typing import NamedTuple

import numpy as np
import jax
import jax.numpy as jnp
from jax import lax
from jax.experimental import pallas as pl
from jax.experimental.pallas import tpu as pltpu

F32 = jnp.float32
BF16 = jnp.bfloat16

HEAD_DIM = 64
NSA_HEADS = 8
KV_GROUPS = 2
HPG = NSA_HEADS // KV_GROUPS
NSA_WIDTH = NSA_HEADS * HEAD_DIM
CMP_BLOCK = 32
CMP_STRIDE = 16
CMP_HIDDEN = 128
SEL_BLOCK = 64
SEL_TOPK = 16
WINDOW = 512
GM_GROUPS = 8
GM_CHUNK = 128
GM_WIDTH = GM_GROUPS * HEAD_DIM
ROPE_THETA = 10000.0
EPS = 1e-6
NEG = -1e30
FORCE = 1e4
LANE = 128
GELU_C = float(np.sqrt(2.0 / np.pi))
LOG2E = float(np.log2(np.e))
GATE_ROWS = 32
V_ROWS = HEAD_DIM + 16

FFN_ROWS = 512
FFN_SPLIT = 11
MIX_ROWS = 512
ATT_ROWS = 256
ATT_CHUNK = 256
VMEM_LIMIT = 56 * 1024 * 1024


def _sigmoid(x):
    return 1.0 / (1.0 + jnp.exp(-x))


def _silu(x):
    return x * _sigmoid(x)


def _rms_mod(x, g, shift, scale):
    y = x * lax.rsqrt(jnp.mean(x * x, axis=-1, keepdims=True) + EPS) * g
    return y * (1.0 + scale) + shift


class _Pick(NamedTuple):
    array: jax.Array
    idx: tuple


def _arr(x):
    return x.array if isinstance(x, _Pick) else x


def _whole(x, single_buffer=False):
    idx = x.idx if isinstance(x, _Pick) else ()
    shape = _arr(x).shape
    mode = dict(pipeline_mode=pl.Buffered(1)) if single_buffer else {}
    return pl.BlockSpec((None,) * len(idx) + shape[len(idx):],
                        lambda *_: idx + (0,) * (len(shape) - len(idx)), **mode)


def _mod_vec(mods, l, sub, k):
    return pl.BlockSpec((None, 1, None, None, 1, mods.shape[-1]),
                        lambda b, *_: (l, b, sub, k, 0, 0))


def _ada_kernel(c_ref, w_ref, b_ref, o_ref):
    s = _silu(c_ref[...])
    o_ref[0] = jnp.dot(s, w_ref[0], preferred_element_type=F32,
                       precision=lax.Precision.HIGHEST) + b_ref[0]


def _ada(c, ada_w, ada_b):
    L, D, N = ada_w.shape
    B = c.shape[0]
    tn = 1024
    return pl.pallas_call(
        _ada_kernel,
        grid=(L, N // tn),
        in_specs=[
            pl.BlockSpec((B, D), lambda l, j: (0, 0)),
            pl.BlockSpec((1, D, tn), lambda l, j: (l, 0, j)),
            pl.BlockSpec((1, 1, tn), lambda l, j: (l, 0, j)),
        ],
        out_specs=pl.BlockSpec((1, B, tn), lambda l, j: (l, 0, j)),
        out_shape=jax.ShapeDtypeStruct((L, B, N), F32),
        compiler_params=pltpu.CompilerParams(
            dimension_semantics=("parallel", "parallel"), vmem_limit_bytes=VMEM_LIMIT),
        name="ada",
    )(c, ada_w, ada_b.reshape(L, 1, N))


def _ffn_kernel(h_ref, shift_ref, scale_ref, gate_ref, g_ref, wi_ref, wo_ref, *rest,
                final_norm):
    fg_ref = rest[0] if final_norm else None
    o_ref = rest[-1]
    F = wo_ref.shape[0]
    tf = F // FFN_SPLIT
    x = h_ref[0]
    n = _rms_mod(x, g_ref[...], shift_ref[0], scale_ref[0]).astype(BF16)
    acc = None
    for j in range(FFN_SPLIT):
        a = jnp.dot(n, wi_ref[:, j * tf:(j + 1) * tf], preferred_element_type=F32)
        b = jnp.dot(n, wi_ref[:, F + j * tf:F + (j + 1) * tf], preferred_element_type=F32)
        hm = (_silu(a) * b).astype(BF16)
        part = jnp.dot(hm, wo_ref[j * tf:(j + 1) * tf, :], preferred_element_type=F32)
        acc = part if acc is None else acc + part
    y = x + (0.5 * gate_ref[0]) * acc
    if final_norm:
        y = y * lax.rsqrt(jnp.mean(y * y, axis=-1, keepdims=True) + EPS) * fg_ref[...]
    o_ref[0] = y


def _ffn(h, mods, l, sub, g, w_in, w_out, final_g=None):
    B, S, D = h.shape
    tm = FFN_ROWS
    row = pl.BlockSpec((1, tm, D), lambda b, i: (b, i, 0))
    extra = () if final_g is None else (final_g,)
    return pl.pallas_call(
        functools.partial(_ffn_kernel, final_norm=final_g is not None),
        grid=(B, S // tm),
        in_specs=[row] + [_mod_vec(mods, l, sub, k) for k in range(3)]
        + [_whole(g), _whole(w_in, True), _whole(w_out, True)] + [_whole(e) for e in extra],
        out_specs=row,
        out_shape=jax.ShapeDtypeStruct((B, S, D), F32),
        compiler_params=pltpu.CompilerParams(
            dimension_semantics=("parallel", "parallel"), vmem_limit_bytes=VMEM_LIMIT),
        name="ffn",
    )(h, mods, mods, mods, _arr(g), _arr(w_in), _arr(w_out), *extra)


def _rope_pair(x, c, s):
    lane = lax.broadcasted_iota(jnp.int32, x.shape, 1)
    first_half = (lane & (HEAD_DIM - 1)) < HEAD_DIM // 2
    swapped = jnp.where(first_half, pltpu.roll(x, LANE - HEAD_DIM // 2, 1),
                        pltpu.roll(x, HEAD_DIM // 2, 1))
    return x * c + swapped * s


def _mix_cols(D):
    kv0 = NSA_WIDTH
    g0 = kv0 + 6 * KV_GROUPS * HEAD_DIM
    tail0 = g0 + 3 * NSA_HEADS
    edges = np.cumsum((0, 2 * GM_WIDTH, D, D))
    tail = [slice(int(a), int(b)) for a, b in zip(edges[:-1], edges[1:])]
    return slice(0, kv0), slice(kv0, g0), slice(g0, g0 + LANE), tail0, tail


def _mix_in_kernel(h_ref, shift_ref, scale_ref, g_ref, w_ref, cos_ref, sin_ref, oh_ref,
                   lng_ref, lnb_ref, ws_ref, bsx_ref, pb_ref,
                   qt_ref, kcmp_ref, vcmp_ref, kselx_ref, vselt_ref, kwinx_ref, vwint_ref,
                   gt_ref, ga_ref, mb_ref, wt_scr):
    tm = h_ref.shape[1]
    CK = vselt_ref.shape[4]
    c_q, c_kv, c_g, tail0, (c_uv, c_ga, c_gb) = _mix_cols(h_ref.shape[2])

    @pl.when((pl.program_id(0) == 0) & (pl.program_id(1) == 0))
    def _():
        wt_scr[...] = w_ref[:, tail0:tail0 + wt_scr.shape[1]]

    n = _rms_mod(h_ref[0], g_ref[...], shift_ref[0], scale_ref[0]).astype(BF16)
    cos = cos_ref[...]
    sin = sin_ref[...]
    low_lanes = lax.broadcasted_iota(jnp.int32, (tm, LANE), 1) < HEAD_DIM

    q = jnp.dot(n, w_ref[:, c_q], preferred_element_type=F32) * (HEAD_DIM ** -0.5 * LOG2E)
    for p in range(NSA_HEADS // 2):
        qp = jnp.transpose(_rope_pair(q[:, p * LANE:(p + 1) * LANE], cos, sin)).astype(BF16)
        qt_ref[0, 2 * p] = qp[:HEAD_DIM]
        qt_ref[0, 2 * p + 1] = qp[HEAD_DIM:]

    kv = jnp.dot(n, w_ref[:, c_kv], preferred_element_type=F32)
    kcmp_ref[0] = kv[:, 0 * LANE:1 * LANE]
    vcmp_ref[0] = kv[:, 1 * LANE:2 * LANE]
    for idx, ref, ext in ((2, kselx_ref, oh_ref[...]), (4, kwinx_ref, 0.0)):
        k2 = _rope_pair(kv[:, idx * LANE:(idx + 1) * LANE], cos, sin)
        ref[0, 0] = jnp.where(low_lanes, k2, ext).astype(BF16)
        ref[0, 1] = jnp.where(low_lanes, pltpu.roll(k2, HEAD_DIM, 1), ext).astype(BF16)
    ones_row = jnp.where(lax.broadcasted_iota(jnp.int32, (V_ROWS - HEAD_DIM, CK), 0) == 0,
                         1.0, 0.0).astype(BF16)
    for idx, ref in ((3, vselt_ref), (5, vwint_ref)):
        vt = jnp.transpose(kv[:, idx * LANE:(idx + 1) * LANE]).astype(BF16)
        for g in range(KV_GROUPS):
            for r in range(tm // CK):
                ref[0, g, r] = jnp.concatenate(
                    [vt[g * HEAD_DIM:(g + 1) * HEAD_DIM, r * CK:(r + 1) * CK], ones_row], axis=0)

    gates = _sigmoid(jnp.dot(n, w_ref[:, c_g], preferred_element_type=F32))
    gt_ref[0] = jnp.transpose(gates)[:GATE_ROWS]
    ga_ref[0] = _sigmoid(jnp.dot(n, wt_scr[:, c_ga], preferred_element_type=F32))

    uv = jnp.dot(n, wt_scr[:, c_uv], preferred_element_type=F32)
    ge = uv * (0.5 * (1.0 + jnp.tanh(GELU_C * (uv + 0.044715 * (uv * uv * uv)))))
    u = ge[:, :GM_WIDTH]
    v = ge[:, GM_WIDTH:]
    mu = jnp.mean(v, axis=-1, keepdims=True)
    var = jnp.mean(jnp.square(v - mu), axis=-1, keepdims=True)
    vln = ((v - mu) * lax.rsqrt(var + EPS) * lng_ref[...] + lnb_ref[...]).astype(BF16)

    ti = lax.broadcasted_iota(jnp.int32, (GM_CHUNK, GM_CHUNK), 0)
    si = lax.broadcasted_iota(jnp.int32, (GM_CHUNK, GM_CHUNK), 1)
    tril = si <= ti
    wm = [jnp.where(tril, ws_ref[gg], 0.0).astype(BF16) for gg in range(GM_GROUPS)]
    low = lax.broadcasted_iota(jnp.int32, (GM_CHUNK, LANE), 1) < HEAD_DIM
    bsx = bsx_ref[...]
    yb_rows = []
    for r in range(tm // GM_CHUNK):
        vch = vln[r * GM_CHUNK:(r + 1) * GM_CHUNK]
        pieces = []
        for p in range(GM_GROUPS // 2):
            vp = vch[:, p * LANE:(p + 1) * LANE]
            a0 = jnp.dot(wm[2 * p], vp, preferred_element_type=F32)
            a1 = jnp.dot(wm[2 * p + 1], vp, preferred_element_type=F32)
            pieces.append(jnp.where(low, a0, a1))
        sv = jnp.concatenate(pieces, axis=1) + bsx
        yb_rows.append(u[r * GM_CHUNK:(r + 1) * GM_CHUNK] * sv)
    yb = jnp.concatenate(yb_rows, axis=0).astype(BF16)
    gb = _sigmoid(jnp.dot(n, wt_scr[:, c_gb], preferred_element_type=F32))
    mb_ref[0] = gb * jnp.dot(yb, pb_ref[...], preferred_element_type=F32)


def _mix_in(h, mods, l, g, w, cos_t, sin_t, oh_t, ln_g, ln_b, ws, bsx, proj_b):
    B, S, D = h.shape
    tail0, tail = _mix_cols(D)[3:]
    assert _arr(w).shape[-2:] == (D, tail0 + tail[-1].stop)
    tm = MIX_ROWS
    G = KV_GROUPS
    CK = ATT_CHUNK
    const = _whole
    row = lambda w: pl.BlockSpec((1, tm, w), lambda b, i: (b, i, 0))
    tab = pl.BlockSpec((tm, LANE), lambda b, i: (i, 0))
    kspec = pl.BlockSpec((1, G, tm, LANE), lambda b, i: (b, 0, i, 0))
    kshape = jax.ShapeDtypeStruct((B, G, S, LANE), BF16)
    vspec = pl.BlockSpec((1, G, tm // CK, V_ROWS, CK), lambda b, i: (b, 0, i, 0, 0))
    vshape = jax.ShapeDtypeStruct((B, G, S // CK, V_ROWS, CK), BF16)
    return pl.pallas_call(
        _mix_in_kernel,
        grid=(B, S // tm),
        in_specs=[row(D), _mod_vec(mods, l, 1, 0), _mod_vec(mods, l, 1, 1), const(g),
                  _whole(w, True), tab, tab, tab, const(ln_g), const(ln_b), const(ws),
                  const(bsx), const(proj_b)],
        out_specs=[
            pl.BlockSpec((1, NSA_HEADS, HEAD_DIM, tm), lambda b, i: (b, 0, 0, i)),
            row(LANE), row(LANE), kspec, vspec, kspec, vspec,
            pl.BlockSpec((1, GATE_ROWS, tm), lambda b, i: (b, 0, i)),
            row(D), row(D)],
        out_shape=[
            jax.ShapeDtypeStruct((B, NSA_HEADS, HEAD_DIM, S), BF16),
            jax.ShapeDtypeStruct((B, S, LANE), F32), jax.ShapeDtypeStruct((B, S, LANE), F32),
            kshape, vshape, kshape, vshape,
            jax.ShapeDtypeStruct((B, GATE_ROWS, S), F32),
            jax.ShapeDtypeStruct((B, S, D), F32), jax.ShapeDtypeStruct((B, S, D), F32)],
        scratch_shapes=[pltpu.VMEM((D, tail[-1].stop), BF16)],
        compiler_params=pltpu.CompilerParams(
            dimension_semantics=("arbitrary", "arbitrary"), vmem_limit_bytes=VMEM_LIMIT),
        name="mix_in",
    )(h, mods, mods, *map(_arr, (g, w, cos_t, sin_t, oh_t, ln_g, ln_b, ws, bsx, proj_b)))


def _compress_kernel(k_ref, v_ref, pe_ref, w1_ref, w2k_ref, w2v_ref, cos_ref, sin_ref,
                     kc_ref, vct_ref):
    NC = kc_ref.shape[2]
    half = CMP_BLOCK // 2

    def hidden(x_ref, j):
        ha = [None] * KV_GROUPS
        hb = [None] * KV_GROUPS
        for l in range(half):
            x = x_ref[0, pl.ds(l, NC, stride=CMP_STRIDE), :]
            for g in range(KV_GROUPS):
                xg = x[:, g * HEAD_DIM:(g + 1) * HEAD_DIM]
                a = jnp.dot((xg + pe_ref[j, l]).astype(BF16), w1_ref[j, l],
                            preferred_element_type=F32)
                b = jnp.dot((xg + pe_ref[j, half + l]).astype(BF16), w1_ref[j, half + l],
                            preferred_element_type=F32)
                ha[g] = a if ha[g] is None else ha[g] + a
                hb[g] = b if hb[g] is None else hb[g] + b
        return [_silu(ha[g] + pltpu.roll(hb[g], NC - 1, 0)) for g in range(KV_GROUPS)]

    hk = hidden(k_ref, 0)
    hv = hidden(v_ref, 1)
    for g in range(KV_GROUPS):
        k2 = jnp.dot(hk[g].astype(BF16), w2k_ref[...], preferred_element_type=F32)
        kc = k2[:, :HEAD_DIM] * cos_ref[...] + k2[:, HEAD_DIM:] * sin_ref[...]
        kc_ref[0, g] = kc.astype(BF16)
        vc = jnp.dot(hv[g].astype(BF16), w2v_ref[...],
                     preferred_element_type=F32)
        vct_ref[0, g] = jnp.transpose(vc)[:HEAD_DIM].astype(BF16)


def _compress(kcmp, vcmp, pe, w1, w2k, w2v, cos_c, sin_c):
    B, S, W = kcmp.shape
    NC = S // CMP_STRIDE
    const = _whole
    blk = pl.BlockSpec((1, S, W), lambda b: (b, 0, 0))
    return pl.pallas_call(
        _compress_kernel,
        grid=(B,),
        in_specs=[blk, blk, const(pe), const(w1), const(w2k), const(w2v),
                  const(cos_c), const(sin_c)],
        out_specs=[pl.BlockSpec((1, KV_GROUPS, NC, HEAD_DIM), lambda b: (b, 0, 0, 0)),
                   pl.BlockSpec((1, KV_GROUPS, HEAD_DIM, NC), lambda b: (b, 0, 0, 0))],
        out_shape=[jax.ShapeDtypeStruct((B, KV_GROUPS, NC, HEAD_DIM), BF16),
                   jax.ShapeDtypeStruct((B, KV_GROUPS, HEAD_DIM, NC), BF16)],
        compiler_params=pltpu.CompilerParams(
            dimension_semantics=("parallel",), vmem_limit_bytes=VMEM_LIMIT),
        name="compress",
    )(kcmp, vcmp, *map(_arr, (pe, w1, w2k, w2v, cos_c, sin_c)))


def _fold_rows(x, op):
    parts = [x[8 * i:8 * (i + 1)] for i in range(x.shape[0] // 8)]
    while len(parts) > 1:
        parts = [op(parts[i], parts[i + 1]) for i in range(0, len(parts), 2)]
    return parts[0]


def _attn_kernel(qt_ref, kselx_ref, vselt_ref, kwinx_ref, vwint_ref, kc_ref, vct_ref, gt_ref,
                 ga_ref, mb_ref, h_ref, gate_ref, ovt_ref, bias_ref, pa_ref, wo_ref,
                 o_ref, qx_scr, s_scr, m_scr, a_scr, acc_scr, yt_scr):
    TQ = h_ref.shape[1]
    CK = ATT_CHUNK
    NC = kc_ref.shape[2]
    NSEL = kselx_ref.shape[2] // SEL_BLOCK
    nwin = WINDOW // CK
    qi = pl.program_id(1)

    class Branch(NamedTuple):
        br: int
        kx_ref: object
        vt_ref: object
        first: object
        far: object
        gate_row: int

    def score(b, c, masked, dst):
        kx = [b.kx_ref[0, g, pl.ds(pl.multiple_of(c * CK, CK), CK), :]
              for g in range(KV_GROUPS)]
        if masked:
            bias = bias_ref[jnp.where(c == qi, 1, jnp.where(c == b.far, 2, 0))]
        for hd in range(NSA_HEADS):
            s = jnp.dot(kx[hd // HPG], qx_scr[hd], preferred_element_type=F32)
            if masked:
                s = s + bias
            s_scr[b.br, dst, hd] = s
            m_prev = m_scr[b.br, 1 - dst, hd]
            m_new = jnp.maximum(m_prev, jnp.max(_fold_rows(s, jnp.maximum), axis=0,
                                                keepdims=True))
            a_scr[b.br, dst, hd] = jnp.exp2(m_prev - m_new)
            m_scr[b.br, dst, hd] = m_new

    def weigh(b, c, src):
        vt = [b.vt_ref[0, g, c] for g in range(KV_GROUPS)]
        for hd in range(NSA_HEADS):
            p = jnp.exp2(s_scr[b.br, src, hd] - m_scr[b.br, src, hd]).astype(BF16)
            acc_scr[b.br, hd] = a_scr[b.br, src, hd] * acc_scr[b.br, hd] + jnp.dot(
                vt[hd // HPG], p, preferred_element_type=F32)

    def loop(lo, hi, body):
        def wrapped(c, carry):
            body(c)
            return carry
        lax.fori_loop(lo, hi, wrapped, 0)

    def middle(b):
        def two_chunks(i):
            c = b.first + 2 * i
            score(b, c + 1, False, 1)
            weigh(b, c, 0)
            score(b, c + 2, True, 0)
            weigh(b, c + 1, 1)

        def last_two(c):
            score(b, c + 1, True, 1)
            weigh(b, c, 0)

        rest = qi - b.first
        odd = rest & 1
        loop(0, jnp.right_shift(rest, 1), two_chunks)
        loop(qi - 1, qi - 1 + odd, last_two)
        return odd

    def finish(b):
        for hd in range(NSA_HEADS):
            w = gt_ref[0, 3 * hd + b.gate_row:3 * hd + b.gate_row + 1, :] / acc_scr[
                b.br, hd, HEAD_DIM:HEAD_DIM + 1, :]
            rows = slice(hd * HEAD_DIM, (hd + 1) * HEAD_DIM)
            yt_scr[rows, :] = yt_scr[rows, :] + w * acc_scr[b.br, hd, :HEAD_DIM, :]

    n_idx = lax.broadcasted_iota(jnp.int32, (NC, TQ), 0)
    t_cmp = qi * TQ + lax.broadcasted_iota(jnp.int32, (NC, TQ), 1)
    vis = (n_idx * CMP_STRIDE + (CMP_BLOCK - 1)) <= t_cmp
    visf = jnp.where(vis, 1.0, 0.0)
    j_idx = lax.broadcasted_iota(jnp.int32, (NSEL, TQ), 0)
    t_sel = qi * TQ + lax.broadcasted_iota(jnp.int32, (NSEL, TQ), 1)
    cur = jnp.right_shift(t_sel, SEL_BLOCK.bit_length() - 1)
    forced = (j_idx == 0) | (j_idx == cur) | (j_idx == cur - 1)
    valid = j_idx <= cur

    heads = range(NSA_HEADS)
    m_c = []
    for hd in heads:
        sc = jnp.where(vis, jnp.dot(kc_ref[0, hd // HPG], qt_ref[0, hd],
                                    preferred_element_type=F32), NEG)
        s_scr[0, 0, hd, :NC, :] = sc
        m_c.append(jnp.max(_fold_rows(sc, jnp.maximum), axis=0, keepdims=True))
    inv = []
    for hd in heads:
        ec = jnp.exp2(s_scr[0, 0, hd, :NC, :] - m_c[hd]) * visf
        s_scr[0, 0, hd, :NC, :] = ec
        den = jnp.sum(_fold_rows(ec, jnp.add), axis=0, keepdims=True)
        inv.append(1.0 / jnp.where(den > 0.0, den, 1.0))
    psum = [jnp.zeros((NC, TQ), F32) for _ in range(KV_GROUPS)]
    for hd in heads:
        g = hd // HPG
        pc = s_scr[0, 0, hd, :NC, :] * inv[hd]
        psum[g] = psum[g] + pc
        oc = jnp.dot(vct_ref[0, g], pc.astype(BF16), preferred_element_type=F32)
        yt_scr[hd * HEAD_DIM:(hd + 1) * HEAD_DIM, :] = gt_ref[0, 3 * hd:3 * hd + 1, :] * oc

    row8 = lax.broadcasted_iota(jnp.int32, (8, TQ), 0)
    pad = jnp.zeros((LANE - HEAD_DIM - NSEL, TQ), BF16)
    ranked = (qi + 1) * TQ > SEL_TOPK * SEL_BLOCK

    @pl.when(ranked)
    def _():
        for g in range(KV_GROUPS):
            imp = jnp.dot(ovt_ref[...], psum[g], preferred_element_type=F32,
                          precision=lax.Precision.HIGHEST)
            imp = jnp.where(forced, FORCE, jnp.where(valid, imp, -FORCE))
            tiles = [imp[8 * r:8 * (r + 1)] for r in range(NSEL // 8)]
            rank = [jnp.zeros((8, TQ), jnp.int32) for _ in tiles]
            for kk in range(NSEL):
                rk = imp[kk:kk + 1, :]
                for r, tile in enumerate(tiles):
                    if 8 * r > kk:
                        ahead = rk >= tile
                    elif 8 * r + 7 < kk:
                        ahead = rk > tile
                    else:
                        ahead = (rk > tile) | ((rk == tile) & (row8 > kk - 8 * r))
                    rank[r] = rank[r] + jnp.where(ahead, 1, 0)
            selneg = jnp.where(jnp.concatenate(rank, axis=0) < SEL_TOPK, 0.0,
                               NEG).astype(BF16)
            for hh in range(HPG):
                hd = g * HPG + hh
                qx_scr[hd] = jnp.concatenate([qt_ref[0, hd], selneg, pad], axis=0)

    @pl.when(jnp.logical_not(ranked))
    def _():
        for hd in heads:
            qx_scr[hd] = jnp.concatenate(
                [qt_ref[0, hd], jnp.zeros((LANE - HEAD_DIM, TQ), BF16)], axis=0)

    sel = Branch(0, kselx_ref, vselt_ref, 0, -1, 1)
    win = Branch(1, kwinx_ref, vwint_ref, jnp.maximum(qi - nwin, 0), qi - nwin, 2)
    m_scr[...] = jnp.full_like(m_scr, NEG)
    acc_scr[...] = jnp.zeros_like(acc_scr)
    loop(sel.first, sel.first + 1, lambda c: score(sel, c, True, 0))
    odd = middle(sel)

    def drain_sel_fill_win(src):
        def body(c):
            score(win, win.first, True, 0)
            weigh(sel, c, src)
        return body

    loop(qi, qi + odd, drain_sel_fill_win(1))
    loop(qi, qi + 1 - odd, drain_sel_fill_win(0))
    odd = middle(win)
    loop(qi, qi + odd, lambda c: weigh(win, c, 1))
    loop(qi, qi + 1 - odd, lambda c: weigh(win, c, 0))
    finish(sel)
    finish(win)

    y = jnp.transpose(yt_scr[...]).astype(BF16)
    ya = jnp.dot(y, pa_ref[...], preferred_element_type=F32)
    merged = (ga_ref[0] * ya + mb_ref[0]).astype(BF16)
    o_ref[0] = h_ref[0] + gate_ref[0] * jnp.dot(merged, wo_ref[...],
                                                preferred_element_type=F32)


def _attn(qt, kselx, vselt, kwinx, vwint, kc, vct, gt, ga, mb, h, mods, l, ovt, bias,
          proj_a, w_out):
    B, S, D = h.shape
    TQ = ATT_ROWS
    CK = ATT_CHUNK
    assert TQ == CK and WINDOW % CK == 0 and S % TQ == 0
    const = _whole

    def per_batch(a):
        nd = a.ndim
        return pl.BlockSpec((1,) + a.shape[1:], lambda b, i: (b,) + (0,) * (nd - 1))

    row = lambda w: pl.BlockSpec((1, TQ, w), lambda b, i: (b, i, 0))
    return pl.pallas_call(
        _attn_kernel,
        grid=(B, S // TQ),
        in_specs=[
            pl.BlockSpec((1, NSA_HEADS, HEAD_DIM, TQ), lambda b, i: (b, 0, 0, i)),
            per_batch(kselx), per_batch(vselt), per_batch(kwinx), per_batch(vwint),
            per_batch(kc), per_batch(vct),
            pl.BlockSpec((1, GATE_ROWS, TQ), lambda b, i: (b, 0, i)),
            row(D), row(D), row(D), _mod_vec(mods, l, 1, 2),
            const(ovt), const(bias), const(proj_a), const(w_out)],
        out_specs=row(D),
        out_shape=jax.ShapeDtypeStruct((B, S, D), F32),
        scratch_shapes=[
            pltpu.VMEM((NSA_HEADS, LANE, TQ), BF16),
            pltpu.VMEM((2, 2, NSA_HEADS, CK, TQ), F32),
            pltpu.VMEM((2, 2, NSA_HEADS, 1, TQ), F32), pltpu.VMEM((2, 2, NSA_HEADS, 1, TQ), F32),
            pltpu.VMEM((2, NSA_HEADS, V_ROWS, TQ), F32),
            pltpu.VMEM((NSA_WIDTH, TQ), F32)],
        compiler_params=pltpu.CompilerParams(
            dimension_semantics=("parallel", "parallel"), vmem_limit_bytes=VMEM_LIMIT),
        name="attn",
    )(qt, kselx, vselt, kwinx, vwint, kc, vct, gt, ga, mb, h, mods, ovt, bias,
      _arr(proj_a), _arr(w_out))


def _rope_tables(pos):
    inv = 1.0 / (ROPE_THETA ** (np.arange(0, HEAD_DIM, 2, dtype=np.float64) / HEAD_DIM))
    ang = np.asarray(pos, np.float64)[:, None] * inv[None, :]
    cos = np.concatenate([np.cos(ang), np.cos(ang)], axis=1)
    sin = np.concatenate([-np.sin(ang), np.sin(ang)], axis=1)
    return cos.astype(np.float32), sin.astype(np.float32)


def _tables(S):
    cos, sin = _rope_tables(np.arange(S))
    cos_t = np.concatenate([cos, cos], axis=1)
    sin_t = np.concatenate([sin, sin], axis=1)
    n_cmp_pad = S // CMP_STRIDE
    starts = np.arange(n_cmp_pad) * CMP_STRIDE
    cos_c, sin_c = _rope_tables(starts + CMP_BLOCK - 1)
    n_sel = S // SEL_BLOCK
    sel_start = np.arange(n_sel) * SEL_BLOCK
    overlap = np.clip(np.minimum(starts[:, None] + CMP_BLOCK, sel_start[None, :] + SEL_BLOCK)
                      - np.maximum(starts[:, None], sel_start[None, :]), 0, None) / CMP_BLOCK
    ovt = np.ascontiguousarray(overlap.T).astype(np.float32)
    oh = np.zeros((S, LANE), np.float32)
    oh[np.arange(S), HEAD_DIM + np.arange(S) // SEL_BLOCK] = 1.0
    j = np.arange(ATT_CHUNK)[:, None]
    i = np.arange(ATT_ROWS)[None, :]
    bias = np.stack([np.zeros((ATT_CHUNK, ATT_ROWS)),
                     np.where(j <= i, 0.0, NEG),
                     np.where(j > i, 0.0, NEG)]
                    ).astype(np.float32)
    return tuple(jnp.asarray(a) for a in (cos_t, sin_t, cos_c, sin_c, ovt, oh, bias))


def kernel(x, c, ada_w, ada_b, norm_g, ffn_w_in, ffn_w_out, mix_w_in, cmp_pe, cmp_w1, cmp_w2,
           gm_ln_g, gm_ln_b, gm_ws, gm_bs, proj_a, proj_b, w_out, final_g):
    B, S, D = x.shape
    L = ada_w.shape[0]
    assert S // SEL_BLOCK + HEAD_DIM <= LANE and 3 * NSA_HEADS <= GATE_ROWS
    cos_t, sin_t, cos_c, sin_c, ovt, oh_t, bias = _tables(S)
    swap = np.concatenate([np.arange(HEAD_DIM // 2, HEAD_DIM), np.arange(HEAD_DIM // 2)])

    w_fi, w_fo = ffn_w_in.astype(BF16), ffn_w_out.astype(BF16)
    w_mix = mix_w_in.astype(BF16)
    w_pa, w_pb, w_o = proj_a.astype(BF16), proj_b.astype(BF16), w_out.astype(BF16)
    w1 = cmp_w1.astype(BF16)
    pe = cmp_pe[:, :, :, None, :]
    w2k = jnp.concatenate([cmp_w2[:, 0], cmp_w2[:, 0][..., swap]], axis=-1).astype(BF16)
    w2v = jnp.pad(cmp_w2[:, 1], ((0, 0), (0, 0), (0, LANE - HEAD_DIM))).astype(BF16)
    bsx = jnp.repeat(jnp.swapaxes(gm_bs, 1, 2), HEAD_DIM, axis=2)

    gains = norm_g[:, :, None, :]
    ln_g, ln_b = gm_ln_g[:, None, :], gm_ln_b[:, None, :]

    mods = _ada(c, ada_w, ada_b).reshape(L, B, 3, 3, 1, D)
    h = x
    for l in range(L):
        h = _ffn(h, mods, l, 0, _Pick(gains, (l, 0)), _Pick(w_fi, (l, 0)), _Pick(w_fo, (l, 0)))
        qt, kcmp, vcmp, kselx, vselt, kwinx, vwint, gt, ga, mb = _mix_in(
            h, mods, l, _Pick(gains, (l, 1)), _Pick(w_mix, (l,)), cos_t, sin_t, oh_t,
            _Pick(ln_g, (l,)), _Pick(ln_b, (l,)), _Pick(gm_ws, (l,)), _Pick(bsx, (l,)),
            _Pick(w_pb, (l,)))
        kc, vct = _compress(kcmp, vcmp, _Pick(pe, (l,)), _Pick(w1, (l,)), _Pick(w2k, (l,)),
                            _Pick(w2v, (l,)), cos_c, sin_c)
        h = _attn(qt, kselx, vselt, kwinx, vwint, kc, vct, gt, ga, mb, h, mods, l, ovt, bias,
                  _Pick(w_pa, (l,)), _Pick(w_o, (l,)))
        h = _ffn(h, mods, l, 2, _Pick(gains, (l, 2)), _Pick(w_fi, (l, 1)), _Pick(w_fo, (l, 1)),
                 final_g=final_g[None] if l == L - 1 else None)
    return h
```

```python
import functools
from typing import NamedTuple

import numpy as np
import jax
import jax.numpy as jnp
from jax import lax
from jax.experimental import pallas as pl
from jax.experimental.pallas import tpu as pltpu

F32 = jnp.float32
BF16 = jnp.bfloat16

HEAD_DIM = 64
NSA_HEADS = 8
KV_GROUPS = 2
HPG = NSA_HEADS // KV_GROUPS
NSA_WIDTH = NSA_HEADS * HEAD_DIM
CMP_BLOCK = 32
CMP_STRIDE = 16
CMP_HIDDEN = 128
SEL_BLOCK = 64
SEL_TOPK = 16
WINDOW = 512
GM_GROUPS = 8
GM_CHUNK = 128
GM_WIDTH = GM_GROUPS * HEAD_DIM
ROPE_THETA = 10000.0
EPS = 1e-6
NEG = -1e30
FORCE = 1e4
LANE = 128
GELU_C = float(np.sqrt(2.0 / np.pi))
LOG2E = float(np.log2(np.e))
GATE_ROWS = 32
V_ROWS = HEAD_DIM + 16

FFN_ROWS = 512
FFN_SPLIT = 11
MIX_ROWS = 512
ATT_ROWS = 256
ATT_CHUNK = 256
V7X_VMEM_BYTES = 64 * 1024 * 1024
VMEM_LIMIT = V7X_VMEM_BYTES - 8 * 1024 * 1024


def _sigmoid(x):
    return 1.0 / (1.0 + jnp.exp(-x))


def _silu(x):
    return x * _sigmoid(x)


def _rms_mod(x, g, shift, scale):
    y = x * lax.rsqrt(jnp.mean(x * x, axis=-1, keepdims=True) + EPS) * g
    return y * (1.0 + scale) + shift


class _Pick(NamedTuple):
    array: jax.Array
    idx: tuple


def _arr(x):
    return x.array if isinstance(x, _Pick) else x


def _whole(x, single_buffer=False):
    idx = x.idx if isinstance(x, _Pick) else ()
    shape = _arr(x).shape
    mode = dict(pipeline_mode=pl.Buffered(1)) if single_buffer else {}
    return pl.BlockSpec((None,) * len(idx) + shape[len(idx):],
                        lambda *_: idx + (0,) * (len(shape) - len(idx)), **mode)


def _mod_vec(mods, l, sub, k):
    return pl.BlockSpec((None, 1, None, None, 1, mods.shape[-1]),
                        lambda b, *_: (l, b, sub, k, 0, 0))


def _ada_kernel(c_ref, w_ref, b_ref, o_ref):
    s = _silu(c_ref[...])
    o_ref[0] = jnp.dot(s, w_ref[0], preferred_element_type=F32,
                       precision=lax.Precision.HIGHEST) + b_ref[0]


def _ada(c, ada_w, ada_b):
    L, D, N = ada_w.shape
    B = c.shape[0]
    tn = 1024
    return pl.pallas_call(
        _ada_kernel,
        grid=(L, N // tn),
        in_specs=[
            pl.BlockSpec((B, D), lambda l, j: (0, 0)),
            pl.BlockSpec((1, D, tn), lambda l, j: (l, 0, j)),
            pl.BlockSpec((1, 1, tn), lambda l, j: (l, 0, j)),
        ],
        out_specs=pl.BlockSpec((1, B, tn), lambda l, j: (l, 0, j)),
        out_shape=jax.ShapeDtypeStruct((L, B, N), F32),
        compiler_params=pltpu.CompilerParams(
            dimension_semantics=("parallel", "parallel"), vmem_limit_bytes=VMEM_LIMIT),
        name="ada",
    )(c, ada_w, ada_b.reshape(L, 1, N))


def _ffn_kernel(h_ref, shift_ref, scale_ref, gate_ref, g_ref, wi_ref, wo_ref, *rest,
                final_norm):
    fg_ref = rest[0] if final_norm else None
    o_ref = rest[-1]
    F = wo_ref.shape[0]
    tf = F // FFN_SPLIT
    x = h_ref[0]
    n = _rms_mod(x, g_ref[...], shift_ref[0], scale_ref[0]).astype(BF16)
    acc = None
    for j in range(FFN_SPLIT):
        a = jnp.dot(n, wi_ref[:, j * tf:(j + 1) * tf], preferred_element_type=F32)
        b = jnp.dot(n, wi_ref[:, F + j * tf:F + (j + 1) * tf], preferred_element_type=F32)
        hm = (_silu(a) * b).astype(BF16)
        part = jnp.dot(hm, wo_ref[j * tf:(j + 1) * tf, :], preferred_element_type=F32)
        acc = part if acc is None else acc + part
    y = x + (0.5 * gate_ref[0]) * acc
    if final_norm:
        y = y * lax.rsqrt(jnp.mean(y * y, axis=-1, keepdims=True) + EPS) * fg_ref[...]
    o_ref[0] = y


def _ffn(h, mods, l, sub, g, w_in, w_out, final_g=None):
    B, S, D = h.shape
    tm = FFN_ROWS
    row = pl.BlockSpec((1, tm, D), lambda b, i: (b, i, 0))
    extra = () if final_g is None else (final_g,)
    return pl.pallas_call(
        functools.partial(_ffn_kernel, final_norm=final_g is not None),
        grid=(B, S // tm),
        in_specs=[row] + [_mod_vec(mods, l, sub, k) for k in range(3)]
        + [_whole(g), _whole(w_in, True), _whole(w_out, True)] + [_whole(e) for e in extra],
        out_specs=row,
        out_shape=jax.ShapeDtypeStruct((B, S, D), F32),
        compiler_params=pltpu.CompilerParams(
            dimension_semantics=("parallel", "parallel"), vmem_limit_bytes=VMEM_LIMIT),
        name="ffn",
    )(h, mods, mods, mods, _arr(g), _arr(w_in), _arr(w_out), *extra)


def _rope_pair(x, c, s):
    lane = lax.broadcasted_iota(jnp.int32, x.shape, 1)
    first_half = (lane & (HEAD_DIM - 1)) < HEAD_DIM // 2
    swapped = jnp.where(first_half, pltpu.roll(x, LANE - HEAD_DIM // 2, 1),
                        pltpu.roll(x, HEAD_DIM // 2, 1))
    return x * c + swapped * s


def _mix_cols(D):
    kv0 = NSA_WIDTH
    g0 = kv0 + 6 * KV_GROUPS * HEAD_DIM
    tail0 = g0 + 3 * NSA_HEADS
    edges = np.cumsum((0, 2 * GM_WIDTH, D, D))
    tail = [slice(int(a), int(b)) for a, b in zip(edges[:-1], edges[1:])]
    return slice(0, kv0), slice(kv0, g0), slice(g0, g0 + LANE), tail0, tail


def _mix_in_kernel(h_ref, shift_ref, scale_ref, g_ref, w_ref, cos_ref, sin_ref, oh_ref,
                   lng_ref, lnb_ref, ws_ref, bsx_ref, pb_ref,
                   qt_ref, kcmp_ref, vcmp_ref, kselx_ref, vselt_ref, kwinx_ref, vwint_ref,
                   gt_ref, ga_ref, mb_ref, wt_scr):
    tm = h_ref.shape[1]
    CK = vselt_ref.shape[4]
    c_q, c_kv, c_g, tail0, (c_uv, c_ga, c_gb) = _mix_cols(h_ref.shape[2])

    @pl.when((pl.program_id(0) == 0) & (pl.program_id(1) == 0))
    def _():
        wt_scr[...] = w_ref[:, tail0:tail0 + wt_scr.shape[1]]

    n = _rms_mod(h_ref[0], g_ref[...], shift_ref[0], scale_ref[0]).astype(BF16)
    cos = cos_ref[...]
    sin = sin_ref[...]
    low_lanes = lax.broadcasted_iota(jnp.int32, (tm, LANE), 1) < HEAD_DIM

    q = jnp.dot(n, w_ref[:, c_q], preferred_element_type=F32) * (HEAD_DIM ** -0.5 * LOG2E)
    for p in range(NSA_HEADS // 2):
        qp = jnp.transpose(_rope_pair(q[:, p * LANE:(p + 1) * LANE], cos, sin)).astype(BF16)
        qt_ref[0, 2 * p] = qp[:HEAD_DIM]
        qt_ref[0, 2 * p + 1] = qp[HEAD_DIM:]

    kv = jnp.dot(n, w_ref[:, c_kv], preferred_element_type=F32)
    kcmp_ref[0] = kv[:, 0 * LANE:1 * LANE]
    vcmp_ref[0] = kv[:, 1 * LANE:2 * LANE]
    for idx, ref, ext in ((2, kselx_ref, oh_ref[...]), (4, kwinx_ref, 0.0)):
        k2 = _rope_pair(kv[:, idx * LANE:(idx + 1) * LANE], cos, sin)
        ref[0, 0] = jnp.where(low_lanes, k2, ext).astype(BF16)
        ref[0, 1] = jnp.where(low_lanes, pltpu.roll(k2, HEAD_DIM, 1), ext).astype(BF16)
    ones_row = jnp.where(lax.broadcasted_iota(jnp.int32, (V_ROWS - HEAD_DIM, CK), 0) == 0,
                         1.0, 0.0).astype(BF16)
    for idx, ref in ((3, vselt_ref), (5, vwint_ref)):
        vt = jnp.transpose(kv[:, idx * LANE:(idx + 1) * LANE]).astype(BF16)
        for g in range(KV_GROUPS):
            for r in range(tm // CK):
                ref[0, g, r] = jnp.concatenate(
                    [vt[g * HEAD_DIM:(g + 1) * HEAD_DIM, r * CK:(r + 1) * CK], ones_row], axis=0)

    gates = _sigmoid(jnp.dot(n, w_ref[:, c_g], preferred_element_type=F32))
    gt_ref[0] = jnp.transpose(gates)[:GATE_ROWS]
    ga_ref[0] = _sigmoid(jnp.dot(n, wt_scr[:, c_ga], preferred_element_type=F32))

    uv = jnp.dot(n, wt_scr[:, c_uv], preferred_element_type=F32)
    ge = uv * (0.5 * (1.0 + jnp.tanh(GELU_C * (uv + 0.044715 * (uv * uv * uv)))))
    u = ge[:, :GM_WIDTH]
    v = ge[:, GM_WIDTH:]
    mu = jnp.mean(v, axis=-1, keepdims=True)
    var = jnp.mean(jnp.square(v - mu), axis=-1, keepdims=True)
    vln = ((v - mu) * lax.rsqrt(var + EPS) * lng_ref[...] + lnb_ref[...]).astype(BF16)

    ti = lax.broadcasted_iota(jnp.int32, (GM_CHUNK, GM_CHUNK), 0)
    si = lax.broadcasted_iota(jnp.int32, (GM_CHUNK, GM_CHUNK), 1)
    tril = si <= ti
    wm = [jnp.where(tril, ws_ref[gg], 0.0).astype(BF16) for gg in range(GM_GROUPS)]
    low = lax.broadcasted_iota(jnp.int32, (GM_CHUNK, LANE), 1) < HEAD_DIM
    bsx = bsx_ref[...]
    yb_rows = []
    for r in range(tm // GM_CHUNK):
        vch = vln[r * GM_CHUNK:(r + 1) * GM_CHUNK]
        pieces = []
        for p in range(GM_GROUPS // 2):
            vp = vch[:, p * LANE:(p + 1) * LANE]
            a0 = jnp.dot(wm[2 * p], vp, preferred_element_type=F32)
            a1 = jnp.dot(wm[2 * p + 1], vp, preferred_element_type=F32)
            pieces.append(jnp.where(low, a0, a1))
        sv = jnp.concatenate(pieces, axis=1) + bsx
        yb_rows.append(u[r * GM_CHUNK:(r + 1) * GM_CHUNK] * sv)
    yb = jnp.concatenate(yb_rows, axis=0).astype(BF16)
    gb = _sigmoid(jnp.dot(n, wt_scr[:, c_gb], preferred_element_type=F32))
    mb_ref[0] = gb * jnp.dot(yb, pb_ref[...], preferred_element_type=F32)


def _mix_in(h, mods, l, g, w, cos_t, sin_t, oh_t, ln_g, ln_b, ws, bsx, proj_b):
    B, S, D = h.shape
    tail0, tail = _mix_cols(D)[3:]
    assert _arr(w).shape[-2:] == (D, tail0 + tail[-1].stop)
    tm = MIX_ROWS
    G = KV_GROUPS
    CK = ATT_CHUNK
    const = _whole
    row = lambda w: pl.BlockSpec((1, tm, w), lambda b, i: (b, i, 0))
    tab = pl.BlockSpec((tm, LANE), lambda b, i: (i, 0))
    kspec = pl.BlockSpec((1, G, tm, LANE), lambda b, i: (b, 0, i, 0))
    kshape = jax.ShapeDtypeStruct((B, G, S, LANE), BF16)
    vspec = pl.BlockSpec((1, G, tm // CK, V_ROWS, CK), lambda b, i: (b, 0, i, 0, 0))
    vshape = jax.ShapeDtypeStruct((B, G, S // CK, V_ROWS, CK), BF16)
    return pl.pallas_call(
        _mix_in_kernel,
        grid=(B, S // tm),
        in_specs=[row(D), _mod_vec(mods, l, 1, 0), _mod_vec(mods, l, 1, 1), const(g),
                  _whole(w, True), tab, tab, tab, const(ln_g), const(ln_b), const(ws),
                  const(bsx), const(proj_b)],
        out_specs=[
            pl.BlockSpec((1, NSA_HEADS, HEAD_DIM, tm), lambda b, i: (b, 0, 0, i)),
            row(LANE), row(LANE), kspec, vspec, kspec, vspec,
            pl.BlockSpec((1, GATE_ROWS, tm), lambda b, i: (b, 0, i)),
            row(D), row(D)],
        out_shape=[
            jax.ShapeDtypeStruct((B, NSA_HEADS, HEAD_DIM, S), BF16),
            jax.ShapeDtypeStruct((B, S, LANE), F32), jax.ShapeDtypeStruct((B, S, LANE), F32),
            kshape, vshape, kshape, vshape,
            jax.ShapeDtypeStruct((B, GATE_ROWS, S), F32),
            jax.ShapeDtypeStruct((B, S, D), F32), jax.ShapeDtypeStruct((B, S, D), F32)],
        scratch_shapes=[pltpu.VMEM((D, tail[-1].stop), BF16)],
        compiler_params=pltpu.CompilerParams(
            dimension_semantics=("arbitrary", "arbitrary"), vmem_limit_bytes=VMEM_LIMIT),
        name="mix_in",
    )(h, mods, mods, *map(_arr, (g, w, cos_t, sin_t, oh_t, ln_g, ln_b, ws, bsx, proj_b)))


def _compress_kernel(k_ref, v_ref, pe_ref, w1_ref, w2k_ref, w2v_ref, cos_ref, sin_ref,
                     kc_ref, vct_ref):
    NC = kc_ref.shape[2]
    half = CMP_BLOCK // 2

    def hidden(x_ref, j):
        ha = [None] * KV_GROUPS
        hb = [None] * KV_GROUPS
        for l in range(half):
            x = x_ref[0, pl.ds(l, NC, stride=CMP_STRIDE), :]
            for g in range(KV_GROUPS):
                xg = x[:, g * HEAD_DIM:(g + 1) * HEAD_DIM]
                a = jnp.dot((xg + pe_ref[j, l]).astype(BF16), w1_ref[j, l],
                            preferred_element_type=F32)
                b = jnp.dot((xg + pe_ref[j, half + l]).astype(BF16), w1_ref[j, half + l],
                            preferred_element_type=F32)
                ha[g] = a if ha[g] is None else ha[g] + a
                hb[g] = b if hb[g] is None else hb[g] + b
        return [_silu(ha[g] + pltpu.roll(hb[g], NC - 1, 0)) for g in range(KV_GROUPS)]

    hk = hidden(k_ref, 0)
    hv = hidden(v_ref, 1)
    for g in range(KV_GROUPS):
        k2 = jnp.dot(hk[g].astype(BF16), w2k_ref[...], preferred_element_type=F32)
        kc = k2[:, :HEAD_DIM] * cos_ref[...] + k2[:, HEAD_DIM:] * sin_ref[...]
        kc_ref[0, g] = kc.astype(BF16)
        vc = jnp.dot(hv[g].astype(BF16), w2v_ref[...],
                     preferred_element_type=F32)
        vct_ref[0, g] = jnp.transpose(vc)[:HEAD_DIM].astype(BF16)


def _compress(kcmp, vcmp, pe, w1, w2k, w2v, cos_c, sin_c):
    B, S, W = kcmp.shape
    NC = S // CMP_STRIDE
    const = _whole
    blk = pl.BlockSpec((1, S, W), lambda b: (b, 0, 0))
    return pl.pallas_call(
        _compress_kernel,
        grid=(B,),
        in_specs=[blk, blk, const(pe), const(w1), const(w2k), const(w2v),
                  const(cos_c), const(sin_c)],
        out_specs=[pl.BlockSpec((1, KV_GROUPS, NC, HEAD_DIM), lambda b: (b, 0, 0, 0)),
                   pl.BlockSpec((1, KV_GROUPS, HEAD_DIM, NC), lambda b: (b, 0, 0, 0))],
        out_shape=[jax.ShapeDtypeStruct((B, KV_GROUPS, NC, HEAD_DIM), BF16),
                   jax.ShapeDtypeStruct((B, KV_GROUPS, HEAD_DIM, NC), BF16)],
        compiler_params=pltpu.CompilerParams(
            dimension_semantics=("parallel",), vmem_limit_bytes=VMEM_LIMIT),
        name="compress",
    )(kcmp, vcmp, *map(_arr, (pe, w1, w2k, w2v, cos_c, sin_c)))


def _fold_rows(x, op):
    parts = [x[8 * i:8 * (i + 1)] for i in range(x.shape[0] // 8)]
    while len(parts) > 1:
        parts = [op(parts[i], parts[i + 1]) for i in range(0, len(parts), 2)]
    return parts[0]


def _attn_kernel(qt_ref, kselx_ref, vselt_ref, kwinx_ref, vwint_ref, kc_ref, vct_ref, gt_ref,
                 ga_ref, mb_ref, h_ref, gate_ref, ovt_ref, bias_ref, pa_ref, wo_ref,
                 o_ref, qx_scr, s_scr, m_scr, a_scr, acc_scr, yt_scr):
    TQ = h_ref.shape[1]
    CK = ATT_CHUNK
    NC = kc_ref.shape[2]
    NSEL = kselx_ref.shape[2] // SEL_BLOCK
    nwin = WINDOW // CK
    qi = pl.program_id(1)

    class Branch(NamedTuple):
        br: int
        kx_ref: object
        vt_ref: object
        first: object
        far: object
        gate_row: int

    def score(b, c, masked, dst):
        kx = [b.kx_ref[0, g, pl.ds(pl.multiple_of(c * CK, CK), CK), :]
              for g in range(KV_GROUPS)]
        if masked:
            bias = bias_ref[jnp.where(c == qi, 1, jnp.where(c == b.far, 2, 0))]
        for hd in range(NSA_HEADS):
            s = jnp.dot(kx[hd // HPG], qx_scr[hd], preferred_element_type=F32)
            if masked:
                s = s + bias
            s_scr[b.br, dst, hd] = s
            m_prev = m_scr[b.br, 1 - dst, hd]
            m_new = jnp.maximum(m_prev, jnp.max(_fold_rows(s, jnp.maximum), axis=0,
                                                keepdims=True))
            a_scr[b.br, dst, hd] = jnp.exp2(m_prev - m_new)
            m_scr[b.br, dst, hd] = m_new

    def weigh(b, c, src):
        vt = [b.vt_ref[0, g, c] for g in range(KV_GROUPS)]
        for hd in range(NSA_HEADS):
            p = jnp.exp2(s_scr[b.br, src, hd] - m_scr[b.br, src, hd]).astype(BF16)
            acc_scr[b.br, hd] = a_scr[b.br, src, hd] * acc_scr[b.br, hd] + jnp.dot(
                vt[hd // HPG], p, preferred_element_type=F32)

    def loop(lo, hi, body):
        def wrapped(c, carry):
            body(c)
            return carry
        lax.fori_loop(lo, hi, wrapped, 0)

    def middle(b):
        def two_chunks(i):
            c = b.first + 2 * i
            score(b, c + 1, False, 1)
            weigh(b, c, 0)
            score(b, c + 2, True, 0)
            weigh(b, c + 1, 1)

        def last_two(c):
            score(b, c + 1, True, 1)
            weigh(b, c, 0)

        rest = qi - b.first
        odd = rest & 1
        loop(0, jnp.right_shift(rest, 1), two_chunks)
        loop(qi - 1, qi - 1 + odd, last_two)
        return odd

    def finish(b):
        for hd in range(NSA_HEADS):
            w = gt_ref[0, 3 * hd + b.gate_row:3 * hd + b.gate_row + 1, :] / acc_scr[
                b.br, hd, HEAD_DIM:HEAD_DIM + 1, :]
            rows = slice(hd * HEAD_DIM, (hd + 1) * HEAD_DIM)
            yt_scr[rows, :] = yt_scr[rows, :] + w * acc_scr[b.br, hd, :HEAD_DIM, :]

    n_idx = lax.broadcasted_iota(jnp.int32, (NC, TQ), 0)
    t_cmp = qi * TQ + lax.broadcasted_iota(jnp.int32, (NC, TQ), 1)
    vis = (n_idx * CMP_STRIDE + (CMP_BLOCK - 1)) <= t_cmp
    visf = jnp.where(vis, 1.0, 0.0)
    j_idx = lax.broadcasted_iota(jnp.int32, (NSEL, TQ), 0)
    t_sel = qi * TQ + lax.broadcasted_iota(jnp.int32, (NSEL, TQ), 1)
    cur = jnp.right_shift(t_sel, SEL_BLOCK.bit_length() - 1)
    forced = (j_idx == 0) | (j_idx == cur) | (j_idx == cur - 1)
    valid = j_idx <= cur

    heads = range(NSA_HEADS)
    m_c = []
    for hd in heads:
        sc = jnp.where(vis, jnp.dot(kc_ref[0, hd // HPG], qt_ref[0, hd],
                                    preferred_element_type=F32), NEG)
        s_scr[0, 0, hd, :NC, :] = sc
        m_c.append(jnp.max(_fold_rows(sc, jnp.maximum), axis=0, keepdims=True))
    inv = []
    for hd in heads:
        ec = jnp.exp2(s_scr[0, 0, hd, :NC, :] - m_c[hd]) * visf
        s_scr[0, 0, hd, :NC, :] = ec
        den = jnp.sum(_fold_rows(ec, jnp.add), axis=0, keepdims=True)
        inv.append(1.0 / jnp.where(den > 0.0, den, 1.0))
    psum = [jnp.zeros((NC, TQ), F32) for _ in range(KV_GROUPS)]
    for hd in heads:
        g = hd // HPG
        pc = s_scr[0, 0, hd, :NC, :] * inv[hd]
        psum[g] = psum[g] + pc
        oc = jnp.dot(vct_ref[0, g], pc.astype(BF16), preferred_element_type=F32)
        yt_scr[hd * HEAD_DIM:(hd + 1) * HEAD_DIM, :] = gt_ref[0, 3 * hd:3 * hd + 1, :] * oc

    row8 = lax.broadcasted_iota(jnp.int32, (8, TQ), 0)
    pad = jnp.zeros((LANE - HEAD_DIM - NSEL, TQ), BF16)
    ranked = (qi + 1) * TQ > SEL_TOPK * SEL_BLOCK

    @pl.when(ranked)
    def _():
        for g in range(KV_GROUPS):
            imp = jnp.dot(ovt_ref[...], psum[g], preferred_element_type=F32,
                          precision=lax.Precision.HIGHEST)
            imp = jnp.where(forced, FORCE, jnp.where(valid, imp, -FORCE))
            tiles = [imp[8 * r:8 * (r + 1)] for r in range(NSEL // 8)]
            rank = [jnp.zeros((8, TQ), jnp.int32) for _ in tiles]
            for kk in range(NSEL):
                rk = imp[kk:kk + 1, :]
                for r, tile in enumerate(tiles):
                    if 8 * r > kk:
                        ahead = rk >= tile
                    elif 8 * r + 7 < kk:
                        ahead = rk > tile
                    else:
                        ahead = (rk > tile) | ((rk == tile) & (row8 > kk - 8 * r))
                    rank[r] = rank[r] + jnp.where(ahead, 1, 0)
            selneg = jnp.where(jnp.concatenate(rank, axis=0) < SEL_TOPK, 0.0,
                               NEG).astype(BF16)
            for hh in range(HPG):
                hd = g * HPG + hh
                qx_scr[hd] = jnp.concatenate([qt_ref[0, hd], selneg, pad], axis=0)

    @pl.when(jnp.logical_not(ranked))
    def _():
        for hd in heads:
            qx_scr[hd] = jnp.concatenate(
                [qt_ref[0, hd], jnp.zeros((LANE - HEAD_DIM, TQ), BF16)], axis=0)

    sel = Branch(0, kselx_ref, vselt_ref, 0, -1, 1)
    win = Branch(1, kwinx_ref, vwint_ref, jnp.maximum(qi - nwin, 0), qi - nwin, 2)
    m_scr[...] = jnp.full_like(m_scr, NEG)
    acc_scr[...] = jnp.zeros_like(acc_scr)
    loop(sel.first, sel.first + 1, lambda c: score(sel, c, True, 0))
    odd = middle(sel)

    def drain_sel_fill_win(src):
        def body(c):
            score(win, win.first, True, 0)
            weigh(sel, c, src)
        return body

    loop(qi, qi + odd, drain_sel_fill_win(1))
    loop(qi, qi + 1 - odd, drain_sel_fill_win(0))
    odd = middle(win)
    loop(qi, qi + odd, lambda c: weigh(win, c, 1))
    loop(qi, qi + 1 - odd, lambda c: weigh(win, c, 0))
    finish(sel)
    finish(win)

    y = jnp.transpose(yt_scr[...]).astype(BF16)
    ya = jnp.dot(y, pa_ref[...], preferred_element_type=F32)
    merged = (ga_ref[0] * ya + mb_ref[0]).astype(BF16)
    o_ref[0] = h_ref[0] + gate_ref[0] * jnp.dot(merged, wo_ref[...],
                                                preferred_element_type=F32)


def _attn(qt, kselx, vselt, kwinx, vwint, kc, vct, gt, ga, mb, h, mods, l, ovt, bias,
          proj_a, w_out):
    B, S, D = h.shape
    TQ = ATT_ROWS
    CK = ATT_CHUNK
    assert TQ == CK and WINDOW % CK == 0 and S % TQ == 0
    const = _whole

    def per_batch(a):
        nd = a.ndim
        return pl.BlockSpec((1,) + a.shape[1:], lambda b, i: (b,) + (0,) * (nd - 1))

    row = lambda w: pl.BlockSpec((1, TQ, w), lambda b, i: (b, i, 0))
    return pl.pallas_call(
        _attn_kernel,
        grid=(B, S // TQ),
        in_specs=[
            pl.BlockSpec((1, NSA_HEADS, HEAD_DIM, TQ), lambda b, i: (b, 0, 0, i)),
            per_batch(kselx), per_batch(vselt), per_batch(kwinx), per_batch(vwint),
            per_batch(kc), per_batch(vct),
            pl.BlockSpec((1, GATE_ROWS, TQ), lambda b, i: (b, 0, i)),
            row(D), row(D), row(D), _mod_vec(mods, l, 1, 2),
            const(ovt), const(bias), const(proj_a), const(w_out)],
        out_specs=row(D),
        out_shape=jax.ShapeDtypeStruct((B, S, D), F32),
        scratch_shapes=[
            pltpu.VMEM((NSA_HEADS, LANE, TQ), BF16),
            pltpu.VMEM((2, 2, NSA_HEADS, CK, TQ), F32),
            pltpu.VMEM((2, 2, NSA_HEADS, 1, TQ), F32), pltpu.VMEM((2, 2, NSA_HEADS, 1, TQ), F32),
            pltpu.VMEM((2, NSA_HEADS, V_ROWS, TQ), F32),
            pltpu.VMEM((NSA_WIDTH, TQ), F32)],
        compiler_params=pltpu.CompilerParams(
            dimension_semantics=("parallel", "parallel"), vmem_limit_bytes=VMEM_LIMIT),
        name="attn",
    )(qt, kselx, vselt, kwinx, vwint, kc, vct, gt, ga, mb, h, mods, ovt, bias,
      _arr(proj_a), _arr(w_out))


def _rope_tables(pos):
    inv = 1.0 / (ROPE_THETA ** (np.arange(0, HEAD_DIM, 2, dtype=np.float64) / HEAD_DIM))
    ang = np.asarray(pos, np.float64)[:, None] * inv[None, :]
    cos = np.concatenate([np.cos(ang), np.cos(ang)], axis=1)
    sin = np.concatenate([-np.sin(ang), np.sin(ang)], axis=1)
    return cos.astype(np.float32), sin.astype(np.float32)


def _tables(S):
    cos, sin = _rope_tables(np.arange(S))
    cos_t = np.concatenate([cos, cos], axis=1)
    sin_t = np.concatenate([sin, sin], axis=1)
    n_cmp_pad = S // CMP_STRIDE
    starts = np.arange(n_cmp_pad) * CMP_STRIDE
    cos_c, sin_c = _rope_tables(starts + CMP_BLOCK - 1)
    n_sel = S // SEL_BLOCK
    sel_start = np.arange(n_sel) * SEL_BLOCK
    overlap = np.clip(np.minimum(starts[:, None] + CMP_BLOCK, sel_start[None, :] + SEL_BLOCK)
                      - np.maximum(starts[:, None], sel_start[None, :]), 0, None) / CMP_BLOCK
    ovt = np.ascontiguousarray(overlap.T).astype(np.float32)
    oh = np.zeros((S, LANE), np.float32)
    oh[np.arange(S), HEAD_DIM + np.arange(S) // SEL_BLOCK] = 1.0
    j = np.arange(ATT_CHUNK)[:, None]
    i = np.arange(ATT_ROWS)[None, :]
    bias = np.stack([np.zeros((ATT_CHUNK, ATT_ROWS)),
                     np.where(j <= i, 0.0, NEG),
                     np.where(j > i, 0.0, NEG)]
                    ).astype(np.float32)
    return tuple(jnp.asarray(a) for a in (cos_t, sin_t, cos_c, sin_c, ovt, oh, bias))


def kernel(x, c, ada_w, ada_b, norm_g, ffn_w_in, ffn_w_out, mix_w_in, cmp_pe, cmp_w1, cmp_w2,
           gm_ln_g, gm_ln_b, gm_ws, gm_bs, proj_a, proj_b, w_out, final_g):
    B, S, D = x.shape
    L = ada_w.shape[0]
    assert S // SEL_BLOCK + HEAD_DIM <= LANE and 3 * NSA_HEADS <= GATE_ROWS
    cos_t, sin_t, cos_c, sin_c, ovt, oh_t, bias = _tables(S)
    swap = np.concatenate([np.arange(HEAD_DIM // 2, HEAD_DIM), np.arange(HEAD_DIM // 2)])

    w_fi, w_fo = ffn_w_in.astype(BF16), ffn_w_out.astype(BF16)
    w_mix = mix_w_in.astype(BF16)
    w_pa, w_pb, w_o = proj_a.astype(BF16), proj_b.astype(BF16), w_out.astype(BF16)
    w1 = cmp_w1.astype(BF16)
    pe = cmp_pe[:, :, :, None, :]
    w2k = jnp.concatenate([cmp_w2[:, 0], cmp_w2[:, 0][..., swap]], axis=-1).astype(BF16)
    w2v = jnp.pad(cmp_w2[:, 1], ((0, 0), (0, 0), (0, LANE - HEAD_DIM))).astype(BF16)
    bsx = jnp.repeat(jnp.swapaxes(gm_bs, 1, 2), HEAD_DIM, axis=2)

    gains = norm_g[:, :, None, :]
    ln_g, ln_b = gm_ln_g[:, None, :], gm_ln_b[:, None, :]

    mods = _ada(c, ada_w, ada_b).reshape(L, B, 3, 3, 1, D)
    h = x
    for l in range(L):
        h = _ffn(h, mods, l, 0, _Pick(gains, (l, 0)), _Pick(w_fi, (l, 0)), _Pick(w_fo, (l, 0)))
        qt, kcmp, vcmp, kselx, vselt, kwinx, vwint, gt, ga, mb = _mix_in(
            h, mods, l, _Pick(gains, (l, 1)), _Pick(w_mix, (l,)), cos_t, sin_t, oh_t,
            _Pick(ln_g, (l,)), _Pick(ln_b, (l,)), _Pick(gm_ws, (l,)), _Pick(bsx, (l,)),
            _Pick(w_pb, (l,)))
        kc, vct = _compress(kcmp, vcmp, _Pick(pe, (l,)), _Pick(w1, (l,)), _Pick(w2k, (l,)),
                            _Pick(w2v, (l,)), cos_c, sin_c)
        h = _attn(qt, kselx, vselt, kwinx, vwint, kc, vct, gt, ga, mb, h, mods, l, ovt, bias,
                  _Pick(w_pa, (l,)), _Pick(w_o, (l,)))
        h = _ffn(h, mods, l, 2, _Pick(gains, (l, 2)), _Pick(w_fi, (l, 1)), _Pick(w_fo, (l, 1)),
                 final_g=final_g[None] if l == L - 1 else None)
    return h
```

```python
import functools
from typing import NamedTuple

import numpy as np
import jax
import jax.numpy as jnp
from jax import lax
from jax.experimental import pallas as pl
from jax.experimental.pallas import tpu as pltpu

F32 = jnp.float32
BF16 = jnp.bfloat16

HEAD_DIM = 64
NSA_HEADS = 8
KV_GROUPS = 2
HPG = NSA_HEADS // KV_GROUPS
NSA_WIDTH = NSA_HEADS * HEAD_DIM
CMP_BLOCK = 32
CMP_STRIDE = 16
CMP_HIDDEN = 128
SEL_BLOCK = 64
SEL_TOPK = 16
WINDOW = 512
GM_GROUPS = 8
GM_CHUNK = 128
GM_WIDTH = GM_GROUPS * HEAD_DIM
ROPE_THETA = 10000.0
EPS = 1e-6
NEG = -1e30
FORCE = 1e4
LANE = 128
GELU_C = float(np.sqrt(2.0 / np.pi))
LOG2E = float(np.log2(np.e))
GATE_ROWS = 32
V_ROWS = HEAD_DIM + 16

FFN_ROWS = 512
FFN_SPLIT = 11
MIX_ROWS = 512
ATT_ROWS = 256
ATT_CHUNK = 256
V7X_VMEM_BYTES = 64 * 1024 * 1024
VMEM_LIMIT = V7X_VMEM_BYTES - 8 * 1024 * 1024


def _sigmoid(x):
    return 1.0 / (1.0 + jnp.exp(-x))


def _silu(x):
    return x * _sigmoid(x)


def _rms_mod(x, g, shift, scale):
    y = x * lax.rsqrt(jnp.mean(x * x, axis=-1, keepdims=True) + EPS) * g
    return y * (1.0 + scale) + shift


class _Pick(NamedTuple):
    array: jax.Array
    idx: tuple


def _arr(x):
    return x.array if isinstance(x, _Pick) else x


def _whole(x, single_buffer=False):
    idx = x.idx if isinstance(x, _Pick) else ()
    shape = _arr(x).shape
    mode = dict(pipeline_mode=pl.Buffered(1)) if single_buffer else {}
    return pl.BlockSpec((None,) * len(idx) + shape[len(idx):],
                        lambda *_: idx + (0,) * (len(shape) - len(idx)), **mode)


def _mod_vec(mods, l, sub, k):
    return pl.BlockSpec((None, 1, None, None, 1, mods.shape[-1]),
                        lambda b, *_: (l, b, sub, k, 0, 0))


def _ada_kernel(c_ref, w_ref, b_ref, o_ref):
    s = _silu(c_ref[...])
    o_ref[0] = jnp.dot(s, w_ref[0], preferred_element_type=F32,
                       precision=lax.Precision.HIGHEST) + b_ref[0]


def _ada(c, ada_w, ada_b):
    L, D, N = ada_w.shape
    B = c.shape[0]
    tn = 1024
    return pl.pallas_call(
        _ada_kernel,
        grid=(L, N // tn),
        in_specs=[
            pl.BlockSpec((B, D), lambda l, j: (0, 0)),
            pl.BlockSpec((1, D, tn), lambda l, j: (l, 0, j)),
            pl.BlockSpec((1, 1, tn), lambda l, j: (l, 0, j)),
        ],
        out_specs=pl.BlockSpec((1, B, tn), lambda l, j: (l, 0, j)),
        out_shape=jax.ShapeDtypeStruct((L, B, N), F32),
        compiler_params=pltpu.CompilerParams(
            dimension_semantics=("parallel", "parallel"), vmem_limit_bytes=VMEM_LIMIT),
        name="ada",
    )(c, ada_w, ada_b.reshape(L, 1, N))


def _ffn_kernel(h_ref, shift_ref, scale_ref, gate_ref, g_ref, wi_ref, wo_ref, *rest,
                final_norm):
    fg_ref = rest[0] if final_norm else None
    o_ref = rest[-1]
    F = wo_ref.shape[0]
    tf = F // FFN_SPLIT
    x = h_ref[0]
    n = _rms_mod(x, g_ref[...], shift_ref[0], scale_ref[0]).astype(BF16)
    acc = None
    for j in range(FFN_SPLIT):
        a = jnp.dot(n, wi_ref[:, j * tf:(j + 1) * tf], preferred_element_type=F32)
        b = jnp.dot(n, wi_ref[:, F + j * tf:F + (j + 1) * tf], preferred_element_type=F32)
        hm = (_silu(a) * b).astype(BF16)
        part = jnp.dot(hm, wo_ref[j * tf:(j + 1) * tf, :], preferred_element_type=F32)
        acc = part if acc is None else acc + part
    y = x + (0.5 * gate_ref[0]) * acc
    if final_norm:
        y = y * lax.rsqrt(jnp.mean(y * y, axis=-1, keepdims=True) + EPS) * fg_ref[...]
    o_ref[0] = y


def _ffn(h, mods, l, sub, g, w_in, w_out, final_g=None):
    B, S, D = h.shape
    tm = FFN_ROWS
    row = pl.BlockSpec((1, tm, D), lambda b, i: (b, i, 0))
    extra = () if final_g is None else (final_g,)
    return pl.pallas_call(
        functools.partial(_ffn_kernel, final_norm=final_g is not None),
        grid=(B, S // tm),
        in_specs=[row] + [_mod_vec(mods, l, sub, k) for k in range(3)]
        + [_whole(g), _whole(w_in, True), _whole(w_out, True)] + [_whole(e) for e in extra],
        out_specs=row,
        out_shape=jax.ShapeDtypeStruct((B, S, D), F32),
        compiler_params=pltpu.CompilerParams(
            dimension_semantics=("parallel", "parallel"), vmem_limit_bytes=VMEM_LIMIT),
        name="ffn",
    )(h, mods, mods, mods, _arr(g), _arr(w_in), _arr(w_out), *extra)


def _rope_pair(x, c, s):
    lane = lax.broadcasted_iota(jnp.int32, x.shape, 1)
    first_half = (lane & (HEAD_DIM - 1)) < HEAD_DIM // 2
    swapped = jnp.where(first_half, pltpu.roll(x, LANE - HEAD_DIM // 2, 1),
                        pltpu.roll(x, HEAD_DIM // 2, 1))
    return x * c + swapped * s


def _mix_cols(D):
    kv0 = NSA_WIDTH
    g0 = kv0 + 6 * KV_GROUPS * HEAD_DIM
    tail0 = g0 + 3 * NSA_HEADS
    edges = np.cumsum((0, 2 * GM_WIDTH, D, D))
    tail = [slice(int(a), int(b)) for a, b in zip(edges[:-1], edges[1:])]
    return slice(0, kv0), slice(kv0, g0), slice(g0, g0 + LANE), tail0, tail


def _mix_in_kernel(h_ref, shift_ref, scale_ref, g_ref, w_ref, cos_ref, sin_ref, oh_ref,
                   lng_ref, lnb_ref, ws_ref, bsx_ref, pb_ref,
                   qt_ref, kcmp_ref, vcmp_ref, kselx_ref, vselt_ref, kwinx_ref, vwint_ref,
                   gt_ref, ga_ref, mb_ref, wt_scr):
    tm = h_ref.shape[1]
    CK = vselt_ref.shape[4]
    c_q, c_kv, c_g, tail0, (c_uv, c_ga, c_gb) = _mix_cols(h_ref.shape[2])

    @pl.when((pl.program_id(0) == 0) & (pl.program_id(1) == 0))
    def _():
        wt_scr[...] = w_ref[:, tail0:tail0 + wt_scr.shape[1]]

    n = _rms_mod(h_ref[0], g_ref[...], shift_ref[0], scale_ref[0]).astype(BF16)
    cos = cos_ref[...]
    sin = sin_ref[...]
    low_lanes = lax.broadcasted_iota(jnp.int32, (tm, LANE), 1) < HEAD_DIM

    uv = jnp.dot(n, wt_scr[:, c_uv], preferred_element_type=F32)
    ge = uv * (0.5 * (1.0 + jnp.tanh(GELU_C * (uv + 0.044715 * (uv * uv * uv)))))
    u = ge[:, :GM_WIDTH]
    v = ge[:, GM_WIDTH:]
    mu = jnp.mean(v, axis=-1, keepdims=True)
    var = jnp.mean(jnp.square(v - mu), axis=-1, keepdims=True)
    vln = ((v - mu) * lax.rsqrt(var + EPS) * lng_ref[...] + lnb_ref[...]).astype(BF16)

    q = jnp.dot(n, w_ref[:, c_q], preferred_element_type=F32) * (HEAD_DIM ** -0.5 * LOG2E)
    for p in range(NSA_HEADS // 2):
        qp = jnp.transpose(_rope_pair(q[:, p * LANE:(p + 1) * LANE], cos, sin)).astype(BF16)
        qt_ref[0, 2 * p] = qp[:HEAD_DIM]
        qt_ref[0, 2 * p + 1] = qp[HEAD_DIM:]

    kv = jnp.dot(n, w_ref[:, c_kv], preferred_element_type=F32)
    kcmp_ref[0] = kv[:, 0 * LANE:1 * LANE]
    vcmp_ref[0] = kv[:, 1 * LANE:2 * LANE]
    for idx, ref, ext in ((2, kselx_ref, oh_ref[...]), (4, kwinx_ref, 0.0)):
        k2 = _rope_pair(kv[:, idx * LANE:(idx + 1) * LANE], cos, sin)
        ref[0, 0] = jnp.where(low_lanes, k2, ext).astype(BF16)
        ref[0, 1] = jnp.where(low_lanes, pltpu.roll(k2, HEAD_DIM, 1), ext).astype(BF16)
    ones_row = jnp.where(lax.broadcasted_iota(jnp.int32, (V_ROWS - HEAD_DIM, CK), 0) == 0,
                         1.0, 0.0).astype(BF16)
    for idx, ref in ((3, vselt_ref), (5, vwint_ref)):
        vt = jnp.transpose(kv[:, idx * LANE:(idx + 1) * LANE]).astype(BF16)
        for g in range(KV_GROUPS):
            for r in range(tm // CK):
                ref[0, g, r] = jnp.concatenate(
                    [vt[g * HEAD_DIM:(g + 1) * HEAD_DIM, r * CK:(r + 1) * CK], ones_row], axis=0)

    gates = _sigmoid(jnp.dot(n, w_ref[:, c_g], preferred_element_type=F32))
    gt_ref[0] = jnp.transpose(gates)[:GATE_ROWS]
    ga_ref[0] = _sigmoid(jnp.dot(n, wt_scr[:, c_ga], preferred_element_type=F32))

    ti = lax.broadcasted_iota(jnp.int32, (GM_CHUNK, GM_CHUNK), 0)
    si = lax.broadcasted_iota(jnp.int32, (GM_CHUNK, GM_CHUNK), 1)
    tril = si <= ti
    wm = [jnp.where(tril, ws_ref[gg], 0.0).astype(BF16) for gg in range(GM_GROUPS)]
    low = lax.broadcasted_iota(jnp.int32, (GM_CHUNK, LANE), 1) < HEAD_DIM
    bsx = bsx_ref[...]
    yb_rows = []
    for r in range(tm // GM_CHUNK):
        vch = vln[r * GM_CHUNK:(r + 1) * GM_CHUNK]
        pieces = []
        for p in range(GM_GROUPS // 2):
            vp = vch[:, p * LANE:(p + 1) * LANE]
            a0 = jnp.dot(wm[2 * p], vp, preferred_element_type=F32)
            a1 = jnp.dot(wm[2 * p + 1], vp, preferred_element_type=F32)
            pieces.append(jnp.where(low, a0, a1))
        sv = jnp.concatenate(pieces, axis=1) + bsx
        yb_rows.append(u[r * GM_CHUNK:(r + 1) * GM_CHUNK] * sv)
    yb = jnp.concatenate(yb_rows, axis=0).astype(BF16)
    gb = _sigmoid(jnp.dot(n, wt_scr[:, c_gb], preferred_element_type=F32))
    mb_ref[0] = gb * jnp.dot(yb, pb_ref[...], preferred_element_type=F32)


def _mix_in(h, mods, l, g, w, cos_t, sin_t, oh_t, ln_g, ln_b, ws, bsx, proj_b):
    B, S, D = h.shape
    tail0, tail = _mix_cols(D)[3:]
    assert _arr(w).shape[-2:] == (D, tail0 + tail[-1].stop)
    tm = MIX_ROWS
    G = KV_GROUPS
    CK = ATT_CHUNK
    const = _whole
    row = lambda w: pl.BlockSpec((1, tm, w), lambda b, i: (b, i, 0))
    tab = pl.BlockSpec((tm, LANE), lambda b, i: (i, 0))
    kspec = pl.BlockSpec((1, G, tm, LANE), lambda b, i: (b, 0, i, 0))
    kshape = jax.ShapeDtypeStruct((B, G, S, LANE), BF16)
    vspec = pl.BlockSpec((1, G, tm // CK, V_ROWS, CK), lambda b, i: (b, 0, i, 0, 0))
    vshape = jax.ShapeDtypeStruct((B, G, S // CK, V_ROWS, CK), BF16)
    return pl.pallas_call(
        _mix_in_kernel,
        grid=(B, S // tm),
        in_specs=[row(D), _mod_vec(mods, l, 1, 0), _mod_vec(mods, l, 1, 1), const(g),
                  _whole(w, True), tab, tab, tab, const(ln_g), const(ln_b), const(ws),
                  const(bsx), const(proj_b)],
        out_specs=[
            pl.BlockSpec((1, NSA_HEADS, HEAD_DIM, tm), lambda b, i: (b, 0, 0, i)),
            row(LANE), row(LANE), kspec, vspec, kspec, vspec,
            pl.BlockSpec((1, GATE_ROWS, tm), lambda b, i: (b, 0, i)),
            row(D), row(D)],
        out_shape=[
            jax.ShapeDtypeStruct((B, NSA_HEADS, HEAD_DIM, S), BF16),
            jax.ShapeDtypeStruct((B, S, LANE), F32), jax.ShapeDtypeStruct((B, S, LANE), F32),
            kshape, vshape, kshape, vshape,
            jax.ShapeDtypeStruct((B, GATE_ROWS, S), F32),
            jax.ShapeDtypeStruct((B, S, D), F32), jax.ShapeDtypeStruct((B, S, D), F32)],
        scratch_shapes=[pltpu.VMEM((D, tail[-1].stop), BF16)],
        compiler_params=pltpu.CompilerParams(
            dimension_semantics=("arbitrary", "arbitrary"), vmem_limit_bytes=VMEM_LIMIT),
        name="mix_in",
    )(h, mods, mods, *map(_arr, (g, w, cos_t, sin_t, oh_t, ln_g, ln_b, ws, bsx, proj_b)))


def _compress_kernel(k_ref, v_ref, pe_ref, w1_ref, w2k_ref, w2v_ref, cos_ref, sin_ref,
                     kc_ref, vct_ref):
    NC = kc_ref.shape[2]
    half = CMP_BLOCK // 2

    def hidden(x_ref, j):
        ha = [None] * KV_GROUPS
        hb = [None] * KV_GROUPS
        for l in range(half):
            x = x_ref[0, pl.ds(l, NC, stride=CMP_STRIDE), :]
            for g in range(KV_GROUPS):
                xg = x[:, g * HEAD_DIM:(g + 1) * HEAD_DIM]
                a = jnp.dot((xg + pe_ref[j, l]).astype(BF16), w1_ref[j, l],
                            preferred_element_type=F32)
                b = jnp.dot((xg + pe_ref[j, half + l]).astype(BF16), w1_ref[j, half + l],
                            preferred_element_type=F32)
                ha[g] = a if ha[g] is None else ha[g] + a
                hb[g] = b if hb[g] is None else hb[g] + b
        return [_silu(ha[g] + pltpu.roll(hb[g], NC - 1, 0)) for g in range(KV_GROUPS)]

    hk = hidden(k_ref, 0)
    hv = hidden(v_ref, 1)
    for g in range(KV_GROUPS):
        k2 = jnp.dot(hk[g].astype(BF16), w2k_ref[...], preferred_element_type=F32)
        kc = k2[:, :HEAD_DIM] * cos_ref[...] + k2[:, HEAD_DIM:] * sin_ref[...]
        kc_ref[0, g] = kc.astype(BF16)
        vc = jnp.dot(hv[g].astype(BF16), w2v_ref[...],
                     preferred_element_type=F32)
        vct_ref[0, g] = jnp.transpose(vc)[:HEAD_DIM].astype(BF16)


def _compress(kcmp, vcmp, pe, w1, w2k, w2v, cos_c, sin_c):
    B, S, W = kcmp.shape
    NC = S // CMP_STRIDE
    const = _whole
    blk = pl.BlockSpec((1, S, W), lambda b: (b, 0, 0))
    return pl.pallas_call(
        _compress_kernel,
        grid=(B,),
        in_specs=[blk, blk, const(pe), const(w1), const(w2k), const(w2v),
                  const(cos_c), const(sin_c)],
        out_specs=[pl.BlockSpec((1, KV_GROUPS, NC, HEAD_DIM), lambda b: (b, 0, 0, 0)),
                   pl.BlockSpec((1, KV_GROUPS, HEAD_DIM, NC), lambda b: (b, 0, 0, 0))],
        out_shape=[jax.ShapeDtypeStruct((B, KV_GROUPS, NC, HEAD_DIM), BF16),
                   jax.ShapeDtypeStruct((B, KV_GROUPS, HEAD_DIM, NC), BF16)],
        compiler_params=pltpu.CompilerParams(
            dimension_semantics=("parallel",), vmem_limit_bytes=VMEM_LIMIT),
        name="compress",
    )(kcmp, vcmp, *map(_arr, (pe, w1, w2k, w2v, cos_c, sin_c)))


def _fold_rows(x, op):
    parts = [x[8 * i:8 * (i + 1)] for i in range(x.shape[0] // 8)]
    while len(parts) > 1:
        parts = [op(parts[i], parts[i + 1]) for i in range(0, len(parts), 2)]
    return parts[0]


def _attn_kernel(qt_ref, kselx_ref, vselt_ref, kwinx_ref, vwint_ref, kc_ref, vct_ref, gt_ref,
                 ga_ref, mb_ref, h_ref, gate_ref, ovt_ref, bias_ref, pa_ref, wo_ref,
                 o_ref, qx_scr, s_scr, m_scr, a_scr, acc_scr, yt_scr):
    TQ = h_ref.shape[1]
    CK = ATT_CHUNK
    NC = kc_ref.shape[2]
    NSEL = kselx_ref.shape[2] // SEL_BLOCK
    nwin = WINDOW // CK
    qi = pl.program_id(1)

    class Branch(NamedTuple):
        br: int
        kx_ref: object
        vt_ref: object
        first: object
        far: object
        gate_row: int

    def score(b, c, masked, dst):
        kx = [b.kx_ref[0, g, pl.ds(pl.multiple_of(c * CK, CK), CK), :]
              for g in range(KV_GROUPS)]
        if masked:
            bias = bias_ref[jnp.where(c == qi, 1, jnp.where(c == b.far, 2, 0))]
        for hd in range(NSA_HEADS):
            s = jnp.dot(kx[hd // HPG], qx_scr[hd], preferred_element_type=F32)
            if masked:
                s = s + bias
            s_scr[b.br, dst, hd] = s
            m_prev = m_scr[b.br, 1 - dst, hd]
            m_new = jnp.maximum(m_prev, jnp.max(_fold_rows(s, jnp.maximum), axis=0,
                                                keepdims=True))
            a_scr[b.br, dst, hd] = jnp.exp2(m_prev - m_new)
            m_scr[b.br, dst, hd] = m_new

    def weigh(b, c, src):
        vt = [b.vt_ref[0, g, c] for g in range(KV_GROUPS)]
        for hd in range(NSA_HEADS):
            p = jnp.exp2(s_scr[b.br, src, hd] - m_scr[b.br, src, hd]).astype(BF16)
            acc_scr[b.br, hd] = a_scr[b.br, src, hd] * acc_scr[b.br, hd] + jnp.dot(
                vt[hd // HPG], p, preferred_element_type=F32)

    def loop(lo, hi, body):
        def wrapped(c, carry):
            body(c)
            return carry
        lax.fori_loop(lo, hi, wrapped, 0)

    def middle(b):
        def two_chunks(i):
            c = b.first + 2 * i
            score(b, c + 1, False, 1)
            weigh(b, c, 0)
            score(b, c + 2, True, 0)
            weigh(b, c + 1, 1)

        def last_two(c):
            score(b, c + 1, True, 1)
            weigh(b, c, 0)

        rest = qi - b.first
        odd = rest & 1
        loop(0, jnp.right_shift(rest, 1), two_chunks)
        loop(qi - 1, qi - 1 + odd, last_two)
        return odd

    def finish(b):
        for hd in range(NSA_HEADS):
            w = gt_ref[0, 3 * hd + b.gate_row:3 * hd + b.gate_row + 1, :] / acc_scr[
                b.br, hd, HEAD_DIM:HEAD_DIM + 1, :]
            rows = slice(hd * HEAD_DIM, (hd + 1) * HEAD_DIM)
            yt_scr[rows, :] = yt_scr[rows, :] + w * acc_scr[b.br, hd, :HEAD_DIM, :]

    n_idx = lax.broadcasted_iota(jnp.int32, (NC, TQ), 0)
    t_cmp = qi * TQ + lax.broadcasted_iota(jnp.int32, (NC, TQ), 1)
    vis = (n_idx * CMP_STRIDE + (CMP_BLOCK - 1)) <= t_cmp
    visf = jnp.where(vis, 1.0, 0.0)
    j_idx = lax.broadcasted_iota(jnp.int32, (NSEL, TQ), 0)
    t_sel = qi * TQ + lax.broadcasted_iota(jnp.int32, (NSEL, TQ), 1)
    cur = jnp.right_shift(t_sel, SEL_BLOCK.bit_length() - 1)
    forced = (j_idx == 0) | (j_idx == cur) | (j_idx == cur - 1)
    valid = j_idx <= cur

    heads = range(NSA_HEADS)
    m_c = []
    for hd in heads:
        sc = jnp.where(vis, jnp.dot(kc_ref[0, hd // HPG], qt_ref[0, hd],
                                    preferred_element_type=F32), NEG)
        s_scr[0, 0, hd, :NC, :] = sc
        m_c.append(jnp.max(_fold_rows(sc, jnp.maximum), axis=0, keepdims=True))
    inv = []
    for hd in heads:
        ec = jnp.exp2(s_scr[0, 0, hd, :NC, :] - m_c[hd]) * visf
        s_scr[0, 0, hd, :NC, :] = ec
        den = jnp.sum(_fold_rows(ec, jnp.add), axis=0, keepdims=True)
        inv.append(1.0 / jnp.where(den > 0.0, den, 1.0))
    psum = [jnp.zeros((NC, TQ), F32) for _ in range(KV_GROUPS)]
    for hd in heads:
        g = hd // HPG
        pc = s_scr[0, 0, hd, :NC, :] * inv[hd]
        psum[g] = psum[g] + pc
        oc = jnp.dot(vct_ref[0, g], pc.astype(BF16), preferred_element_type=F32)
        yt_scr[hd * HEAD_DIM:(hd + 1) * HEAD_DIM, :] = gt_ref[0, 3 * hd:3 * hd + 1, :] * oc

    row8 = lax.broadcasted_iota(jnp.int32, (8, TQ), 0)
    pad = jnp.zeros((LANE - HEAD_DIM - NSEL, TQ), BF16)
    ranked = (qi + 1) * TQ > SEL_TOPK * SEL_BLOCK

    @pl.when(ranked)
    def _():
        for g in range(KV_GROUPS):
            imp = jnp.dot(ovt_ref[...], psum[g], preferred_element_type=F32,
                          precision=lax.Precision.HIGHEST)
            imp = jnp.where(forced, FORCE, jnp.where(valid, imp, -FORCE))
            tiles = [imp[8 * r:8 * (r + 1)] for r in range(NSEL // 8)]
            rank = [jnp.zeros((8, TQ), jnp.int32) for _ in tiles]
            for kk in range(NSEL):
                rk = imp[kk:kk + 1, :]
                for r, tile in enumerate(tiles):
                    if 8 * r > kk:
                        ahead = rk >= tile
                    elif 8 * r + 7 < kk:
                        ahead = rk > tile
                    else:
                        ahead = (rk > tile) | ((rk == tile) & (row8 > kk - 8 * r))
                    rank[r] = rank[r] + jnp.where(ahead, 1, 0)
            selneg = jnp.where(jnp.concatenate(rank, axis=0) < SEL_TOPK, 0.0,
                               NEG).astype(BF16)
            for hh in range(HPG):
                hd = g * HPG + hh
                qx_scr[hd] = jnp.concatenate([qt_ref[0, hd], selneg, pad], axis=0)

    @pl.when(jnp.logical_not(ranked))
    def _():
        for hd in heads:
            qx_scr[hd] = jnp.concatenate(
                [qt_ref[0, hd], jnp.zeros((LANE - HEAD_DIM, TQ), BF16)], axis=0)

    sel = Branch(0, kselx_ref, vselt_ref, 0, -1, 1)
    win = Branch(1, kwinx_ref, vwint_ref, jnp.maximum(qi - nwin, 0), qi - nwin, 2)
    m_scr[...] = jnp.full_like(m_scr, NEG)
    acc_scr[...] = jnp.zeros_like(acc_scr)
    loop(sel.first, sel.first + 1, lambda c: score(sel, c, True, 0))
    odd = middle(sel)

    def drain_sel_fill_win(src):
        def body(c):
            score(win, win.first, True, 0)
            weigh(sel, c, src)
        return body

    loop(qi, qi + odd, drain_sel_fill_win(1))
    loop(qi, qi + 1 - odd, drain_sel_fill_win(0))
    odd = middle(win)
    loop(qi, qi + odd, lambda c: weigh(win, c, 1))
    loop(qi, qi + 1 - odd, lambda c: weigh(win, c, 0))
    finish(sel)
    finish(win)

    y = jnp.transpose(yt_scr[...]).astype(BF16)
    ya = jnp.dot(y, pa_ref[...], preferred_element_type=F32)
    merged = (ga_ref[0] * ya + mb_ref[0]).astype(BF16)
    o_ref[0] = h_ref[0] + gate_ref[0] * jnp.dot(merged, wo_ref[...],
                                                preferred_element_type=F32)


def _attn(qt, kselx, vselt, kwinx, vwint, kc, vct, gt, ga, mb, h, mods, l, ovt, bias,
          proj_a, w_out):
    B, S, D = h.shape
    TQ = ATT_ROWS
    CK = ATT_CHUNK
    assert TQ == CK and WINDOW % CK == 0 and S % TQ == 0
    const = _whole

    def per_batch(a):
        nd = a.ndim
        return pl.BlockSpec((1,) + a.shape[1:], lambda b, i: (b,) + (0,) * (nd - 1))

    row = lambda w: pl.BlockSpec((1, TQ, w), lambda b, i: (b, i, 0))
    return pl.pallas_call(
        _attn_kernel,
        grid=(B, S // TQ),
        in_specs=[
            pl.BlockSpec((1, NSA_HEADS, HEAD_DIM, TQ), lambda b, i: (b, 0, 0, i)),
            per_batch(kselx), per_batch(vselt), per_batch(kwinx), per_batch(vwint),
            per_batch(kc), per_batch(vct),
            pl.BlockSpec((1, GATE_ROWS, TQ), lambda b, i: (b, 0, i)),
            row(D), row(D), row(D), _mod_vec(mods, l, 1, 2),
            const(ovt), const(bias), const(proj_a), const(w_out)],
        out_specs=row(D),
        out_shape=jax.ShapeDtypeStruct((B, S, D), F32),
        scratch_shapes=[
            pltpu.VMEM((NSA_HEADS, LANE, TQ), BF16),
            pltpu.VMEM((2, 2, NSA_HEADS, CK, TQ), F32),
            pltpu.VMEM((2, 2, NSA_HEADS, 1, TQ), F32), pltpu.VMEM((2, 2, NSA_HEADS, 1, TQ), F32),
            pltpu.VMEM((2, NSA_HEADS, V_ROWS, TQ), F32),
            pltpu.VMEM((NSA_WIDTH, TQ), F32)],
        compiler_params=pltpu.CompilerParams(
            dimension_semantics=("parallel", "parallel"), vmem_limit_bytes=VMEM_LIMIT),
        name="attn",
    )(qt, kselx, vselt, kwinx, vwint, kc, vct, gt, ga, mb, h, mods, ovt, bias,
      _arr(proj_a), _arr(w_out))


def _rope_tables(pos):
    inv = 1.0 / (ROPE_THETA ** (np.arange(0, HEAD_DIM, 2, dtype=np.float64) / HEAD_DIM))
    ang = np.asarray(pos, np.float64)[:, None] * inv[None, :]
    cos = np.concatenate([np.cos(ang), np.cos(ang)], axis=1)
    sin = np.concatenate([-np.sin(ang), np.sin(ang)], axis=1)
    return cos.astype(np.float32), sin.astype(np.float32)


def _tables(S):
    cos, sin = _rope_tables(np.arange(S))
    cos_t = np.concatenate([cos, cos], axis=1)
    sin_t = np.concatenate([sin, sin], axis=1)
    n_cmp_pad = S // CMP_STRIDE
    starts = np.arange(n_cmp_pad) * CMP_STRIDE
    cos_c, sin_c = _rope_tables(starts + CMP_BLOCK - 1)
    n_sel = S // SEL_BLOCK
    sel_start = np.arange(n_sel) * SEL_BLOCK
    overlap = np.clip(np.minimum(starts[:, None] + CMP_BLOCK, sel_start[None, :] + SEL_BLOCK)
                      - np.maximum(starts[:, None], sel_start[None, :]), 0, None) / CMP_BLOCK
    ovt = np.ascontiguousarray(overlap.T).astype(np.float32)
    oh = np.zeros((S, LANE), np.float32)
    oh[np.arange(S), HEAD_DIM + np.arange(S) // SEL_BLOCK] = 1.0
    j = np.arange(ATT_CHUNK)[:, None]
    i = np.arange(ATT_ROWS)[None, :]
    bias = np.stack([np.zeros((ATT_CHUNK, ATT_ROWS)),
                     np.where(j <= i, 0.0, NEG),
                     np.where(j > i, 0.0, NEG)]
                    ).astype(np.float32)
    return tuple(jnp.asarray(a) for a in (cos_t, sin_t, cos_c, sin_c, ovt, oh, bias))


def kernel(x, c, ada_w, ada_b, norm_g, ffn_w_in, ffn_w_out, mix_w_in, cmp_pe, cmp_w1, cmp_w2,
           gm_ln_g, gm_ln_b, gm_ws, gm_bs, proj_a, proj_b, w_out, final_g):
    B, S, D = x.shape
    L = ada_w.shape[0]
    assert S // SEL_BLOCK + HEAD_DIM <= LANE and 3 * NSA_HEADS <= GATE_ROWS
    cos_t, sin_t, cos_c, sin_c, ovt, oh_t, bias = _tables(S)
    swap = np.concatenate([np.arange(HEAD_DIM // 2, HEAD_DIM), np.arange(HEAD_DIM // 2)])

    w_fi, w_fo = ffn_w_in.astype(BF16), ffn_w_out.astype(BF16)
    w_mix = mix_w_in.astype(BF16)
    w_pa, w_pb, w_o = proj_a.astype(BF16), proj_b.astype(BF16), w_out.astype(BF16)
    w1 = cmp_w1.astype(BF16)
    pe = cmp_pe[:, :, :, None, :]
    w2k = jnp.concatenate([cmp_w2[:, 0], cmp_w2[:, 0][..., swap]], axis=-1).astype(BF16)
    w2v = jnp.pad(cmp_w2[:, 1], ((0, 0), (0, 0), (0, LANE - HEAD_DIM))).astype(BF16)
    bsx = jnp.repeat(jnp.swapaxes(gm_bs, 1, 2), HEAD_DIM, axis=2)

    gains = norm_g[:, :, None, :]
    ln_g, ln_b = gm_ln_g[:, None, :], gm_ln_b[:, None, :]

    mods = _ada(c, ada_w, ada_b).reshape(L, B, 3, 3, 1, D)
    h = x
    for l in range(L):
        h = _ffn(h, mods, l, 0, _Pick(gains, (l, 0)), _Pick(w_fi, (l, 0)), _Pick(w_fo, (l, 0)))
        qt, kcmp, vcmp, kselx, vselt, kwinx, vwint, gt, ga, mb = _mix_in(
            h, mods, l, _Pick(gains, (l, 1)), _Pick(w_mix, (l,)), cos_t, sin_t, oh_t,
            _Pick(ln_g, (l,)), _Pick(ln_b, (l,)), _Pick(gm_ws, (l,)), _Pick(bsx, (l,)),
            _Pick(w_pb, (l,)))
        kc, vct = _compress(kcmp, vcmp, _Pick(pe, (l,)), _Pick(w1, (l,)), _Pick(w2k, (l,)),
                            _Pick(w2v, (l,)), cos_c, sin_c)
        h = _attn(qt, kselx, vselt, kwinx, vwint, kc, vct, gt, ga, mb, h, mods, l, ovt, bias,
                  _Pick(w_pa, (l,)), _Pick(w_o, (l,)))
        h = _ffn(h, mods, l, 2, _Pick(gains, (l, 2)), _Pick(w_fi, (l, 1)), _Pick(w_fo, (l, 1)),
                 final_g=final_g[None] if l == L - 1 else None)
    return h
```

```python
import functools
from typing import NamedTuple

import numpy as np
import jax
import jax.numpy as jnp
from jax import lax
from jax.experimental import pallas as pl
from jax.experimental.pallas import tpu as pltpu

F32 = jnp.float32
BF16 = jnp.bfloat16

HEAD_DIM = 64
NSA_HEADS = 8
KV_GROUPS = 2
HPG = NSA_HEADS // KV_GROUPS
NSA_WIDTH = NSA_HEADS * HEAD_DIM
CMP_BLOCK = 32
CMP_STRIDE = 16
CMP_HIDDEN = 128
SEL_BLOCK = 64
SEL_TOPK = 16
WINDOW = 512
GM_GROUPS = 8
GM_CHUNK = 128
GM_WIDTH = GM_GROUPS * HEAD_DIM
ROPE_THETA = 10000.0
EPS = 1e-6
NEG = -1e30
FORCE = 1e4
LANE = 128
GELU_C = float(np.sqrt(2.0 / np.pi))
LOG2E = float(np.log2(np.e))
GATE_ROWS = 32
V_ROWS = HEAD_DIM + 16

FFN_ROWS = 512
FFN_SPLIT = 11
MIX_ROWS = 512
ATT_ROWS = 256
ATT_CHUNK = 256
ATT_HEAD_PART = 2
V7X_VMEM_BYTES = 64 * 1024 * 1024
VMEM_LIMIT = V7X_VMEM_BYTES - 8 * 1024 * 1024


def _sigmoid(x):
    return 1.0 / (1.0 + jnp.exp(-x))


def _silu(x):
    return x * _sigmoid(x)


def _rms_mod(x, g, shift, scale):
    y = x * lax.rsqrt(jnp.mean(x * x, axis=-1, keepdims=True) + EPS) * g
    return y * (1.0 + scale) + shift


class _Pick(NamedTuple):
    array: jax.Array
    idx: tuple


def _arr(x):
    return x.array if isinstance(x, _Pick) else x


def _whole(x, single_buffer=False):
    idx = x.idx if isinstance(x, _Pick) else ()
    shape = _arr(x).shape
    mode = dict(pipeline_mode=pl.Buffered(1)) if single_buffer else {}
    return pl.BlockSpec((None,) * len(idx) + shape[len(idx):],
                        lambda *_: idx + (0,) * (len(shape) - len(idx)), **mode)


def _mod_vec(mods, l, sub, k):
    return pl.BlockSpec((None, 1, None, None, 1, mods.shape[-1]),
                        lambda b, *_: (l, b, sub, k, 0, 0))


def _ada_kernel(c_ref, w_ref, b_ref, o_ref):
    s = _silu(c_ref[...])
    o_ref[0] = jnp.dot(s, w_ref[0], preferred_element_type=F32,
                       precision=lax.Precision.HIGHEST) + b_ref[0]


def _ada(c, ada_w, ada_b):
    L, D, N = ada_w.shape
    B = c.shape[0]
    tn = 1024
    return pl.pallas_call(
        _ada_kernel,
        grid=(L, N // tn),
        in_specs=[
            pl.BlockSpec((B, D), lambda l, j: (0, 0)),
            pl.BlockSpec((1, D, tn), lambda l, j: (l, 0, j)),
            pl.BlockSpec((1, 1, tn), lambda l, j: (l, 0, j)),
        ],
        out_specs=pl.BlockSpec((1, B, tn), lambda l, j: (l, 0, j)),
        out_shape=jax.ShapeDtypeStruct((L, B, N), F32),
        compiler_params=pltpu.CompilerParams(
            dimension_semantics=("parallel", "parallel"), vmem_limit_bytes=VMEM_LIMIT),
        name="ada",
    )(c, ada_w, ada_b.reshape(L, 1, N))


def _ffn_kernel(h_ref, shift_ref, scale_ref, gate_ref, g_ref, wi_ref, wo_ref, *rest,
                final_norm):
    fg_ref = rest[0] if final_norm else None
    o_ref = rest[-1]
    F = wo_ref.shape[0]
    tf = F // FFN_SPLIT
    x = h_ref[0]
    n = _rms_mod(x, g_ref[...], shift_ref[0], scale_ref[0]).astype(BF16)
    acc = None
    for j in range(FFN_SPLIT):
        a = jnp.dot(n, wi_ref[:, j * tf:(j + 1) * tf], preferred_element_type=F32)
        b = jnp.dot(n, wi_ref[:, F + j * tf:F + (j + 1) * tf], preferred_element_type=F32)
        hm = (_silu(a) * b).astype(BF16)
        part = jnp.dot(hm, wo_ref[j * tf:(j + 1) * tf, :], preferred_element_type=F32)
        acc = part if acc is None else acc + part
    y = x + (0.5 * gate_ref[0]) * acc
    if final_norm:
        y = y * lax.rsqrt(jnp.mean(y * y, axis=-1, keepdims=True) + EPS) * fg_ref[...]
    o_ref[0] = y


def _ffn(h, mods, l, sub, g, w_in, w_out, final_g=None):
    B, S, D = h.shape
    tm = FFN_ROWS
    row = pl.BlockSpec((1, tm, D), lambda b, i: (b, i, 0))
    extra = () if final_g is None else (final_g,)
    return pl.pallas_call(
        functools.partial(_ffn_kernel, final_norm=final_g is not None),
        grid=(B, S // tm),
        in_specs=[row] + [_mod_vec(mods, l, sub, k) for k in range(3)]
        + [_whole(g), _whole(w_in, True), _whole(w_out, True)] + [_whole(e) for e in extra],
        out_specs=row,
        out_shape=jax.ShapeDtypeStruct((B, S, D), F32),
        compiler_params=pltpu.CompilerParams(
            dimension_semantics=("parallel", "parallel"), vmem_limit_bytes=VMEM_LIMIT),
        name="ffn",
    )(h, mods, mods, mods, _arr(g), _arr(w_in), _arr(w_out), *extra)


def _rope_pair(x, c, s):
    lane = lax.broadcasted_iota(jnp.int32, x.shape, 1)
    first_half = (lane & (HEAD_DIM - 1)) < HEAD_DIM // 2
    swapped = jnp.where(first_half, pltpu.roll(x, LANE - HEAD_DIM // 2, 1),
                        pltpu.roll(x, HEAD_DIM // 2, 1))
    return x * c + swapped * s


def _mix_cols(D):
    kv0 = NSA_WIDTH
    g0 = kv0 + 6 * KV_GROUPS * HEAD_DIM
    tail0 = g0 + 3 * NSA_HEADS
    edges = np.cumsum((0, 2 * GM_WIDTH, D, D))
    tail = [slice(int(a), int(b)) for a, b in zip(edges[:-1], edges[1:])]
    return slice(0, kv0), slice(kv0, g0), slice(g0, g0 + LANE), tail0, tail


def _mix_in_kernel(h_ref, shift_ref, scale_ref, g_ref, w_ref, cos_ref, sin_ref, oh_ref,
                   lng_ref, lnb_ref, ws_ref, bsx_ref, pb_ref,
                   qt_ref, kcmp_ref, vcmp_ref, kselx_ref, vselt_ref, kwinx_ref, vwint_ref,
                   gt_ref, ga_ref, mb_ref, wt_scr):
    tm = h_ref.shape[1]
    CK = vselt_ref.shape[4]
    c_q, c_kv, c_g, tail0, (c_uv, c_ga, c_gb) = _mix_cols(h_ref.shape[2])

    @pl.when((pl.program_id(0) == 0) & (pl.program_id(1) == 0))
    def _():
        wt_scr[...] = w_ref[:, tail0:tail0 + wt_scr.shape[1]]

    n = _rms_mod(h_ref[0], g_ref[...], shift_ref[0], scale_ref[0]).astype(BF16)
    cos = cos_ref[...]
    sin = sin_ref[...]
    low_lanes = lax.broadcasted_iota(jnp.int32, (tm, LANE), 1) < HEAD_DIM

    uv = jnp.dot(n, wt_scr[:, c_uv], preferred_element_type=F32)
    ge = uv * (0.5 * (1.0 + jnp.tanh(GELU_C * (uv + 0.044715 * (uv * uv * uv)))))
    u = ge[:, :GM_WIDTH]
    v = ge[:, GM_WIDTH:]
    mu = jnp.mean(v, axis=-1, keepdims=True)
    var = jnp.mean(jnp.square(v - mu), axis=-1, keepdims=True)
    vln = ((v - mu) * lax.rsqrt(var + EPS) * lng_ref[...] + lnb_ref[...]).astype(BF16)

    q = jnp.dot(n, w_ref[:, c_q], preferred_element_type=F32) * (HEAD_DIM ** -0.5 * LOG2E)
    for p in range(NSA_HEADS // 2):
        qp = jnp.transpose(_rope_pair(q[:, p * LANE:(p + 1) * LANE], cos, sin)).astype(BF16)
        qt_ref[0, 2 * p] = qp[:HEAD_DIM]
        qt_ref[0, 2 * p + 1] = qp[HEAD_DIM:]

    kv = jnp.dot(n, w_ref[:, c_kv], preferred_element_type=F32)
    kcmp_ref[0] = kv[:, 0 * LANE:1 * LANE]
    vcmp_ref[0] = kv[:, 1 * LANE:2 * LANE]
    for idx, ref, ext in ((2, kselx_ref, oh_ref[...]), (4, kwinx_ref, 0.0)):
        k2 = _rope_pair(kv[:, idx * LANE:(idx + 1) * LANE], cos, sin)
        ref[0, 0] = jnp.where(low_lanes, k2, ext).astype(BF16)
        ref[0, 1] = jnp.where(low_lanes, pltpu.roll(k2, HEAD_DIM, 1), ext).astype(BF16)
    ones_row = jnp.where(lax.broadcasted_iota(jnp.int32, (V_ROWS - HEAD_DIM, CK), 0) == 0,
                         1.0, 0.0).astype(BF16)
    for idx, ref in ((3, vselt_ref), (5, vwint_ref)):
        vt = jnp.transpose(kv[:, idx * LANE:(idx + 1) * LANE]).astype(BF16)
        for g in range(KV_GROUPS):
            for r in range(tm // CK):
                ref[0, g, r] = jnp.concatenate(
                    [vt[g * HEAD_DIM:(g + 1) * HEAD_DIM, r * CK:(r + 1) * CK], ones_row], axis=0)

    gates = _sigmoid(jnp.dot(n, w_ref[:, c_g], preferred_element_type=F32))
    gt_ref[0] = jnp.transpose(gates)[:GATE_ROWS]
    ga_ref[0] = _sigmoid(jnp.dot(n, wt_scr[:, c_ga], preferred_element_type=F32))

    ti = lax.broadcasted_iota(jnp.int32, (GM_CHUNK, GM_CHUNK), 0)
    si = lax.broadcasted_iota(jnp.int32, (GM_CHUNK, GM_CHUNK), 1)
    tril = si <= ti
    wm = [jnp.where(tril, ws_ref[gg], 0.0).astype(BF16) for gg in range(GM_GROUPS)]
    low = lax.broadcasted_iota(jnp.int32, (GM_CHUNK, LANE), 1) < HEAD_DIM
    bsx = bsx_ref[...]
    yb_rows = []
    for r in range(tm // GM_CHUNK):
        vch = vln[r * GM_CHUNK:(r + 1) * GM_CHUNK]
        pieces = []
        for p in range(GM_GROUPS // 2):
            vp = vch[:, p * LANE:(p + 1) * LANE]
            a0 = jnp.dot(wm[2 * p], vp, preferred_element_type=F32)
            a1 = jnp.dot(wm[2 * p + 1], vp, preferred_element_type=F32)
            pieces.append(jnp.where(low, a0, a1))
        sv = jnp.concatenate(pieces, axis=1) + bsx
        yb_rows.append(u[r * GM_CHUNK:(r + 1) * GM_CHUNK] * sv)
    yb = jnp.concatenate(yb_rows, axis=0).astype(BF16)
    gb = _sigmoid(jnp.dot(n, wt_scr[:, c_gb], preferred_element_type=F32))
    mb_ref[0] = gb * jnp.dot(yb, pb_ref[...], preferred_element_type=F32)


def _mix_in(h, mods, l, g, w, cos_t, sin_t, oh_t, ln_g, ln_b, ws, bsx, proj_b):
    B, S, D = h.shape
    tail0, tail = _mix_cols(D)[3:]
    assert _arr(w).shape[-2:] == (D, tail0 + tail[-1].stop)
    tm = MIX_ROWS
    G = KV_GROUPS
    CK = ATT_CHUNK
    const = _whole
    row = lambda w: pl.BlockSpec((1, tm, w), lambda b, i: (b, i, 0))
    tab = pl.BlockSpec((tm, LANE), lambda b, i: (i, 0))
    kspec = pl.BlockSpec((1, G, tm, LANE), lambda b, i: (b, 0, i, 0))
    kshape = jax.ShapeDtypeStruct((B, G, S, LANE), BF16)
    vspec = pl.BlockSpec((1, G, tm // CK, V_ROWS, CK), lambda b, i: (b, 0, i, 0, 0))
    vshape = jax.ShapeDtypeStruct((B, G, S // CK, V_ROWS, CK), BF16)
    return pl.pallas_call(
        _mix_in_kernel,
        grid=(B, S // tm),
        in_specs=[row(D), _mod_vec(mods, l, 1, 0), _mod_vec(mods, l, 1, 1), const(g),
                  _whole(w, True), tab, tab, tab, const(ln_g), const(ln_b), const(ws),
                  const(bsx), const(proj_b)],
        out_specs=[
            pl.BlockSpec((1, NSA_HEADS, HEAD_DIM, tm), lambda b, i: (b, 0, 0, i)),
            row(LANE), row(LANE), kspec, vspec, kspec, vspec,
            pl.BlockSpec((1, GATE_ROWS, tm), lambda b, i: (b, 0, i)),
            row(D), row(D)],
        out_shape=[
            jax.ShapeDtypeStruct((B, NSA_HEADS, HEAD_DIM, S), BF16),
            jax.ShapeDtypeStruct((B, S, LANE), F32), jax.ShapeDtypeStruct((B, S, LANE), F32),
            kshape, vshape, kshape, vshape,
            jax.ShapeDtypeStruct((B, GATE_ROWS, S), F32),
            jax.ShapeDtypeStruct((B, S, D), F32), jax.ShapeDtypeStruct((B, S, D), F32)],
        scratch_shapes=[pltpu.VMEM((D, tail[-1].stop), BF16)],
        compiler_params=pltpu.CompilerParams(
            dimension_semantics=("arbitrary", "arbitrary"), vmem_limit_bytes=VMEM_LIMIT),
        name="mix_in",
    )(h, mods, mods, *map(_arr, (g, w, cos_t, sin_t, oh_t, ln_g, ln_b, ws, bsx, proj_b)))


def _compress_kernel(k_ref, v_ref, pe_ref, w1_ref, w2k_ref, w2v_ref, cos_ref, sin_ref,
                     kc_ref, vct_ref):
    NC = kc_ref.shape[2]
    half = CMP_BLOCK // 2

    def hidden(x_ref, j):
        ha = [None] * KV_GROUPS
        hb = [None] * KV_GROUPS
        for l in range(half):
            x = x_ref[0, pl.ds(l, NC, stride=CMP_STRIDE), :]
            for g in range(KV_GROUPS):
                xg = x[:, g * HEAD_DIM:(g + 1) * HEAD_DIM]
                a = jnp.dot((xg + pe_ref[j, l]).astype(BF16), w1_ref[j, l],
                            preferred_element_type=F32)
                b = jnp.dot((xg + pe_ref[j, half + l]).astype(BF16), w1_ref[j, half + l],
                            preferred_element_type=F32)
                ha[g] = a if ha[g] is None else ha[g] + a
                hb[g] = b if hb[g] is None else hb[g] + b
        return [_silu(ha[g] + pltpu.roll(hb[g], NC - 1, 0)) for g in range(KV_GROUPS)]

    hk = hidden(k_ref, 0)
    hv = hidden(v_ref, 1)
    for g in range(KV_GROUPS):
        k2 = jnp.dot(hk[g].astype(BF16), w2k_ref[...], preferred_element_type=F32)
        kc = k2[:, :HEAD_DIM] * cos_ref[...] + k2[:, HEAD_DIM:] * sin_ref[...]
        kc_ref[0, g] = kc.astype(BF16)
        vc = jnp.dot(hv[g].astype(BF16), w2v_ref[...],
                     preferred_element_type=F32)
        vct_ref[0, g] = jnp.transpose(vc)[:HEAD_DIM].astype(BF16)


def _compress(kcmp, vcmp, pe, w1, w2k, w2v, cos_c, sin_c):
    B, S, W = kcmp.shape
    NC = S // CMP_STRIDE
    const = _whole
    blk = pl.BlockSpec((1, S, W), lambda b: (b, 0, 0))
    return pl.pallas_call(
        _compress_kernel,
        grid=(B,),
        in_specs=[blk, blk, const(pe), const(w1), const(w2k), const(w2v),
                  const(cos_c), const(sin_c)],
        out_specs=[pl.BlockSpec((1, KV_GROUPS, NC, HEAD_DIM), lambda b: (b, 0, 0, 0)),
                   pl.BlockSpec((1, KV_GROUPS, HEAD_DIM, NC), lambda b: (b, 0, 0, 0))],
        out_shape=[jax.ShapeDtypeStruct((B, KV_GROUPS, NC, HEAD_DIM), BF16),
                   jax.ShapeDtypeStruct((B, KV_GROUPS, HEAD_DIM, NC), BF16)],
        compiler_params=pltpu.CompilerParams(
            dimension_semantics=("parallel",), vmem_limit_bytes=VMEM_LIMIT),
        name="compress",
    )(kcmp, vcmp, *map(_arr, (pe, w1, w2k, w2v, cos_c, sin_c)))


def _fold_rows(x, op):
    parts = [x[8 * i:8 * (i + 1)] for i in range(x.shape[0] // 8)]
    while len(parts) > 1:
        parts = [op(parts[i], parts[i + 1]) for i in range(0, len(parts), 2)]
    return parts[0]


def _attn_kernel(qt_ref, kselx_ref, vselt_ref, kwinx_ref, vwint_ref, kc_ref, vct_ref, gt_ref,
                 ga_ref, mb_ref, h_ref, gate_ref, ovt_ref, bias_ref, pa_ref, wo_ref,
                 o_ref, qx_scr, s_scr, m_scr, a_scr, acc_scr, yt_scr):
    TQ = h_ref.shape[1]
    CK = ATT_CHUNK
    NC = kc_ref.shape[2]
    NSEL = kselx_ref.shape[2] // SEL_BLOCK
    nwin = WINDOW // CK
    qi = pl.program_id(1)

    class Branch(NamedTuple):
        br: int
        kx_ref: object
        vt_ref: object
        first: object
        far: object
        gate_row: int

    def score(b, c, masked, dst, hds=range(NSA_HEADS)):
        kx = [b.kx_ref[0, g, pl.ds(pl.multiple_of(c * CK, CK), CK), :]
              for g in range(KV_GROUPS)]
        if masked:
            bias = bias_ref[jnp.where(c == qi, 1, jnp.where(c == b.far, 2, 0))]
        for hd in hds:
            s = jnp.dot(kx[hd // HPG], qx_scr[hd], preferred_element_type=F32)
            if masked:
                s = s + bias
            s_scr[b.br, dst, hd] = s
            m_prev = m_scr[b.br, 1 - dst, hd]
            m_new = jnp.maximum(m_prev, jnp.max(_fold_rows(s, jnp.maximum), axis=0,
                                                keepdims=True))
            a_scr[b.br, dst, hd] = jnp.exp2(m_prev - m_new)
            m_scr[b.br, dst, hd] = m_new

    def weigh(b, c, src, hds=range(NSA_HEADS)):
        vt = [b.vt_ref[0, g, c] for g in range(KV_GROUPS)]
        for hd in hds:
            p = jnp.exp2(s_scr[b.br, src, hd] - m_scr[b.br, src, hd]).astype(BF16)
            acc_scr[b.br, hd] = a_scr[b.br, src, hd] * acc_scr[b.br, hd] + jnp.dot(
                vt[hd // HPG], p, preferred_element_type=F32)

    def loop(lo, hi, body):
        def wrapped(c, carry):
            body(c)
            return carry
        lax.fori_loop(lo, hi, wrapped, 0)

    PARTS = [range(i, i + ATT_HEAD_PART) for i in range(0, NSA_HEADS, ATT_HEAD_PART)]

    def middle(b):
        def two_chunks(i):
            c = b.first + 2 * i
            for part in PARTS:
                score(b, c + 1, False, 1, part)
                weigh(b, c, 0, part)
            for part in PARTS:
                score(b, c + 2, True, 0, part)
                weigh(b, c + 1, 1, part)

        def last_two(c):
            for part in PARTS:
                score(b, c + 1, True, 1, part)
                weigh(b, c, 0, part)

        rest = qi - b.first
        odd = rest & 1
        loop(0, jnp.right_shift(rest, 1), two_chunks)
        loop(qi - 1, qi - 1 + odd, last_two)
        return odd

    def finish(b):
        for hd in range(NSA_HEADS):
            w = gt_ref[0, 3 * hd + b.gate_row:3 * hd + b.gate_row + 1, :] / acc_scr[
                b.br, hd, HEAD_DIM:HEAD_DIM + 1, :]
            rows = slice(hd * HEAD_DIM, (hd + 1) * HEAD_DIM)
            yt_scr[rows, :] = yt_scr[rows, :] + w * acc_scr[b.br, hd, :HEAD_DIM, :]

    n_idx = lax.broadcasted_iota(jnp.int32, (NC, TQ), 0)
    t_cmp = qi * TQ + lax.broadcasted_iota(jnp.int32, (NC, TQ), 1)
    vis = (n_idx * CMP_STRIDE + (CMP_BLOCK - 1)) <= t_cmp
    visf = jnp.where(vis, 1.0, 0.0)
    j_idx = lax.broadcasted_iota(jnp.int32, (NSEL, TQ), 0)
    t_sel = qi * TQ + lax.broadcasted_iota(jnp.int32, (NSEL, TQ), 1)
    cur = jnp.right_shift(t_sel, SEL_BLOCK.bit_length() - 1)
    forced = (j_idx == 0) | (j_idx == cur) | (j_idx == cur - 1)
    valid = j_idx <= cur

    heads = range(NSA_HEADS)
    m_c = []
    for hd in heads:
        sc = jnp.where(vis, jnp.dot(kc_ref[0, hd // HPG], qt_ref[0, hd],
                                    preferred_element_type=F32), NEG)
        s_scr[0, 0, hd, :NC, :] = sc
        m_c.append(jnp.max(_fold_rows(sc, jnp.maximum), axis=0, keepdims=True))
    inv = []
    for hd in heads:
        ec = jnp.exp2(s_scr[0, 0, hd, :NC, :] - m_c[hd]) * visf
        s_scr[0, 0, hd, :NC, :] = ec
        den = jnp.sum(_fold_rows(ec, jnp.add), axis=0, keepdims=True)
        inv.append(1.0 / jnp.where(den > 0.0, den, 1.0))
    psum = [jnp.zeros((NC, TQ), F32) for _ in range(KV_GROUPS)]
    for hd in heads:
        g = hd // HPG
        pc = s_scr[0, 0, hd, :NC, :] * inv[hd]
        psum[g] = psum[g] + pc
        oc = jnp.dot(vct_ref[0, g], pc.astype(BF16), preferred_element_type=F32)
        yt_scr[hd * HEAD_DIM:(hd + 1) * HEAD_DIM, :] = gt_ref[0, 3 * hd:3 * hd + 1, :] * oc

    row8 = lax.broadcasted_iota(jnp.int32, (8, TQ), 0)
    pad = jnp.zeros((LANE - HEAD_DIM - NSEL, TQ), BF16)
    ranked = (qi + 1) * TQ > SEL_TOPK * SEL_BLOCK

    @pl.when(ranked)
    def _():
        for g in range(KV_GROUPS):
            imp = jnp.dot(ovt_ref[...], psum[g], preferred_element_type=F32,
                          precision=lax.Precision.HIGHEST)
            imp = jnp.where(forced, FORCE, jnp.where(valid, imp, -FORCE))
            tiles = [imp[8 * r:8 * (r + 1)] for r in range(NSEL // 8)]
            rank = [jnp.zeros((8, TQ), jnp.int32) for _ in tiles]
            for kk in range(NSEL):
                rk = imp[kk:kk + 1, :]
                for r, tile in enumerate(tiles):
                    if 8 * r > kk:
                        ahead = rk >= tile
                    elif 8 * r + 7 < kk:
                        ahead = rk > tile
                    else:
                        ahead = (rk > tile) | ((rk == tile) & (row8 > kk - 8 * r))
                    rank[r] = rank[r] + jnp.where(ahead, 1, 0)
            selneg = jnp.where(jnp.concatenate(rank, axis=0) < SEL_TOPK, 0.0,
                               NEG).astype(BF16)
            for hh in range(HPG):
                hd = g * HPG + hh
                qx_scr[hd] = jnp.concatenate([qt_ref[0, hd], selneg, pad], axis=0)

    @pl.when(jnp.logical_not(ranked))
    def _():
        for hd in heads:
            qx_scr[hd] = jnp.concatenate(
                [qt_ref[0, hd], jnp.zeros((LANE - HEAD_DIM, TQ), BF16)], axis=0)

    sel = Branch(0, kselx_ref, vselt_ref, 0, -1, 1)
    win = Branch(1, kwinx_ref, vwint_ref, jnp.maximum(qi - nwin, 0), qi - nwin, 2)
    m_scr[...] = jnp.full_like(m_scr, NEG)
    acc_scr[...] = jnp.zeros_like(acc_scr)
    loop(sel.first, sel.first + 1, lambda c: score(sel, c, True, 0))
    odd = middle(sel)

    def drain_sel_fill_win(src):
        def body(c):
            for part in PARTS:
                score(win, win.first, True, 0, part)
                weigh(sel, c, src, part)
        return body

    loop(qi, qi + odd, drain_sel_fill_win(1))
    loop(qi, qi + 1 - odd, drain_sel_fill_win(0))
    odd = middle(win)
    loop(qi, qi + odd, lambda c: weigh(win, c, 1))
    loop(qi, qi + 1 - odd, lambda c: weigh(win, c, 0))
    finish(sel)
    finish(win)

    y = jnp.transpose(yt_scr[...]).astype(BF16)
    ya = jnp.dot(y, pa_ref[...], preferred_element_type=F32)
    merged = (ga_ref[0] * ya + mb_ref[0]).astype(BF16)
    o_ref[0] = h_ref[0] + gate_ref[0] * jnp.dot(merged, wo_ref[...],
                                                preferred_element_type=F32)


def _attn(qt, kselx, vselt, kwinx, vwint, kc, vct, gt, ga, mb, h, mods, l, ovt, bias,
          proj_a, w_out):
    B, S, D = h.shape
    TQ = ATT_ROWS
    CK = ATT_CHUNK
    assert TQ == CK and WINDOW % CK == 0 and S % TQ == 0
    const = _whole

    def per_batch(a):
        nd = a.ndim
        return pl.BlockSpec((1,) + a.shape[1:], lambda b, i: (b,) + (0,) * (nd - 1))

    row = lambda w: pl.BlockSpec((1, TQ, w), lambda b, i: (b, i, 0))
    return pl.pallas_call(
        _attn_kernel,
        grid=(B, S // TQ),
        in_specs=[
            pl.BlockSpec((1, NSA_HEADS, HEAD_DIM, TQ), lambda b, i: (b, 0, 0, i)),
            per_batch(kselx), per_batch(vselt), per_batch(kwinx), per_batch(vwint),
            per_batch(kc), per_batch(vct),
            pl.BlockSpec((1, GATE_ROWS, TQ), lambda b, i: (b, 0, i)),
            row(D), row(D), row(D), _mod_vec(mods, l, 1, 2),
            const(ovt), const(bias), const(proj_a), const(w_out)],
        out_specs=row(D),
        out_shape=jax.ShapeDtypeStruct((B, S, D), F32),
        scratch_shapes=[
            pltpu.VMEM((NSA_HEADS, LANE, TQ), BF16),
            pltpu.VMEM((2, 2, NSA_HEADS, CK, TQ), F32),
            pltpu.VMEM((2, 2, NSA_HEADS, 1, TQ), F32), pltpu.VMEM((2, 2, NSA_HEADS, 1, TQ), F32),
            pltpu.VMEM((2, NSA_HEADS, V_ROWS, TQ), F32),
            pltpu.VMEM((NSA_WIDTH, TQ), F32)],
        compiler_params=pltpu.CompilerParams(
            dimension_semantics=("parallel", "parallel"), vmem_limit_bytes=VMEM_LIMIT),
        name="attn",
    )(qt, kselx, vselt, kwinx, vwint, kc, vct, gt, ga, mb, h, mods, ovt, bias,
      _arr(proj_a), _arr(w_out))


def _rope_tables(pos):
    inv = 1.0 / (ROPE_THETA ** (np.arange(0, HEAD_DIM, 2, dtype=np.float64) / HEAD_DIM))
    ang = np.asarray(pos, np.float64)[:, None] * inv[None, :]
    cos = np.concatenate([np.cos(ang), np.cos(ang)], axis=1)
    sin = np.concatenate([-np.sin(ang), np.sin(ang)], axis=1)
    return cos.astype(np.float32), sin.astype(np.float32)


def _tables(S):
    cos, sin = _rope_tables(np.arange(S))
    cos_t = np.concatenate([cos, cos], axis=1)
    sin_t = np.concatenate([sin, sin], axis=1)
    n_cmp_pad = S // CMP_STRIDE
    starts = np.arange(n_cmp_pad) * CMP_STRIDE
    cos_c, sin_c = _rope_tables(starts + CMP_BLOCK - 1)
    n_sel = S // SEL_BLOCK
    sel_start = np.arange(n_sel) * SEL_BLOCK
    overlap = np.clip(np.minimum(starts[:, None] + CMP_BLOCK, sel_start[None, :] + SEL_BLOCK)
                      - np.maximum(starts[:, None], sel_start[None, :]), 0, None) / CMP_BLOCK
    ovt = np.ascontiguousarray(overlap.T).astype(np.float32)
    oh = np.zeros((S, LANE), np.float32)
    oh[np.arange(S), HEAD_DIM + np.arange(S) // SEL_BLOCK] = 1.0
    j = np.arange(ATT_CHUNK)[:, None]
    i = np.arange(ATT_ROWS)[None, :]
    bias = np.stack([np.zeros((ATT_CHUNK, ATT_ROWS)),
                     np.where(j <= i, 0.0, NEG),
                     np.where(j > i, 0.0, NEG)]
                    ).astype(np.float32)
    return tuple(jnp.asarray(a) for a in (cos_t, sin_t, cos_c, sin_c, ovt, oh, bias))


def kernel(x, c, ada_w, ada_b, norm_g, ffn_w_in, ffn_w_out, mix_w_in, cmp_pe, cmp_w1, cmp_w2,
           gm_ln_g, gm_ln_b, gm_ws, gm_bs, proj_a, proj_b, w_out, final_g):
    B, S, D = x.shape
    L = ada_w.shape[0]
    assert S // SEL_BLOCK + HEAD_DIM <= LANE and 3 * NSA_HEADS <= GATE_ROWS
    cos_t, sin_t, cos_c, sin_c, ovt, oh_t, bias = _tables(S)
    swap = np.concatenate([np.arange(HEAD_DIM // 2, HEAD_DIM), np.arange(HEAD_DIM // 2)])

    w_fi, w_fo = ffn_w_in.astype(BF16), ffn_w_out.astype(BF16)
    w_mix = mix_w_in.astype(BF16)
    w_pa, w_pb, w_o = proj_a.astype(BF16), proj_b.astype(BF16), w_out.astype(BF16)
    w1 = cmp_w1.astype(BF16)
    pe = cmp_pe[:, :, :, None, :]
    w2k = jnp.concatenate([cmp_w2[:, 0], cmp_w2[:, 0][..., swap]], axis=-1).astype(BF16)
    w2v = jnp.pad(cmp_w2[:, 1], ((0, 0), (0, 0), (0, LANE - HEAD_DIM))).astype(BF16)
    bsx = jnp.repeat(jnp.swapaxes(gm_bs, 1, 2), HEAD_DIM, axis=2)

    gains = norm_g[:, :, None, :]
    ln_g, ln_b = gm_ln_g[:, None, :], gm_ln_b[:, None, :]

    mods = _ada(c, ada_w, ada_b).reshape(L, B, 3, 3, 1, D)
    h = x
    for l in range(L):
        h = _ffn(h, mods, l, 0, _Pick(gains, (l, 0)), _Pick(w_fi, (l, 0)), _Pick(w_fo, (l, 0)))
        qt, kcmp, vcmp, kselx, vselt, kwinx, vwint, gt, ga, mb = _mix_in(
            h, mods, l, _Pick(gains, (l, 1)), _Pick(w_mix, (l,)), cos_t, sin_t, oh_t,
            _Pick(ln_g, (l,)), _Pick(ln_b, (l,)), _Pick(gm_ws, (l,)), _Pick(bsx, (l,)),
            _Pick(w_pb, (l,)))
        kc, vct = _compress(kcmp, vcmp, _Pick(pe, (l,)), _Pick(w1, (l,)), _Pick(w2k, (l,)),
                            _Pick(w2v, (l,)), cos_c, sin_c)
        h = _attn(qt, kselx, vselt, kwinx, vwint, kc, vct, gt, ga, mb, h, mods, l, ovt, bias,
                  _Pick(w_pa, (l,)), _Pick(w_o, (l,)))
        h = _ffn(h, mods, l, 2, _Pick(gains, (l, 2)), _Pick(w_fi, (l, 1)), _Pick(w_fo, (l, 1)),
                 final_g=final_g[None] if l == L - 1 else None)
    return h
```
